```python
import math
import jax, jax.numpy as jnp
from jax import lax
import numpy as np

D_MODEL = 1024
BATCH = 8
SEQ = 2048
DEPTH = 1

MEM_LEN = 256
HEAD_DIM = 64
ATTN_WIDTH = D_MODEL // 2
N_Q_HEADS = ATTN_WIDTH // HEAD_DIM
N_KV_HEADS = N_Q_HEADS // 4
Q_PER_KV = N_Q_HEADS // N_KV_HEADS
WINDOW = 128
BLOCK = 128
CONV_CH = D_MODEL // 4
CONV_WIDTH = 31
MEM_WIDTH = D_MODEL // 4
N_MEM_HEADS = MEM_WIDTH // HEAD_DIM
MIX_WIDTH = ATTN_WIDTH + CONV_CH + MEM_WIDTH
ROPE_THETA = 500000.0
ROPE_DIM = HEAD_DIM // 4
D_FF = 256 * math.ceil(8 * D_MODEL / 3 / 256)
EPS = 1e-6

Q_COLS = N_Q_HEADS * HEAD_DIM
KV_COLS = N_KV_HEADS * HEAD_DIM
GLU_COLS = 2 * CONV_CH
MQ_COLS = N_MEM_HEADS * HEAD_DIM
IN_COLS = Q_COLS + 2 * KV_COLS + GLU_COLS + MQ_COLS
SPLITS = [Q_COLS, Q_COLS + KV_COLS, Q_COLS + 2 * KV_COLS, Q_COLS + 2 * KV_COLS + GLU_COLS]

kernel_name = "hymba_conformer_swa_sink_memory_layer"


def rms_norm(x, g):
    xf = x.astype(jnp.float32)
    y = xf * lax.rsqrt(jnp.mean(xf * xf, axis=-1, keepdims=True) + EPS)
    return (y * g.astype(jnp.float32)).astype(x.dtype)


def layer_norm(x, g, b):
    xf = x.astype(jnp.float32)
    mu = jnp.mean(xf, axis=-1, keepdims=True)
    var = jnp.mean(jnp.square(xf - mu), axis=-1, keepdims=True)
    y = (xf - mu) * lax.rsqrt(var + EPS)
    return (y * g.astype(jnp.float32) + b.astype(jnp.float32)).astype(x.dtype)


def swiglu(x, w_gate, w_up, w_down):
    return (jax.nn.silu(x @ w_gate) * (x @ w_up)) @ w_down


def rope_partial(x, positions):
    half = ROPE_DIM // 2
    inv_freq = ROPE_THETA ** (-jnp.arange(half, dtype=jnp.float32) / half)
    ang = positions.astype(jnp.float32)[..., None] * inv_freq
    cos = jnp.cos(ang)[:, :, None, :]
    sin = jnp.sin(ang)[:, :, None, :]
    xr = x[..., :ROPE_DIM].astype(jnp.float32)
    x1, x2 = xr[..., :half], xr[..., half:]
    rot = jnp.concatenate([x1 * cos - x2 * sin, x2 * cos + x1 * sin], axis=-1).astype(x.dtype)
    return jnp.concatenate([rot, x[..., ROPE_DIM:]], axis=-1)


def sliding_window_attention(q, k, v, sinks):
    B, T = q.shape[0], q.shape[1]
    nb = T // BLOCK
    qb = q.reshape(B, nb, BLOCK, N_KV_HEADS, Q_PER_KV, HEAD_DIM)

    def with_prev(a):
        a = a.reshape(B, nb, BLOCK, N_KV_HEADS, HEAD_DIM)
        prev = jnp.pad(a, ((0, 0), (1, 0), (0, 0), (0, 0), (0, 0)))[:, :-1]
        return jnp.concatenate([prev, a], axis=2)

    kb, vb = with_prev(k), with_prev(v)
    s = jnp.einsum('bnqgrd,bnkgd->bngrqk', qb, kb).astype(jnp.float32) * (HEAD_DIM ** -0.5)
    qi = jnp.arange(BLOCK)[:, None] + BLOCK
    ki = jnp.arange(2 * BLOCK)[None, :]
    rel = qi - ki
    band = (rel >= 0) & (rel < WINDOW)
    blk = jnp.arange(nb)[:, None, None]
    valid = band[None] & ((blk > 0) | (ki >= BLOCK)[None])
    s = jnp.where(valid[None, :, None, None], s, -jnp.inf)
    sink = sinks.astype(jnp.float32).reshape(N_KV_HEADS, Q_PER_KV)[None, None, :, :, None, None]
    m = jnp.maximum(jnp.max(s, axis=-1, keepdims=True), sink)
    p = jnp.exp(s - m)
    denom = jnp.sum(p, axis=-1, keepdims=True) + jnp.exp(sink - m)
    w = (p / denom).astype(v.dtype)
    o = jnp.einsum('bngrqk,bnkgd->bnqgrd', w, vb)
    return o.reshape(B, T, N_Q_HEADS * HEAD_DIM)


def memory_cross_attention(q, k, v):
    B, T = q.shape[0], q.shape[1]
    s = jnp.einsum('bthd,bmhd->bhtm', q, k).astype(jnp.float32) * (HEAD_DIM ** -0.5)
    w = jax.nn.softmax(s, axis=-1).astype(v.dtype)
    o = jnp.einsum('bhtm,bmhd->bthd', w, v)
    return o.reshape(B, T, N_MEM_HEADS * HEAD_DIM)


def conformer_conv(u, w_dw, b_dw, g_ln, b_ln):
    a, gate = jnp.split(u, 2, axis=-1)
    h = a * jax.nn.sigmoid(gate)
    h = lax.conv_general_dilated(
        h, w_dw[:, None, :].astype(h.dtype), window_strides=(1,),
        padding=((CONV_WIDTH - 1, 0),), dimension_numbers=('NWC', 'WIO', 'NWC'),
        feature_group_count=CONV_CH) + b_dw
    return jax.nn.silu(layer_norm(h, g_ln, b_ln))


def _fwd_setup_inputs(seed: int = 0) -> dict:
    key = jax.random.key(seed)
    ks = jax.random.split(key, 32)
    f32 = jnp.float32

    def w(k, shape, fan_in):
        return jax.random.normal(k, shape, f32) * (fan_in ** -0.5)

    def gain(k, shape):
        return 1.0 + 0.05 * jax.random.normal(k, shape, f32)

    L = DEPTH
    start = jax.random.randint(ks[2], (BATCH, 1), 0, 1024, dtype=jnp.int32)
    positions = start + jnp.arange(SEQ, dtype=jnp.int32)[None, :]
    return {
        "x": jax.random.normal(ks[0], (BATCH, SEQ, D_MODEL), f32),
        "mem": jax.random.normal(ks[1], (BATCH, MEM_LEN, D_MODEL), f32),
        "positions": positions,
        "g_ffn1": gain(ks[3], (L, D_MODEL)),
        "w_ffn1_gate": w(ks[4], (L, D_MODEL, D_FF), D_MODEL),
        "w_ffn1_up": w(ks[5], (L, D_MODEL, D_FF), D_MODEL),
        "w_ffn1_down": w(ks[6], (L, D_FF, D_MODEL), D_FF),
        "g_mix": gain(ks[7], (L, D_MODEL)),
        "w_in": w(ks[8], (L, D_MODEL, IN_COLS), D_MODEL),
        "g_q": gain(ks[9], (L, HEAD_DIM)),
        "g_k": gain(ks[10], (L, HEAD_DIM)),
        "sinks": 0.5 * jax.random.normal(ks[11], (L, N_Q_HEADS), f32),
        "w_dw": w(ks[12], (L, CONV_WIDTH, CONV_CH), CONV_WIDTH),
        "b_dw": 0.02 * jax.random.normal(ks[13], (L, CONV_CH), f32),
        "g_conv_ln": gain(ks[14], (L, CONV_CH)),
        "b_conv_ln": 0.02 * jax.random.normal(ks[15], (L, CONV_CH), f32),
        "g_mem": gain(ks[16], (L, D_MODEL)),
        "w_mem_kv": w(ks[17], (L, D_MODEL, 2 * MQ_COLS), D_MODEL),
        "g_mq": gain(ks[18], (L, HEAD_DIM)),
        "g_mk": gain(ks[19], (L, HEAD_DIM)),
        "w_out": w(ks[20], (L, MIX_WIDTH, D_MODEL), MIX_WIDTH),
        "g_ffn2": gain(ks[21], (L, D_MODEL)),
        "w_ffn2_gate": w(ks[22], (L, D_MODEL, D_FF), D_MODEL),
        "w_ffn2_up": w(ks[23], (L, D_MODEL, D_FF), D_MODEL),
        "w_ffn2_down": w(ks[24], (L, D_FF, D_MODEL), D_FF),
    }


def _fwd_reference(x, mem, positions, g_ffn1, w_ffn1_gate, w_ffn1_up, w_ffn1_down,
              g_mix, w_in, g_q, g_k, sinks, w_dw, b_dw, g_conv_ln, b_conv_ln,
              g_mem, w_mem_kv, g_mq, g_mk, w_out,
              g_ffn2, w_ffn2_gate, w_ffn2_up, w_ffn2_down):
    B, T = x.shape[0], x.shape[1]
    M = mem.shape[1]
    for l in range(DEPTH):
        x = x + 0.5 * swiglu(rms_norm(x, g_ffn1[l]), w_ffn1_gate[l], w_ffn1_up[l], w_ffn1_down[l])

        h = rms_norm(x, g_mix[l])
        proj = h @ w_in[l]
        q, k, v, u, mq = jnp.split(proj, SPLITS, axis=-1)
        q = q.reshape(B, T, N_Q_HEADS, HEAD_DIM)
        k = k.reshape(B, T, N_KV_HEADS, HEAD_DIM)
        v = v.reshape(B, T, N_KV_HEADS, HEAD_DIM)
        q = rope_partial(rms_norm(q, g_q[l]), positions)
        k = rope_partial(rms_norm(k, g_k[l]), positions)
        y_attn = sliding_window_attention(q, k, v, sinks[l])

        y_conv = conformer_conv(u, w_dw[l], b_dw[l], g_conv_ln[l], b_conv_ln[l])

        mkv = rms_norm(mem, g_mem[l]) @ w_mem_kv[l]
        mk, mv = jnp.split(mkv, 2, axis=-1)
        mq = rms_norm(mq.reshape(B, T, N_MEM_HEADS, HEAD_DIM), g_mq[l])
        mk = rms_norm(mk.reshape(B, M, N_MEM_HEADS, HEAD_DIM), g_mk[l])
        mv = mv.reshape(B, M, N_MEM_HEADS, HEAD_DIM)
        y_mem = memory_cross_attention(mq, mk, mv)

        x = x + jnp.concatenate([y_attn, y_conv, y_mem], axis=-1) @ w_out[l]

        x = x + 0.5 * swiglu(rms_norm(x, g_ffn2[l]), w_ffn2_gate[l], w_ffn2_up[l], w_ffn2_down[l])
    return x


import jax as _jax
import jax.numpy as _jnp

TWIN_FORMAT = 'train_step'
FWD_PARAMS = ['x', 'mem', 'positions', 'g_ffn1', 'w_ffn1_gate', 'w_ffn1_up', 'w_ffn1_down', 'g_mix', 'w_in', 'g_q', 'g_k', 'sinks', 'w_dw', 'b_dw', 'g_conv_ln', 'b_conv_ln', 'g_mem', 'w_mem_kv', 'g_mq', 'g_mk', 'w_out', 'g_ffn2', 'w_ffn2_gate', 'w_ffn2_up', 'w_ffn2_down']
TWIN_WEIGHTS = ['g_ffn1', 'w_ffn1_gate', 'w_ffn1_up', 'w_ffn1_down', 'g_mix', 'w_in', 'g_q', 'g_k', 'sinks', 'w_dw', 'b_dw', 'g_conv_ln', 'b_conv_ln', 'g_mem', 'w_mem_kv', 'g_mq', 'g_mk', 'w_out', 'g_ffn2', 'w_ffn2_gate', 'w_ffn2_up', 'w_ffn2_down']
TWIN_DIFF_INPUT = 'x'
TWIN_INPUTS = ['x', 'mem', 'positions', 'g_ffn1', 'w_ffn1_gate', 'w_ffn1_up', 'w_ffn1_down', 'g_mix', 'w_in', 'g_q', 'g_k', 'sinks', 'w_dw', 'b_dw', 'g_conv_ln', 'b_conv_ln', 'g_mem', 'w_mem_kv', 'g_mq', 'g_mk', 'w_out', 'g_ffn2', 'w_ffn2_gate', 'w_ffn2_up', 'w_ffn2_down', 'loss_target', 'm_g_ffn1', 'm_w_ffn1_gate', 'm_w_ffn1_up', 'm_w_ffn1_down', 'm_g_mix', 'm_w_in', 'm_g_q', 'm_g_k', 'm_sinks', 'm_w_dw', 'm_b_dw', 'm_g_conv_ln', 'm_b_conv_ln', 'm_g_mem', 'm_w_mem_kv', 'm_g_mq', 'm_g_mk', 'm_w_out', 'm_g_ffn2', 'm_w_ffn2_gate', 'm_w_ffn2_up', 'm_w_ffn2_down', 'v_g_ffn1', 'v_w_ffn1_gate', 'v_w_ffn1_up', 'v_w_ffn1_down', 'v_g_mix', 'v_w_in', 'v_g_q', 'v_g_k', 'v_sinks', 'v_w_dw', 'v_b_dw', 'v_g_conv_ln', 'v_b_conv_ln', 'v_g_mem', 'v_w_mem_kv', 'v_g_mq', 'v_g_mk', 'v_w_out', 'v_g_ffn2', 'v_w_ffn2_gate', 'v_w_ffn2_up', 'v_w_ffn2_down']
TWIN_OUTPUTS = ['loss', 'grad_x', 'grad_g_ffn1', 'grad_w_ffn1_gate', 'grad_w_ffn1_up', 'grad_w_ffn1_down', 'grad_g_mix', 'grad_w_in', 'grad_g_q', 'grad_g_k', 'grad_sinks', 'grad_w_dw', 'grad_b_dw', 'grad_g_conv_ln', 'grad_b_conv_ln', 'grad_g_mem', 'grad_w_mem_kv', 'grad_g_mq', 'grad_g_mk', 'grad_w_out', 'grad_g_ffn2', 'grad_w_ffn2_gate', 'grad_w_ffn2_up', 'grad_w_ffn2_down', 'delta_g_ffn1', 'delta_w_ffn1_gate', 'delta_w_ffn1_up', 'delta_w_ffn1_down', 'delta_g_mix', 'delta_w_in', 'delta_g_q', 'delta_g_k', 'delta_sinks', 'delta_w_dw', 'delta_b_dw', 'delta_g_conv_ln', 'delta_b_conv_ln', 'delta_g_mem', 'delta_w_mem_kv', 'delta_g_mq', 'delta_g_mk', 'delta_w_out', 'delta_g_ffn2', 'delta_w_ffn2_gate', 'delta_w_ffn2_up', 'delta_w_ffn2_down', 'new_m_g_ffn1', 'new_m_w_ffn1_gate', 'new_m_w_ffn1_up', 'new_m_w_ffn1_down', 'new_m_g_mix', 'new_m_w_in', 'new_m_g_q', 'new_m_g_k', 'new_m_sinks', 'new_m_w_dw', 'new_m_b_dw', 'new_m_g_conv_ln', 'new_m_b_conv_ln', 'new_m_g_mem', 'new_m_w_mem_kv', 'new_m_g_mq', 'new_m_g_mk', 'new_m_w_out', 'new_m_g_ffn2', 'new_m_w_ffn2_gate', 'new_m_w_ffn2_up', 'new_m_w_ffn2_down', 'new_v_g_ffn1', 'new_v_w_ffn1_gate', 'new_v_w_ffn1_up', 'new_v_w_ffn1_down', 'new_v_g_mix', 'new_v_w_in', 'new_v_g_q', 'new_v_g_k', 'new_v_sinks', 'new_v_w_dw', 'new_v_b_dw', 'new_v_g_conv_ln', 'new_v_b_conv_ln', 'new_v_g_mem', 'new_v_w_mem_kv', 'new_v_g_mq', 'new_v_g_mk', 'new_v_w_out', 'new_v_g_ffn2', 'new_v_w_ffn2_gate', 'new_v_w_ffn2_up', 'new_v_w_ffn2_down']
TWIN_LEAF_KINDS = {'loss': 'loss', 'grad_x': 'grad_x', 'grad_g_ffn1': 'grad_w', 'grad_w_ffn1_gate': 'grad_w', 'grad_w_ffn1_up': 'grad_w', 'grad_w_ffn1_down': 'grad_w', 'grad_g_mix': 'grad_w', 'grad_w_in': 'grad_w', 'grad_g_q': 'grad_w', 'grad_g_k': 'grad_w', 'grad_sinks': 'grad_w', 'grad_w_dw': 'grad_w', 'grad_b_dw': 'grad_w', 'grad_g_conv_ln': 'grad_w', 'grad_b_conv_ln': 'grad_w', 'grad_g_mem': 'grad_w', 'grad_w_mem_kv': 'grad_w', 'grad_g_mq': 'grad_w', 'grad_g_mk': 'grad_w', 'grad_w_out': 'grad_w', 'grad_g_ffn2': 'grad_w', 'grad_w_ffn2_gate': 'grad_w', 'grad_w_ffn2_up': 'grad_w', 'grad_w_ffn2_down': 'grad_w', 'delta_g_ffn1': 'delta_w', 'delta_w_ffn1_gate': 'delta_w', 'delta_w_ffn1_up': 'delta_w', 'delta_w_ffn1_down': 'delta_w', 'delta_g_mix': 'delta_w', 'delta_w_in': 'delta_w', 'delta_g_q': 'delta_w', 'delta_g_k': 'delta_w', 'delta_sinks': 'delta_w', 'delta_w_dw': 'delta_w', 'delta_b_dw': 'delta_w', 'delta_g_conv_ln': 'delta_w', 'delta_b_conv_ln': 'delta_w', 'delta_g_mem': 'delta_w', 'delta_w_mem_kv': 'delta_w', 'delta_g_mq': 'delta_w', 'delta_g_mk': 'delta_w', 'delta_w_out': 'delta_w', 'delta_g_ffn2': 'delta_w', 'delta_w_ffn2_gate': 'delta_w', 'delta_w_ffn2_up': 'delta_w', 'delta_w_ffn2_down': 'delta_w', 'new_m_g_ffn1': 'new_m', 'new_m_w_ffn1_gate': 'new_m', 'new_m_w_ffn1_up': 'new_m', 'new_m_w_ffn1_down': 'new_m', 'new_m_g_mix': 'new_m', 'new_m_w_in': 'new_m', 'new_m_g_q': 'new_m', 'new_m_g_k': 'new_m', 'new_m_sinks': 'new_m', 'new_m_w_dw': 'new_m', 'new_m_b_dw': 'new_m', 'new_m_g_conv_ln': 'new_m', 'new_m_b_conv_ln': 'new_m', 'new_m_g_mem': 'new_m', 'new_m_w_mem_kv': 'new_m', 'new_m_g_mq': 'new_m', 'new_m_g_mk': 'new_m', 'new_m_w_out': 'new_m', 'new_m_g_ffn2': 'new_m', 'new_m_w_ffn2_gate': 'new_m', 'new_m_w_ffn2_up': 'new_m', 'new_m_w_ffn2_down': 'new_m', 'new_v_g_ffn1': 'new_v', 'new_v_w_ffn1_gate': 'new_v', 'new_v_w_ffn1_up': 'new_v', 'new_v_w_ffn1_down': 'new_v', 'new_v_g_mix': 'new_v', 'new_v_w_in': 'new_v', 'new_v_g_q': 'new_v', 'new_v_g_k': 'new_v', 'new_v_sinks': 'new_v', 'new_v_w_dw': 'new_v', 'new_v_b_dw': 'new_v', 'new_v_g_conv_ln': 'new_v', 'new_v_b_conv_ln': 'new_v', 'new_v_g_mem': 'new_v', 'new_v_w_mem_kv': 'new_v', 'new_v_g_mq': 'new_v', 'new_v_g_mk': 'new_v', 'new_v_w_out': 'new_v', 'new_v_g_ffn2': 'new_v', 'new_v_w_ffn2_gate': 'new_v', 'new_v_w_ffn2_up': 'new_v', 'new_v_w_ffn2_down': 'new_v'}


def _forward(args):
    return _fwd_reference(*[args[k] for k in FWD_PARAMS])


def _output_shape():
    out = _jax.eval_shape(lambda: _forward(_fwd_setup_inputs(0)))
    return out.shape, out.dtype

N_MICROBATCH = 1
ADAM_LR = 0.001
ADAM_B1 = 0.9
ADAM_B2 = 0.999
ADAM_EPS = 1e-08
ADAM_WD = 0.01
ADAM_STEP = 10
PER_EXAMPLE_BATCH_AXIS = {'x': 0, 'mem': 0, 'positions': 0, 'loss_target': 0}
SHARED_INPUTS = []
_WEIGHT_DTYPES = {'g_ffn1': _jnp.float32, 'w_ffn1_gate': _jnp.float32, 'w_ffn1_up': _jnp.float32, 'w_ffn1_down': _jnp.float32, 'g_mix': _jnp.float32, 'w_in': _jnp.float32, 'g_q': _jnp.float32, 'g_k': _jnp.float32, 'sinks': _jnp.float32, 'w_dw': _jnp.float32, 'b_dw': _jnp.float32, 'g_conv_ln': _jnp.float32, 'b_conv_ln': _jnp.float32, 'g_mem': _jnp.float32, 'w_mem_kv': _jnp.float32, 'g_mq': _jnp.float32, 'g_mk': _jnp.float32, 'w_out': _jnp.float32, 'g_ffn2': _jnp.float32, 'w_ffn2_gate': _jnp.float32, 'w_ffn2_up': _jnp.float32, 'w_ffn2_down': _jnp.float32}
MOMENT_SCALE = {'g_ffn1': 3.067102e+00, 'w_ffn1_gate': 4.463309e-02, 'w_ffn1_up': 4.783522e-02, 'w_ffn1_down': 7.756223e-02, 'g_mix': 1.499645e-01, 'w_in': 8.033807e-02, 'g_q': 1.782101e+00, 'g_k': 1.786370e+00, 'sinks': 5.202356e-01, 'w_dw': 2.020362e-01, 'b_dw': 2.447662e+00, 'g_conv_ln': 7.061342e+00, 'b_conv_ln': 4.635852e+00, 'g_mem': 5.515174e-02, 'w_mem_kv': 4.567201e-02, 'g_mq': 6.263054e-01, 'g_mk': 6.352758e-01, 'w_out': 1.865729e-01, 'g_ffn2': 3.149304e+00, 'w_ffn2_gate': 5.050532e-02, 'w_ffn2_up': 4.735334e-02, 'w_ffn2_down': 7.581849e-02}


def _to_microbatches(a, axis):
    t = _jnp.moveaxis(a, axis, 0)
    t = t.reshape((N_MICROBATCH, t.shape[0] // N_MICROBATCH) + t.shape[1:])
    return _jnp.moveaxis(t, 1, axis + 1)


def setup_inputs(seed: int = 0) -> dict:
    inp = _fwd_setup_inputs(seed)
    key = _jax.random.fold_in(_jax.random.key(seed), 7919)
    shape, _ = _output_shape()
    out = dict(inp)
    out["loss_target"] = _jax.random.normal(_jax.random.fold_in(key, 0), shape, _jnp.float32)
    for i, name in enumerate(TWIN_WEIGHTS):
        w = inp[name].astype(_jnp.float32)
        if MOMENT_SCALE is None:
            s = _jnp.sqrt(_jnp.mean(_jnp.square(w)) + 1e-30)
        else:
            s = MOMENT_SCALE[name]
        km, kv = _jax.random.split(_jax.random.fold_in(key, i + 1))
        out[name] = w
        out["m_" + name] = s * _jax.random.normal(km, w.shape, _jnp.float32)
        out["v_" + name] = (s * s) * _jax.random.uniform(kv, w.shape, _jnp.float32, 0.5, 1.5)
    if N_MICROBATCH > 1:
        for name, axis in PER_EXAMPLE_BATCH_AXIS.items():
            out[name] = _to_microbatches(out[name], axis)
    return {'x': out['x'], 'mem': out['mem'], 'positions': out['positions'], 'g_ffn1': out['g_ffn1'], 'w_ffn1_gate': out['w_ffn1_gate'], 'w_ffn1_up': out['w_ffn1_up'], 'w_ffn1_down': out['w_ffn1_down'], 'g_mix': out['g_mix'], 'w_in': out['w_in'], 'g_q': out['g_q'], 'g_k': out['g_k'], 'sinks': out['sinks'], 'w_dw': out['w_dw'], 'b_dw': out['b_dw'], 'g_conv_ln': out['g_conv_ln'], 'b_conv_ln': out['b_conv_ln'], 'g_mem': out['g_mem'], 'w_mem_kv': out['w_mem_kv'], 'g_mq': out['g_mq'], 'g_mk': out['g_mk'], 'w_out': out['w_out'], 'g_ffn2': out['g_ffn2'], 'w_ffn2_gate': out['w_ffn2_gate'], 'w_ffn2_up': out['w_ffn2_up'], 'w_ffn2_down': out['w_ffn2_down'], 'loss_target': out['loss_target'], 'm_g_ffn1': out['m_g_ffn1'], 'm_w_ffn1_gate': out['m_w_ffn1_gate'], 'm_w_ffn1_up': out['m_w_ffn1_up'], 'm_w_ffn1_down': out['m_w_ffn1_down'], 'm_g_mix': out['m_g_mix'], 'm_w_in': out['m_w_in'], 'm_g_q': out['m_g_q'], 'm_g_k': out['m_g_k'], 'm_sinks': out['m_sinks'], 'm_w_dw': out['m_w_dw'], 'm_b_dw': out['m_b_dw'], 'm_g_conv_ln': out['m_g_conv_ln'], 'm_b_conv_ln': out['m_b_conv_ln'], 'm_g_mem': out['m_g_mem'], 'm_w_mem_kv': out['m_w_mem_kv'], 'm_g_mq': out['m_g_mq'], 'm_g_mk': out['m_g_mk'], 'm_w_out': out['m_w_out'], 'm_g_ffn2': out['m_g_ffn2'], 'm_w_ffn2_gate': out['m_w_ffn2_gate'], 'm_w_ffn2_up': out['m_w_ffn2_up'], 'm_w_ffn2_down': out['m_w_ffn2_down'], 'v_g_ffn1': out['v_g_ffn1'], 'v_w_ffn1_gate': out['v_w_ffn1_gate'], 'v_w_ffn1_up': out['v_w_ffn1_up'], 'v_w_ffn1_down': out['v_w_ffn1_down'], 'v_g_mix': out['v_g_mix'], 'v_w_in': out['v_w_in'], 'v_g_q': out['v_g_q'], 'v_g_k': out['v_g_k'], 'v_sinks': out['v_sinks'], 'v_w_dw': out['v_w_dw'], 'v_b_dw': out['v_b_dw'], 'v_g_conv_ln': out['v_g_conv_ln'], 'v_b_conv_ln': out['v_b_conv_ln'], 'v_g_mem': out['v_g_mem'], 'v_w_mem_kv': out['v_w_mem_kv'], 'v_g_mq': out['v_g_mq'], 'v_g_mk': out['v_g_mk'], 'v_w_out': out['v_w_out'], 'v_g_ffn2': out['v_g_ffn2'], 'v_w_ffn2_gate': out['v_w_ffn2_gate'], 'v_w_ffn2_up': out['v_w_ffn2_up'], 'v_w_ffn2_down': out['v_w_ffn2_down']}


def _loss(weights, diff, rest, loss_target):
    with _jax.named_scope("forward"):
        args = {**rest, TWIN_DIFF_INPUT: diff, **{k: w.astype(_WEIGHT_DTYPES[k]) for k, w in weights.items()}}
        y = _forward(args)
    with _jax.named_scope("loss_head"):
        err = _jnp.square(y.astype(_jnp.float32) - loss_target)
        return 0.5 * _jnp.sum(_jnp.mean(err, axis=-1)) if err.ndim else 0.5 * err


def _adamw(w, g, m, v):
    m = ADAM_B1 * m + (1.0 - ADAM_B1) * g
    v = ADAM_B2 * v + (1.0 - ADAM_B2) * _jnp.square(g)
    m_hat = m / (1.0 - ADAM_B1 ** ADAM_STEP)
    v_hat = v / (1.0 - ADAM_B2 ** ADAM_STEP)
    delta = -ADAM_LR * (m_hat / (_jnp.sqrt(v_hat) + ADAM_EPS) + ADAM_WD * w)
    return delta, m, v


def reference(x, mem, positions, g_ffn1, w_ffn1_gate, w_ffn1_up, w_ffn1_down, g_mix, w_in, g_q, g_k, sinks, w_dw, b_dw, g_conv_ln, b_conv_ln, g_mem, w_mem_kv, g_mq, g_mk, w_out, g_ffn2, w_ffn2_gate, w_ffn2_up, w_ffn2_down, loss_target, m_g_ffn1, m_w_ffn1_gate, m_w_ffn1_up, m_w_ffn1_down, m_g_mix, m_w_in, m_g_q, m_g_k, m_sinks, m_w_dw, m_b_dw, m_g_conv_ln, m_b_conv_ln, m_g_mem, m_w_mem_kv, m_g_mq, m_g_mk, m_w_out, m_g_ffn2, m_w_ffn2_gate, m_w_ffn2_up, m_w_ffn2_down, v_g_ffn1, v_w_ffn1_gate, v_w_ffn1_up, v_w_ffn1_down, v_g_mix, v_w_in, v_g_q, v_g_k, v_sinks, v_w_dw, v_b_dw, v_g_conv_ln, v_b_conv_ln, v_g_mem, v_w_mem_kv, v_g_mq, v_g_mk, v_w_out, v_g_ffn2, v_w_ffn2_gate, v_w_ffn2_up, v_w_ffn2_down):
    given = dict(x=x, mem=mem, positions=positions, g_ffn1=g_ffn1, w_ffn1_gate=w_ffn1_gate, w_ffn1_up=w_ffn1_up, w_ffn1_down=w_ffn1_down, g_mix=g_mix, w_in=w_in, g_q=g_q, g_k=g_k, sinks=sinks, w_dw=w_dw, b_dw=b_dw, g_conv_ln=g_conv_ln, b_conv_ln=b_conv_ln, g_mem=g_mem, w_mem_kv=w_mem_kv, g_mq=g_mq, g_mk=g_mk, w_out=w_out, g_ffn2=g_ffn2, w_ffn2_gate=w_ffn2_gate, w_ffn2_up=w_ffn2_up, w_ffn2_down=w_ffn2_down, loss_target=loss_target, m_g_ffn1=m_g_ffn1, m_w_ffn1_gate=m_w_ffn1_gate, m_w_ffn1_up=m_w_ffn1_up, m_w_ffn1_down=m_w_ffn1_down, m_g_mix=m_g_mix, m_w_in=m_w_in, m_g_q=m_g_q, m_g_k=m_g_k, m_sinks=m_sinks, m_w_dw=m_w_dw, m_b_dw=m_b_dw, m_g_conv_ln=m_g_conv_ln, m_b_conv_ln=m_b_conv_ln, m_g_mem=m_g_mem, m_w_mem_kv=m_w_mem_kv, m_g_mq=m_g_mq, m_g_mk=m_g_mk, m_w_out=m_w_out, m_g_ffn2=m_g_ffn2, m_w_ffn2_gate=m_w_ffn2_gate, m_w_ffn2_up=m_w_ffn2_up, m_w_ffn2_down=m_w_ffn2_down, v_g_ffn1=v_g_ffn1, v_w_ffn1_gate=v_w_ffn1_gate, v_w_ffn1_up=v_w_ffn1_up, v_w_ffn1_down=v_w_ffn1_down, v_g_mix=v_g_mix, v_w_in=v_w_in, v_g_q=v_g_q, v_g_k=v_g_k, v_sinks=v_sinks, v_w_dw=v_w_dw, v_b_dw=v_b_dw, v_g_conv_ln=v_g_conv_ln, v_b_conv_ln=v_b_conv_ln, v_g_mem=v_g_mem, v_w_mem_kv=v_w_mem_kv, v_g_mq=v_g_mq, v_g_mk=v_g_mk, v_w_out=v_w_out, v_g_ffn2=v_g_ffn2, v_w_ffn2_gate=v_w_ffn2_gate, v_w_ffn2_up=v_w_ffn2_up, v_w_ffn2_down=v_w_ffn2_down)
    weights = {n: given[n] for n in TWIN_WEIGHTS}
    shared = {n: given[n] for n in SHARED_INPUTS}
    per_example = {n: given[n] for n in ['x', 'mem', 'positions']}
    grad_fn = _jax.value_and_grad(_loss, argnums=(0, 1))

    def one_microbatch(ex, loss_target):
        ex = dict(ex)
        diff = ex.pop(TWIN_DIFF_INPUT)
        return grad_fn(weights, diff, {**shared, **ex}, loss_target)

    if N_MICROBATCH == 1:
        loss, (grad_w, grad_x) = one_microbatch(per_example, given["loss_target"])
    else:
        def body(carry, xs):
            loss_sum, grad_sum = carry
            l_k, (gw_k, gx_k) = one_microbatch(xs[0], xs[1])
            with _jax.named_scope("update"):
                return (loss_sum + l_k, _jax.tree.map(_jnp.add, grad_sum, gw_k)), gx_k

        init = (_jnp.zeros((), _jnp.float32), _jax.tree.map(_jnp.zeros_like, weights))
        (loss, grad_w), grad_x = _jax.lax.scan(body, init, (per_example, given["loss_target"]))
    with _jax.named_scope("update"):
        delta_w, new_m, new_v = {}, {}, {}
        for n in TWIN_WEIGHTS:
            delta_w[n], new_m[n], new_v[n] = _adamw(weights[n], grad_w[n], given["m_" + n], given["v_" + n])
    return (loss, grad_x, *[grad_w[n] for n in TWIN_WEIGHTS], *[delta_w[n] for n in TWIN_WEIGHTS],
            *[new_m[n] for n in TWIN_WEIGHTS], *[new_v[n] for n in TWIN_WEIGHTS])
```

```python
import functools
import math

import jax
import jax.numpy as jnp
from jax import lax
from jax.experimental import pallas as pl
from jax.experimental.pallas import tpu as pltpu

F32 = jnp.float32
BF16 = jnp.bfloat16

D_MODEL = 1024
SEQ = 2048
MEM_LEN = 256
HEAD_DIM = 64
N_Q_HEADS = 8
N_KV_HEADS = 2
Q_PER_KV = 4
N_MEM_HEADS = 4
BLOCK = 128
CONV_CH = 256
CONV_WIDTH = 31
ROPE_THETA = 500000.0
ROPE_DIM = 16
D_FF = 2816
EPS = 1e-6
Q_COLS = 512
KV_COLS = 128
MQ_COLS = 256
IN_COLS = 1536

N_CHIPS = 4
FF_CHUNK = D_FF // N_CHIPS
IN_CHUNK = IN_COLS // N_CHIPS

ADAM_LR = 0.001
ADAM_B1 = 0.9
ADAM_B2 = 0.999
ADAM_EPS = 1e-08
ADAM_WD = 0.01
ADAM_STEP = 10

LANES = 128
VMEM_LIMIT = 56 * 1024 * 1024
ROW_TILE = 512
MESH = pl.DeviceIdType.MESH
NEG = -1e30


def _pcall(body, **kw):
    return pl.pallas_call(body, **kw)


def _params(sem=None):
    return pltpu.CompilerParams(dimension_semantics=sem, vmem_limit_bytes=VMEM_LIMIT)


def _dot(a, b):
    return jnp.dot(a, b, preferred_element_type=F32)


def _dot_nt(a, b):
    return lax.dot_general(a, b, (((1,), (1,)), ((), ())), preferred_element_type=F32)


def _dot_tn(a, b):
    return lax.dot_general(a, b, (((0,), (0,)), ((), ())), preferred_element_type=F32)


def _sigmoid(x):
    return 1.0 / (1.0 + jnp.exp(-x))


def _full(shape):
    n = len(shape)
    return pl.BlockSpec(shape, lambda *_: (0,) * n)


def _ffn_fwd(x, g, wg, wu, wd, name):
    T, D = x.shape
    nt = T // ROW_TILE

    def body(x_ref, g_ref, wg_ref, wu_ref, wd_ref, xo_ref, h_ref, a_ref, b_ref):
        j = pl.program_id(1)

        @pl.when(j == 0)
        def _():
            xv = x_ref[...]
            rstd = lax.rsqrt(jnp.mean(xv * xv, axis=-1, keepdims=True) + EPS)
            h_ref[...] = (xv * rstd * g_ref[...]).astype(BF16)
            xo_ref[...] = jnp.zeros_like(xo_ref)

        h = h_ref[...]
        a = _dot(h, wg_ref[0])
        b = _dot(h, wu_ref[0])
        a_ref[0] = a.astype(BF16)
        b_ref[0] = b.astype(BF16)
        s = (a * _sigmoid(a)) * b
        xo_ref[...] += _dot(s.astype(BF16), wd_ref[0])

        @pl.when(j == N_CHIPS - 1)
        def _():
            xo_ref[...] = x_ref[...] + 0.5 * xo_ref[...]

    return _pcall(
        body, name=name, grid=(nt, N_CHIPS),
        in_specs=[
            pl.BlockSpec((ROW_TILE, D), lambda t, j: (t, 0)),
            pl.BlockSpec((1, D), lambda t, j: (0, 0)),
            pl.BlockSpec((1, D, FF_CHUNK), lambda t, j: (j, 0, 0)),
            pl.BlockSpec((1, D, FF_CHUNK), lambda t, j: (j, 0, 0)),
            pl.BlockSpec((1, FF_CHUNK, D), lambda t, j: (j, 0, 0)),
        ],
        out_specs=[
            pl.BlockSpec((ROW_TILE, D), lambda t, j: (t, 0)),
            pl.BlockSpec((ROW_TILE, D), lambda t, j: (t, 0)),
            pl.BlockSpec((1, ROW_TILE, FF_CHUNK), lambda t, j: (j, t, 0)),
            pl.BlockSpec((1, ROW_TILE, FF_CHUNK), lambda t, j: (j, t, 0)),
        ],
        out_shape=[
            jax.ShapeDtypeStruct((T, D), F32),
            jax.ShapeDtypeStruct((T, D), BF16),
            jax.ShapeDtypeStruct((N_CHIPS, T, FF_CHUNK), BF16),
            jax.ShapeDtypeStruct((N_CHIPS, T, FF_CHUNK), BF16),
        ],
        compiler_params=_params(("arbitrary", "arbitrary")),
    )(x, g, wg, wu, wd)


def _ffn_bwd(dxo, h, a, b, wg, wu, wd, name):
    T, D = dxo.shape
    tt = 256
    nt = T // tt

    def body(dxo_ref, h_ref, a_ref, b_ref, wg_ref, wu_ref, wd_ref,
             dh_hbm, dwg_ref, dwu_ref, dwd_ref, dh_acc, acc_g, acc_u, acc_d):
        j = pl.program_id(0)
        t = pl.program_id(1)
        do = (0.5 * dxo_ref[...]).astype(BF16)
        av = a_ref[0].astype(F32)
        bv = b_ref[0].astype(F32)
        sig = _sigmoid(av)
        sa = av * sig
        ds = _dot_nt(do, wd_ref[0])
        da = (ds * bv * (sig * (1.0 + av * (1.0 - sig)))).astype(BF16)
        db = (ds * sa).astype(BF16)
        hv = h_ref[...]
        part_d = _dot_tn((sa * bv).astype(BF16), do)
        part_g = _dot_tn(hv, da)
        part_u = _dot_tn(hv, db)
        part_h = _dot_nt(da, wg_ref[0]) + _dot_nt(db, wu_ref[0])
        rows = pl.ds(pl.multiple_of(t * tt, tt), tt)

        @pl.when(j == 0)
        def _():
            dh_acc[rows, :] = part_h

        @pl.when(j > 0)
        def _():
            dh_acc[rows, :] += part_h

        @pl.when(t == 0)
        def _():
            acc_g[...] = part_g
            acc_u[...] = part_u
            acc_d[...] = part_d

        @pl.when(t > 0)
        def _():
            acc_g[...] += part_g
            acc_u[...] += part_u
            acc_d[...] += part_d

        @pl.when(t == nt - 1)
        def _():
            dwg_ref[0] = acc_g[...].astype(BF16)
            dwu_ref[0] = acc_u[...].astype(BF16)
            dwd_ref[0] = acc_d[...].astype(BF16)

        @pl.when((t == nt - 1) & (j == N_CHIPS - 1))
        def _():
            pltpu.sync_copy(dh_acc, dh_hbm)

    return _pcall(
        body, name=name, grid=(N_CHIPS, nt),
        in_specs=[
            pl.BlockSpec((tt, D), lambda j, t: (t, 0)),
            pl.BlockSpec((tt, D), lambda j, t: (t, 0)),
            pl.BlockSpec((1, tt, FF_CHUNK), lambda j, t: (j, t, 0)),
            pl.BlockSpec((1, tt, FF_CHUNK), lambda j, t: (j, t, 0)),
            pl.BlockSpec((1, D, FF_CHUNK), lambda j, t: (j, 0, 0)),
            pl.BlockSpec((1, D, FF_CHUNK), lambda j, t: (j, 0, 0)),
            pl.BlockSpec((1, FF_CHUNK, D), lambda j, t: (j, 0, 0)),
        ],
        out_specs=[
            pl.BlockSpec(memory_space=pl.ANY),
            pl.BlockSpec((1, D, FF_CHUNK), lambda j, t: (j, 0, 0)),
            pl.BlockSpec((1, D, FF_CHUNK), lambda j, t: (j, 0, 0)),
            pl.BlockSpec((1, FF_CHUNK, D), lambda j, t: (j, 0, 0)),
        ],
        out_shape=[
            jax.ShapeDtypeStruct((T, D), F32),
            jax.ShapeDtypeStruct((N_CHIPS, D, FF_CHUNK), BF16),
            jax.ShapeDtypeStruct((N_CHIPS, D, FF_CHUNK), BF16),
            jax.ShapeDtypeStruct((N_CHIPS, FF_CHUNK, D), BF16),
        ],
        scratch_shapes=[
            pltpu.VMEM((T, D), F32),
            pltpu.VMEM((D, FF_CHUNK), F32),
            pltpu.VMEM((D, FF_CHUNK), F32),
            pltpu.VMEM((FF_CHUNK, D), F32),
        ],
        compiler_params=_params(("arbitrary", "arbitrary")),
    )(dxo, h, a, b, wg, wu, wd)


def _rms_fwd(x, g, name):
    T, D = x.shape
    tt = min(ROW_TILE, T)

    def body(x_ref, g_ref, h_ref):
        xv = x_ref[...]
        rstd = lax.rsqrt(jnp.mean(xv * xv, axis=-1, keepdims=True) + EPS)
        h_ref[...] = (xv * rstd * g_ref[...]).astype(BF16)

    return _pcall(
        body, name=name, grid=(T // tt,),
        in_specs=[pl.BlockSpec((tt, D), lambda t: (t, 0)), pl.BlockSpec((1, D), lambda t: (0, 0))],
        out_specs=pl.BlockSpec((tt, D), lambda t: (t, 0)),
        out_shape=jax.ShapeDtypeStruct((T, D), BF16),
        compiler_params=_params(("arbitrary",)),
    )(x, g)


def _rms_bwd(x, g, dh, dres, name):
    T, D = x.shape
    tt = min(ROW_TILE, T)
    has_res = dres is not None

    def body(*refs):
        if has_res:
            x_ref, g_ref, dh_ref, dres_ref, dx_ref, dg_ref = refs
        else:
            x_ref, g_ref, dh_ref, dx_ref, dg_ref = refs
        t = pl.program_id(0)
        xv = x_ref[...]
        rstd = lax.rsqrt(jnp.mean(xv * xv, axis=-1, keepdims=True) + EPS)
        xhat = xv * rstd
        dhv = dh_ref[...]
        gy = dhv * g_ref[...]
        dx = rstd * (gy - xhat * jnp.mean(gy * xhat, axis=-1, keepdims=True))
        if has_res:
            dx = dx + dres_ref[...]
        dx_ref[...] = dx
        part = jnp.sum(dhv * xhat, axis=0, keepdims=True)

        @pl.when(t == 0)
        def _():
            dg_ref[...] = part

        @pl.when(t > 0)
        def _():
            dg_ref[...] += part

    tile = pl.BlockSpec((tt, D), lambda t: (t, 0))
    vec = pl.BlockSpec((1, D), lambda t: (0, 0))
    args = [x, g, dh] + ([dres] if has_res else [])
    return _pcall(
        body, name=name, grid=(T // tt,),
        in_specs=[tile, vec, tile] + ([tile] if has_res else []),
        out_specs=[tile, vec],
        out_shape=[jax.ShapeDtypeStruct((T, D), F32), jax.ShapeDtypeStruct((1, D), F32)],
        compiler_params=_params(("arbitrary",)),
    )(*args)


def _mm_nn(a_list, b, res, name):
    T = a_list[0].shape[0]
    K, N = b.shape
    tt = min(ROW_TILE, T)
    ks = [a.shape[1] for a in a_list]
    na = len(a_list)
    has_res = res is not None

    def body(*refs):
        a_refs = refs[:na]
        b_ref = refs[na]
        o_ref = refs[-1]
        acc = res_v = None
        off = 0
        for a_ref, k in zip(a_refs, ks):
            part = _dot(a_ref[...].astype(BF16), b_ref[off:off + k, :])
            acc = part if acc is None else acc + part
            off += k
        if has_res:
            acc = refs[na + 1][...] + acc
        o_ref[...] = acc

    in_specs = [pl.BlockSpec((tt, k), lambda t: (t, 0)) for k in ks] + [pl.BlockSpec((K, N), lambda t: (0, 0))]
    args = list(a_list) + [b]
    if has_res:
        in_specs.append(pl.BlockSpec((tt, N), lambda t: (t, 0)))
        args.append(res)
    return _pcall(
        body, name=name, grid=(T // tt,), in_specs=in_specs,
        out_specs=pl.BlockSpec((tt, N), lambda t: (t, 0)),
        out_shape=jax.ShapeDtypeStruct((T, N), F32),
        compiler_params=_params(("arbitrary",)),
    )(*args)


def _mm_nt(a_list, b, name):
    T = a_list[0].shape[0]
    K, N = b.shape
    tt = min(ROW_TILE, T)
    ns = [a.shape[1] for a in a_list]
    na = len(a_list)

    def body(*refs):
        b_ref = refs[na]
        o_ref = refs[-1]
        acc = None
        off = 0
        for a_ref, n in zip(refs[:na], ns):
            part = _dot_nt(a_ref[...].astype(BF16), b_ref[:, off:off + n])
            acc = part if acc is None else acc + part
            off += n
        o_ref[...] = acc

    return _pcall(
        body, name=name, grid=(T // tt,),
        in_specs=[pl.BlockSpec((tt, n), lambda t: (t, 0)) for n in ns] + [pl.BlockSpec((K, N), lambda t: (0, 0))],
        out_specs=pl.BlockSpec((tt, K), lambda t: (t, 0)),
        out_shape=jax.ShapeDtypeStruct((T, K), F32),
        compiler_params=_params(("arbitrary",)),
    )(*a_list, b)


def _mm_tn(a_list, b_list, col_chunks, name):
    T = a_list[0].shape[0]
    tt = min(ROW_TILE, T)
    nt = T // tt
    ms = [a.shape[1] for a in a_list]
    ns = [b.shape[1] for b in b_list]
    M, N = sum(ms), sum(ns)
    na, nb = len(a_list), len(b_list)
    cw = N // col_chunks

    def body(*refs):
        a_refs, b_refs = refs[:na], refs[na:na + nb]
        o_ref, acc = refs[na + nb], refs[na + nb + 1]
        t = pl.program_id(0)

        @pl.when(t == 0)
        def _():
            acc[...] = jnp.zeros_like(acc)

        ro = 0
        for a_ref, m in zip(a_refs, ms):
            av = a_ref[...].astype(BF16)
            co = 0
            for b_ref, n in zip(b_refs, ns):
                acc[ro:ro + m, co:co + n] += _dot_tn(av, b_ref[...].astype(BF16))
                co += n
            ro += m

        @pl.when(t == nt - 1)
        def _():
            if col_chunks == 1:
                o_ref[...] = acc[...].astype(BF16)
            else:
                for q in range(col_chunks):
                    o_ref[q] = acc[:, q * cw:(q + 1) * cw].astype(BF16)

    out_shape = (M, N) if col_chunks == 1 else (col_chunks, M, cw)
    return _pcall(
        body, name=name, grid=(nt,),
        in_specs=[pl.BlockSpec((tt, m), lambda t: (t, 0)) for m in ms]
        + [pl.BlockSpec((tt, n), lambda t: (t, 0)) for n in ns],
        out_specs=_full(out_shape),
        out_shape=jax.ShapeDtypeStruct(out_shape, BF16),
        scratch_shapes=[pltpu.VMEM((M, N), F32)],
        compiler_params=_params(("arbitrary",)),
    )(*a_list, *b_list)


def _head_masks():
    lane = lax.broadcasted_iota(jnp.int32, (1, LANES), 1)
    l64 = lane & (HEAD_DIM - 1)
    return lane < HEAD_DIM, l64 < ROPE_DIM // 2, l64 < ROPE_DIM


def _head_mean(v, lo):
    s_lo = jnp.sum(jnp.where(lo, v, 0.0), axis=-1, keepdims=True)
    s_hi = jnp.sum(jnp.where(lo, 0.0, v), axis=-1, keepdims=True)
    return jnp.where(lo, s_lo, s_hi) * (1.0 / HEAD_DIM)


def _rope_swap(v, first, rot):
    up = pltpu.roll(v, LANES - ROPE_DIM // 2, 1)
    down = pltpu.roll(v, ROPE_DIM // 2, 1)
    return jnp.where(first, up, jnp.where(rot, down, 0.0))


def _head_norm(x, g, lo):
    rstd = lax.rsqrt(_head_mean(x * x, lo) + EPS)
    return x * rstd * g


def _head_norm_bwd(x, g, dy, lo):
    rstd = lax.rsqrt(_head_mean(x * x, lo) + EPS)
    xhat = x * rstd
    gy = dy * g
    dx = rstd * (gy - xhat * _head_mean(gy * xhat, lo))
    return dx, dy * xhat


def _rope(xn, cos, sin, first, rot):
    return xn * cos + _rope_swap(xn, first, rot) * sin


def _rope_bwd(dy, cos, sin, first, rot):
    return dy * cos + _rope_swap(dy * sin, first, rot)


def _fold_heads(v):
    return v + pltpu.roll(v, HEAD_DIM, 1)


ATT_ROWS = 256


def _attn_prepare(q_ref, k_ref, v_ref, cos_ref, sin_ref, gq_ref, gk_ref, qs, ks, vs):
    T = q_ref.shape[0]
    lo, first, rot = _head_masks()
    ks[0:BLOCK, :] = jnp.zeros((BLOCK, KV_COLS), BF16)
    vs[0:BLOCK, :] = jnp.zeros((BLOCK, KV_COLS), BF16)

    def step(i, _):
        r0 = pl.multiple_of(i * ATT_ROWS, ATT_ROWS)
        rows = pl.ds(r0, ATT_ROWS)
        prow = pl.ds(r0 + BLOCK, ATT_ROWS)
        cos, sin = cos_ref[rows, :], sin_ref[rows, :]
        for p in range(Q_COLS // LANES):
            cols = slice(p * LANES, (p + 1) * LANES)
            xr = _rope(_head_norm(q_ref[rows, cols], gq_ref[...], lo), cos, sin, first, rot)
            qs[rows, cols] = (xr * (HEAD_DIM ** -0.5)).astype(BF16)
        kr = _rope(_head_norm(k_ref[rows, :], gk_ref[...], lo), cos, sin, first, rot)
        ks[prow, :] = kr.astype(BF16)
        vs[prow, :] = v_ref[rows, :].astype(BF16)
        return 0

    lax.fori_loop(0, T // ATT_ROWS, step, 0)


def _attn_scores(qh, kw, blk, sink):
    s = _dot_nt(qh, kw)
    qi = lax.broadcasted_iota(jnp.int32, (BLOCK, 2 * BLOCK), 0) + BLOCK
    ki = lax.broadcasted_iota(jnp.int32, (BLOCK, 2 * BLOCK), 1)
    rel = qi - ki
    valid = (rel >= 0) & (rel < BLOCK) & ((blk > 0) | (ki >= BLOCK))
    s = jnp.where(valid, s, NEG)
    m = jnp.maximum(jnp.max(s, axis=-1, keepdims=True), sink)
    p = jnp.exp(s - m)
    e_sink = jnp.exp(sink - m)
    inv = 1.0 / (jnp.sum(p, axis=-1, keepdims=True) + e_sink)
    return p * inv, e_sink * inv


def _attn_fwd(proj, cos, sin, gq2, gk2, sinks, name):
    T = proj.shape[0]
    nb = T // BLOCK

    def body(q_ref, k_ref, v_ref, cos_ref, sin_ref, gq_ref, gk_ref, sink_ref, y_ref, qs, ks, vs):
        _attn_prepare(q_ref, k_ref, v_ref, cos_ref, sin_ref, gq_ref, gk_ref, qs, ks, vs)

        def blk_step(blk, _):
            r0 = pl.multiple_of(blk * BLOCK, BLOCK)
            for g in range(N_KV_HEADS):
                gc = slice(g * HEAD_DIM, (g + 1) * HEAD_DIM)
                kw = ks[pl.ds(r0, 2 * BLOCK), gc]
                vw = vs[pl.ds(r0, 2 * BLOCK), gc]
                for r in range(Q_PER_KV):
                    h = g * Q_PER_KV + r
                    hc = slice(h * HEAD_DIM, (h + 1) * HEAD_DIM)
                    w, _ws = _attn_scores(qs[pl.ds(r0, BLOCK), hc], kw, blk, sink_ref[0, h])
                    y_ref[pl.ds(r0, BLOCK), hc] = _dot(w.astype(BF16), vw).astype(BF16)
            return 0

        lax.fori_loop(0, nb, blk_step, 0)

    return _pcall(
        body, name=name, grid=(1,),
        in_specs=[
            pl.BlockSpec((T, Q_COLS), lambda i: (0, 0)),
            pl.BlockSpec((T, KV_COLS), lambda i: (0, Q_COLS // KV_COLS)),
            pl.BlockSpec((T, KV_COLS), lambda i: (0, Q_COLS // KV_COLS + 1)),
            _full((T, LANES)), _full((T, LANES)), _full((1, LANES)), _full((1, LANES)),
            pl.BlockSpec(memory_space=pltpu.SMEM),
        ],
        out_specs=_full((T, Q_COLS)),
        out_shape=jax.ShapeDtypeStruct((T, Q_COLS), BF16),
        scratch_shapes=[
            pltpu.VMEM((T, Q_COLS), BF16),
            pltpu.VMEM((T + BLOCK, KV_COLS), BF16),
            pltpu.VMEM((T + BLOCK, KV_COLS), BF16),
        ],
        compiler_params=_params(("arbitrary",)),
    )(proj, proj, proj, cos, sin, gq2, gk2, sinks)


def _attn_bwd(proj, cos, sin, gq2, gk2, sinks, dyc, name):
    T = proj.shape[0]
    nb = T // BLOCK

    def body(q_ref, k_ref, v_ref, cos_ref, sin_ref, gq_ref, gk_ref, sink_ref, dy_ref,
             dq_ref, dk_ref, dv_ref, dgq_ref, dgk_ref, dsink_ref, qs, ks, vs, dqs, dks, dvs):
        _attn_prepare(q_ref, k_ref, v_ref, cos_ref, sin_ref, gq_ref, gk_ref, qs, ks, vs)
        dks[...] = jnp.zeros_like(dks)
        dvs[...] = jnp.zeros_like(dvs)
        lane = lax.broadcasted_iota(jnp.int32, (1, LANES), 1)

        def blk_step(blk, dsink):
            r0 = pl.multiple_of(blk * BLOCK, BLOCK)
            win = pl.ds(r0, 2 * BLOCK)
            for g in range(N_KV_HEADS):
                gc = slice(g * HEAD_DIM, (g + 1) * HEAD_DIM)
                kw = ks[win, gc]
                vw = vs[win, gc]
                for r in range(Q_PER_KV):
                    h = g * Q_PER_KV + r
                    hc = slice(h * HEAD_DIM, (h + 1) * HEAD_DIM)
                    qh = qs[pl.ds(r0, BLOCK), hc]
                    w, w_sink = _attn_scores(qh, kw, blk, sink_ref[0, h])
                    do = dy_ref[pl.ds(r0, BLOCK), hc].astype(BF16)
                    dvs[win, gc] += _dot_tn(w.astype(BF16), do)
                    dw = _dot_nt(do, vw)
                    delta = jnp.sum(w * dw, axis=-1, keepdims=True)
                    ds = (w * (dw - delta)).astype(BF16)
                    dsink = dsink + jnp.where(lane == h, -jnp.sum(w_sink * delta, axis=0, keepdims=True), 0.0)
                    dqs[pl.ds(r0, BLOCK), hc] = _dot(ds, kw)
                    dks[win, gc] += _dot_tn(ds, qh)
            return dsink

        dsink_ref[...] = lax.fori_loop(0, nb, blk_step, jnp.zeros((1, LANES), F32))

        lo, first, rot = _head_masks()

        def step(i, carry):
            dgq, dgk = carry
            r0 = pl.multiple_of(i * ATT_ROWS, ATT_ROWS)
            rows = pl.ds(r0, ATT_ROWS)
            prow = pl.ds(r0 + BLOCK, ATT_ROWS)
            cos, sin = cos_ref[rows, :], sin_ref[rows, :]
            for p in range(Q_COLS // LANES):
                cols = slice(p * LANES, (p + 1) * LANES)
                dxn = _rope_bwd(dqs[rows, cols] * (HEAD_DIM ** -0.5), cos, sin, first, rot)
                dx, dgp = _head_norm_bwd(q_ref[rows, cols], gq_ref[...], dxn, lo)
                dq_ref[rows, cols] = dx
                dgq = dgq + jnp.sum(dgp, axis=0, keepdims=True)
            dkn = _rope_bwd(dks[prow, :], cos, sin, first, rot)
            dx, dgp = _head_norm_bwd(k_ref[rows, :], gk_ref[...], dkn, lo)
            dk_ref[rows, :] = dx
            dgk = dgk + jnp.sum(dgp, axis=0, keepdims=True)
            dv_ref[rows, :] = dvs[prow, :]
            return dgq, dgk

        zero = jnp.zeros((1, LANES), F32)
        dgq, dgk = lax.fori_loop(0, T // ATT_ROWS, step, (zero, zero))
        dgq_ref[...] = _fold_heads(dgq)
        dgk_ref[...] = _fold_heads(dgk)

    vec = jax.ShapeDtypeStruct((1, LANES), F32)
    return _pcall(
        body, name=name, grid=(1,),
        in_specs=[
            pl.BlockSpec((T, Q_COLS), lambda i: (0, 0)),
            pl.BlockSpec((T, KV_COLS), lambda i: (0, Q_COLS // KV_COLS)),
            pl.BlockSpec((T, KV_COLS), lambda i: (0, Q_COLS // KV_COLS + 1)),
            _full((T, LANES)), _full((T, LANES)), _full((1, LANES)), _full((1, LANES)),
            pl.BlockSpec(memory_space=pltpu.SMEM),
            pl.BlockSpec((T, Q_COLS), lambda i: (0, 0)),
        ],
        out_specs=[_full((T, Q_COLS)), _full((T, KV_COLS)), _full((T, KV_COLS)),
                   _full((1, LANES)), _full((1, LANES)), _full((1, LANES))],
        out_shape=[jax.ShapeDtypeStruct((T, Q_COLS), F32), jax.ShapeDtypeStruct((T, KV_COLS), F32),
                   jax.ShapeDtypeStruct((T, KV_COLS), F32), vec, vec, vec],
        scratch_shapes=[
            pltpu.VMEM((T, Q_COLS), BF16),
            pltpu.VMEM((T + BLOCK, KV_COLS), BF16),
            pltpu.VMEM((T + BLOCK, KV_COLS), BF16),
            pltpu.VMEM((T, Q_COLS), F32),
            pltpu.VMEM((T + BLOCK, KV_COLS), F32),
            pltpu.VMEM((T + BLOCK, KV_COLS), F32),
        ],
        compiler_params=_params(("arbitrary",)),
    )(proj, proj, proj, cos, sin, gq2, gk2, sinks, dyc)


CONV_PAD = 32
CONV_ROWS = 256


def _conv_taps(src, w_ref, r0, first_off, step_sign):
    acc = None
    for i in range(CONV_WIDTH):
        term = w_ref[i:i + 1, :] * src[r0 + first_off + step_sign * i:r0 + first_off + step_sign * i + CONV_ROWS, :]
        acc = term if acc is None else acc + term
    return acc


def _conv_fwd(proj, w_dw, b_dw, g_ln, b_ln, name):
    T = proj.shape[0]
    a_blk = (Q_COLS + 2 * KV_COLS) // CONV_CH

    def body(a_ref, gate_ref, w_ref, bdw_ref, g_ref, b_ref, y_ref, c_ref, pad):
        pad[0:CONV_PAD, :] = jnp.zeros((CONV_PAD, CONV_CH), F32)
        pad[CONV_PAD:, :] = a_ref[...] * _sigmoid(gate_ref[...])
        for n in range(T // CONV_ROWS):
            r0 = n * CONV_ROWS
            c = _conv_taps(pad, w_ref, r0, CONV_PAD - (CONV_WIDTH - 1), 1) + bdw_ref[...]
            c_ref[r0:r0 + CONV_ROWS, :] = c
            mu = jnp.mean(c, axis=-1, keepdims=True)
            cc = c - mu
            rstd = lax.rsqrt(jnp.mean(cc * cc, axis=-1, keepdims=True) + EPS)
            z = cc * rstd * g_ref[...] + b_ref[...]
            y_ref[r0:r0 + CONV_ROWS, :] = (z * _sigmoid(z)).astype(BF16)

    vec = _full((1, CONV_CH))
    return _pcall(
        body, name=name, grid=(1,),
        in_specs=[
            pl.BlockSpec((T, CONV_CH), lambda i: (0, a_blk)),
            pl.BlockSpec((T, CONV_CH), lambda i: (0, a_blk + 1)),
            _full((CONV_WIDTH, CONV_CH)), vec, vec, vec,
        ],
        out_specs=[_full((T, CONV_CH)), _full((T, CONV_CH))],
        out_shape=[jax.ShapeDtypeStruct((T, CONV_CH), BF16), jax.ShapeDtypeStruct((T, CONV_CH), F32)],
        scratch_shapes=[pltpu.VMEM((T + CONV_PAD, CONV_CH), F32)],
        compiler_params=_params(("arbitrary",)),
    )(proj, proj, w_dw, b_dw, g_ln, b_ln)


def _conv_bwd(proj, c, w_dw, g_ln, b_ln, dyc, name):
    T = proj.shape[0]
    a_blk = (Q_COLS + 2 * KV_COLS) // CONV_CH
    y_blk = Q_COLS // CONV_CH

    def body(a_ref, gate_ref, c_ref, w_ref, g_ref, b_ref, dy_ref,
             da_ref, dgate_ref, dw_ref, dbdw_ref, dg_ref, db_ref, pad, dcp):
        pad[0:CONV_PAD, :] = jnp.zeros((CONV_PAD, CONV_CH), F32)
        sg = _sigmoid(gate_ref[...])
        pad[CONV_PAD:, :] = a_ref[...] * sg
        dcp[T:, :] = jnp.zeros((CONV_PAD, CONV_CH), F32)
        dg = db = dbdw = jnp.zeros((1, CONV_CH), F32)
        for n in range(T // CONV_ROWS):
            rows = slice(n * CONV_ROWS, (n + 1) * CONV_ROWS)
            cv = c_ref[rows, :]
            mu = jnp.mean(cv, axis=-1, keepdims=True)
            cc = cv - mu
            rstd = lax.rsqrt(jnp.mean(cc * cc, axis=-1, keepdims=True) + EPS)
            chat = cc * rstd
            z = chat * g_ref[...] + b_ref[...]
            sz = _sigmoid(z)
            dz = dy_ref[rows, :] * (sz * (1.0 + z * (1.0 - sz)))
            dg = dg + jnp.sum(dz * chat, axis=0, keepdims=True)
            db = db + jnp.sum(dz, axis=0, keepdims=True)
            dch = dz * g_ref[...]
            dc = rstd * (dch - jnp.mean(dch, axis=-1, keepdims=True)
                         - chat * jnp.mean(dch * chat, axis=-1, keepdims=True))
            dbdw = dbdw + jnp.sum(dc, axis=0, keepdims=True)
            dcp[rows, :] = dc
        dg_ref[...] = dg
        db_ref[...] = db
        dbdw_ref[...] = dbdw
        dw_ref[CONV_WIDTH:, :] = jnp.zeros((CONV_PAD - CONV_WIDTH, CONV_CH), F32)
        for i in range(CONV_WIDTH):
            off = CONV_PAD - (CONV_WIDTH - 1) + i
            acc = jnp.zeros((1, CONV_CH), F32)
            for n in range(T // CONV_ROWS):
                r0 = n * CONV_ROWS
                acc = acc + jnp.sum(dcp[r0:r0 + CONV_ROWS, :] * pad[r0 + off:r0 + off + CONV_ROWS, :],
                                    axis=0, keepdims=True)
            dw_ref[i:i + 1, :] = acc
        for n in range(T // CONV_ROWS):
            r0 = n * CONV_ROWS
            rows = slice(r0, r0 + CONV_ROWS)
            dhg = _conv_taps(dcp, w_ref, r0, CONV_WIDTH - 1, -1)
            sgv = sg[rows, :]
            da_ref[rows, :] = dhg * sgv
            dgate_ref[rows, :] = dhg * a_ref[rows, :] * sgv * (1.0 - sgv)

    vec = _full((1, CONV_CH))
    vshape = jax.ShapeDtypeStruct((1, CONV_CH), F32)
    return _pcall(
        body, name=name, grid=(1,),
        in_specs=[
            pl.BlockSpec((T, CONV_CH), lambda i: (0, a_blk)),
            pl.BlockSpec((T, CONV_CH), lambda i: (0, a_blk + 1)),
            _full((T, CONV_CH)), _full((CONV_WIDTH, CONV_CH)), vec, vec,
            pl.BlockSpec((T, CONV_CH), lambda i: (0, y_blk)),
        ],
        out_specs=[_full((T, CONV_CH)), _full((T, CONV_CH)), _full((CONV_PAD, CONV_CH)), vec, vec, vec],
        out_shape=[jax.ShapeDtypeStruct((T, CONV_CH), F32), jax.ShapeDtypeStruct((T, CONV_CH), F32),
                   jax.ShapeDtypeStruct((CONV_PAD, CONV_CH), F32), vshape, vshape, vshape],
        scratch_shapes=[pltpu.VMEM((T + CONV_PAD, CONV_CH), F32), pltpu.VMEM((T + CONV_PAD, CONV_CH), F32)],
        compiler_params=_params(("arbitrary",)),
    )(proj, proj, c, w_dw, g_ln, b_ln, dyc)


def _mem_kv(mkv_ref, gk_ref, lo, kn_s, vv_s):
    for p in range(MQ_COLS // LANES):
        cols = slice(p * LANES, (p + 1) * LANES)
        kn_s[:, cols] = _head_norm(mkv_ref[:, cols], gk_ref[...], lo).astype(BF16)
    vv_s[...] = mkv_ref[:, MQ_COLS:].astype(BF16)


def _mem_softmax(qh, kh):
    s = _dot_nt(qh, kh)
    m = jnp.max(s, axis=-1, keepdims=True)
    p = jnp.exp(s - m)
    return p / jnp.sum(p, axis=-1, keepdims=True)


def _mem_fwd(proj, mkv, gq2, gk2, name):
    T = proj.shape[0]
    tt = ROW_TILE
    q_blk = (IN_COLS - MQ_COLS) // MQ_COLS

    def body(q_ref, mkv_ref, gq_ref, gk_ref, y_ref, kn_s, vv_s, qn_s):
        lo, _, _ = _head_masks()
        _mem_kv(mkv_ref, gk_ref, lo, kn_s, vv_s)
        for p in range(MQ_COLS // LANES):
            cols = slice(p * LANES, (p + 1) * LANES)
            qn_s[:, cols] = (_head_norm(q_ref[:, cols], gq_ref[...], lo) * (HEAD_DIM ** -0.5)).astype(BF16)
        for h in range(N_MEM_HEADS):
            hc = slice(h * HEAD_DIM, (h + 1) * HEAD_DIM)
            w = _mem_softmax(qn_s[:, hc], kn_s[:, hc])
            y_ref[:, hc] = _dot(w.astype(BF16), vv_s[:, hc]).astype(BF16)

    return _pcall(
        body, name=name, grid=(T // tt,),
        in_specs=[
            pl.BlockSpec((tt, MQ_COLS), lambda t: (t, q_blk)),
            pl.BlockSpec((MEM_LEN, 2 * MQ_COLS), lambda t: (0, 0)),
            pl.BlockSpec((1, LANES), lambda t: (0, 0)), pl.BlockSpec((1, LANES), lambda t: (0, 0)),
        ],
        out_specs=pl.BlockSpec((tt, MQ_COLS), lambda t: (t, 0)),
        out_shape=jax.ShapeDtypeStruct((T, MQ_COLS), BF16),
        scratch_shapes=[pltpu.VMEM((MEM_LEN, MQ_COLS), BF16), pltpu.VMEM((MEM_LEN, MQ_COLS), BF16),
                        pltpu.VMEM((tt, MQ_COLS), BF16)],
        compiler_params=_params(("arbitrary",)),
    )(proj, mkv, gq2, gk2)


def _mem_bwd(proj, mkv, gq2, gk2, dyc, name):
    T = proj.shape[0]
    tt = ROW_TILE
    nt = T // tt
    q_blk = (IN_COLS - MQ_COLS) // MQ_COLS
    y_blk = (Q_COLS + CONV_CH) // MQ_COLS

    def body(q_ref, mkv_ref, gq_ref, gk_ref, dy_ref, dq_ref, dmkv_ref, dgq_ref, dgk_ref,
             kn_s, vv_s, qn_s, dqn_s, dkn_acc):
        t = pl.program_id(0)
        lo, _, _ = _head_masks()
        _mem_kv(mkv_ref, gk_ref, lo, kn_s, vv_s)

        @pl.when(t == 0)
        def _():
            dkn_acc[...] = jnp.zeros_like(dkn_acc)
            dmkv_ref[...] = jnp.zeros_like(dmkv_ref)
            dgq_ref[...] = jnp.zeros_like(dgq_ref)

        for p in range(MQ_COLS // LANES):
            cols = slice(p * LANES, (p + 1) * LANES)
            qn_s[:, cols] = (_head_norm(q_ref[:, cols], gq_ref[...], lo) * (HEAD_DIM ** -0.5)).astype(BF16)
        for h in range(N_MEM_HEADS):
            hc = slice(h * HEAD_DIM, (h + 1) * HEAD_DIM)
            vc = slice(MQ_COLS + h * HEAD_DIM, MQ_COLS + (h + 1) * HEAD_DIM)
            qh = qn_s[:, hc]
            w = _mem_softmax(qh, kn_s[:, hc])
            do = dy_ref[:, hc].astype(BF16)
            dmkv_ref[:, vc] += _dot_tn(w.astype(BF16), do)
            dw = _dot_nt(do, vv_s[:, hc])
            ds = (w * (dw - jnp.sum(w * dw, axis=-1, keepdims=True))).astype(BF16)
            dqn_s[:, hc] = _dot(ds, kn_s[:, hc])
            dkn_acc[:, hc] += _dot_tn(ds, qh)
        dgq = jnp.zeros((1, LANES), F32)
        for p in range(MQ_COLS // LANES):
            cols = slice(p * LANES, (p + 1) * LANES)
            dx, dgp = _head_norm_bwd(q_ref[:, cols], gq_ref[...], dqn_s[:, cols] * (HEAD_DIM ** -0.5), lo)
            dq_ref[:, cols] = dx
            dgq = dgq + jnp.sum(dgp, axis=0, keepdims=True)
        dgq_ref[...] += dgq

        @pl.when(t == nt - 1)
        def _():
            dgk = jnp.zeros((1, LANES), F32)
            for p in range(MQ_COLS // LANES):
                cols = slice(p * LANES, (p + 1) * LANES)
                dx, dgp = _head_norm_bwd(mkv_ref[:, cols], gk_ref[...], dkn_acc[:, cols], lo)
                dmkv_ref[:, cols] = dx
                dgk = dgk + jnp.sum(dgp, axis=0, keepdims=True)
            dgk_ref[...] = _fold_heads(dgk)
            dgq_ref[...] = _fold_heads(dgq_ref[...])

    vec = pl.BlockSpec((1, LANES), lambda t: (0, 0))
    vshape = jax.ShapeDtypeStruct((1, LANES), F32)
    return _pcall(
        body, name=name, grid=(nt,),
        in_specs=[
            pl.BlockSpec((tt, MQ_COLS), lambda t: (t, q_blk)),
            pl.BlockSpec((MEM_LEN, 2 * MQ_COLS), lambda t: (0, 0)),
            vec, vec,
            pl.BlockSpec((tt, MQ_COLS), lambda t: (t, y_blk)),
        ],
        out_specs=[pl.BlockSpec((tt, MQ_COLS), lambda t: (t, 0)),
                   pl.BlockSpec((MEM_LEN, 2 * MQ_COLS), lambda t: (0, 0)), vec, vec],
        out_shape=[jax.ShapeDtypeStruct((T, MQ_COLS), F32), jax.ShapeDtypeStruct((MEM_LEN, 2 * MQ_COLS), F32),
                   vshape, vshape],
        scratch_shapes=[pltpu.VMEM((MEM_LEN, MQ_COLS), BF16), pltpu.VMEM((MEM_LEN, MQ_COLS), BF16),
                        pltpu.VMEM((tt, MQ_COLS), BF16), pltpu.VMEM((tt, MQ_COLS), F32),
                        pltpu.VMEM((MEM_LEN, MQ_COLS), F32)],
        compiler_params=_params(("arbitrary",)),
    )(proj, mkv, gq2, gk2, dyc)


def _loss_head(y, target, name):
    T, D = y.shape
    tt = ROW_TILE

    def body(y_ref, t_ref, dy_ref, loss_ref):
        t = pl.program_id(0)
        err = y_ref[...] - t_ref[...]
        dy_ref[...] = err * (1.0 / D)
        part = 0.5 * jnp.sum(jnp.mean(err * err, axis=-1, keepdims=True), axis=0, keepdims=True)

        @pl.when(t == 0)
        def _():
            loss_ref[...] = jnp.zeros_like(loss_ref)

        loss_ref[...] += jnp.broadcast_to(part, loss_ref.shape)

    tile = pl.BlockSpec((tt, D), lambda t: (t, 0))
    return _pcall(
        body, name=name, grid=(T // tt,),
        in_specs=[tile, tile],
        out_specs=[tile, pl.BlockSpec((1, LANES), lambda t: (0, 0))],
        out_shape=[jax.ShapeDtypeStruct((T, D), F32), jax.ShapeDtypeStruct((1, LANES), F32)],
        compiler_params=_params(("arbitrary",)),
    )(y, target)


def _adamw(w, g, m, v, name):
    R, C = w.shape
    tr = next((r for r in (512, 352, 256, 128) if R % r == 0), R)

    def body(w_ref, g_ref, m_ref, v_ref, d_ref, nm_ref, nv_ref):
        gv = g_ref[...]
        nm = ADAM_B1 * m_ref[...] + (1.0 - ADAM_B1) * gv
        nv = ADAM_B2 * v_ref[...] + (1.0 - ADAM_B2) * (gv * gv)
        m_hat = nm / (1.0 - ADAM_B1 ** ADAM_STEP)
        v_hat = nv / (1.0 - ADAM_B2 ** ADAM_STEP)
        d_ref[...] = -ADAM_LR * (m_hat / (jnp.sqrt(v_hat) + ADAM_EPS) + ADAM_WD * w_ref[...])
        nm_ref[...] = nm
        nv_ref[...] = nv

    tile = pl.BlockSpec((tr, C), lambda i: (i, 0))
    shape = jax.ShapeDtypeStruct((R, C), F32)
    return _pcall(
        body, name=name, grid=(R // tr,),
        in_specs=[tile] * 4, out_specs=[tile] * 3, out_shape=[shape] * 3,
        compiler_params=_params(("arbitrary",)),
    )(w, g, m, v)


def _mesh_pos():
    return lax.axis_index("x"), lax.axis_index("y"), lax.axis_index("c")


def _other_chips(x, y):
    return [(1 - x, y), (x, 1 - y), (1 - x, 1 - y)]


def _gather_weights(shards, dtypes, name):
    n = len(shards)

    def body(*refs):
        ins, outs = refs[:n], refs[n:2 * n]
        st32, st16 = refs[2 * n:3 * n], refs[3 * n:4 * n]
        in_sems, own_sems, send_sems, recv_sems = refs[4 * n:]
        x, y, c = _mesh_pos()
        chip = 2 * x + y
        loads = [pltpu.make_async_copy(ins[i], st32[i], in_sems.at[i]) for i in range(n)]
        for cp in loads:
            cp.start()
        sends, owns = [], []
        for i in range(n):
            loads[i].wait()
            st16[i][...] = st32[i][...].astype(dtypes[i])
            own = pltpu.make_async_copy(st16[i], outs[i].at[chip], own_sems.at[i])
            own.start()
            owns.append(own)
            for k, (px, py) in enumerate(_other_chips(x, y)):
                cp = pltpu.make_async_remote_copy(
                    src_ref=st16[i], dst_ref=outs[i].at[chip], send_sem=send_sems.at[i, k],
                    recv_sem=recv_sems.at[i, k], device_id=(px, py, c), device_id_type=MESH)
                cp.start()
                sends.append(cp)
        for i in range(n):
            for k, (px, py) in enumerate(_other_chips(x, y)):
                pltpu.make_async_remote_copy(
                    src_ref=st16[i], dst_ref=outs[i].at[2 * px + py], send_sem=send_sems.at[i, k],
                    recv_sem=recv_sems.at[i, k], device_id=(px, py, c), device_id_type=MESH).wait_recv()
        for cp in sends:
            cp.wait_send()
        for cp in owns:
            cp.wait()

    hbm = pl.BlockSpec(memory_space=pl.ANY)
    return _pcall(
        body, name=name,
        in_specs=[hbm] * n, out_specs=[hbm] * n,
        out_shape=[jax.ShapeDtypeStruct((N_CHIPS,) + s.shape, d) for s, d in zip(shards, dtypes)],
        scratch_shapes=[pltpu.VMEM(s.shape, F32) for s in shards] + [pltpu.VMEM(s.shape, d) for s, d in zip(shards, dtypes)]
        + [pltpu.SemaphoreType.DMA((n,)), pltpu.SemaphoreType.DMA((n,)),
           pltpu.SemaphoreType.DMA((n, 3)), pltpu.SemaphoreType.DMA((n, 3))],
        compiler_params=pltpu.CompilerParams(vmem_limit_bytes=VMEM_LIMIT),
    )(*shards)


def _reduce_quarter(g, name):
    _, R, C = g.shape
    hr = R // 2

    def body(g_ref, out_ref, sib, snd, rcv, sem_a, sem_d, send_sems, recv_sems):
        x, y, c = _mesh_pos()
        chip = 2 * x + y
        sibling = (x, y, 1 - c)
        mine = pl.ds(pl.multiple_of(c * hr, 16), hr)
        theirs = pl.ds(pl.multiple_of((1 - c) * hr, 16), hr)
        swap = pltpu.make_async_remote_copy(
            src_ref=g_ref.at[:, theirs, :], dst_ref=sib, send_sem=sem_a.at[0], recv_sem=sem_a.at[1],
            device_id=sibling, device_id_type=MESH)
        swap.start()
        swap.wait()
        sends = []
        for k, (px, py) in enumerate(_other_chips(x, y)):
            q = 2 * px + py
            snd[k] = (g_ref[q, mine, :].astype(F32) + sib[q].astype(F32)).astype(BF16)
            cp = pltpu.make_async_remote_copy(
                src_ref=snd.at[k], dst_ref=rcv.at[k], send_sem=send_sems.at[k], recv_sem=recv_sems.at[k],
                device_id=(px, py, c), device_id_type=MESH)
            cp.start()
            sends.append(cp)
        acc = g_ref[chip, mine, :].astype(F32) + sib[chip].astype(F32)
        for k in range(3):
            sends[k].wait_recv()
            acc = acc + rcv[k].astype(F32)
        out_ref[mine, :] = acc
        back = pltpu.make_async_remote_copy(
            src_ref=out_ref.at[mine, :], dst_ref=out_ref.at[mine, :], send_sem=sem_d.at[0], recv_sem=sem_d.at[1],
            device_id=sibling, device_id_type=MESH)
        back.start()
        back.wait()
        for cp in sends:
            cp.wait_send()

    vmem = pl.BlockSpec(memory_space=pltpu.VMEM)
    return _pcall(
        body, name=name,
        in_specs=[vmem], out_specs=vmem,
        out_shape=jax.ShapeDtypeStruct((R, C), F32),
        scratch_shapes=[
            pltpu.VMEM((N_CHIPS, hr, C), BF16), pltpu.VMEM((3, hr, C), BF16), pltpu.VMEM((3, hr, C), BF16),
            pltpu.SemaphoreType.DMA((2,)), pltpu.SemaphoreType.DMA((2,)),
            pltpu.SemaphoreType.DMA((3,)), pltpu.SemaphoreType.DMA((3,)),
        ],
        compiler_params=pltpu.CompilerParams(vmem_limit_bytes=VMEM_LIMIT),
    )(g)


def _allreduce_small(v, name):
    R, C = v.shape
    n_dev = 8

    def body(v_ref, out_ref, buf, send_sems, recv_sems):
        x, y, c = _mesh_pos()
        me = 4 * x + 2 * y + c
        buf[me] = v_ref[...]
        peers = []
        for k in range(1, n_dev):
            kx, ky, kc = (k >> 2) & 1, (k >> 1) & 1, k & 1
            px = 1 - x if kx else x
            py = 1 - y if ky else y
            pc = 1 - c if kc else c
            peers.append((px, py, pc))
        sends = []
        for k, peer in enumerate(peers):
            cp = pltpu.make_async_remote_copy(
                src_ref=v_ref, dst_ref=buf.at[me], send_sem=send_sems.at[k], recv_sem=recv_sems.at[k],
                device_id=peer, device_id_type=MESH)
            cp.start()
            sends.append(cp)
        for k, (px, py, pc) in enumerate(peers):
            pltpu.make_async_remote_copy(
                src_ref=v_ref, dst_ref=buf.at[4 * px + 2 * py + pc], send_sem=send_sems.at[k],
                recv_sem=recv_sems.at[k], device_id=(px, py, pc), device_id_type=MESH).wait_recv()
        for cp in sends:
            cp.wait_send()
        acc = buf[0]
        for i in range(1, n_dev):
            acc = acc + buf[i]
        out_ref[...] = acc

    vmem = pl.BlockSpec(memory_space=pltpu.VMEM)
    return _pcall(
        body, name=name,
        in_specs=[vmem], out_specs=vmem,
        out_shape=jax.ShapeDtypeStruct((R, C), F32),
        scratch_shapes=[pltpu.VMEM((n_dev, R, C), F32),
                        pltpu.SemaphoreType.DMA((n_dev - 1,)), pltpu.SemaphoreType.DMA((n_dev - 1,))],
        compiler_params=pltpu.CompilerParams(vmem_limit_bytes=VMEM_LIMIT),
    )(v)


def _rope_tables(positions):
    half = ROPE_DIM // 2
    inv_freq = ROPE_THETA ** (-jnp.arange(half, dtype=F32) / half)
    ang = positions.astype(F32)[:, None] * inv_freq
    cos, sin = jnp.cos(ang), jnp.sin(ang)
    T = positions.shape[0]
    ones = jnp.ones((T, HEAD_DIM - ROPE_DIM), F32)
    c64 = jnp.concatenate([cos, cos, ones], axis=1)
    s64 = jnp.concatenate([-sin, sin, 0.0 * ones], axis=1)
    return jnp.tile(c64, (1, 2)), jnp.tile(s64, (1, 2))


def _local_step(x, mem, positions, target, small, big):
    T = x.shape[0]
    cos, sin = _rope_tables(positions)
    two = lambda g: jnp.tile(g, (1, 2))
    gq2, gk2, gmq2, gmk2 = two(small["g_q"]), two(small["g_k"]), two(small["g_mq"]), two(small["g_mk"])

    x1, h1, a1, b1 = _ffn_fwd(x, small["g_ffn1"], big["wg1"], big["wu1"], big["wd1"], "ffn1_fwd")
    hm = _rms_fwd(x1, small["g_mix"], "mix_norm")
    proj = _mm_nn([hm], big["w_in"], None, "in_proj")
    hmem = _rms_fwd(mem, small["g_mem"], "mem_norm")
    mkv = _mm_nn([hmem], big["w_mkv"], None, "mem_proj")
    ya = _attn_fwd(proj, cos, sin, gq2, gk2, small["sinks"], "swa_fwd")
    yc, cpre = _conv_fwd(proj, small["w_dw"], small["b_dw"], small["g_conv_ln"], small["b_conv_ln"], "conv_fwd")
    ym = _mem_fwd(proj, mkv, gmq2, gmk2, "memattn_fwd")
    x2 = _mm_nn([ya, yc, ym], big["w_out"], x1, "out_proj")
    x3, h2, a2, b2 = _ffn_fwd(x2, small["g_ffn2"], big["wg2"], big["wu2"], big["wd2"], "ffn2_fwd")
    dx3, loss = _loss_head(x3, target, "loss_head")

    dh2, dwg2, dwu2, dwd2 = _ffn_bwd(dx3, h2, a2, b2, big["wg2"], big["wu2"], big["wd2"], "ffn2_bwd")
    dx2, dg_ffn2 = _rms_bwd(x2, small["g_ffn2"], dh2, dx3, "ffn2_norm_bwd")
    dyc = _mm_nt([dx2], big["w_out"], "out_proj_bwd")
    dw_out = _mm_tn([ya, yc, ym], [dx2], 1, "out_proj_wgrad")
    dq, dk, dv, dgq, dgk, dsinks = _attn_bwd(proj, cos, sin, gq2, gk2, small["sinks"], dyc, "swa_bwd")
    da, dgate, dw_dw, db_dw, dg_ln, db_ln = _conv_bwd(
        proj, cpre, small["w_dw"], small["g_conv_ln"], small["b_conv_ln"], dyc, "conv_bwd")
    dmq, dmkv, dgmq, dgmk = _mem_bwd(proj, mkv, gmq2, gmk2, dyc, "memattn_bwd")
    pieces = [dq, dk, dv, da, dgate, dmq]
    dhm = _mm_nt(pieces, big["w_in"], "in_proj_bwd")
    dw_in = _mm_tn([hm], pieces, N_CHIPS, "in_proj_wgrad")
    dhmem = _mm_nt([dmkv], big["w_mkv"], "mem_proj_bwd")
    dw_mkv = _mm_tn([hmem], [dmkv], 1, "mem_proj_wgrad")
    _, dg_mem = _rms_bwd(mem, small["g_mem"], dhmem, None, "mem_norm_bwd")
    dx1, dg_mix = _rms_bwd(x1, small["g_mix"], dhm, dx2, "mix_norm_bwd")
    dh1, dwg1, dwu1, dwd1 = _ffn_bwd(dx1, h1, a1, b1, big["wg1"], big["wu1"], big["wd1"], "ffn1_bwd")
    dx, dg_ffn1 = _rms_bwd(x, small["g_ffn1"], dh1, dx1, "ffn1_norm_bwd")

    big_grads = dict(wg1=dwg1, wu1=dwu1, wd1=dwd1, w_in=dw_in, w_mkv=dw_mkv.reshape(N_CHIPS, -1, 2 * MQ_COLS),
                     w_out=dw_out.reshape(N_CHIPS, -1, D_MODEL), wg2=dwg2, wu2=dwu2, wd2=dwd2)
    small_grads = dict(
        g_ffn1=dg_ffn1, g_mix=dg_mix, g_q=dgq[:, :HEAD_DIM], g_k=dgk[:, :HEAD_DIM], sinks=dsinks[:, :N_Q_HEADS],
        w_dw=dw_dw[:CONV_WIDTH], b_dw=db_dw, g_conv_ln=dg_ln, b_conv_ln=db_ln, g_mem=dg_mem,
        g_mq=dgmq[:, :HEAD_DIM], g_mk=dgmk[:, :HEAD_DIM], g_ffn2=dg_ffn2)
    return loss, dx, big_grads, small_grads


SMALL_NAMES = ["g_ffn1", "g_mix", "g_q", "g_k", "sinks", "b_dw", "g_conv_ln", "b_conv_ln", "g_mem", "g_mq", "g_mk",
               "g_ffn2"]
PACK_COLS = 1024


def _pack(parts):
    flat = [p.reshape(-1) for p in parts]
    offs, o = [], 0
    for f in flat:
        offs.append(o)
        o += f.shape[0]
    rows = -(-o // (8 * PACK_COLS)) * 8
    pad = jnp.zeros((rows * PACK_COLS - o,), F32)
    return jnp.concatenate(flat + [pad]).reshape(rows, PACK_COLS), offs


def _unpack(packed, offs, shapes):
    flat = packed.reshape(-1)
    return [flat[o:o + math.prod(s)].reshape(s) for o, s in zip(offs, shapes)]


def kernel(x, mem, positions, g_ffn1, w_ffn1_gate, w_ffn1_up, w_ffn1_down, g_mix, w_in, g_q, g_k, sinks, w_dw, b_dw, g_conv_ln, b_conv_ln, g_mem, w_mem_kv, g_mq, g_mk, w_out, g_ffn2, w_ffn2_gate, w_ffn2_up, w_ffn2_down, loss_target, m_g_ffn1, m_w_ffn1_gate, m_w_ffn1_up, m_w_ffn1_down, m_g_mix, m_w_in, m_g_q, m_g_k, m_sinks, m_w_dw, m_b_dw, m_g_conv_ln, m_b_conv_ln, m_g_mem, m_w_mem_kv, m_g_mq, m_g_mk, m_w_out, m_g_ffn2, m_w_ffn2_gate, m_w_ffn2_up, m_w_ffn2_down, v_g_ffn1, v_w_ffn1_gate, v_w_ffn1_up, v_w_ffn1_down, v_g_mix, v_w_in, v_g_q, v_g_k, v_sinks, v_w_dw, v_b_dw, v_g_conv_ln, v_b_conv_ln, v_g_mem, v_w_mem_kv, v_g_mq, v_g_mk, v_w_out, v_g_ffn2, v_w_ffn2_gate, v_w_ffn2_up, v_w_ffn2_down):
    args = dict(locals())
    weight_names = ["g_ffn1", "w_ffn1_gate", "w_ffn1_up", "w_ffn1_down", "g_mix", "w_in", "g_q", "g_k", "sinks",
                    "w_dw", "b_dw", "g_conv_ln", "b_conv_ln", "g_mem", "w_mem_kv", "g_mq", "g_mk", "w_out", "g_ffn2",
                    "w_ffn2_gate", "w_ffn2_up", "w_ffn2_down"]
    big_names = ["w_ffn1_gate", "w_ffn1_up", "w_ffn1_down", "w_in", "w_mem_kv", "w_out",
                 "w_ffn2_gate", "w_ffn2_up", "w_ffn2_down"]
    short = dict(w_ffn1_gate="wg1", w_ffn1_up="wu1", w_ffn1_down="wd1", w_in="w_in", w_mem_kv="w_mkv",
                 w_out="w_out", w_ffn2_gate="wg2", w_ffn2_up="wu2", w_ffn2_down="wd2")

    shards = [args[n][0] for n in big_names]
    gathered = _gather_weights(shards + [w_dw[0]], [BF16] * len(shards) + [F32], "gather_weights")
    big = {short[n]: gathered[i] for i, n in enumerate(big_names)}
    big["w_in"] = jnp.transpose(big["w_in"], (1, 0, 2)).reshape(D_MODEL, IN_COLS)
    big["w_mkv"] = big["w_mkv"].reshape(D_MODEL, 2 * MQ_COLS)
    big["w_out"] = big["w_out"].reshape(D_MODEL, D_MODEL)
    small = {n: args[n] for n in SMALL_NAMES}
    small["w_dw"] = jnp.transpose(gathered[-1], (1, 0, 2)).reshape(CONV_WIDTH, CONV_CH)

    loss, dx, big_grads, small_grads = _local_step(x[0], mem[0], positions[0], loss_target[0], small, big)

    small_order = SMALL_NAMES + ["w_dw"]
    packed, offs = _pack([small_grads[n] for n in small_order] + [loss[:, :1]])
    total = _allreduce_small(packed, "allreduce_small")
    shapes = [small_grads[n].shape for n in small_order] + [(1, 1)]
    summed = dict(zip(small_order + ["loss"], _unpack(total, offs, shapes)))
    chip = 2 * lax.axis_index("x") + lax.axis_index("y")
    dw_dw_full = summed.pop("w_dw")
    loss_out = summed.pop("loss").reshape(())

    grads = {n: summed[n] for n in SMALL_NAMES}
    grads["w_dw"] = lax.dynamic_slice_in_dim(dw_dw_full, chip * (CONV_CH // N_CHIPS), CONV_CH // N_CHIPS, axis=1)
    for n in big_names:
        grads[n] = _reduce_quarter(big_grads[short[n]], "reduce_" + short[n])

    delta, new_m, new_v = {}, {}, {}
    for n in big_names:
        d, nm, nv = _adamw(args[n][0], grads[n], args["m_" + n][0], args["v_" + n][0], "adamw_" + short[n])
        delta[n], new_m[n], new_v[n] = d[None], nm[None], nv[None]
    tiny = SMALL_NAMES + ["w_dw"]
    pw, poffs = _pack([args[n] for n in tiny])
    pg, _ = _pack([grads[n] for n in tiny])
    pm, _ = _pack([args["m_" + n] for n in tiny])
    pv, _ = _pack([args["v_" + n] for n in tiny])
    pd, pnm, pnv = _adamw(pw, pg, pm, pv, "adamw_small")
    tshapes = [args[n].shape for n in tiny]
    for store, packed_out in ((delta, pd), (new_m, pnm), (new_v, pnv)):
        for n, val in zip(tiny, _unpack(packed_out, poffs, tshapes)):
            store[n] = val

    def shaped(n, v):
        return v.reshape(args[n].shape)

    return (loss_out, dx[None],
            *[shaped(n, grads[n]) for n in weight_names],
            *[shaped(n, delta[n]) for n in weight_names],
            *[shaped(n, new_m[n]) for n in weight_names],
            *[shaped(n, new_v[n]) for n in weight_names])
```

```python
import functools
import math

import jax
import jax.numpy as jnp
from jax import lax
from jax.experimental import pallas as pl
from jax.experimental.pallas import tpu as pltpu

F32 = jnp.float32
BF16 = jnp.bfloat16

D_MODEL = 1024
SEQ = 2048
MEM_LEN = 256
HEAD_DIM = 64
N_Q_HEADS = 8
N_KV_HEADS = 2
Q_PER_KV = 4
N_MEM_HEADS = 4
BLOCK = 128
CONV_CH = 256
CONV_WIDTH = 31
ROPE_THETA = 500000.0
ROPE_DIM = 16
D_FF = 2816
EPS = 1e-6
Q_COLS = 512
KV_COLS = 128
MQ_COLS = 256
IN_COLS = 1536

N_CHIPS = 4
FF_CHUNK = D_FF // N_CHIPS
IN_CHUNK = IN_COLS // N_CHIPS

ADAM_LR = 0.001
ADAM_B1 = 0.9
ADAM_B2 = 0.999
ADAM_EPS = 1e-08
ADAM_WD = 0.01
ADAM_STEP = 10

LANES = 128
VMEM_LIMIT = 56 * 1024 * 1024
ROW_TILE = 512
MESH = pl.DeviceIdType.MESH
NEG = -1e30


def _pcall(body, **kw):
    return pl.pallas_call(body, **kw)


def _params(sem=None):
    return pltpu.CompilerParams(dimension_semantics=sem, vmem_limit_bytes=VMEM_LIMIT)


def _dot(a, b):
    return jnp.dot(a, b, preferred_element_type=F32)


def _dot_nt(a, b):
    return lax.dot_general(a, b, (((1,), (1,)), ((), ())), preferred_element_type=F32)


def _dot_tn(a, b):
    return lax.dot_general(a, b, (((0,), (0,)), ((), ())), preferred_element_type=F32)


def _sigmoid(x):
    return 1.0 / (1.0 + jnp.exp(-x))


def _full(shape):
    n = len(shape)
    return pl.BlockSpec(shape, lambda *_: (0,) * n)


def _ffn_fwd(x, g, wg, wu, wd, name):
    T, D = x.shape
    nt = T // ROW_TILE

    def body(x_ref, g_ref, wg_ref, wu_ref, wd_ref, xo_ref, h_ref, a_ref, b_ref):
        j = pl.program_id(1)

        @pl.when(j == 0)
        def _():
            xv = x_ref[...]
            rstd = lax.rsqrt(jnp.mean(xv * xv, axis=-1, keepdims=True) + EPS)
            h_ref[...] = (xv * rstd * g_ref[...]).astype(BF16)
            xo_ref[...] = jnp.zeros_like(xo_ref)

        h = h_ref[...]
        a = _dot_nt(h, wg_ref[0])
        b = _dot_nt(h, wu_ref[0])
        a_ref[0] = a.astype(BF16)
        b_ref[0] = b.astype(BF16)
        s = (a * _sigmoid(a)) * b
        xo_ref[...] += _dot(s.astype(BF16), wd_ref[0])

        @pl.when(j == N_CHIPS - 1)
        def _():
            xo_ref[...] = x_ref[...] + 0.5 * xo_ref[...]

    return _pcall(
        body, name=name, grid=(nt, N_CHIPS),
        in_specs=[
            pl.BlockSpec((ROW_TILE, D), lambda t, j: (t, 0)),
            pl.BlockSpec((1, D), lambda t, j: (0, 0)),
            pl.BlockSpec((1, FF_CHUNK, D), lambda t, j: (j, 0, 0)),
            pl.BlockSpec((1, FF_CHUNK, D), lambda t, j: (j, 0, 0)),
            pl.BlockSpec((1, FF_CHUNK, D), lambda t, j: (j, 0, 0)),
        ],
        out_specs=[
            pl.BlockSpec((ROW_TILE, D), lambda t, j: (t, 0)),
            pl.BlockSpec((ROW_TILE, D), lambda t, j: (t, 0)),
            pl.BlockSpec((1, ROW_TILE, FF_CHUNK), lambda t, j: (j, t, 0)),
            pl.BlockSpec((1, ROW_TILE, FF_CHUNK), lambda t, j: (j, t, 0)),
        ],
        out_shape=[
            jax.ShapeDtypeStruct((T, D), F32),
            jax.ShapeDtypeStruct((T, D), BF16),
            jax.ShapeDtypeStruct((N_CHIPS, T, FF_CHUNK), BF16),
            jax.ShapeDtypeStruct((N_CHIPS, T, FF_CHUNK), BF16),
        ],
        compiler_params=_params(("arbitrary", "arbitrary")),
    )(x, g, wg, wu, wd)


def _ffn_bwd(dxo, h, a, b, wg, wu, wd, name):
    T, D = dxo.shape
    tt = 256
    nt = T // tt

    def body(dxo_ref, h_ref, a_ref, b_ref, wg_ref, wu_ref, wd_ref,
             dh_hbm, dwg_ref, dwu_ref, dwd_ref, dh_acc, acc_g, acc_u, acc_d):
        j = pl.program_id(0)
        t = pl.program_id(1)
        do = (0.5 * dxo_ref[...]).astype(BF16)
        av = a_ref[0].astype(F32)
        bv = b_ref[0].astype(F32)
        sig = _sigmoid(av)
        sa = av * sig
        ds = _dot_nt(do, wd_ref[0])
        da = (ds * bv * (sig * (1.0 + av * (1.0 - sig)))).astype(BF16)
        db = (ds * sa).astype(BF16)
        hv = h_ref[...]
        part_d = _dot_tn((sa * bv).astype(BF16), do)
        part_g = _dot_tn(da, hv)
        part_u = _dot_tn(db, hv)
        part_h = _dot(da, wg_ref[0]) + _dot(db, wu_ref[0])
        rows = pl.ds(pl.multiple_of(t * tt, tt), tt)

        @pl.when(j == 0)
        def _():
            dh_acc[rows, :] = part_h

        @pl.when(j > 0)
        def _():
            dh_acc[rows, :] += part_h

        @pl.when(t == 0)
        def _():
            acc_g[...] = part_g
            acc_u[...] = part_u
            acc_d[...] = part_d

        @pl.when(t > 0)
        def _():
            acc_g[...] += part_g
            acc_u[...] += part_u
            acc_d[...] += part_d

        @pl.when(t == nt - 1)
        def _():
            dwg_ref[0] = acc_g[...].astype(BF16)
            dwu_ref[0] = acc_u[...].astype(BF16)
            dwd_ref[0] = acc_d[...].astype(BF16)

        @pl.when((t == nt - 1) & (j == N_CHIPS - 1))
        def _():
            pltpu.sync_copy(dh_acc, dh_hbm)

    return _pcall(
        body, name=name, grid=(N_CHIPS, nt),
        in_specs=[
            pl.BlockSpec((tt, D), lambda j, t: (t, 0)),
            pl.BlockSpec((tt, D), lambda j, t: (t, 0)),
            pl.BlockSpec((1, tt, FF_CHUNK), lambda j, t: (j, t, 0)),
            pl.BlockSpec((1, tt, FF_CHUNK), lambda j, t: (j, t, 0)),
            pl.BlockSpec((1, FF_CHUNK, D), lambda j, t: (j, 0, 0)),
            pl.BlockSpec((1, FF_CHUNK, D), lambda j, t: (j, 0, 0)),
            pl.BlockSpec((1, FF_CHUNK, D), lambda j, t: (j, 0, 0)),
        ],
        out_specs=[
            pl.BlockSpec(memory_space=pl.ANY),
            pl.BlockSpec((1, FF_CHUNK, D), lambda j, t: (j, 0, 0)),
            pl.BlockSpec((1, FF_CHUNK, D), lambda j, t: (j, 0, 0)),
            pl.BlockSpec((1, FF_CHUNK, D), lambda j, t: (j, 0, 0)),
        ],
        out_shape=[
            jax.ShapeDtypeStruct((T, D), F32),
            jax.ShapeDtypeStruct((N_CHIPS, FF_CHUNK, D), BF16),
            jax.ShapeDtypeStruct((N_CHIPS, FF_CHUNK, D), BF16),
            jax.ShapeDtypeStruct((N_CHIPS, FF_CHUNK, D), BF16),
        ],
        scratch_shapes=[
            pltpu.VMEM((T, D), F32),
            pltpu.VMEM((FF_CHUNK, D), F32),
            pltpu.VMEM((FF_CHUNK, D), F32),
            pltpu.VMEM((FF_CHUNK, D), F32),
        ],
        compiler_params=_params(("arbitrary", "arbitrary")),
    )(dxo, h, a, b, wg, wu, wd)


def _rms_fwd(x, g, name):
    T, D = x.shape
    tt = min(ROW_TILE, T)

    def body(x_ref, g_ref, h_ref):
        xv = x_ref[...]
        rstd = lax.rsqrt(jnp.mean(xv * xv, axis=-1, keepdims=True) + EPS)
        h_ref[...] = (xv * rstd * g_ref[...]).astype(BF16)

    return _pcall(
        body, name=name, grid=(T // tt,),
        in_specs=[pl.BlockSpec((tt, D), lambda t: (t, 0)), pl.BlockSpec((1, D), lambda t: (0, 0))],
        out_specs=pl.BlockSpec((tt, D), lambda t: (t, 0)),
        out_shape=jax.ShapeDtypeStruct((T, D), BF16),
        compiler_params=_params(("arbitrary",)),
    )(x, g)


def _rms_bwd(x, g, dh, dres, name):
    T, D = x.shape
    tt = min(ROW_TILE, T)
    has_res = dres is not None

    def body(*refs):
        if has_res:
            x_ref, g_ref, dh_ref, dres_ref, dx_ref, dg_ref = refs
        else:
            x_ref, g_ref, dh_ref, dx_ref, dg_ref = refs
        t = pl.program_id(0)
        xv = x_ref[...]
        rstd = lax.rsqrt(jnp.mean(xv * xv, axis=-1, keepdims=True) + EPS)
        xhat = xv * rstd
        dhv = dh_ref[...]
        gy = dhv * g_ref[...]
        dx = rstd * (gy - xhat * jnp.mean(gy * xhat, axis=-1, keepdims=True))
        if has_res:
            dx = dx + dres_ref[...]
        dx_ref[...] = dx
        part = jnp.sum(dhv * xhat, axis=0, keepdims=True)

        @pl.when(t == 0)
        def _():
            dg_ref[...] = part

        @pl.when(t > 0)
        def _():
            dg_ref[...] += part

    tile = pl.BlockSpec((tt, D), lambda t: (t, 0))
    vec = pl.BlockSpec((1, D), lambda t: (0, 0))
    args = [x, g, dh] + ([dres] if has_res else [])
    return _pcall(
        body, name=name, grid=(T // tt,),
        in_specs=[tile, vec, tile] + ([tile] if has_res else []),
        out_specs=[tile, vec],
        out_shape=[jax.ShapeDtypeStruct((T, D), F32), jax.ShapeDtypeStruct((1, D), F32)],
        compiler_params=_params(("arbitrary",)),
    )(*args)


def _mm_nn(a_list, b, res, name):
    T = a_list[0].shape[0]
    K, N = b.shape
    tt = min(ROW_TILE, T)
    ks = [a.shape[1] for a in a_list]
    na = len(a_list)
    has_res = res is not None

    def body(*refs):
        a_refs = refs[:na]
        b_ref = refs[na]
        o_ref = refs[-1]
        acc = res_v = None
        off = 0
        for a_ref, k in zip(a_refs, ks):
            part = _dot(a_ref[...].astype(BF16), b_ref[off:off + k, :])
            acc = part if acc is None else acc + part
            off += k
        if has_res:
            acc = refs[na + 1][...] + acc
        o_ref[...] = acc

    in_specs = [pl.BlockSpec((tt, k), lambda t: (t, 0)) for k in ks] + [pl.BlockSpec((K, N), lambda t: (0, 0))]
    args = list(a_list) + [b]
    if has_res:
        in_specs.append(pl.BlockSpec((tt, N), lambda t: (t, 0)))
        args.append(res)
    return _pcall(
        body, name=name, grid=(T // tt,), in_specs=in_specs,
        out_specs=pl.BlockSpec((tt, N), lambda t: (t, 0)),
        out_shape=jax.ShapeDtypeStruct((T, N), F32),
        compiler_params=_params(("arbitrary",)),
    )(*args)


def _mm_nt(a_list, b, name):
    T = a_list[0].shape[0]
    K, N = b.shape
    tt = min(ROW_TILE, T)
    ns = [a.shape[1] for a in a_list]
    na = len(a_list)

    def body(*refs):
        b_ref = refs[na]
        o_ref = refs[-1]
        acc = None
        off = 0
        for a_ref, n in zip(refs[:na], ns):
            part = _dot_nt(a_ref[...].astype(BF16), b_ref[:, off:off + n])
            acc = part if acc is None else acc + part
            off += n
        o_ref[...] = acc

    return _pcall(
        body, name=name, grid=(T // tt,),
        in_specs=[pl.BlockSpec((tt, n), lambda t: (t, 0)) for n in ns] + [pl.BlockSpec((K, N), lambda t: (0, 0))],
        out_specs=pl.BlockSpec((tt, K), lambda t: (t, 0)),
        out_shape=jax.ShapeDtypeStruct((T, K), F32),
        compiler_params=_params(("arbitrary",)),
    )(*a_list, b)


def _mm_tn(a_list, b_list, col_chunks, name):
    T = a_list[0].shape[0]
    tt = min(ROW_TILE, T)
    nt = T // tt
    ms = [a.shape[1] for a in a_list]
    ns = [b.shape[1] for b in b_list]
    M, N = sum(ms), sum(ns)
    na, nb = len(a_list), len(b_list)
    cw = N // col_chunks

    def body(*refs):
        a_refs, b_refs = refs[:na], refs[na:na + nb]
        o_ref, acc = refs[na + nb], refs[na + nb + 1]
        t = pl.program_id(0)

        @pl.when(t == 0)
        def _():
            acc[...] = jnp.zeros_like(acc)

        ro = 0
        for a_ref, m in zip(a_refs, ms):
            av = a_ref[...].astype(BF16)
            co = 0
            for b_ref, n in zip(b_refs, ns):
                acc[ro:ro + m, co:co + n] += _dot_tn(av, b_ref[...].astype(BF16))
                co += n
            ro += m

        @pl.when(t == nt - 1)
        def _():
            if col_chunks == 1:
                o_ref[...] = acc[...].astype(BF16)
            else:
                for q in range(col_chunks):
                    o_ref[q] = acc[:, q * cw:(q + 1) * cw].astype(BF16)

    out_shape = (M, N) if col_chunks == 1 else (col_chunks, M, cw)
    return _pcall(
        body, name=name, grid=(nt,),
        in_specs=[pl.BlockSpec((tt, m), lambda t: (t, 0)) for m in ms]
        + [pl.BlockSpec((tt, n), lambda t: (t, 0)) for n in ns],
        out_specs=_full(out_shape),
        out_shape=jax.ShapeDtypeStruct(out_shape, BF16),
        scratch_shapes=[pltpu.VMEM((M, N), F32)],
        compiler_params=_params(("arbitrary",)),
    )(*a_list, *b_list)


def _head_masks():
    lane = lax.broadcasted_iota(jnp.int32, (1, LANES), 1)
    l64 = lane & (HEAD_DIM - 1)
    return lane < HEAD_DIM, l64 < ROPE_DIM // 2, l64 < ROPE_DIM


def _head_mean(v, lo):
    s_lo = jnp.sum(jnp.where(lo, v, 0.0), axis=-1, keepdims=True)
    s_hi = jnp.sum(jnp.where(lo, 0.0, v), axis=-1, keepdims=True)
    return jnp.where(lo, s_lo, s_hi) * (1.0 / HEAD_DIM)


def _rope_swap(v, first, rot):
    up = pltpu.roll(v, LANES - ROPE_DIM // 2, 1)
    down = pltpu.roll(v, ROPE_DIM // 2, 1)
    return jnp.where(first, up, jnp.where(rot, down, 0.0))


def _head_norm(x, g, lo):
    rstd = lax.rsqrt(_head_mean(x * x, lo) + EPS)
    return x * rstd * g


def _head_norm_bwd(x, g, dy, lo):
    rstd = lax.rsqrt(_head_mean(x * x, lo) + EPS)
    xhat = x * rstd
    gy = dy * g
    dx = rstd * (gy - xhat * _head_mean(gy * xhat, lo))
    return dx, dy * xhat


def _rope(xn, cos, sin, first, rot):
    return xn * cos + _rope_swap(xn, first, rot) * sin


def _rope_bwd(dy, cos, sin, first, rot):
    return dy * cos + _rope_swap(dy * sin, first, rot)


def _fold_heads(v):
    return v + pltpu.roll(v, HEAD_DIM, 1)


ATT_ROWS = 256


def _attn_prepare(q_ref, k_ref, v_ref, cos_ref, sin_ref, gq_ref, gk_ref, qs, ks, vs):
    T = q_ref.shape[0]
    lo, first, rot = _head_masks()
    ks[0:BLOCK, :] = jnp.zeros((BLOCK, KV_COLS), BF16)
    vs[0:BLOCK, :] = jnp.zeros((BLOCK, KV_COLS), BF16)

    def step(i, _):
        r0 = pl.multiple_of(i * ATT_ROWS, ATT_ROWS)
        rows = pl.ds(r0, ATT_ROWS)
        prow = pl.ds(r0 + BLOCK, ATT_ROWS)
        cos, sin = cos_ref[rows, :], sin_ref[rows, :]
        for p in range(Q_COLS // LANES):
            cols = slice(p * LANES, (p + 1) * LANES)
            xr = _rope(_head_norm(q_ref[rows, cols], gq_ref[...], lo), cos, sin, first, rot)
            qs[rows, cols] = (xr * (HEAD_DIM ** -0.5)).astype(BF16)
        kr = _rope(_head_norm(k_ref[rows, :], gk_ref[...], lo), cos, sin, first, rot)
        ks[prow, :] = kr.astype(BF16)
        vs[prow, :] = v_ref[rows, :].astype(BF16)
        return 0

    lax.fori_loop(0, T // ATT_ROWS, step, 0)


def _attn_scores(qh, kw, blk, sink):
    s = _dot_nt(qh, kw)
    qi = lax.broadcasted_iota(jnp.int32, (BLOCK, 2 * BLOCK), 0) + BLOCK
    ki = lax.broadcasted_iota(jnp.int32, (BLOCK, 2 * BLOCK), 1)
    rel = qi - ki
    valid = (rel >= 0) & (rel < BLOCK) & ((blk > 0) | (ki >= BLOCK))
    s = jnp.where(valid, s, NEG)
    m = jnp.maximum(jnp.max(s, axis=-1, keepdims=True), sink)
    p = jnp.exp(s - m)
    e_sink = jnp.exp(sink - m)
    inv = 1.0 / (jnp.sum(p, axis=-1, keepdims=True) + e_sink)
    return p * inv, e_sink * inv


def _attn_fwd(proj, cos, sin, gq2, gk2, sinks, name):
    T = proj.shape[0]
    nb = T // BLOCK

    def body(q_ref, k_ref, v_ref, cos_ref, sin_ref, gq_ref, gk_ref, sink_ref, y_ref, qs, ks, vs):
        _attn_prepare(q_ref, k_ref, v_ref, cos_ref, sin_ref, gq_ref, gk_ref, qs, ks, vs)

        def blk_step(blk, _):
            r0 = pl.multiple_of(blk * BLOCK, BLOCK)
            for g in range(N_KV_HEADS):
                gc = slice(g * HEAD_DIM, (g + 1) * HEAD_DIM)
                kw = ks[pl.ds(r0, 2 * BLOCK), gc]
                vw = vs[pl.ds(r0, 2 * BLOCK), gc]
                for r in range(Q_PER_KV):
                    h = g * Q_PER_KV + r
                    hc = slice(h * HEAD_DIM, (h + 1) * HEAD_DIM)
                    w, _ws = _attn_scores(qs[pl.ds(r0, BLOCK), hc], kw, blk, sink_ref[0, h])
                    y_ref[pl.ds(r0, BLOCK), hc] = _dot(w.astype(BF16), vw).astype(BF16)
            return 0

        lax.fori_loop(0, nb, blk_step, 0)

    return _pcall(
        body, name=name, grid=(1,),
        in_specs=[
            pl.BlockSpec((T, Q_COLS), lambda i: (0, 0)),
            pl.BlockSpec((T, KV_COLS), lambda i: (0, Q_COLS // KV_COLS)),
            pl.BlockSpec((T, KV_COLS), lambda i: (0, Q_COLS // KV_COLS + 1)),
            _full((T, LANES)), _full((T, LANES)), _full((1, LANES)), _full((1, LANES)),
            pl.BlockSpec(memory_space=pltpu.SMEM),
        ],
        out_specs=_full((T, Q_COLS)),
        out_shape=jax.ShapeDtypeStruct((T, Q_COLS), BF16),
        scratch_shapes=[
            pltpu.VMEM((T, Q_COLS), BF16),
            pltpu.VMEM((T + BLOCK, KV_COLS), BF16),
            pltpu.VMEM((T + BLOCK, KV_COLS), BF16),
        ],
        compiler_params=_params(("arbitrary",)),
    )(proj, proj, proj, cos, sin, gq2, gk2, sinks)


def _attn_bwd(proj, cos, sin, gq2, gk2, sinks, dyc, name):
    T = proj.shape[0]
    nb = T // BLOCK

    def body(q_ref, k_ref, v_ref, cos_ref, sin_ref, gq_ref, gk_ref, sink_ref, dy_ref,
             dq_ref, dk_ref, dv_ref, dgq_ref, dgk_ref, dsink_ref, qs, ks, vs, dqs, dks, dvs):
        _attn_prepare(q_ref, k_ref, v_ref, cos_ref, sin_ref, gq_ref, gk_ref, qs, ks, vs)
        dks[...] = jnp.zeros_like(dks)
        dvs[...] = jnp.zeros_like(dvs)
        lane = lax.broadcasted_iota(jnp.int32, (1, LANES), 1)

        def blk_step(blk, dsink):
            r0 = pl.multiple_of(blk * BLOCK, BLOCK)
            win = pl.ds(r0, 2 * BLOCK)
            for g in range(N_KV_HEADS):
                gc = slice(g * HEAD_DIM, (g + 1) * HEAD_DIM)
                kw = ks[win, gc]
                vw = vs[win, gc]
                for r in range(Q_PER_KV):
                    h = g * Q_PER_KV + r
                    hc = slice(h * HEAD_DIM, (h + 1) * HEAD_DIM)
                    qh = qs[pl.ds(r0, BLOCK), hc]
                    w, w_sink = _attn_scores(qh, kw, blk, sink_ref[0, h])
                    do = dy_ref[pl.ds(r0, BLOCK), hc].astype(BF16)
                    dvs[win, gc] += _dot_tn(w.astype(BF16), do)
                    dw = _dot_nt(do, vw)
                    delta = jnp.sum(w * dw, axis=-1, keepdims=True)
                    ds = (w * (dw - delta)).astype(BF16)
                    dsink = dsink + jnp.where(lane == h, -jnp.sum(w_sink * delta, axis=0, keepdims=True), 0.0)
                    dqs[pl.ds(r0, BLOCK), hc] = _dot(ds, kw)
                    dks[win, gc] += _dot_tn(ds, qh)
            return dsink

        dsink_ref[...] = lax.fori_loop(0, nb, blk_step, jnp.zeros((1, LANES), F32))

        lo, first, rot = _head_masks()

        def step(i, carry):
            dgq, dgk = carry
            r0 = pl.multiple_of(i * ATT_ROWS, ATT_ROWS)
            rows = pl.ds(r0, ATT_ROWS)
            prow = pl.ds(r0 + BLOCK, ATT_ROWS)
            cos, sin = cos_ref[rows, :], sin_ref[rows, :]
            for p in range(Q_COLS // LANES):
                cols = slice(p * LANES, (p + 1) * LANES)
                dxn = _rope_bwd(dqs[rows, cols] * (HEAD_DIM ** -0.5), cos, sin, first, rot)
                dx, dgp = _head_norm_bwd(q_ref[rows, cols], gq_ref[...], dxn, lo)
                dq_ref[rows, cols] = dx
                dgq = dgq + jnp.sum(dgp, axis=0, keepdims=True)
            dkn = _rope_bwd(dks[prow, :], cos, sin, first, rot)
            dx, dgp = _head_norm_bwd(k_ref[rows, :], gk_ref[...], dkn, lo)
            dk_ref[rows, :] = dx
            dgk = dgk + jnp.sum(dgp, axis=0, keepdims=True)
            dv_ref[rows, :] = dvs[prow, :]
            return dgq, dgk

        zero = jnp.zeros((1, LANES), F32)
        dgq, dgk = lax.fori_loop(0, T // ATT_ROWS, step, (zero, zero))
        dgq_ref[...] = _fold_heads(dgq)
        dgk_ref[...] = _fold_heads(dgk)

    vec = jax.ShapeDtypeStruct((1, LANES), F32)
    return _pcall(
        body, name=name, grid=(1,),
        in_specs=[
            pl.BlockSpec((T, Q_COLS), lambda i: (0, 0)),
            pl.BlockSpec((T, KV_COLS), lambda i: (0, Q_COLS // KV_COLS)),
            pl.BlockSpec((T, KV_COLS), lambda i: (0, Q_COLS // KV_COLS + 1)),
            _full((T, LANES)), _full((T, LANES)), _full((1, LANES)), _full((1, LANES)),
            pl.BlockSpec(memory_space=pltpu.SMEM),
            pl.BlockSpec((T, Q_COLS), lambda i: (0, 0)),
        ],
        out_specs=[_full((T, Q_COLS)), _full((T, KV_COLS)), _full((T, KV_COLS)),
                   _full((1, LANES)), _full((1, LANES)), _full((1, LANES))],
        out_shape=[jax.ShapeDtypeStruct((T, Q_COLS), F32), jax.ShapeDtypeStruct((T, KV_COLS), F32),
                   jax.ShapeDtypeStruct((T, KV_COLS), F32), vec, vec, vec],
        scratch_shapes=[
            pltpu.VMEM((T, Q_COLS), BF16),
            pltpu.VMEM((T + BLOCK, KV_COLS), BF16),
            pltpu.VMEM((T + BLOCK, KV_COLS), BF16),
            pltpu.VMEM((T, Q_COLS), F32),
            pltpu.VMEM((T + BLOCK, KV_COLS), F32),
            pltpu.VMEM((T + BLOCK, KV_COLS), F32),
        ],
        compiler_params=_params(("arbitrary",)),
    )(proj, proj, proj, cos, sin, gq2, gk2, sinks, dyc)


CONV_PAD = 32
CONV_ROWS = 256


def _conv_taps(src, w_ref, r0, first_off, step_sign):
    acc = None
    for i in range(CONV_WIDTH):
        term = w_ref[i:i + 1, :] * src[r0 + first_off + step_sign * i:r0 + first_off + step_sign * i + CONV_ROWS, :]
        acc = term if acc is None else acc + term
    return acc


def _conv_fwd(proj, w_dw, b_dw, g_ln, b_ln, name):
    T = proj.shape[0]
    a_blk = (Q_COLS + 2 * KV_COLS) // CONV_CH

    def body(a_ref, gate_ref, w_ref, bdw_ref, g_ref, b_ref, y_ref, c_ref, pad):
        pad[0:CONV_PAD, :] = jnp.zeros((CONV_PAD, CONV_CH), F32)
        pad[CONV_PAD:, :] = a_ref[...] * _sigmoid(gate_ref[...])
        for n in range(T // CONV_ROWS):
            r0 = n * CONV_ROWS
            c = _conv_taps(pad, w_ref, r0, CONV_PAD - (CONV_WIDTH - 1), 1) + bdw_ref[...]
            c_ref[r0:r0 + CONV_ROWS, :] = c
            mu = jnp.mean(c, axis=-1, keepdims=True)
            cc = c - mu
            rstd = lax.rsqrt(jnp.mean(cc * cc, axis=-1, keepdims=True) + EPS)
            z = cc * rstd * g_ref[...] + b_ref[...]
            y_ref[r0:r0 + CONV_ROWS, :] = (z * _sigmoid(z)).astype(BF16)

    vec = _full((1, CONV_CH))
    return _pcall(
        body, name=name, grid=(1,),
        in_specs=[
            pl.BlockSpec((T, CONV_CH), lambda i: (0, a_blk)),
            pl.BlockSpec((T, CONV_CH), lambda i: (0, a_blk + 1)),
            _full((CONV_WIDTH, CONV_CH)), vec, vec, vec,
        ],
        out_specs=[_full((T, CONV_CH)), _full((T, CONV_CH))],
        out_shape=[jax.ShapeDtypeStruct((T, CONV_CH), BF16), jax.ShapeDtypeStruct((T, CONV_CH), F32)],
        scratch_shapes=[pltpu.VMEM((T + CONV_PAD, CONV_CH), F32)],
        compiler_params=_params(("arbitrary",)),
    )(proj, proj, w_dw, b_dw, g_ln, b_ln)


def _conv_bwd(proj, c, w_dw, g_ln, b_ln, dyc, name):
    T = proj.shape[0]
    a_blk = (Q_COLS + 2 * KV_COLS) // CONV_CH
    y_blk = Q_COLS // CONV_CH

    def body(a_ref, gate_ref, c_ref, w_ref, g_ref, b_ref, dy_ref,
             da_ref, dgate_ref, dw_ref, dbdw_ref, dg_ref, db_ref, pad, dcp):
        pad[0:CONV_PAD, :] = jnp.zeros((CONV_PAD, CONV_CH), F32)
        sg = _sigmoid(gate_ref[...])
        pad[CONV_PAD:, :] = a_ref[...] * sg
        dcp[T:, :] = jnp.zeros((CONV_PAD, CONV_CH), F32)
        dg = db = dbdw = jnp.zeros((1, CONV_CH), F32)
        for n in range(T // CONV_ROWS):
            rows = slice(n * CONV_ROWS, (n + 1) * CONV_ROWS)
            cv = c_ref[rows, :]
            mu = jnp.mean(cv, axis=-1, keepdims=True)
            cc = cv - mu
            rstd = lax.rsqrt(jnp.mean(cc * cc, axis=-1, keepdims=True) + EPS)
            chat = cc * rstd
            z = chat * g_ref[...] + b_ref[...]
            sz = _sigmoid(z)
            dz = dy_ref[rows, :] * (sz * (1.0 + z * (1.0 - sz)))
            dg = dg + jnp.sum(dz * chat, axis=0, keepdims=True)
            db = db + jnp.sum(dz, axis=0, keepdims=True)
            dch = dz * g_ref[...]
            dc = rstd * (dch - jnp.mean(dch, axis=-1, keepdims=True)
                         - chat * jnp.mean(dch * chat, axis=-1, keepdims=True))
            dbdw = dbdw + jnp.sum(dc, axis=0, keepdims=True)
            dcp[rows, :] = dc
        dg_ref[...] = dg
        db_ref[...] = db
        dbdw_ref[...] = dbdw
        dw_ref[CONV_WIDTH:, :] = jnp.zeros((CONV_PAD - CONV_WIDTH, CONV_CH), F32)
        for i in range(CONV_WIDTH):
            off = CONV_PAD - (CONV_WIDTH - 1) + i
            acc = jnp.zeros((1, CONV_CH), F32)
            for n in range(T // CONV_ROWS):
                r0 = n * CONV_ROWS
                acc = acc + jnp.sum(dcp[r0:r0 + CONV_ROWS, :] * pad[r0 + off:r0 + off + CONV_ROWS, :],
                                    axis=0, keepdims=True)
            dw_ref[i:i + 1, :] = acc
        for n in range(T // CONV_ROWS):
            r0 = n * CONV_ROWS
            rows = slice(r0, r0 + CONV_ROWS)
            dhg = _conv_taps(dcp, w_ref, r0, CONV_WIDTH - 1, -1)
            sgv = sg[rows, :]
            da_ref[rows, :] = dhg * sgv
            dgate_ref[rows, :] = dhg * a_ref[rows, :] * sgv * (1.0 - sgv)

    vec = _full((1, CONV_CH))
    vshape = jax.ShapeDtypeStruct((1, CONV_CH), F32)
    return _pcall(
        body, name=name, grid=(1,),
        in_specs=[
            pl.BlockSpec((T, CONV_CH), lambda i: (0, a_blk)),
            pl.BlockSpec((T, CONV_CH), lambda i: (0, a_blk + 1)),
            _full((T, CONV_CH)), _full((CONV_WIDTH, CONV_CH)), vec, vec,
            pl.BlockSpec((T, CONV_CH), lambda i: (0, y_blk)),
        ],
        out_specs=[_full((T, CONV_CH)), _full((T, CONV_CH)), _full((CONV_PAD, CONV_CH)), vec, vec, vec],
        out_shape=[jax.ShapeDtypeStruct((T, CONV_CH), F32), jax.ShapeDtypeStruct((T, CONV_CH), F32),
                   jax.ShapeDtypeStruct((CONV_PAD, CONV_CH), F32), vshape, vshape, vshape],
        scratch_shapes=[pltpu.VMEM((T + CONV_PAD, CONV_CH), F32), pltpu.VMEM((T + CONV_PAD, CONV_CH), F32)],
        compiler_params=_params(("arbitrary",)),
    )(proj, proj, c, w_dw, g_ln, b_ln, dyc)


def _mem_kv(mkv_ref, gk_ref, lo, kn_s, vv_s):
    for p in range(MQ_COLS // LANES):
        cols = slice(p * LANES, (p + 1) * LANES)
        kn_s[:, cols] = _head_norm(mkv_ref[:, cols], gk_ref[...], lo).astype(BF16)
    vv_s[...] = mkv_ref[:, MQ_COLS:].astype(BF16)


def _mem_softmax(qh, kh):
    s = _dot_nt(qh, kh)
    m = jnp.max(s, axis=-1, keepdims=True)
    p = jnp.exp(s - m)
    return p / jnp.sum(p, axis=-1, keepdims=True)


def _mem_fwd(proj, mkv, gq2, gk2, name):
    T = proj.shape[0]
    tt = ROW_TILE
    q_blk = (IN_COLS - MQ_COLS) // MQ_COLS

    def body(q_ref, mkv_ref, gq_ref, gk_ref, y_ref, kn_s, vv_s, qn_s):
        lo, _, _ = _head_masks()
        _mem_kv(mkv_ref, gk_ref, lo, kn_s, vv_s)
        for p in range(MQ_COLS // LANES):
            cols = slice(p * LANES, (p + 1) * LANES)
            qn_s[:, cols] = (_head_norm(q_ref[:, cols], gq_ref[...], lo) * (HEAD_DIM ** -0.5)).astype(BF16)
        for h in range(N_MEM_HEADS):
            hc = slice(h * HEAD_DIM, (h + 1) * HEAD_DIM)
            w = _mem_softmax(qn_s[:, hc], kn_s[:, hc])
            y_ref[:, hc] = _dot(w.astype(BF16), vv_s[:, hc]).astype(BF16)

    return _pcall(
        body, name=name, grid=(T // tt,),
        in_specs=[
            pl.BlockSpec((tt, MQ_COLS), lambda t: (t, q_blk)),
            pl.BlockSpec((MEM_LEN, 2 * MQ_COLS), lambda t: (0, 0)),
            pl.BlockSpec((1, LANES), lambda t: (0, 0)), pl.BlockSpec((1, LANES), lambda t: (0, 0)),
        ],
        out_specs=pl.BlockSpec((tt, MQ_COLS), lambda t: (t, 0)),
        out_shape=jax.ShapeDtypeStruct((T, MQ_COLS), BF16),
        scratch_shapes=[pltpu.VMEM((MEM_LEN, MQ_COLS), BF16), pltpu.VMEM((MEM_LEN, MQ_COLS), BF16),
                        pltpu.VMEM((tt, MQ_COLS), BF16)],
        compiler_params=_params(("arbitrary",)),
    )(proj, mkv, gq2, gk2)


def _mem_bwd(proj, mkv, gq2, gk2, dyc, name):
    T = proj.shape[0]
    tt = ROW_TILE
    nt = T // tt
    q_blk = (IN_COLS - MQ_COLS) // MQ_COLS
    y_blk = (Q_COLS + CONV_CH) // MQ_COLS

    def body(q_ref, mkv_ref, gq_ref, gk_ref, dy_ref, dq_ref, dmkv_ref, dgq_ref, dgk_ref,
             kn_s, vv_s, qn_s, dqn_s, dkn_acc):
        t = pl.program_id(0)
        lo, _, _ = _head_masks()
        _mem_kv(mkv_ref, gk_ref, lo, kn_s, vv_s)

        @pl.when(t == 0)
        def _():
            dkn_acc[...] = jnp.zeros_like(dkn_acc)
            dmkv_ref[...] = jnp.zeros_like(dmkv_ref)
            dgq_ref[...] = jnp.zeros_like(dgq_ref)

        for p in range(MQ_COLS // LANES):
            cols = slice(p * LANES, (p + 1) * LANES)
            qn_s[:, cols] = (_head_norm(q_ref[:, cols], gq_ref[...], lo) * (HEAD_DIM ** -0.5)).astype(BF16)
        for h in range(N_MEM_HEADS):
            hc = slice(h * HEAD_DIM, (h + 1) * HEAD_DIM)
            vc = slice(MQ_COLS + h * HEAD_DIM, MQ_COLS + (h + 1) * HEAD_DIM)
            qh = qn_s[:, hc]
            w = _mem_softmax(qh, kn_s[:, hc])
            do = dy_ref[:, hc].astype(BF16)
            dmkv_ref[:, vc] += _dot_tn(w.astype(BF16), do)
            dw = _dot_nt(do, vv_s[:, hc])
            ds = (w * (dw - jnp.sum(w * dw, axis=-1, keepdims=True))).astype(BF16)
            dqn_s[:, hc] = _dot(ds, kn_s[:, hc])
            dkn_acc[:, hc] += _dot_tn(ds, qh)
        dgq = jnp.zeros((1, LANES), F32)
        for p in range(MQ_COLS // LANES):
            cols = slice(p * LANES, (p + 1) * LANES)
            dx, dgp = _head_norm_bwd(q_ref[:, cols], gq_ref[...], dqn_s[:, cols] * (HEAD_DIM ** -0.5), lo)
            dq_ref[:, cols] = dx
            dgq = dgq + jnp.sum(dgp, axis=0, keepdims=True)
        dgq_ref[...] += dgq

        @pl.when(t == nt - 1)
        def _():
            dgk = jnp.zeros((1, LANES), F32)
            for p in range(MQ_COLS // LANES):
                cols = slice(p * LANES, (p + 1) * LANES)
                dx, dgp = _head_norm_bwd(mkv_ref[:, cols], gk_ref[...], dkn_acc[:, cols], lo)
                dmkv_ref[:, cols] = dx
                dgk = dgk + jnp.sum(dgp, axis=0, keepdims=True)
            dgk_ref[...] = _fold_heads(dgk)
            dgq_ref[...] = _fold_heads(dgq_ref[...])

    vec = pl.BlockSpec((1, LANES), lambda t: (0, 0))
    vshape = jax.ShapeDtypeStruct((1, LANES), F32)
    return _pcall(
        body, name=name, grid=(nt,),
        in_specs=[
            pl.BlockSpec((tt, MQ_COLS), lambda t: (t, q_blk)),
            pl.BlockSpec((MEM_LEN, 2 * MQ_COLS), lambda t: (0, 0)),
            vec, vec,
            pl.BlockSpec((tt, MQ_COLS), lambda t: (t, y_blk)),
        ],
        out_specs=[pl.BlockSpec((tt, MQ_COLS), lambda t: (t, 0)),
                   pl.BlockSpec((MEM_LEN, 2 * MQ_COLS), lambda t: (0, 0)), vec, vec],
        out_shape=[jax.ShapeDtypeStruct((T, MQ_COLS), F32), jax.ShapeDtypeStruct((MEM_LEN, 2 * MQ_COLS), F32),
                   vshape, vshape],
        scratch_shapes=[pltpu.VMEM((MEM_LEN, MQ_COLS), BF16), pltpu.VMEM((MEM_LEN, MQ_COLS), BF16),
                        pltpu.VMEM((tt, MQ_COLS), BF16), pltpu.VMEM((tt, MQ_COLS), F32),
                        pltpu.VMEM((MEM_LEN, MQ_COLS), F32)],
        compiler_params=_params(("arbitrary",)),
    )(proj, mkv, gq2, gk2, dyc)


def _loss_head(y, target, name):
    T, D = y.shape
    tt = ROW_TILE

    def body(y_ref, t_ref, dy_ref, loss_ref):
        t = pl.program_id(0)
        err = y_ref[...] - t_ref[...]
        dy_ref[...] = err * (1.0 / D)
        part = 0.5 * jnp.sum(jnp.mean(err * err, axis=-1, keepdims=True), axis=0, keepdims=True)

        @pl.when(t == 0)
        def _():
            loss_ref[...] = jnp.zeros_like(loss_ref)

        loss_ref[...] += jnp.broadcast_to(part, loss_ref.shape)

    tile = pl.BlockSpec((tt, D), lambda t: (t, 0))
    return _pcall(
        body, name=name, grid=(T // tt,),
        in_specs=[tile, tile],
        out_specs=[tile, pl.BlockSpec((1, LANES), lambda t: (0, 0))],
        out_shape=[jax.ShapeDtypeStruct((T, D), F32), jax.ShapeDtypeStruct((1, LANES), F32)],
        compiler_params=_params(("arbitrary",)),
    )(y, target)


def _adamw(w, g, m, v, name):
    R, C = w.shape
    tr = next((r for r in (512, 352, 256, 128) if R % r == 0), R)

    def body(w_ref, g_ref, m_ref, v_ref, go_ref, d_ref, nm_ref, nv_ref):
        gv = g_ref[...]
        go_ref[...] = gv
        nm = ADAM_B1 * m_ref[...] + (1.0 - ADAM_B1) * gv
        nv = ADAM_B2 * v_ref[...] + (1.0 - ADAM_B2) * (gv * gv)
        m_hat = nm / (1.0 - ADAM_B1 ** ADAM_STEP)
        v_hat = nv / (1.0 - ADAM_B2 ** ADAM_STEP)
        d_ref[...] = -ADAM_LR * (m_hat / (jnp.sqrt(v_hat) + ADAM_EPS) + ADAM_WD * w_ref[...])
        nm_ref[...] = nm
        nv_ref[...] = nv

    tile = pl.BlockSpec((tr, C), lambda i: (i, 0))
    shape = jax.ShapeDtypeStruct((R, C), F32)
    return _pcall(
        body, name=name, grid=(R // tr,),
        in_specs=[tile] * 4, out_specs=[tile] * 4, out_shape=[shape] * 4,
        compiler_params=_params(("arbitrary",)),
    )(w, g, m, v)


def _mesh_pos():
    return lax.axis_index("x"), lax.axis_index("y"), lax.axis_index("c")


def _other_chips(x, y):
    return [(1 - x, y), (x, 1 - y), (1 - x, 1 - y)]


def _quarter(ref, layout, q, rows, cols):
    if layout == "cols":
        return ref.at[rows, pl.ds(pl.multiple_of(q * cols, LANES), cols)]
    return ref.at[q, rows, :]


def _gather_weights(shards, dtypes, layouts, name):
    n = len(shards)
    all_rows = slice(None)

    def body(*refs):
        ins, outs = refs[:n], refs[n:2 * n]
        st32, st16 = refs[2 * n:3 * n], refs[3 * n:4 * n]
        in_sems, own_sems, send_sems, recv_sems, fwd_send_sems, fwd_recv_sems = refs[4 * n:]
        x, y, c = _mesh_pos()
        chip = 2 * x + y
        sibling = (x, y, 1 - c)
        chips = _other_chips(x, y)

        def half(i, which):
            if layouts[i] == "whole":
                return all_rows
            hr = shards[i].shape[0] // 2
            return pl.ds(pl.multiple_of(which * hr, 16), hr)

        def place(i, q, rows):
            return _quarter(outs[i], layouts[i], q, rows, shards[i].shape[1])

        def ici(i, k, origin_chip, src):
            px, py = chips[k]
            return pltpu.make_async_remote_copy(
                src_ref=src, dst_ref=place(i, origin_chip, half(i, c)), send_sem=send_sems.at[i, k],
                recv_sem=recv_sems.at[i, k], device_id=(px, py, c), device_id_type=MESH)

        def forward(i, k, rows):
            px, py = chips[k]
            there = place(i, 2 * px + py, rows)
            return pltpu.make_async_remote_copy(
                src_ref=there, dst_ref=there, send_sem=fwd_send_sems.at[i, k],
                recv_sem=fwd_recv_sems.at[i, k], device_id=sibling, device_id_type=MESH)

        loads = [pltpu.make_async_copy(ins[i], st32[i], in_sems.at[i]) for i in range(n)]
        for cp in loads:
            cp.start()
        owns, sent = [], []
        for i in range(n):
            loads[i].wait()
            st16[i][...] = st32[i][...].astype(dtypes[i])
            own = pltpu.make_async_copy(st16[i], place(i, chip, all_rows), own_sems.at[i])
            own.start()
            owns.append(own)
            for k in range(3):
                cp = ici(i, k, chip, st16[i].at[half(i, c)])
                cp.start()
                sent.append(cp)
        split = [i for i in range(n) if layouts[i] != "whole"]
        for i in range(n):
            for k, (px, py) in enumerate(chips):
                ici(i, k, 2 * px + py, st16[i].at[half(i, c)]).wait_recv()
                if i in split:
                    cp = forward(i, k, half(i, c))
                    cp.start()
                    sent.append(cp)
        for i in split:
            for k in range(3):
                forward(i, k, half(i, 1 - c)).wait_recv()
        for cp in sent:
            cp.wait_send()
        for cp in owns:
            cp.wait()

    def out_shape(s, d, layout):
        R, C = s.shape
        return jax.ShapeDtypeStruct((R, N_CHIPS * C) if layout == "cols" else (N_CHIPS, R, C), d)

    hbm = pl.BlockSpec(memory_space=pl.ANY)
    return _pcall(
        body, name=name,
        in_specs=[hbm] * n, out_specs=[hbm] * n,
        out_shape=[out_shape(s, d, lay) for s, d, lay in zip(shards, dtypes, layouts)],
        scratch_shapes=[pltpu.VMEM(s.shape, F32) for s in shards] + [pltpu.VMEM(s.shape, d) for s, d in zip(shards, dtypes)]
        + [pltpu.SemaphoreType.DMA((n,)), pltpu.SemaphoreType.DMA((n,)),
           pltpu.SemaphoreType.DMA((n, 3)), pltpu.SemaphoreType.DMA((n, 3)),
           pltpu.SemaphoreType.DMA((n, 3)), pltpu.SemaphoreType.DMA((n, 3))],
        compiler_params=pltpu.CompilerParams(vmem_limit_bytes=VMEM_LIMIT),
    )(*shards)


def _reduce_quarter(g, name):
    _, R, C = g.shape
    hr = R // 2

    def body(g_ref, out_ref, sib, snd, rcv, sem_a, sem_d, send_sems, recv_sems):
        x, y, c = _mesh_pos()
        chip = 2 * x + y
        sibling = (x, y, 1 - c)
        mine = pl.ds(pl.multiple_of(c * hr, 16), hr)
        theirs = pl.ds(pl.multiple_of((1 - c) * hr, 16), hr)
        swap = pltpu.make_async_remote_copy(
            src_ref=g_ref.at[:, theirs, :], dst_ref=sib, send_sem=sem_a.at[0], recv_sem=sem_a.at[1],
            device_id=sibling, device_id_type=MESH)
        swap.start()
        swap.wait()
        sends = []
        for k, (px, py) in enumerate(_other_chips(x, y)):
            q = 2 * px + py
            snd[k] = (g_ref[q, mine, :].astype(F32) + sib[q].astype(F32)).astype(BF16)
            cp = pltpu.make_async_remote_copy(
                src_ref=snd.at[k], dst_ref=rcv.at[k], send_sem=send_sems.at[k], recv_sem=recv_sems.at[k],
                device_id=(px, py, c), device_id_type=MESH)
            cp.start()
            sends.append(cp)
        acc = g_ref[chip, mine, :].astype(F32) + sib[chip].astype(F32)
        for k in range(3):
            sends[k].wait_recv()
            acc = acc + rcv[k].astype(F32)
        out_ref[mine, :] = acc
        back = pltpu.make_async_remote_copy(
            src_ref=out_ref.at[mine, :], dst_ref=out_ref.at[mine, :], send_sem=sem_d.at[0], recv_sem=sem_d.at[1],
            device_id=sibling, device_id_type=MESH)
        back.start()
        back.wait()
        for cp in sends:
            cp.wait_send()

    vmem = pl.BlockSpec(memory_space=pltpu.VMEM)
    return _pcall(
        body, name=name,
        in_specs=[vmem], out_specs=vmem,
        out_shape=jax.ShapeDtypeStruct((R, C), F32),
        scratch_shapes=[
            pltpu.VMEM((N_CHIPS, hr, C), BF16), pltpu.VMEM((3, hr, C), BF16), pltpu.VMEM((3, hr, C), BF16),
            pltpu.SemaphoreType.DMA((2,)), pltpu.SemaphoreType.DMA((2,)),
            pltpu.SemaphoreType.DMA((3,)), pltpu.SemaphoreType.DMA((3,)),
        ],
        compiler_params=pltpu.CompilerParams(vmem_limit_bytes=VMEM_LIMIT),
    )(g)


def _allreduce_small(v, name):
    R, C = v.shape
    n_dev = 8

    def body(v_ref, out_ref, buf, send_sems, recv_sems):
        x, y, c = _mesh_pos()
        me = 4 * x + 2 * y + c
        buf[me] = v_ref[...]
        peers = []
        for k in range(1, n_dev):
            kx, ky, kc = (k >> 2) & 1, (k >> 1) & 1, k & 1
            px = 1 - x if kx else x
            py = 1 - y if ky else y
            pc = 1 - c if kc else c
            peers.append((px, py, pc))
        sends = []
        for k, peer in enumerate(peers):
            cp = pltpu.make_async_remote_copy(
                src_ref=v_ref, dst_ref=buf.at[me], send_sem=send_sems.at[k], recv_sem=recv_sems.at[k],
                device_id=peer, device_id_type=MESH)
            cp.start()
            sends.append(cp)
        for k, (px, py, pc) in enumerate(peers):
            pltpu.make_async_remote_copy(
                src_ref=v_ref, dst_ref=buf.at[4 * px + 2 * py + pc], send_sem=send_sems.at[k],
                recv_sem=recv_sems.at[k], device_id=(px, py, pc), device_id_type=MESH).wait_recv()
        for cp in sends:
            cp.wait_send()
        acc = buf[0]
        for i in range(1, n_dev):
            acc = acc + buf[i]
        out_ref[...] = acc

    vmem = pl.BlockSpec(memory_space=pltpu.VMEM)
    return _pcall(
        body, name=name,
        in_specs=[vmem], out_specs=vmem,
        out_shape=jax.ShapeDtypeStruct((R, C), F32),
        scratch_shapes=[pltpu.VMEM((n_dev, R, C), F32),
                        pltpu.SemaphoreType.DMA((n_dev - 1,)), pltpu.SemaphoreType.DMA((n_dev - 1,))],
        compiler_params=pltpu.CompilerParams(vmem_limit_bytes=VMEM_LIMIT),
    )(v)


def _rope_tables(positions):
    half = ROPE_DIM // 2
    inv_freq = ROPE_THETA ** (-jnp.arange(half, dtype=F32) / half)
    ang = positions.astype(F32)[:, None] * inv_freq
    cos, sin = jnp.cos(ang), jnp.sin(ang)
    T = positions.shape[0]
    ones = jnp.ones((T, HEAD_DIM - ROPE_DIM), F32)
    c64 = jnp.concatenate([cos, cos, ones], axis=1)
    s64 = jnp.concatenate([-sin, sin, 0.0 * ones], axis=1)
    return jnp.tile(c64, (1, 2)), jnp.tile(s64, (1, 2))


def _local_step(x, mem, positions, target, small, big):
    T = x.shape[0]
    cos, sin = _rope_tables(positions)
    two = lambda g: jnp.tile(g, (1, 2))
    gq2, gk2, gmq2, gmk2 = two(small["g_q"]), two(small["g_k"]), two(small["g_mq"]), two(small["g_mk"])

    x1, h1, a1, b1 = _ffn_fwd(x, small["g_ffn1"], big["wg1"], big["wu1"], big["wd1"], "ffn1_fwd")
    hm = _rms_fwd(x1, small["g_mix"], "mix_norm")
    proj = _mm_nn([hm], big["w_in"], None, "in_proj")
    hmem = _rms_fwd(mem, small["g_mem"], "mem_norm")
    mkv = _mm_nn([hmem], big["w_mkv"], None, "mem_proj")
    ya = _attn_fwd(proj, cos, sin, gq2, gk2, small["sinks"], "swa_fwd")
    yc, cpre = _conv_fwd(proj, small["w_dw"], small["b_dw"], small["g_conv_ln"], small["b_conv_ln"], "conv_fwd")
    ym = _mem_fwd(proj, mkv, gmq2, gmk2, "memattn_fwd")
    x2 = _mm_nn([ya, yc, ym], big["w_out"], x1, "out_proj")
    x3, h2, a2, b2 = _ffn_fwd(x2, small["g_ffn2"], big["wg2"], big["wu2"], big["wd2"], "ffn2_fwd")
    dx3, loss = _loss_head(x3, target, "loss_head")

    dh2, dwg2, dwu2, dwd2 = _ffn_bwd(dx3, h2, a2, b2, big["wg2"], big["wu2"], big["wd2"], "ffn2_bwd")
    dx2, dg_ffn2 = _rms_bwd(x2, small["g_ffn2"], dh2, dx3, "ffn2_norm_bwd")
    dyc = _mm_nt([dx2], big["w_out"], "out_proj_bwd")
    dw_out = _mm_tn([ya, yc, ym], [dx2], 1, "out_proj_wgrad")
    dq, dk, dv, dgq, dgk, dsinks = _attn_bwd(proj, cos, sin, gq2, gk2, small["sinks"], dyc, "swa_bwd")
    da, dgate, dw_dw, db_dw, dg_ln, db_ln = _conv_bwd(
        proj, cpre, small["w_dw"], small["g_conv_ln"], small["b_conv_ln"], dyc, "conv_bwd")
    dmq, dmkv, dgmq, dgmk = _mem_bwd(proj, mkv, gmq2, gmk2, dyc, "memattn_bwd")
    pieces = [dq, dk, dv, da, dgate, dmq]
    dhm = _mm_nt(pieces, big["w_in"], "in_proj_bwd")
    dw_in = _mm_tn([hm], pieces, N_CHIPS, "in_proj_wgrad")
    dhmem = _mm_nt([dmkv], big["w_mkv"], "mem_proj_bwd")
    dw_mkv = _mm_tn([hmem], [dmkv], 1, "mem_proj_wgrad")
    _, dg_mem = _rms_bwd(mem, small["g_mem"], dhmem, None, "mem_norm_bwd")
    dx1, dg_mix = _rms_bwd(x1, small["g_mix"], dhm, dx2, "mix_norm_bwd")
    dh1, dwg1, dwu1, dwd1 = _ffn_bwd(dx1, h1, a1, b1, big["wg1"], big["wu1"], big["wd1"], "ffn1_bwd")
    dx, dg_ffn1 = _rms_bwd(x, small["g_ffn1"], dh1, dx1, "ffn1_norm_bwd")

    big_grads = dict(wg1=dwg1, wu1=dwu1, wd1=dwd1, w_in=dw_in, w_mkv=dw_mkv.reshape(N_CHIPS, -1, 2 * MQ_COLS),
                     w_out=dw_out.reshape(N_CHIPS, -1, D_MODEL), wg2=dwg2, wu2=dwu2, wd2=dwd2)
    small_grads = dict(
        g_ffn1=dg_ffn1, g_mix=dg_mix, g_q=dgq[:, :HEAD_DIM], g_k=dgk[:, :HEAD_DIM], sinks=dsinks[:, :N_Q_HEADS],
        w_dw=dw_dw[:CONV_WIDTH], b_dw=db_dw, g_conv_ln=dg_ln, b_conv_ln=db_ln, g_mem=dg_mem,
        g_mq=dgmq[:, :HEAD_DIM], g_mk=dgmk[:, :HEAD_DIM], g_ffn2=dg_ffn2)
    return loss, dx, big_grads, small_grads


SMALL_NAMES = ["g_ffn1", "g_mix", "g_q", "g_k", "sinks", "b_dw", "g_conv_ln", "b_conv_ln", "g_mem", "g_mq", "g_mk",
               "g_ffn2"]
PACK_COLS = 1024


def _pack(parts):
    flat = [p.reshape(-1) for p in parts]
    offs, o = [], 0
    for f in flat:
        offs.append(o)
        o += f.shape[0]
    rows = -(-o // (8 * PACK_COLS)) * 8
    pad = jnp.zeros((rows * PACK_COLS - o,), F32)
    return jnp.concatenate(flat + [pad]).reshape(rows, PACK_COLS), offs


def _unpack(packed, offs, shapes):
    flat = packed.reshape(-1)
    return [flat[o:o + math.prod(s)].reshape(s) for o, s in zip(offs, shapes)]


def kernel(x, mem, positions, g_ffn1, w_ffn1_gate, w_ffn1_up, w_ffn1_down, g_mix, w_in, g_q, g_k, sinks, w_dw, b_dw, g_conv_ln, b_conv_ln, g_mem, w_mem_kv, g_mq, g_mk, w_out, g_ffn2, w_ffn2_gate, w_ffn2_up, w_ffn2_down, loss_target, m_g_ffn1, m_w_ffn1_gate, m_w_ffn1_up, m_w_ffn1_down, m_g_mix, m_w_in, m_g_q, m_g_k, m_sinks, m_w_dw, m_b_dw, m_g_conv_ln, m_b_conv_ln, m_g_mem, m_w_mem_kv, m_g_mq, m_g_mk, m_w_out, m_g_ffn2, m_w_ffn2_gate, m_w_ffn2_up, m_w_ffn2_down, v_g_ffn1, v_w_ffn1_gate, v_w_ffn1_up, v_w_ffn1_down, v_g_mix, v_w_in, v_g_q, v_g_k, v_sinks, v_w_dw, v_b_dw, v_g_conv_ln, v_b_conv_ln, v_g_mem, v_w_mem_kv, v_g_mq, v_g_mk, v_w_out, v_g_ffn2, v_w_ffn2_gate, v_w_ffn2_up, v_w_ffn2_down):
    args = dict(locals())
    weight_names = ["g_ffn1", "w_ffn1_gate", "w_ffn1_up", "w_ffn1_down", "g_mix", "w_in", "g_q", "g_k", "sinks",
                    "w_dw", "b_dw", "g_conv_ln", "b_conv_ln", "g_mem", "w_mem_kv", "g_mq", "g_mk", "w_out", "g_ffn2",
                    "w_ffn2_gate", "w_ffn2_up", "w_ffn2_down"]
    big_names = ["w_ffn1_gate", "w_ffn1_up", "w_ffn1_down", "w_in", "w_mem_kv", "w_out",
                 "w_ffn2_gate", "w_ffn2_up", "w_ffn2_down"]
    short = dict(w_ffn1_gate="wg1", w_ffn1_up="wu1", w_ffn1_down="wd1", w_in="w_in", w_mem_kv="w_mkv",
                 w_out="w_out", w_ffn2_gate="wg2", w_ffn2_up="wu2", w_ffn2_down="wd2")

    transposed = ("w_ffn1_gate", "w_ffn1_up", "w_ffn2_gate", "w_ffn2_up")

    def quarter(a, n):
        return jnp.swapaxes(a, 1, 2)[0] if n in transposed else a[0]

    def unquarter(a, n):
        return jnp.swapaxes(a[None], 1, 2) if n in transposed else a[None]

    shards = [quarter(args[n], n) for n in big_names]
    layouts = ["cols" if n == "w_in" else "stack" for n in big_names]
    gathered = _gather_weights(shards + [w_dw[0]], [BF16] * len(shards) + [F32], layouts + ["whole"],
                               "gather_weights")
    big = {short[n]: gathered[i] for i, n in enumerate(big_names)}
    big["w_mkv"] = big["w_mkv"].reshape(D_MODEL, 2 * MQ_COLS)
    big["w_out"] = big["w_out"].reshape(D_MODEL, D_MODEL)
    small = {n: args[n] for n in SMALL_NAMES}
    small["w_dw"] = jnp.transpose(gathered[-1], (1, 0, 2)).reshape(CONV_WIDTH, CONV_CH)

    loss, dx, big_grads, small_grads = _local_step(x[0], mem[0], positions[0], loss_target[0], small, big)

    small_order = SMALL_NAMES + ["w_dw"]
    packed, offs = _pack([small_grads[n] for n in small_order] + [loss[:, :1]])
    total = _allreduce_small(packed, "allreduce_small")
    shapes = [small_grads[n].shape for n in small_order] + [(1, 1)]
    summed = dict(zip(small_order + ["loss"], _unpack(total, offs, shapes)))
    chip = 2 * lax.axis_index("x") + lax.axis_index("y")
    dw_dw_full = summed.pop("w_dw")
    loss_out = summed.pop("loss").reshape(())

    grads = {n: summed[n] for n in SMALL_NAMES}
    grads["w_dw"] = lax.dynamic_slice_in_dim(dw_dw_full, chip * (CONV_CH // N_CHIPS), CONV_CH // N_CHIPS, axis=1)
    for n in big_names:
        grads[n] = _reduce_quarter(big_grads[short[n]], "reduce_" + short[n])

    delta, new_m, new_v = {}, {}, {}
    for n in big_names:
        g, d, nm, nv = _adamw(quarter(args[n], n), grads[n], quarter(args["m_" + n], n), quarter(args["v_" + n], n),
                              "adamw_" + short[n])
        grads[n], delta[n], new_m[n], new_v[n] = (unquarter(a, n) for a in (g, d, nm, nv))
    tiny = SMALL_NAMES + ["w_dw"]
    pw, poffs = _pack([args[n] for n in tiny])
    pg, _ = _pack([grads[n] for n in tiny])
    pm, _ = _pack([args["m_" + n] for n in tiny])
    pv, _ = _pack([args["v_" + n] for n in tiny])
    _, pd, pnm, pnv = _adamw(pw, pg, pm, pv, "adamw_small")
    tshapes = [args[n].shape for n in tiny]
    for store, packed_out in ((delta, pd), (new_m, pnm), (new_v, pnv)):
        for n, val in zip(tiny, _unpack(packed_out, poffs, tshapes)):
            store[n] = val

    def shaped(n, v):
        return v.reshape(args[n].shape)

    return (loss_out, dx[None],
            *[shaped(n, grads[n]) for n in weight_names],
            *[shaped(n, delta[n]) for n in weight_names],
            *[shaped(n, new_m[n]) for n in weight_names],
            *[shaped(n, new_v[n]) for n in weight_names])
```

```python
import functools
import math

import jax
import jax.numpy as jnp
from jax import lax
from jax.experimental import pallas as pl
from jax.experimental.pallas import tpu as pltpu

F32 = jnp.float32
BF16 = jnp.bfloat16

D_MODEL = 1024
SEQ = 2048
MEM_LEN = 256
HEAD_DIM = 64
N_Q_HEADS = 8
N_KV_HEADS = 2
Q_PER_KV = 4
N_MEM_HEADS = 4
BLOCK = 128
CONV_CH = 256
CONV_WIDTH = 31
ROPE_THETA = 500000.0
ROPE_DIM = 16
D_FF = 2816
EPS = 1e-6
Q_COLS = 512
KV_COLS = 128
MQ_COLS = 256
IN_COLS = 1536

N_CHIPS = 4
FF_CHUNK = D_FF // N_CHIPS
IN_CHUNK = IN_COLS // N_CHIPS

ADAM_LR = 0.001
ADAM_B1 = 0.9
ADAM_B2 = 0.999
ADAM_EPS = 1e-08
ADAM_WD = 0.01
ADAM_STEP = 10

LANES = 128
VMEM_LIMIT = 56 * 1024 * 1024
ROW_TILE = 512
MESH = pl.DeviceIdType.MESH
NEG = -1e30


class _Comm:
    def __init__(self, ins, out_shapes, n_sems, plan, aliases=None):
        self.ins, self.out_shapes, self.n_sems, self.plan, self.aliases = list(ins), list(out_shapes), n_sems, plan, aliases or {}


def _pcall(body, comm=None, **kw):
    if comm is None:
        return pl.pallas_call(body, **kw)
    grid = kw["grid"]
    in_specs = list(kw["in_specs"])
    single = not isinstance(kw["out_shape"], (list, tuple))
    out_specs = [kw["out_specs"]] if single else list(kw["out_specs"])
    out_shape = [kw["out_shape"]] if single else list(kw["out_shape"])
    scratch = list(kw.get("scratch_shapes", ()))
    n_in, n_out, n_scr = len(in_specs), len(out_shape), len(scratch)
    n_ci, n_co = len(comm.ins), len(comm.out_shapes)

    def wrapped(*refs):
        o = 0
        parts = []
        for cnt in (n_in, n_ci, n_out, n_co, n_scr):
            parts.append(refs[o:o + cnt])
            o += cnt
        ins, c_ins, outs, c_outs, scr = parts
        sems = refs[o]
        first = last = None
        for d, size in enumerate(grid):
            at0, at_end = pl.program_id(d) == 0, pl.program_id(d) == size - 1
            first = at0 if first is None else first & at0
            last = at_end if last is None else last & at_end

        @pl.when(first)
        def _():
            local, sends, _ = comm.plan(c_ins, c_outs, sems)
            for cp in local + sends:
                cp.start()

        body(*ins, *outs, *scr)

        @pl.when(last)
        def _():
            local, sends, recvs = comm.plan(c_ins, c_outs, sems)
            for cp in recvs:
                cp.wait_recv()
            for cp in sends:
                cp.wait_send()
            for cp in local:
                cp.wait()

    hbm = pl.BlockSpec(memory_space=pl.ANY)
    kw = dict(kw, in_specs=in_specs + [hbm] * n_ci, out_specs=out_specs + [hbm] * n_co,
              out_shape=out_shape + comm.out_shapes,
              scratch_shapes=scratch + [pltpu.SemaphoreType.DMA((comm.n_sems,))])
    if comm.aliases:
        kw["input_output_aliases"] = {n_in + i: n_out + o for i, o in comm.aliases.items()}
    call = pl.pallas_call(wrapped, **kw)

    def run(*args):
        res = call(*args, *comm.ins)
        return (res[0] if single else list(res[:n_out])), list(res[n_out:])

    return run


def _params(sem=None):
    return pltpu.CompilerParams(dimension_semantics=sem, vmem_limit_bytes=VMEM_LIMIT)


def _dot(a, b):
    return jnp.dot(a, b, preferred_element_type=F32)


def _dot_nt(a, b):
    return lax.dot_general(a, b, (((1,), (1,)), ((), ())), preferred_element_type=F32)


def _dot_tn(a, b):
    return lax.dot_general(a, b, (((0,), (0,)), ((), ())), preferred_element_type=F32)


def _sigmoid(x):
    return 1.0 / (1.0 + jnp.exp(-x))


def _full(shape):
    n = len(shape)
    return pl.BlockSpec(shape, lambda *_: (0,) * n)


def _ffn_fwd(x, g, wg, wu, wd, name, comm=None):
    T, D = x.shape
    nt = T // ROW_TILE

    def body(x_ref, g_ref, wg_ref, wu_ref, wd_ref, xo_ref, h_ref, a_ref, b_ref):
        j = pl.program_id(1)

        @pl.when(j == 0)
        def _():
            xv = x_ref[...]
            rstd = lax.rsqrt(jnp.mean(xv * xv, axis=-1, keepdims=True) + EPS)
            h_ref[...] = (xv * rstd * g_ref[...]).astype(BF16)
            xo_ref[...] = jnp.zeros_like(xo_ref)

        h = h_ref[...]
        a = _dot_nt(h, wg_ref[0])
        b = _dot_nt(h, wu_ref[0])
        a_ref[0] = a.astype(BF16)
        b_ref[0] = b.astype(BF16)
        s = (a * _sigmoid(a)) * b
        xo_ref[...] += _dot(s.astype(BF16), wd_ref[0])

        @pl.when(j == N_CHIPS - 1)
        def _():
            xo_ref[...] = x_ref[...] + 0.5 * xo_ref[...]

    return _pcall(
        body, comm=comm, name=name, grid=(nt, N_CHIPS),
        in_specs=[
            pl.BlockSpec((ROW_TILE, D), lambda t, j: (t, 0)),
            pl.BlockSpec((1, D), lambda t, j: (0, 0)),
            pl.BlockSpec((1, FF_CHUNK, D), lambda t, j: (j, 0, 0)),
            pl.BlockSpec((1, FF_CHUNK, D), lambda t, j: (j, 0, 0)),
            pl.BlockSpec((1, FF_CHUNK, D), lambda t, j: (j, 0, 0)),
        ],
        out_specs=[
            pl.BlockSpec((ROW_TILE, D), lambda t, j: (t, 0)),
            pl.BlockSpec((ROW_TILE, D), lambda t, j: (t, 0)),
            pl.BlockSpec((1, ROW_TILE, FF_CHUNK), lambda t, j: (j, t, 0)),
            pl.BlockSpec((1, ROW_TILE, FF_CHUNK), lambda t, j: (j, t, 0)),
        ],
        out_shape=[
            jax.ShapeDtypeStruct((T, D), F32),
            jax.ShapeDtypeStruct((T, D), BF16),
            jax.ShapeDtypeStruct((N_CHIPS, T, FF_CHUNK), BF16),
            jax.ShapeDtypeStruct((N_CHIPS, T, FF_CHUNK), BF16),
        ],
        compiler_params=_params(("arbitrary", "arbitrary")),
    )(x, g, wg, wu, wd)


def _ffn_bwd(dxo, h, a, b, wg, wu, wd, name, comm=None):
    T, D = dxo.shape
    tt = 256
    nt = T // tt

    def body(dxo_ref, h_ref, a_ref, b_ref, wg_ref, wu_ref, wd_ref,
             dh_hbm, dwg_ref, dwu_ref, dwd_ref, dh_acc, acc_g, acc_u, acc_d):
        j = pl.program_id(0)
        t = pl.program_id(1)
        do = (0.5 * dxo_ref[...]).astype(BF16)
        av = a_ref[0].astype(F32)
        bv = b_ref[0].astype(F32)
        sig = _sigmoid(av)
        sa = av * sig
        ds = _dot_nt(do, wd_ref[0])
        da = (ds * bv * (sig * (1.0 + av * (1.0 - sig)))).astype(BF16)
        db = (ds * sa).astype(BF16)
        hv = h_ref[...]
        part_d = _dot_tn((sa * bv).astype(BF16), do)
        part_g = _dot_tn(da, hv)
        part_u = _dot_tn(db, hv)
        part_h = _dot(da, wg_ref[0]) + _dot(db, wu_ref[0])
        rows = pl.ds(pl.multiple_of(t * tt, tt), tt)

        @pl.when(j == 0)
        def _():
            dh_acc[rows, :] = part_h

        @pl.when(j > 0)
        def _():
            dh_acc[rows, :] += part_h

        @pl.when(t == 0)
        def _():
            acc_g[...] = part_g
            acc_u[...] = part_u
            acc_d[...] = part_d

        @pl.when(t > 0)
        def _():
            acc_g[...] += part_g
            acc_u[...] += part_u
            acc_d[...] += part_d

        @pl.when(t == nt - 1)
        def _():
            dwg_ref[0] = acc_g[...].astype(BF16)
            dwu_ref[0] = acc_u[...].astype(BF16)
            dwd_ref[0] = acc_d[...].astype(BF16)

        @pl.when((t == nt - 1) & (j == N_CHIPS - 1))
        def _():
            pltpu.sync_copy(dh_acc, dh_hbm)

    return _pcall(
        body, comm=comm, name=name, grid=(N_CHIPS, nt),
        in_specs=[
            pl.BlockSpec((tt, D), lambda j, t: (t, 0)),
            pl.BlockSpec((tt, D), lambda j, t: (t, 0)),
            pl.BlockSpec((1, tt, FF_CHUNK), lambda j, t: (j, t, 0)),
            pl.BlockSpec((1, tt, FF_CHUNK), lambda j, t: (j, t, 0)),
            pl.BlockSpec((1, FF_CHUNK, D), lambda j, t: (j, 0, 0)),
            pl.BlockSpec((1, FF_CHUNK, D), lambda j, t: (j, 0, 0)),
            pl.BlockSpec((1, FF_CHUNK, D), lambda j, t: (j, 0, 0)),
        ],
        out_specs=[
            pl.BlockSpec(memory_space=pl.ANY),
            pl.BlockSpec((1, FF_CHUNK, D), lambda j, t: (j, 0, 0)),
            pl.BlockSpec((1, FF_CHUNK, D), lambda j, t: (j, 0, 0)),
            pl.BlockSpec((1, FF_CHUNK, D), lambda j, t: (j, 0, 0)),
        ],
        out_shape=[
            jax.ShapeDtypeStruct((T, D), F32),
            jax.ShapeDtypeStruct((N_CHIPS, FF_CHUNK, D), BF16),
            jax.ShapeDtypeStruct((N_CHIPS, FF_CHUNK, D), BF16),
            jax.ShapeDtypeStruct((N_CHIPS, FF_CHUNK, D), BF16),
        ],
        scratch_shapes=[
            pltpu.VMEM((T, D), F32),
            pltpu.VMEM((FF_CHUNK, D), F32),
            pltpu.VMEM((FF_CHUNK, D), F32),
            pltpu.VMEM((FF_CHUNK, D), F32),
        ],
        compiler_params=_params(("arbitrary", "arbitrary")),
    )(dxo, h, a, b, wg, wu, wd)


def _rms_fwd(x, g, name, comm=None):
    T, D = x.shape
    tt = min(ROW_TILE, T)

    def body(x_ref, g_ref, h_ref):
        xv = x_ref[...]
        rstd = lax.rsqrt(jnp.mean(xv * xv, axis=-1, keepdims=True) + EPS)
        h_ref[...] = (xv * rstd * g_ref[...]).astype(BF16)

    return _pcall(
        body, comm=comm, name=name, grid=(T // tt,),
        in_specs=[pl.BlockSpec((tt, D), lambda t: (t, 0)), pl.BlockSpec((1, D), lambda t: (0, 0))],
        out_specs=pl.BlockSpec((tt, D), lambda t: (t, 0)),
        out_shape=jax.ShapeDtypeStruct((T, D), BF16),
        compiler_params=_params(("arbitrary",)),
    )(x, g)


def _rms_bwd(x, g, dh, dres, name, comm=None):
    T, D = x.shape
    tt = min(ROW_TILE, T)
    has_res = dres is not None

    def body(*refs):
        if has_res:
            x_ref, g_ref, dh_ref, dres_ref, dx_ref, dg_ref = refs
        else:
            x_ref, g_ref, dh_ref, dx_ref, dg_ref = refs
        t = pl.program_id(0)
        xv = x_ref[...]
        rstd = lax.rsqrt(jnp.mean(xv * xv, axis=-1, keepdims=True) + EPS)
        xhat = xv * rstd
        dhv = dh_ref[...]
        gy = dhv * g_ref[...]
        dx = rstd * (gy - xhat * jnp.mean(gy * xhat, axis=-1, keepdims=True))
        if has_res:
            dx = dx + dres_ref[...]
        dx_ref[...] = dx
        part = jnp.sum(dhv * xhat, axis=0, keepdims=True)

        @pl.when(t == 0)
        def _():
            dg_ref[...] = part

        @pl.when(t > 0)
        def _():
            dg_ref[...] += part

    tile = pl.BlockSpec((tt, D), lambda t: (t, 0))
    vec = pl.BlockSpec((1, D), lambda t: (0, 0))
    args = [x, g, dh] + ([dres] if has_res else [])
    return _pcall(
        body, comm=comm, name=name, grid=(T // tt,),
        in_specs=[tile, vec, tile] + ([tile] if has_res else []),
        out_specs=[tile, vec],
        out_shape=[jax.ShapeDtypeStruct((T, D), F32), jax.ShapeDtypeStruct((1, D), F32)],
        compiler_params=_params(("arbitrary",)),
    )(*args)


def _mm_nn(a_list, b, res, name, comm=None):
    T = a_list[0].shape[0]
    K, N = b.shape
    tt = min(ROW_TILE, T)
    ks = [a.shape[1] for a in a_list]
    na = len(a_list)
    has_res = res is not None

    def body(*refs):
        a_refs = refs[:na]
        b_ref = refs[na]
        o_ref = refs[-1]
        acc = res_v = None
        off = 0
        for a_ref, k in zip(a_refs, ks):
            part = _dot(a_ref[...].astype(BF16), b_ref[off:off + k, :])
            acc = part if acc is None else acc + part
            off += k
        if has_res:
            acc = refs[na + 1][...] + acc
        o_ref[...] = acc

    in_specs = [pl.BlockSpec((tt, k), lambda t: (t, 0)) for k in ks] + [pl.BlockSpec((K, N), lambda t: (0, 0))]
    args = list(a_list) + [b]
    if has_res:
        in_specs.append(pl.BlockSpec((tt, N), lambda t: (t, 0)))
        args.append(res)
    return _pcall(
        body, comm=comm, name=name, grid=(T // tt,), in_specs=in_specs,
        out_specs=pl.BlockSpec((tt, N), lambda t: (t, 0)),
        out_shape=jax.ShapeDtypeStruct((T, N), F32),
        compiler_params=_params(("arbitrary",)),
    )(*args)


def _mm_nt(a_list, b, name, comm=None):
    T = a_list[0].shape[0]
    K, N = b.shape
    tt = min(ROW_TILE, T)
    ns = [a.shape[1] for a in a_list]
    na = len(a_list)

    def body(*refs):
        b_ref = refs[na]
        o_ref = refs[-1]
        acc = None
        off = 0
        for a_ref, n in zip(refs[:na], ns):
            part = _dot_nt(a_ref[...].astype(BF16), b_ref[:, off:off + n])
            acc = part if acc is None else acc + part
            off += n
        o_ref[...] = acc

    return _pcall(
        body, comm=comm, name=name, grid=(T // tt,),
        in_specs=[pl.BlockSpec((tt, n), lambda t: (t, 0)) for n in ns] + [pl.BlockSpec((K, N), lambda t: (0, 0))],
        out_specs=pl.BlockSpec((tt, K), lambda t: (t, 0)),
        out_shape=jax.ShapeDtypeStruct((T, K), F32),
        compiler_params=_params(("arbitrary",)),
    )(*a_list, b)


def _mm_tn(a_list, b_list, col_chunks, name, comm=None):
    T = a_list[0].shape[0]
    tt = min(ROW_TILE, T)
    nt = T // tt
    ms = [a.shape[1] for a in a_list]
    ns = [b.shape[1] for b in b_list]
    M, N = sum(ms), sum(ns)
    na, nb = len(a_list), len(b_list)
    cw = N // col_chunks

    def body(*refs):
        a_refs, b_refs = refs[:na], refs[na:na + nb]
        o_ref, acc = refs[na + nb], refs[na + nb + 1]
        t = pl.program_id(0)

        @pl.when(t == 0)
        def _():
            acc[...] = jnp.zeros_like(acc)

        ro = 0
        for a_ref, m in zip(a_refs, ms):
            av = a_ref[...].astype(BF16)
            co = 0
            for b_ref, n in zip(b_refs, ns):
                acc[ro:ro + m, co:co + n] += _dot_tn(av, b_ref[...].astype(BF16))
                co += n
            ro += m

        @pl.when(t == nt - 1)
        def _():
            if col_chunks == 1:
                o_ref[...] = acc[...].astype(BF16)
            else:
                for q in range(col_chunks):
                    o_ref[q] = acc[:, q * cw:(q + 1) * cw].astype(BF16)

    out_shape = (M, N) if col_chunks == 1 else (col_chunks, M, cw)
    return _pcall(
        body, comm=comm, name=name, grid=(nt,),
        in_specs=[pl.BlockSpec((tt, m), lambda t: (t, 0)) for m in ms]
        + [pl.BlockSpec((tt, n), lambda t: (t, 0)) for n in ns],
        out_specs=_full(out_shape),
        out_shape=jax.ShapeDtypeStruct(out_shape, BF16),
        scratch_shapes=[pltpu.VMEM((M, N), F32)],
        compiler_params=_params(("arbitrary",)),
    )(*a_list, *b_list)


def _head_masks():
    lane = lax.broadcasted_iota(jnp.int32, (1, LANES), 1)
    l64 = lane & (HEAD_DIM - 1)
    return lane < HEAD_DIM, l64 < ROPE_DIM // 2, l64 < ROPE_DIM


def _head_mean(v, lo):
    s_lo = jnp.sum(jnp.where(lo, v, 0.0), axis=-1, keepdims=True)
    s_hi = jnp.sum(jnp.where(lo, 0.0, v), axis=-1, keepdims=True)
    return jnp.where(lo, s_lo, s_hi) * (1.0 / HEAD_DIM)


def _rope_swap(v, first, rot):
    up = pltpu.roll(v, LANES - ROPE_DIM // 2, 1)
    down = pltpu.roll(v, ROPE_DIM // 2, 1)
    return jnp.where(first, up, jnp.where(rot, down, 0.0))


def _head_norm(x, g, lo):
    rstd = lax.rsqrt(_head_mean(x * x, lo) + EPS)
    return x * rstd * g


def _head_norm_bwd(x, g, dy, lo):
    rstd = lax.rsqrt(_head_mean(x * x, lo) + EPS)
    xhat = x * rstd
    gy = dy * g
    dx = rstd * (gy - xhat * _head_mean(gy * xhat, lo))
    return dx, dy * xhat


def _rope(xn, cos, sin, first, rot):
    return xn * cos + _rope_swap(xn, first, rot) * sin


def _rope_bwd(dy, cos, sin, first, rot):
    return dy * cos + _rope_swap(dy * sin, first, rot)


def _fold_heads(v):
    return v + pltpu.roll(v, HEAD_DIM, 1)


ATT_ROWS = 256


def _attn_prepare(q_ref, k_ref, v_ref, cos_ref, sin_ref, gq_ref, gk_ref, qs, ks, vs):
    T = q_ref.shape[0]
    lo, first, rot = _head_masks()
    ks[0:BLOCK, :] = jnp.zeros((BLOCK, KV_COLS), BF16)
    vs[0:BLOCK, :] = jnp.zeros((BLOCK, KV_COLS), BF16)

    def step(i, _):
        r0 = pl.multiple_of(i * ATT_ROWS, ATT_ROWS)
        rows = pl.ds(r0, ATT_ROWS)
        prow = pl.ds(r0 + BLOCK, ATT_ROWS)
        cos, sin = cos_ref[rows, :], sin_ref[rows, :]
        for p in range(Q_COLS // LANES):
            cols = slice(p * LANES, (p + 1) * LANES)
            xr = _rope(_head_norm(q_ref[rows, cols], gq_ref[...], lo), cos, sin, first, rot)
            qs[rows, cols] = (xr * (HEAD_DIM ** -0.5)).astype(BF16)
        kr = _rope(_head_norm(k_ref[rows, :], gk_ref[...], lo), cos, sin, first, rot)
        ks[prow, :] = kr.astype(BF16)
        vs[prow, :] = v_ref[rows, :].astype(BF16)
        return 0

    lax.fori_loop(0, T // ATT_ROWS, step, 0)


def _attn_scores(qh, kw, blk, sink):
    s = _dot_nt(qh, kw)
    qi = lax.broadcasted_iota(jnp.int32, (BLOCK, 2 * BLOCK), 0) + BLOCK
    ki = lax.broadcasted_iota(jnp.int32, (BLOCK, 2 * BLOCK), 1)
    rel = qi - ki
    valid = (rel >= 0) & (rel < BLOCK) & ((blk > 0) | (ki >= BLOCK))
    s = jnp.where(valid, s, NEG)
    m = jnp.maximum(jnp.max(s, axis=-1, keepdims=True), sink)
    p = jnp.exp(s - m)
    e_sink = jnp.exp(sink - m)
    inv = 1.0 / (jnp.sum(p, axis=-1, keepdims=True) + e_sink)
    return p * inv, e_sink * inv


def _attn_fwd(proj, cos, sin, gq2, gk2, sinks, name, comm=None):
    T = proj.shape[0]
    nb = T // BLOCK

    def body(q_ref, k_ref, v_ref, cos_ref, sin_ref, gq_ref, gk_ref, sink_ref, y_ref, qs, ks, vs):
        _attn_prepare(q_ref, k_ref, v_ref, cos_ref, sin_ref, gq_ref, gk_ref, qs, ks, vs)

        def blk_step(blk, _):
            r0 = pl.multiple_of(blk * BLOCK, BLOCK)
            for g in range(N_KV_HEADS):
                gc = slice(g * HEAD_DIM, (g + 1) * HEAD_DIM)
                kw = ks[pl.ds(r0, 2 * BLOCK), gc]
                vw = vs[pl.ds(r0, 2 * BLOCK), gc]
                for r in range(Q_PER_KV):
                    h = g * Q_PER_KV + r
                    hc = slice(h * HEAD_DIM, (h + 1) * HEAD_DIM)
                    w, _ws = _attn_scores(qs[pl.ds(r0, BLOCK), hc], kw, blk, sink_ref[0, h])
                    y_ref[pl.ds(r0, BLOCK), hc] = _dot(w.astype(BF16), vw).astype(BF16)
            return 0

        lax.fori_loop(0, nb, blk_step, 0)

    return _pcall(
        body, comm=comm, name=name, grid=(1,),
        in_specs=[
            pl.BlockSpec((T, Q_COLS), lambda i: (0, 0)),
            pl.BlockSpec((T, KV_COLS), lambda i: (0, Q_COLS // KV_COLS)),
            pl.BlockSpec((T, KV_COLS), lambda i: (0, Q_COLS // KV_COLS + 1)),
            _full((T, LANES)), _full((T, LANES)), _full((1, LANES)), _full((1, LANES)),
            pl.BlockSpec(memory_space=pltpu.SMEM),
        ],
        out_specs=_full((T, Q_COLS)),
        out_shape=jax.ShapeDtypeStruct((T, Q_COLS), BF16),
        scratch_shapes=[
            pltpu.VMEM((T, Q_COLS), BF16),
            pltpu.VMEM((T + BLOCK, KV_COLS), BF16),
            pltpu.VMEM((T + BLOCK, KV_COLS), BF16),
        ],
        compiler_params=_params(("arbitrary",)),
    )(proj, proj, proj, cos, sin, gq2, gk2, sinks)


def _attn_bwd(proj, cos, sin, gq2, gk2, sinks, dyc, name, comm=None):
    T = proj.shape[0]
    nb = T // BLOCK

    def body(q_ref, k_ref, v_ref, cos_ref, sin_ref, gq_ref, gk_ref, sink_ref, dy_ref,
             dq_ref, dk_ref, dv_ref, dgq_ref, dgk_ref, dsink_ref, qs, ks, vs, dqs, dks, dvs):
        _attn_prepare(q_ref, k_ref, v_ref, cos_ref, sin_ref, gq_ref, gk_ref, qs, ks, vs)
        dks[...] = jnp.zeros_like(dks)
        dvs[...] = jnp.zeros_like(dvs)
        lane = lax.broadcasted_iota(jnp.int32, (1, LANES), 1)

        def blk_step(blk, dsink):
            r0 = pl.multiple_of(blk * BLOCK, BLOCK)
            win = pl.ds(r0, 2 * BLOCK)
            for g in range(N_KV_HEADS):
                gc = slice(g * HEAD_DIM, (g + 1) * HEAD_DIM)
                kw = ks[win, gc]
                vw = vs[win, gc]
                for r in range(Q_PER_KV):
                    h = g * Q_PER_KV + r
                    hc = slice(h * HEAD_DIM, (h + 1) * HEAD_DIM)
                    qh = qs[pl.ds(r0, BLOCK), hc]
                    w, w_sink = _attn_scores(qh, kw, blk, sink_ref[0, h])
                    do = dy_ref[pl.ds(r0, BLOCK), hc].astype(BF16)
                    dvs[win, gc] += _dot_tn(w.astype(BF16), do)
                    dw = _dot_nt(do, vw)
                    delta = jnp.sum(w * dw, axis=-1, keepdims=True)
                    ds = (w * (dw - delta)).astype(BF16)
                    dsink = dsink + jnp.where(lane == h, -jnp.sum(w_sink * delta, axis=0, keepdims=True), 0.0)
                    dqs[pl.ds(r0, BLOCK), hc] = _dot(ds, kw)
                    dks[win, gc] += _dot_tn(ds, qh)
            return dsink

        dsink_ref[...] = lax.fori_loop(0, nb, blk_step, jnp.zeros((1, LANES), F32))

        lo, first, rot = _head_masks()

        def step(i, carry):
            dgq, dgk = carry
            r0 = pl.multiple_of(i * ATT_ROWS, ATT_ROWS)
            rows = pl.ds(r0, ATT_ROWS)
            prow = pl.ds(r0 + BLOCK, ATT_ROWS)
            cos, sin = cos_ref[rows, :], sin_ref[rows, :]
            for p in range(Q_COLS // LANES):
                cols = slice(p * LANES, (p + 1) * LANES)
                dxn = _rope_bwd(dqs[rows, cols] * (HEAD_DIM ** -0.5), cos, sin, first, rot)
                dx, dgp = _head_norm_bwd(q_ref[rows, cols], gq_ref[...], dxn, lo)
                dq_ref[rows, cols] = dx
                dgq = dgq + jnp.sum(dgp, axis=0, keepdims=True)
            dkn = _rope_bwd(dks[prow, :], cos, sin, first, rot)
            dx, dgp = _head_norm_bwd(k_ref[rows, :], gk_ref[...], dkn, lo)
            dk_ref[rows, :] = dx
            dgk = dgk + jnp.sum(dgp, axis=0, keepdims=True)
            dv_ref[rows, :] = dvs[prow, :]
            return dgq, dgk

        zero = jnp.zeros((1, LANES), F32)
        dgq, dgk = lax.fori_loop(0, T // ATT_ROWS, step, (zero, zero))
        dgq_ref[...] = _fold_heads(dgq)
        dgk_ref[...] = _fold_heads(dgk)

    vec = jax.ShapeDtypeStruct((1, LANES), F32)
    return _pcall(
        body, comm=comm, name=name, grid=(1,),
        in_specs=[
            pl.BlockSpec((T, Q_COLS), lambda i: (0, 0)),
            pl.BlockSpec((T, KV_COLS), lambda i: (0, Q_COLS // KV_COLS)),
            pl.BlockSpec((T, KV_COLS), lambda i: (0, Q_COLS // KV_COLS + 1)),
            _full((T, LANES)), _full((T, LANES)), _full((1, LANES)), _full((1, LANES)),
            pl.BlockSpec(memory_space=pltpu.SMEM),
            pl.BlockSpec((T, Q_COLS), lambda i: (0, 0)),
        ],
        out_specs=[_full((T, Q_COLS)), _full((T, KV_COLS)), _full((T, KV_COLS)),
                   _full((1, LANES)), _full((1, LANES)), _full((1, LANES))],
        out_shape=[jax.ShapeDtypeStruct((T, Q_COLS), F32), jax.ShapeDtypeStruct((T, KV_COLS), F32),
                   jax.ShapeDtypeStruct((T, KV_COLS), F32), vec, vec, vec],
        scratch_shapes=[
            pltpu.VMEM((T, Q_COLS), BF16),
            pltpu.VMEM((T + BLOCK, KV_COLS), BF16),
            pltpu.VMEM((T + BLOCK, KV_COLS), BF16),
            pltpu.VMEM((T, Q_COLS), F32),
            pltpu.VMEM((T + BLOCK, KV_COLS), F32),
            pltpu.VMEM((T + BLOCK, KV_COLS), F32),
        ],
        compiler_params=_params(("arbitrary",)),
    )(proj, proj, proj, cos, sin, gq2, gk2, sinks, dyc)


CONV_PAD = 32
CONV_ROWS = 256


def _conv_taps(src, w_ref, r0, first_off, step_sign):
    acc = None
    for i in range(CONV_WIDTH):
        term = w_ref[i:i + 1, :] * src[r0 + first_off + step_sign * i:r0 + first_off + step_sign * i + CONV_ROWS, :]
        acc = term if acc is None else acc + term
    return acc


def _conv_fwd(proj, w_dw, b_dw, g_ln, b_ln, name, comm=None):
    T = proj.shape[0]
    a_blk = (Q_COLS + 2 * KV_COLS) // CONV_CH

    def body(a_ref, gate_ref, w_ref, bdw_ref, g_ref, b_ref, y_ref, c_ref, pad):
        pad[0:CONV_PAD, :] = jnp.zeros((CONV_PAD, CONV_CH), F32)
        pad[CONV_PAD:, :] = a_ref[...] * _sigmoid(gate_ref[...])
        for n in range(T // CONV_ROWS):
            r0 = n * CONV_ROWS
            c = _conv_taps(pad, w_ref, r0, CONV_PAD - (CONV_WIDTH - 1), 1) + bdw_ref[...]
            c_ref[r0:r0 + CONV_ROWS, :] = c
            mu = jnp.mean(c, axis=-1, keepdims=True)
            cc = c - mu
            rstd = lax.rsqrt(jnp.mean(cc * cc, axis=-1, keepdims=True) + EPS)
            z = cc * rstd * g_ref[...] + b_ref[...]
            y_ref[r0:r0 + CONV_ROWS, :] = (z * _sigmoid(z)).astype(BF16)

    vec = _full((1, CONV_CH))
    return _pcall(
        body, comm=comm, name=name, grid=(1,),
        in_specs=[
            pl.BlockSpec((T, CONV_CH), lambda i: (0, a_blk)),
            pl.BlockSpec((T, CONV_CH), lambda i: (0, a_blk + 1)),
            _full((CONV_WIDTH, CONV_CH)), vec, vec, vec,
        ],
        out_specs=[_full((T, CONV_CH)), _full((T, CONV_CH))],
        out_shape=[jax.ShapeDtypeStruct((T, CONV_CH), BF16), jax.ShapeDtypeStruct((T, CONV_CH), F32)],
        scratch_shapes=[pltpu.VMEM((T + CONV_PAD, CONV_CH), F32)],
        compiler_params=_params(("arbitrary",)),
    )(proj, proj, w_dw, b_dw, g_ln, b_ln)


def _conv_bwd(proj, c, w_dw, g_ln, b_ln, dyc, name, comm=None):
    T = proj.shape[0]
    a_blk = (Q_COLS + 2 * KV_COLS) // CONV_CH
    y_blk = Q_COLS // CONV_CH

    def body(a_ref, gate_ref, c_ref, w_ref, g_ref, b_ref, dy_ref,
             da_ref, dgate_ref, dw_ref, dbdw_ref, dg_ref, db_ref, pad, dcp):
        pad[0:CONV_PAD, :] = jnp.zeros((CONV_PAD, CONV_CH), F32)
        sg = _sigmoid(gate_ref[...])
        pad[CONV_PAD:, :] = a_ref[...] * sg
        dcp[T:, :] = jnp.zeros((CONV_PAD, CONV_CH), F32)
        dg = db = dbdw = jnp.zeros((1, CONV_CH), F32)
        for n in range(T // CONV_ROWS):
            rows = slice(n * CONV_ROWS, (n + 1) * CONV_ROWS)
            cv = c_ref[rows, :]
            mu = jnp.mean(cv, axis=-1, keepdims=True)
            cc = cv - mu
            rstd = lax.rsqrt(jnp.mean(cc * cc, axis=-1, keepdims=True) + EPS)
            chat = cc * rstd
            z = chat * g_ref[...] + b_ref[...]
            sz = _sigmoid(z)
            dz = dy_ref[rows, :] * (sz * (1.0 + z * (1.0 - sz)))
            dg = dg + jnp.sum(dz * chat, axis=0, keepdims=True)
            db = db + jnp.sum(dz, axis=0, keepdims=True)
            dch = dz * g_ref[...]
            dc = rstd * (dch - jnp.mean(dch, axis=-1, keepdims=True)
                         - chat * jnp.mean(dch * chat, axis=-1, keepdims=True))
            dbdw = dbdw + jnp.sum(dc, axis=0, keepdims=True)
            dcp[rows, :] = dc
        dg_ref[...] = dg
        db_ref[...] = db
        dbdw_ref[...] = dbdw
        dw_ref[CONV_WIDTH:, :] = jnp.zeros((CONV_PAD - CONV_WIDTH, CONV_CH), F32)
        for i in range(CONV_WIDTH):
            off = CONV_PAD - (CONV_WIDTH - 1) + i
            acc = jnp.zeros((1, CONV_CH), F32)
            for n in range(T // CONV_ROWS):
                r0 = n * CONV_ROWS
                acc = acc + jnp.sum(dcp[r0:r0 + CONV_ROWS, :] * pad[r0 + off:r0 + off + CONV_ROWS, :],
                                    axis=0, keepdims=True)
            dw_ref[i:i + 1, :] = acc
        for n in range(T // CONV_ROWS):
            r0 = n * CONV_ROWS
            rows = slice(r0, r0 + CONV_ROWS)
            dhg = _conv_taps(dcp, w_ref, r0, CONV_WIDTH - 1, -1)
            sgv = sg[rows, :]
            da_ref[rows, :] = dhg * sgv
            dgate_ref[rows, :] = dhg * a_ref[rows, :] * sgv * (1.0 - sgv)

    vec = _full((1, CONV_CH))
    vshape = jax.ShapeDtypeStruct((1, CONV_CH), F32)
    return _pcall(
        body, comm=comm, name=name, grid=(1,),
        in_specs=[
            pl.BlockSpec((T, CONV_CH), lambda i: (0, a_blk)),
            pl.BlockSpec((T, CONV_CH), lambda i: (0, a_blk + 1)),
            _full((T, CONV_CH)), _full((CONV_WIDTH, CONV_CH)), vec, vec,
            pl.BlockSpec((T, CONV_CH), lambda i: (0, y_blk)),
        ],
        out_specs=[_full((T, CONV_CH)), _full((T, CONV_CH)), _full((CONV_PAD, CONV_CH)), vec, vec, vec],
        out_shape=[jax.ShapeDtypeStruct((T, CONV_CH), F32), jax.ShapeDtypeStruct((T, CONV_CH), F32),
                   jax.ShapeDtypeStruct((CONV_PAD, CONV_CH), F32), vshape, vshape, vshape],
        scratch_shapes=[pltpu.VMEM((T + CONV_PAD, CONV_CH), F32), pltpu.VMEM((T + CONV_PAD, CONV_CH), F32)],
        compiler_params=_params(("arbitrary",)),
    )(proj, proj, c, w_dw, g_ln, b_ln, dyc)


def _mem_kv(mkv_ref, gk_ref, lo, kn_s, vv_s):
    for p in range(MQ_COLS // LANES):
        cols = slice(p * LANES, (p + 1) * LANES)
        kn_s[:, cols] = _head_norm(mkv_ref[:, cols], gk_ref[...], lo).astype(BF16)
    vv_s[...] = mkv_ref[:, MQ_COLS:].astype(BF16)


def _mem_softmax(qh, kh):
    s = _dot_nt(qh, kh)
    m = jnp.max(s, axis=-1, keepdims=True)
    p = jnp.exp(s - m)
    return p / jnp.sum(p, axis=-1, keepdims=True)


def _mem_fwd(proj, mkv, gq2, gk2, name, comm=None):
    T = proj.shape[0]
    tt = ROW_TILE
    q_blk = (IN_COLS - MQ_COLS) // MQ_COLS

    def body(q_ref, mkv_ref, gq_ref, gk_ref, y_ref, kn_s, vv_s, qn_s):
        lo, _, _ = _head_masks()
        _mem_kv(mkv_ref, gk_ref, lo, kn_s, vv_s)
        for p in range(MQ_COLS // LANES):
            cols = slice(p * LANES, (p + 1) * LANES)
            qn_s[:, cols] = (_head_norm(q_ref[:, cols], gq_ref[...], lo) * (HEAD_DIM ** -0.5)).astype(BF16)
        for h in range(N_MEM_HEADS):
            hc = slice(h * HEAD_DIM, (h + 1) * HEAD_DIM)
            w = _mem_softmax(qn_s[:, hc], kn_s[:, hc])
            y_ref[:, hc] = _dot(w.astype(BF16), vv_s[:, hc]).astype(BF16)

    return _pcall(
        body, comm=comm, name=name, grid=(T // tt,),
        in_specs=[
            pl.BlockSpec((tt, MQ_COLS), lambda t: (t, q_blk)),
            pl.BlockSpec((MEM_LEN, 2 * MQ_COLS), lambda t: (0, 0)),
            pl.BlockSpec((1, LANES), lambda t: (0, 0)), pl.BlockSpec((1, LANES), lambda t: (0, 0)),
        ],
        out_specs=pl.BlockSpec((tt, MQ_COLS), lambda t: (t, 0)),
        out_shape=jax.ShapeDtypeStruct((T, MQ_COLS), BF16),
        scratch_shapes=[pltpu.VMEM((MEM_LEN, MQ_COLS), BF16), pltpu.VMEM((MEM_LEN, MQ_COLS), BF16),
                        pltpu.VMEM((tt, MQ_COLS), BF16)],
        compiler_params=_params(("arbitrary",)),
    )(proj, mkv, gq2, gk2)


def _mem_bwd(proj, mkv, gq2, gk2, dyc, name, comm=None):
    T = proj.shape[0]
    tt = ROW_TILE
    nt = T // tt
    q_blk = (IN_COLS - MQ_COLS) // MQ_COLS
    y_blk = (Q_COLS + CONV_CH) // MQ_COLS

    def body(q_ref, mkv_ref, gq_ref, gk_ref, dy_ref, dq_ref, dmkv_ref, dgq_ref, dgk_ref,
             kn_s, vv_s, qn_s, dqn_s, dkn_acc):
        t = pl.program_id(0)
        lo, _, _ = _head_masks()
        _mem_kv(mkv_ref, gk_ref, lo, kn_s, vv_s)

        @pl.when(t == 0)
        def _():
            dkn_acc[...] = jnp.zeros_like(dkn_acc)
            dmkv_ref[...] = jnp.zeros_like(dmkv_ref)
            dgq_ref[...] = jnp.zeros_like(dgq_ref)

        for p in range(MQ_COLS // LANES):
            cols = slice(p * LANES, (p + 1) * LANES)
            qn_s[:, cols] = (_head_norm(q_ref[:, cols], gq_ref[...], lo) * (HEAD_DIM ** -0.5)).astype(BF16)
        for h in range(N_MEM_HEADS):
            hc = slice(h * HEAD_DIM, (h + 1) * HEAD_DIM)
            vc = slice(MQ_COLS + h * HEAD_DIM, MQ_COLS + (h + 1) * HEAD_DIM)
            qh = qn_s[:, hc]
            w = _mem_softmax(qh, kn_s[:, hc])
            do = dy_ref[:, hc].astype(BF16)
            dmkv_ref[:, vc] += _dot_tn(w.astype(BF16), do)
            dw = _dot_nt(do, vv_s[:, hc])
            ds = (w * (dw - jnp.sum(w * dw, axis=-1, keepdims=True))).astype(BF16)
            dqn_s[:, hc] = _dot(ds, kn_s[:, hc])
            dkn_acc[:, hc] += _dot_tn(ds, qh)
        dgq = jnp.zeros((1, LANES), F32)
        for p in range(MQ_COLS // LANES):
            cols = slice(p * LANES, (p + 1) * LANES)
            dx, dgp = _head_norm_bwd(q_ref[:, cols], gq_ref[...], dqn_s[:, cols] * (HEAD_DIM ** -0.5), lo)
            dq_ref[:, cols] = dx
            dgq = dgq + jnp.sum(dgp, axis=0, keepdims=True)
        dgq_ref[...] += dgq

        @pl.when(t == nt - 1)
        def _():
            dgk = jnp.zeros((1, LANES), F32)
            for p in range(MQ_COLS // LANES):
                cols = slice(p * LANES, (p + 1) * LANES)
                dx, dgp = _head_norm_bwd(mkv_ref[:, cols], gk_ref[...], dkn_acc[:, cols], lo)
                dmkv_ref[:, cols] = dx
                dgk = dgk + jnp.sum(dgp, axis=0, keepdims=True)
            dgk_ref[...] = _fold_heads(dgk)
            dgq_ref[...] = _fold_heads(dgq_ref[...])

    vec = pl.BlockSpec((1, LANES), lambda t: (0, 0))
    vshape = jax.ShapeDtypeStruct((1, LANES), F32)
    return _pcall(
        body, comm=comm, name=name, grid=(nt,),
        in_specs=[
            pl.BlockSpec((tt, MQ_COLS), lambda t: (t, q_blk)),
            pl.BlockSpec((MEM_LEN, 2 * MQ_COLS), lambda t: (0, 0)),
            vec, vec,
            pl.BlockSpec((tt, MQ_COLS), lambda t: (t, y_blk)),
        ],
        out_specs=[pl.BlockSpec((tt, MQ_COLS), lambda t: (t, 0)),
                   pl.BlockSpec((MEM_LEN, 2 * MQ_COLS), lambda t: (0, 0)), vec, vec],
        out_shape=[jax.ShapeDtypeStruct((T, MQ_COLS), F32), jax.ShapeDtypeStruct((MEM_LEN, 2 * MQ_COLS), F32),
                   vshape, vshape],
        scratch_shapes=[pltpu.VMEM((MEM_LEN, MQ_COLS), BF16), pltpu.VMEM((MEM_LEN, MQ_COLS), BF16),
                        pltpu.VMEM((tt, MQ_COLS), BF16), pltpu.VMEM((tt, MQ_COLS), F32),
                        pltpu.VMEM((MEM_LEN, MQ_COLS), F32)],
        compiler_params=_params(("arbitrary",)),
    )(proj, mkv, gq2, gk2, dyc)


def _loss_head(y, target, name, comm=None):
    T, D = y.shape
    tt = ROW_TILE

    def body(y_ref, t_ref, dy_ref, loss_ref):
        t = pl.program_id(0)
        err = y_ref[...] - t_ref[...]
        dy_ref[...] = err * (1.0 / D)
        part = 0.5 * jnp.sum(jnp.mean(err * err, axis=-1, keepdims=True), axis=0, keepdims=True)

        @pl.when(t == 0)
        def _():
            loss_ref[...] = jnp.zeros_like(loss_ref)

        loss_ref[...] += jnp.broadcast_to(part, loss_ref.shape)

    tile = pl.BlockSpec((tt, D), lambda t: (t, 0))
    return _pcall(
        body, comm=comm, name=name, grid=(T // tt,),
        in_specs=[tile, tile],
        out_specs=[tile, pl.BlockSpec((1, LANES), lambda t: (0, 0))],
        out_shape=[jax.ShapeDtypeStruct((T, D), F32), jax.ShapeDtypeStruct((1, LANES), F32)],
        compiler_params=_params(("arbitrary",)),
    )(y, target)


def _adamw(w, g, m, v, name, comm=None):
    R, C = w.shape
    tr = next((r for r in (512, 352, 256, 128) if R % r == 0), R)

    def body(w_ref, g_ref, m_ref, v_ref, go_ref, d_ref, nm_ref, nv_ref):
        gv = g_ref[...]
        go_ref[...] = gv
        nm = ADAM_B1 * m_ref[...] + (1.0 - ADAM_B1) * gv
        nv = ADAM_B2 * v_ref[...] + (1.0 - ADAM_B2) * (gv * gv)
        m_hat = nm / (1.0 - ADAM_B1 ** ADAM_STEP)
        v_hat = nv / (1.0 - ADAM_B2 ** ADAM_STEP)
        d_ref[...] = -ADAM_LR * (m_hat / (jnp.sqrt(v_hat) + ADAM_EPS) + ADAM_WD * w_ref[...])
        nm_ref[...] = nm
        nv_ref[...] = nv

    tile = pl.BlockSpec((tr, C), lambda i: (i, 0))
    shape = jax.ShapeDtypeStruct((R, C), F32)
    return _pcall(
        body, comm=comm, name=name, grid=(R // tr,),
        in_specs=[tile] * 4, out_specs=[tile] * 4, out_shape=[shape] * 4,
        compiler_params=_params(("arbitrary",)),
    )(w, g, m, v)


def _mesh_pos():
    return lax.axis_index("x"), lax.axis_index("y"), lax.axis_index("c")


def _other_chips(x, y):
    return [(1 - x, y), (x, 1 - y), (1 - x, 1 - y)]


def _quarter(ref, layout, q, rows, cols):
    if layout == "cols":
        return ref.at[rows, pl.ds(pl.multiple_of(q * cols, LANES), cols)]
    return ref.at[q, rows, :]


def _gather_weights(shards, dtypes, layouts, name, comm=None):
    n = len(shards)
    all_rows = slice(None)
    remote = [i for i in range(n) if layouts[i] != "own"]
    split = [i for i in remote if layouts[i] != "whole"]

    def body(*refs):
        ins, outs = refs[:n], refs[n:2 * n]
        st32, st16 = refs[2 * n:3 * n], refs[3 * n:4 * n]
        in_sems, own_sems, send_sems, recv_sems, fwd_send_sems, fwd_recv_sems = refs[4 * n:]
        x, y, c = _mesh_pos()
        chip = 2 * x + y
        sibling = (x, y, 1 - c)
        chips = _other_chips(x, y)

        def half(i, which):
            if i not in split:
                return all_rows
            hr = shards[i].shape[0] // 2
            return pl.ds(pl.multiple_of(which * hr, 16), hr)

        def place(i, q, rows):
            if layouts[i] == "own":
                return outs[i]
            return _quarter(outs[i], layouts[i], q, rows, shards[i].shape[1])

        def ici(i, k, origin_chip, src):
            px, py = chips[k]
            return pltpu.make_async_remote_copy(
                src_ref=src, dst_ref=place(i, origin_chip, half(i, c)), send_sem=send_sems.at[i, k],
                recv_sem=recv_sems.at[i, k], device_id=(px, py, c), device_id_type=MESH)

        def forward(i, k, rows):
            px, py = chips[k]
            there = place(i, 2 * px + py, rows)
            return pltpu.make_async_remote_copy(
                src_ref=there, dst_ref=there, send_sem=fwd_send_sems.at[i, k],
                recv_sem=fwd_recv_sems.at[i, k], device_id=sibling, device_id_type=MESH)

        loads = [pltpu.make_async_copy(ins[i], st32[i], in_sems.at[i]) for i in range(n)]
        for cp in loads:
            cp.start()
        owns, sent = [], []
        for i in range(n):
            loads[i].wait()
            st16[i][...] = st32[i][...].astype(dtypes[i])
            own = pltpu.make_async_copy(st16[i], place(i, chip, all_rows), own_sems.at[i])
            own.start()
            owns.append(own)
            for k in range(3 if i in remote else 0):
                cp = ici(i, k, chip, st16[i].at[half(i, c)])
                cp.start()
                sent.append(cp)
        for i in remote:
            for k, (px, py) in enumerate(chips):
                ici(i, k, 2 * px + py, st16[i].at[half(i, c)]).wait_recv()
                if i in split:
                    cp = forward(i, k, half(i, c))
                    cp.start()
                    sent.append(cp)
        for i in split:
            for k in range(3):
                forward(i, k, half(i, 1 - c)).wait_recv()
        for cp in sent:
            cp.wait_send()
        for cp in owns:
            cp.wait()

    hbm = pl.BlockSpec(memory_space=pl.ANY)
    return _pcall(
        body, comm=comm, name=name,
        in_specs=[hbm] * n, out_specs=[hbm] * n,
        out_shape=[_gathered_shape(s.shape, d, lay) for s, d, lay in zip(shards, dtypes, layouts)],
        scratch_shapes=[pltpu.VMEM(s.shape, F32) for s in shards] + [pltpu.VMEM(s.shape, d) for s, d in zip(shards, dtypes)]
        + [pltpu.SemaphoreType.DMA((n,)), pltpu.SemaphoreType.DMA((n,)),
           pltpu.SemaphoreType.DMA((n, 3)), pltpu.SemaphoreType.DMA((n, 3)),
           pltpu.SemaphoreType.DMA((n, 3)), pltpu.SemaphoreType.DMA((n, 3))],
        compiler_params=pltpu.CompilerParams(vmem_limit_bytes=VMEM_LIMIT),
    )(*shards)


def _gathered_shape(quarter_shape, dtype, layout):
    R, C = quarter_shape
    shape = {"cols": (R, N_CHIPS * C), "own": (R, C)}.get(layout, (N_CHIPS, R, C))
    return jax.ShapeDtypeStruct(shape, dtype)


def _remote(src, dst, sems, j, device):
    return pltpu.make_async_remote_copy(src_ref=src, dst_ref=dst, send_sem=sems.at[2 * j], recv_sem=sems.at[2 * j + 1],
                                        device_id=device, device_id_type=MESH)


def _half_rows(rows, which):
    hr = rows // 2
    return pl.ds(pl.multiple_of(which * hr, 16), hr)


def _spread_plan(owns, layouts):
    def plan(ins, outs, sems):
        x, y, c = _mesh_pos()
        chip = 2 * x + y
        local, sends, recvs = [], [], []
        for i, (own, full) in enumerate(zip(ins, outs)):
            R, C = own.shape
            mine = _half_rows(R, c)
            local.append(pltpu.make_async_copy(own, _quarter(full, layouts[i], chip, slice(None), C), sems.at[8 * i + 6]))
            for k, (px, py) in enumerate(_other_chips(x, y)):
                src = own.at[mine, :]
                sends.append(_remote(src, _quarter(full, layouts[i], chip, mine, C), sems, 4 * i + k, (px, py, c)))
                recvs.append(_remote(src, _quarter(full, layouts[i], 2 * px + py, mine, C), sems, 4 * i + k, (px, py, c)))
        return local, sends, recvs

    shapes = [_gathered_shape(o.shape, o.dtype, lay) for o, lay in zip(owns, layouts)]
    return _Comm(owns, shapes, 8 * len(owns), plan)


def _forward_plan(fulls, quarter_shapes, layouts):
    def plan(ins, outs, sems):
        x, y, c = _mesh_pos()
        sends, recvs = [], []
        for i, full in enumerate(outs):
            R, C = quarter_shapes[i]
            for k, (px, py) in enumerate(_other_chips(x, y)):
                mine = _quarter(full, layouts[i], 2 * px + py, _half_rows(R, c), C)
                theirs = _quarter(full, layouts[i], 2 * px + py, _half_rows(R, 1 - c), C)
                sends.append(_remote(mine, mine, sems, 3 * i + k, (x, y, 1 - c)))
                recvs.append(_remote(mine, theirs, sems, 3 * i + k, (x, y, 1 - c)))
        return [], sends, recvs

    shapes = [jax.ShapeDtypeStruct(f.shape, f.dtype) for f in fulls]
    return _Comm(fulls, shapes, 6 * len(fulls), plan, aliases={i: i for i in range(len(fulls))})


def _swap_plan(grads):
    def plan(ins, outs, sems):
        x, y, c = _mesh_pos()
        sends = [_remote(g.at[:, _half_rows(g.shape[1], 1 - c), :], sib, sems, i, (x, y, 1 - c))
                 for i, (g, sib) in enumerate(zip(ins, outs))]
        return [], sends, sends

    shapes = [jax.ShapeDtypeStruct((N_CHIPS, g.shape[1] // 2, g.shape[2]), BF16) for g in grads]
    return _Comm(grads, shapes, 2 * len(grads), plan)


def _pair_sum(g, sib, name, comm=None):
    _, R, C = g.shape
    hr = R // 2

    def body(g_ref, sib_ref, o_ref):
        mine = _half_rows(R, lax.axis_index("c"))
        o_ref[0] = (g_ref[0, mine, :].astype(F32) + sib_ref[0].astype(F32)).astype(BF16)

    return _pcall(
        body, comm=comm, name=name, grid=(N_CHIPS,),
        in_specs=[pl.BlockSpec((1, R, C), lambda q: (q, 0, 0)), pl.BlockSpec((1, hr, C), lambda q: (q, 0, 0))],
        out_specs=pl.BlockSpec((1, hr, C), lambda q: (q, 0, 0)),
        out_shape=jax.ShapeDtypeStruct((N_CHIPS, hr, C), BF16),
        compiler_params=_params(("arbitrary",)),
    )(g, sib)


def _ici_plan(sums):
    def plan(ins, outs, sems):
        x, y, c = _mesh_pos()
        sends = []
        for i, (s, rcv) in enumerate(zip(ins, outs)):
            for k, (px, py) in enumerate(_other_chips(x, y)):
                sends.append(_remote(s.at[2 * px + py], rcv.at[k], sems, 3 * i + k, (px, py, c)))
        return [], sends, sends

    shapes = [jax.ShapeDtypeStruct((3,) + s.shape[1:], BF16) for s in sums]
    return _Comm(sums, shapes, 6 * len(sums), plan)


def _run_comm(comm, name):
    def body():
        pass

    _, landed = _pcall(body, comm=comm, name=name, grid=(1,), in_specs=[], out_specs=[], out_shape=[])()
    return landed


def _finish_quarter(s, rcv, name, comm=None):
    _, hr, C = s.shape

    def body(s_ref, rcv_ref, out_ref, sems):
        x, y, c = _mesh_pos()
        mine = _half_rows(2 * hr, c)
        acc = s_ref[2 * x + y].astype(F32)
        for k in range(3):
            acc = acc + rcv_ref[k].astype(F32)
        out_ref[mine, :] = acc
        back = _remote(out_ref.at[mine, :], out_ref.at[mine, :], sems, 0, (x, y, 1 - c))
        back.start()
        back.wait()

    vmem = pl.BlockSpec(memory_space=pltpu.VMEM)
    return _pcall(
        body, comm=comm, name=name,
        in_specs=[vmem, vmem], out_specs=vmem,
        out_shape=jax.ShapeDtypeStruct((2 * hr, C), F32),
        scratch_shapes=[pltpu.SemaphoreType.DMA((2,))],
        compiler_params=pltpu.CompilerParams(vmem_limit_bytes=VMEM_LIMIT),
    )(s, rcv)


def _allreduce_small(v, name, comm=None):
    R, C = v.shape
    n_dev = 8

    def body(v_ref, out_ref, buf, send_sems, recv_sems):
        x, y, c = _mesh_pos()
        me = 4 * x + 2 * y + c
        buf[me] = v_ref[...]
        peers = []
        for k in range(1, n_dev):
            kx, ky, kc = (k >> 2) & 1, (k >> 1) & 1, k & 1
            px = 1 - x if kx else x
            py = 1 - y if ky else y
            pc = 1 - c if kc else c
            peers.append((px, py, pc))
        sends = []
        for k, peer in enumerate(peers):
            cp = pltpu.make_async_remote_copy(
                src_ref=v_ref, dst_ref=buf.at[me], send_sem=send_sems.at[k], recv_sem=recv_sems.at[k],
                device_id=peer, device_id_type=MESH)
            cp.start()
            sends.append(cp)
        for k, (px, py, pc) in enumerate(peers):
            pltpu.make_async_remote_copy(
                src_ref=v_ref, dst_ref=buf.at[4 * px + 2 * py + pc], send_sem=send_sems.at[k],
                recv_sem=recv_sems.at[k], device_id=(px, py, pc), device_id_type=MESH).wait_recv()
        for cp in sends:
            cp.wait_send()
        acc = buf[0]
        for i in range(1, n_dev):
            acc = acc + buf[i]
        out_ref[...] = acc

    vmem = pl.BlockSpec(memory_space=pltpu.VMEM)
    return _pcall(
        body, comm=comm, name=name,
        in_specs=[vmem], out_specs=vmem,
        out_shape=jax.ShapeDtypeStruct((R, C), F32),
        scratch_shapes=[pltpu.VMEM((n_dev, R, C), F32),
                        pltpu.SemaphoreType.DMA((n_dev - 1,)), pltpu.SemaphoreType.DMA((n_dev - 1,))],
        compiler_params=pltpu.CompilerParams(vmem_limit_bytes=VMEM_LIMIT),
    )(v)


def _rope_tables(positions):
    half = ROPE_DIM // 2
    inv_freq = ROPE_THETA ** (-jnp.arange(half, dtype=F32) / half)
    ang = positions.astype(F32)[:, None] * inv_freq
    cos, sin = jnp.cos(ang), jnp.sin(ang)
    T = positions.shape[0]
    ones = jnp.ones((T, HEAD_DIM - ROPE_DIM), F32)
    c64 = jnp.concatenate([cos, cos, ones], axis=1)
    s64 = jnp.concatenate([-sin, sin, 0.0 * ones], axis=1)
    return jnp.tile(c64, (1, 2)), jnp.tile(s64, (1, 2))


def _local_step(x, mem, positions, target, small, first, own):
    cos, sin = _rope_tables(positions)
    two = lambda g: jnp.tile(g, (1, 2))
    gq2, gk2, gmq2, gmk2 = two(small["g_q"]), two(small["g_k"]), two(small["g_mq"]), two(small["g_mk"])
    mix_names = ["w_in", "w_mkv", "w_out"]
    mix_layouts = ["cols", "stack", "stack"]
    ffn2_names = ["wg2", "wu2", "wd2"]

    spread = _spread_plan([own[n] for n in mix_names] + [own["wg2"]], mix_layouts + ["stack"])
    (x1, h1, a1, b1), landed = _ffn_fwd(x, small["g_ffn1"], first["wg1"], first["wu1"], first["wd1"], "ffn1_fwd",
                                         comm=spread)
    half_wg2 = landed[3]
    passing = _forward_plan(landed[:3], [own[n].shape for n in mix_names], mix_layouts)
    hm, (w_in, w_mkv, w_out) = _rms_fwd(x1, small["g_mix"], "mix_norm", comm=passing)
    w_mkv = w_mkv.reshape(D_MODEL, 2 * MQ_COLS)
    w_out = w_out.reshape(D_MODEL, D_MODEL)
    proj = _mm_nn([hm], w_in, None, "in_proj")
    hmem = _rms_fwd(mem, small["g_mem"], "mem_norm")
    mkv = _mm_nn([hmem], w_mkv, None, "mem_proj")
    ya, (half_wu2,) = _attn_fwd(proj, cos, sin, gq2, gk2, small["sinks"], "swa_fwd",
                                comm=_spread_plan([own["wu2"]], ["stack"]))
    yc, cpre = _conv_fwd(proj, small["w_dw"], small["b_dw"], small["g_conv_ln"], small["b_conv_ln"], "conv_fwd")
    ym, (half_wd2,) = _mem_fwd(proj, mkv, gmq2, gmk2, "memattn_fwd", comm=_spread_plan([own["wd2"]], ["stack"]))
    passing = _forward_plan([half_wg2, half_wu2, half_wd2], [own[n].shape for n in ffn2_names], ["stack"] * 3)
    x2, (wg2, wu2, wd2) = _mm_nn([ya, yc, ym], w_out, x1, "out_proj", comm=passing)
    x3, h2, a2, b2 = _ffn_fwd(x2, small["g_ffn2"], wg2, wu2, wd2, "ffn2_fwd")
    dx3, loss = _loss_head(x3, target, "loss_head")

    def pair_sums(grads, sibs, names):
        return [_pair_sum(g, s, "pair_sum_" + n) for g, s, n in zip(grads, sibs, names)]

    dh2, dwg2, dwu2, dwd2 = _ffn_bwd(dx3, h2, a2, b2, wg2, wu2, wd2, "ffn2_bwd")
    (dx2, dg_ffn2), sibs = _rms_bwd(x2, small["g_ffn2"], dh2, dx3, "ffn2_norm_bwd", comm=_swap_plan([dwg2, dwu2, dwd2]))
    sums_ffn2 = pair_sums([dwg2, dwu2, dwd2], sibs, ffn2_names)
    dyc = _mm_nt([dx2], w_out, "out_proj_bwd")
    dw_out = _mm_tn([ya, yc, ym], [dx2], 1, "out_proj_wgrad").reshape(N_CHIPS, -1, D_MODEL)
    (dq, dk, dv, dgq, dgk, dsinks), rcv_ffn2 = _attn_bwd(proj, cos, sin, gq2, gk2, small["sinks"], dyc, "swa_bwd",
                                                          comm=_ici_plan(sums_ffn2))
    (da, dgate, dw_dw, db_dw, dg_ln, db_ln), sibs = _conv_bwd(
        proj, cpre, small["w_dw"], small["g_conv_ln"], small["b_conv_ln"], dyc, "conv_bwd", comm=_swap_plan([dw_out]))
    dmq, dmkv, dgmq, dgmk = _mem_bwd(proj, mkv, gmq2, gmk2, dyc, "memattn_bwd")
    sums_out = pair_sums([dw_out], sibs, ["w_out"])
    pieces = [dq, dk, dv, da, dgate, dmq]
    dhm, rcv_out = _mm_nt(pieces, w_in, "in_proj_bwd", comm=_ici_plan(sums_out))
    dw_in = _mm_tn([hm], pieces, N_CHIPS, "in_proj_wgrad")
    dhmem = _mm_nt([dmkv], w_mkv, "mem_proj_bwd")
    dw_mkv = _mm_tn([hmem], [dmkv], 1, "mem_proj_wgrad").reshape(N_CHIPS, -1, 2 * MQ_COLS)
    _, dg_mem = _rms_bwd(mem, small["g_mem"], dhmem, None, "mem_norm_bwd")
    (dx1, dg_mix), sibs = _rms_bwd(x1, small["g_mix"], dhm, dx2, "mix_norm_bwd", comm=_swap_plan([dw_in, dw_mkv]))
    sums_in = pair_sums([dw_in, dw_mkv], sibs, ["w_in", "w_mkv"])
    (dh1, dwg1, dwu1, dwd1), rcv_in = _ffn_bwd(dx1, h1, a1, b1, first["wg1"], first["wu1"], first["wd1"], "ffn1_bwd",
                                                comm=_ici_plan(sums_in))
    (dx, dg_ffn1), sibs = _rms_bwd(x, small["g_ffn1"], dh1, dx1, "ffn1_norm_bwd", comm=_swap_plan([dwg1, dwu1, dwd1]))
    sums_ffn1 = pair_sums([dwg1, dwu1, dwd1], sibs, ["wg1", "wu1", "wd1"])
    rcv_ffn1 = _run_comm(_ici_plan(sums_ffn1), "ffn1_grads_ici")

    sums = dict(zip(["wg1", "wu1", "wd1", "w_in", "w_mkv", "w_out"] + ffn2_names,
                    sums_ffn1 + sums_in + sums_out + sums_ffn2))
    rcvs = dict(zip(["wg1", "wu1", "wd1", "w_in", "w_mkv", "w_out"] + ffn2_names,
                    rcv_ffn1 + rcv_in + rcv_out + rcv_ffn2))
    big_grads = {n: _finish_quarter(sums[n], rcvs[n], "finish_" + n) for n in sums}
    small_grads = dict(
        g_ffn1=dg_ffn1, g_mix=dg_mix, g_q=dgq[:, :HEAD_DIM], g_k=dgk[:, :HEAD_DIM], sinks=dsinks[:, :N_Q_HEADS],
        w_dw=dw_dw[:CONV_WIDTH], b_dw=db_dw, g_conv_ln=dg_ln, b_conv_ln=db_ln, g_mem=dg_mem,
        g_mq=dgmq[:, :HEAD_DIM], g_mk=dgmk[:, :HEAD_DIM], g_ffn2=dg_ffn2)
    return loss, dx, big_grads, small_grads


SMALL_NAMES = ["g_ffn1", "g_mix", "g_q", "g_k", "sinks", "b_dw", "g_conv_ln", "b_conv_ln", "g_mem", "g_mq", "g_mk",
               "g_ffn2"]
PACK_COLS = 1024


def _pack(parts):
    flat = [p.reshape(-1) for p in parts]
    offs, o = [], 0
    for f in flat:
        offs.append(o)
        o += f.shape[0]
    rows = -(-o // (8 * PACK_COLS)) * 8
    pad = jnp.zeros((rows * PACK_COLS - o,), F32)
    return jnp.concatenate(flat + [pad]).reshape(rows, PACK_COLS), offs


def _unpack(packed, offs, shapes):
    flat = packed.reshape(-1)
    return [flat[o:o + math.prod(s)].reshape(s) for o, s in zip(offs, shapes)]


def kernel(x, mem, positions, g_ffn1, w_ffn1_gate, w_ffn1_up, w_ffn1_down, g_mix, w_in, g_q, g_k, sinks, w_dw, b_dw, g_conv_ln, b_conv_ln, g_mem, w_mem_kv, g_mq, g_mk, w_out, g_ffn2, w_ffn2_gate, w_ffn2_up, w_ffn2_down, loss_target, m_g_ffn1, m_w_ffn1_gate, m_w_ffn1_up, m_w_ffn1_down, m_g_mix, m_w_in, m_g_q, m_g_k, m_sinks, m_w_dw, m_b_dw, m_g_conv_ln, m_b_conv_ln, m_g_mem, m_w_mem_kv, m_g_mq, m_g_mk, m_w_out, m_g_ffn2, m_w_ffn2_gate, m_w_ffn2_up, m_w_ffn2_down, v_g_ffn1, v_w_ffn1_gate, v_w_ffn1_up, v_w_ffn1_down, v_g_mix, v_w_in, v_g_q, v_g_k, v_sinks, v_w_dw, v_b_dw, v_g_conv_ln, v_b_conv_ln, v_g_mem, v_w_mem_kv, v_g_mq, v_g_mk, v_w_out, v_g_ffn2, v_w_ffn2_gate, v_w_ffn2_up, v_w_ffn2_down):
    args = dict(locals())
    weight_names = ["g_ffn1", "w_ffn1_gate", "w_ffn1_up", "w_ffn1_down", "g_mix", "w_in", "g_q", "g_k", "sinks",
                    "w_dw", "b_dw", "g_conv_ln", "b_conv_ln", "g_mem", "w_mem_kv", "g_mq", "g_mk", "w_out", "g_ffn2",
                    "w_ffn2_gate", "w_ffn2_up", "w_ffn2_down"]
    big_names = ["w_ffn1_gate", "w_ffn1_up", "w_ffn1_down", "w_in", "w_mem_kv", "w_out",
                 "w_ffn2_gate", "w_ffn2_up", "w_ffn2_down"]
    short = dict(w_ffn1_gate="wg1", w_ffn1_up="wu1", w_ffn1_down="wd1", w_in="w_in", w_mem_kv="w_mkv",
                 w_out="w_out", w_ffn2_gate="wg2", w_ffn2_up="wu2", w_ffn2_down="wd2")

    transposed = ("w_ffn1_gate", "w_ffn1_up", "w_ffn2_gate", "w_ffn2_up")

    def quarter(a, n):
        return jnp.swapaxes(a, 1, 2)[0] if n in transposed else a[0]

    def unquarter(a, n):
        return jnp.swapaxes(a[None], 1, 2) if n in transposed else a[None]

    shards = [quarter(args[n], n) for n in big_names]
    layouts = ["stack" if short[n] in ("wg1", "wu1", "wd1") else "own" for n in big_names]
    gathered = _gather_weights(shards + [w_dw[0]], [BF16] * len(shards) + [F32], layouts + ["whole"],
                               "gather_first")
    first = {short[n]: gathered[i] for i, n in enumerate(big_names) if layouts[i] == "stack"}
    own = {short[n]: gathered[i] for i, n in enumerate(big_names) if layouts[i] == "own"}
    small = {n: args[n] for n in SMALL_NAMES}
    small["w_dw"] = jnp.transpose(gathered[-1], (1, 0, 2)).reshape(CONV_WIDTH, CONV_CH)

    loss, dx, big_grads, small_grads = _local_step(x[0], mem[0], positions[0], loss_target[0], small, first, own)

    small_order = SMALL_NAMES + ["w_dw"]
    packed, offs = _pack([small_grads[n] for n in small_order] + [loss[:, :1]])
    total = _allreduce_small(packed, "allreduce_small")
    shapes = [small_grads[n].shape for n in small_order] + [(1, 1)]
    summed = dict(zip(small_order + ["loss"], _unpack(total, offs, shapes)))
    chip = 2 * lax.axis_index("x") + lax.axis_index("y")
    dw_dw_full = summed.pop("w_dw")
    loss_out = summed.pop("loss").reshape(())

    grads = {n: summed[n] for n in SMALL_NAMES}
    grads["w_dw"] = lax.dynamic_slice_in_dim(dw_dw_full, chip * (CONV_CH // N_CHIPS), CONV_CH // N_CHIPS, axis=1)
    for n in big_names:
        grads[n] = big_grads[short[n]]

    delta, new_m, new_v = {}, {}, {}
    for n in big_names:
        g, d, nm, nv = _adamw(quarter(args[n], n), grads[n], quarter(args["m_" + n], n), quarter(args["v_" + n], n),
                              "adamw_" + short[n])
        grads[n], delta[n], new_m[n], new_v[n] = (unquarter(a, n) for a in (g, d, nm, nv))
    tiny = SMALL_NAMES + ["w_dw"]
    pw, poffs = _pack([args[n] for n in tiny])
    pg, _ = _pack([grads[n] for n in tiny])
    pm, _ = _pack([args["m_" + n] for n in tiny])
    pv, _ = _pack([args["v_" + n] for n in tiny])
    _, pd, pnm, pnv = _adamw(pw, pg, pm, pv, "adamw_small")
    tshapes = [args[n].shape for n in tiny]
    for store, packed_out in ((delta, pd), (new_m, pnm), (new_v, pnv)):
        for n, val in zip(tiny, _unpack(packed_out, poffs, tshapes)):
            store[n] = val

    def shaped(n, v):
        return v.reshape(args[n].shape)

    return (loss_out, dx[None],
            *[shaped(n, grads[n]) for n in weight_names],
            *[shaped(n, delta[n]) for n in weight_names],
            *[shaped(n, new_m[n]) for n in weight_names],
            *[shaped(n, new_v[n]) for n in weight_names])
```

```python
import functools
import math

import jax
import jax.numpy as jnp
from jax import lax
from jax.experimental import pallas as pl
from jax.experimental.pallas import tpu as pltpu

F32 = jnp.float32
BF16 = jnp.bfloat16

D_MODEL = 1024
SEQ = 2048
MEM_LEN = 256
HEAD_DIM = 64
N_Q_HEADS = 8
N_KV_HEADS = 2
Q_PER_KV = 4
N_MEM_HEADS = 4
BLOCK = 128
CONV_CH = 256
CONV_WIDTH = 31
ROPE_THETA = 500000.0
ROPE_DIM = 16
D_FF = 2816
EPS = 1e-6
Q_COLS = 512
KV_COLS = 128
MQ_COLS = 256
IN_COLS = 1536

N_CHIPS = 4
FF_CHUNK = D_FF // N_CHIPS
IN_CHUNK = IN_COLS // N_CHIPS

ADAM_LR = 0.001
ADAM_B1 = 0.9
ADAM_B2 = 0.999
ADAM_EPS = 1e-08
ADAM_WD = 0.01
ADAM_STEP = 10

LANES = 128
VMEM_LIMIT = 56 * 1024 * 1024
ROW_TILE = 512
MESH = pl.DeviceIdType.MESH
NEG = -1e30


class _Comm:
    def __init__(self, ins, out_shapes, n_sems, plan, aliases=None):
        self.ins, self.out_shapes, self.n_sems, self.plan = list(ins), list(out_shapes), n_sems, plan
        self.aliases = aliases or {}


def _pcall(body, comm=None, **kw):
    if comm is None:
        return pl.pallas_call(body, **kw)
    grid = kw["grid"]
    in_specs = list(kw["in_specs"])
    single = not isinstance(kw["out_shape"], (list, tuple))
    out_specs = [kw["out_specs"]] if single else list(kw["out_specs"])
    out_shape = [kw["out_shape"]] if single else list(kw["out_shape"])
    scratch = list(kw.get("scratch_shapes", ()))
    n_in, n_out, n_scr = len(in_specs), len(out_shape), len(scratch)
    n_ci, n_co = len(comm.ins), len(comm.out_shapes)

    def wrapped(*refs):
        o = 0
        parts = []
        for cnt in (n_in, n_ci, n_out, n_co, n_scr):
            parts.append(refs[o:o + cnt])
            o += cnt
        ins, c_ins, outs, c_outs, scr = parts
        sems = refs[o]
        first = last = None
        for d, size in enumerate(grid):
            at0, at_end = pl.program_id(d) == 0, pl.program_id(d) == size - 1
            first = at0 if first is None else first & at0
            last = at_end if last is None else last & at_end

        @pl.when(first)
        def _():
            local, sends, _ = comm.plan(c_ins, c_outs, sems, False)
            for cp in local + sends:
                cp.start()

        body(*ins, *outs, *scr)

        @pl.when(last)
        def _():
            local, sends, recvs = comm.plan(c_ins, c_outs, sems, True)
            for cp in recvs:
                cp.wait_recv()
            for cp in sends:
                cp.wait_send()
            for cp in local:
                cp.wait()

    hbm = pl.BlockSpec(memory_space=pl.ANY)
    kw = dict(kw, in_specs=in_specs + [hbm] * n_ci, out_specs=out_specs + [hbm] * n_co,
              out_shape=out_shape + comm.out_shapes,
              scratch_shapes=scratch + [pltpu.SemaphoreType.DMA((comm.n_sems,))])
    if comm.aliases:
        kw["input_output_aliases"] = {n_in + i: n_out + o for i, o in comm.aliases.items()}
    call = pl.pallas_call(wrapped, **kw)

    def run(*args):
        res = call(*args, *comm.ins)
        return (res[0] if single else list(res[:n_out])), list(res[n_out:])

    return run


def _params(sem=None):
    return pltpu.CompilerParams(dimension_semantics=sem, vmem_limit_bytes=VMEM_LIMIT)


def _dot(a, b):
    return jnp.dot(a, b, preferred_element_type=F32)


def _dot_nt(a, b):
    return lax.dot_general(a, b, (((1,), (1,)), ((), ())), preferred_element_type=F32)


def _dot_tn(a, b):
    return lax.dot_general(a, b, (((0,), (0,)), ((), ())), preferred_element_type=F32)


def _sigmoid(x):
    return 1.0 / (1.0 + jnp.exp(-x))


def _full(shape):
    n = len(shape)
    return pl.BlockSpec(shape, lambda *_: (0,) * n)


def _ffn_fwd(x, g, wg, wu, wd, name, comm=None):
    T, D = x.shape
    nt = T // ROW_TILE

    def body(x_ref, g_ref, wg_ref, wu_ref, wd_ref, xo_ref, h_ref, a_ref, b_ref):
        j = pl.program_id(1)

        @pl.when(j == 0)
        def _():
            xv = x_ref[...]
            rstd = lax.rsqrt(jnp.mean(xv * xv, axis=-1, keepdims=True) + EPS)
            h_ref[...] = (xv * rstd * g_ref[...]).astype(BF16)
            xo_ref[...] = jnp.zeros_like(xo_ref)

        h = h_ref[...]
        a = _dot_nt(h, wg_ref[0])
        b = _dot_nt(h, wu_ref[0])
        a_ref[0] = a.astype(BF16)
        b_ref[0] = b.astype(BF16)
        s = (a * _sigmoid(a)) * b
        xo_ref[...] += _dot(s.astype(BF16), wd_ref[0])

        @pl.when(j == N_CHIPS - 1)
        def _():
            xo_ref[...] = x_ref[...] + 0.5 * xo_ref[...]

    return _pcall(
        body, comm=comm, name=name, grid=(nt, N_CHIPS),
        in_specs=[
            pl.BlockSpec((ROW_TILE, D), lambda t, j: (t, 0)),
            pl.BlockSpec((1, D), lambda t, j: (0, 0)),
            pl.BlockSpec((1, FF_CHUNK, D), lambda t, j: (j, 0, 0)),
            pl.BlockSpec((1, FF_CHUNK, D), lambda t, j: (j, 0, 0)),
            pl.BlockSpec((1, FF_CHUNK, D), lambda t, j: (j, 0, 0)),
        ],
        out_specs=[
            pl.BlockSpec((ROW_TILE, D), lambda t, j: (t, 0)),
            pl.BlockSpec((ROW_TILE, D), lambda t, j: (t, 0)),
            pl.BlockSpec((1, ROW_TILE, FF_CHUNK), lambda t, j: (j, t, 0)),
            pl.BlockSpec((1, ROW_TILE, FF_CHUNK), lambda t, j: (j, t, 0)),
        ],
        out_shape=[
            jax.ShapeDtypeStruct((T, D), F32),
            jax.ShapeDtypeStruct((T, D), BF16),
            jax.ShapeDtypeStruct((N_CHIPS, T, FF_CHUNK), BF16),
            jax.ShapeDtypeStruct((N_CHIPS, T, FF_CHUNK), BF16),
        ],
        compiler_params=_params(("arbitrary", "arbitrary")),
    )(x, g, wg, wu, wd)


def _ffn_bwd(dxo, h, a, b, wg, wu, wd, name, comm=None):
    T, D = dxo.shape
    tt = ROW_TILE
    nt = T // tt

    def body(dxo_ref, h_ref, a_ref, b_ref, wg_ref, wu_ref, wd_ref,
             dh_hbm, dwg_ref, dwu_ref, dwd_ref, dh_acc, acc_g, acc_u, acc_d):
        j = pl.program_id(0)
        t = pl.program_id(1)
        do = (0.5 * dxo_ref[...]).astype(BF16)
        av = a_ref[0].astype(F32)
        bv = b_ref[0].astype(F32)
        sig = _sigmoid(av)
        sa = av * sig
        ds = _dot_nt(do, wd_ref[0])
        da = (ds * bv * (sig * (1.0 + av * (1.0 - sig)))).astype(BF16)
        db = (ds * sa).astype(BF16)
        hv = h_ref[...]
        rows = pl.ds(pl.multiple_of(t * tt, tt), tt)

        @pl.when(j == 0)
        def _():
            dh_acc[rows, :] = jnp.zeros((tt, D), F32)

        @pl.when(t == 0)
        def _():
            acc_g[...] = jnp.zeros_like(acc_g)
            acc_u[...] = jnp.zeros_like(acc_u)
            acc_d[...] = jnp.zeros_like(acc_d)

        acc_d[...] += _dot_tn((sa * bv).astype(BF16), do)
        acc_g[...] += _dot_tn(da, hv)
        acc_u[...] += _dot_tn(db, hv)
        dh_acc[rows, :] += _dot(da, wg_ref[0]) + _dot(db, wu_ref[0])

        @pl.when(t == nt - 1)
        def _():
            dwg_ref[0] = acc_g[...].astype(BF16)
            dwu_ref[0] = acc_u[...].astype(BF16)
            dwd_ref[0] = acc_d[...].astype(BF16)

        @pl.when((t == nt - 1) & (j == N_CHIPS - 1))
        def _():
            pltpu.sync_copy(dh_acc, dh_hbm)

    return _pcall(
        body, comm=comm, name=name, grid=(N_CHIPS, nt),
        in_specs=[
            pl.BlockSpec((tt, D), lambda j, t: (t, 0)),
            pl.BlockSpec((tt, D), lambda j, t: (t, 0)),
            pl.BlockSpec((1, tt, FF_CHUNK), lambda j, t: (j, t, 0)),
            pl.BlockSpec((1, tt, FF_CHUNK), lambda j, t: (j, t, 0)),
            pl.BlockSpec((1, FF_CHUNK, D), lambda j, t: (j, 0, 0)),
            pl.BlockSpec((1, FF_CHUNK, D), lambda j, t: (j, 0, 0)),
            pl.BlockSpec((1, FF_CHUNK, D), lambda j, t: (j, 0, 0)),
        ],
        out_specs=[
            pl.BlockSpec(memory_space=pl.ANY),
            pl.BlockSpec((1, FF_CHUNK, D), lambda j, t: (j, 0, 0)),
            pl.BlockSpec((1, FF_CHUNK, D), lambda j, t: (j, 0, 0)),
            pl.BlockSpec((1, FF_CHUNK, D), lambda j, t: (j, 0, 0)),
        ],
        out_shape=[
            jax.ShapeDtypeStruct((T, D), F32),
            jax.ShapeDtypeStruct((N_CHIPS, FF_CHUNK, D), BF16),
            jax.ShapeDtypeStruct((N_CHIPS, FF_CHUNK, D), BF16),
            jax.ShapeDtypeStruct((N_CHIPS, FF_CHUNK, D), BF16),
        ],
        scratch_shapes=[
            pltpu.VMEM((T, D), F32),
            pltpu.VMEM((FF_CHUNK, D), F32),
            pltpu.VMEM((FF_CHUNK, D), F32),
            pltpu.VMEM((FF_CHUNK, D), F32),
        ],
        compiler_params=_params(("arbitrary", "arbitrary")),
    )(dxo, h, a, b, wg, wu, wd)


def _rms_fwd(x, g, name, comm=None):
    T, D = x.shape
    tt = min(ROW_TILE, T)

    def body(x_ref, g_ref, h_ref):
        xv = x_ref[...]
        rstd = lax.rsqrt(jnp.mean(xv * xv, axis=-1, keepdims=True) + EPS)
        h_ref[...] = (xv * rstd * g_ref[...]).astype(BF16)

    return _pcall(
        body, comm=comm, name=name, grid=(T // tt,),
        in_specs=[pl.BlockSpec((tt, D), lambda t: (t, 0)), pl.BlockSpec((1, D), lambda t: (0, 0))],
        out_specs=pl.BlockSpec((tt, D), lambda t: (t, 0)),
        out_shape=jax.ShapeDtypeStruct((T, D), BF16),
        compiler_params=_params(("arbitrary",)),
    )(x, g)


def _rms_bwd(x, g, dh, dres, name, comm=None):
    T, D = x.shape
    tt = min(ROW_TILE, T)
    has_res = dres is not None

    def body(*refs):
        if has_res:
            x_ref, g_ref, dh_ref, dres_ref, dx_ref, dg_ref = refs
        else:
            x_ref, g_ref, dh_ref, dx_ref, dg_ref = refs
        t = pl.program_id(0)
        xv = x_ref[...]
        rstd = lax.rsqrt(jnp.mean(xv * xv, axis=-1, keepdims=True) + EPS)
        xhat = xv * rstd
        dhv = dh_ref[...]
        gy = dhv * g_ref[...]
        dx = rstd * (gy - xhat * jnp.mean(gy * xhat, axis=-1, keepdims=True))
        if has_res:
            dx = dx + dres_ref[...]
        dx_ref[...] = dx
        part = jnp.sum(dhv * xhat, axis=0, keepdims=True)

        @pl.when(t == 0)
        def _():
            dg_ref[...] = part

        @pl.when(t > 0)
        def _():
            dg_ref[...] += part

    tile = pl.BlockSpec((tt, D), lambda t: (t, 0))
    vec = pl.BlockSpec((1, D), lambda t: (0, 0))
    args = [x, g, dh] + ([dres] if has_res else [])
    return _pcall(
        body, comm=comm, name=name, grid=(T // tt,),
        in_specs=[tile, vec, tile] + ([tile] if has_res else []),
        out_specs=[tile, vec],
        out_shape=[jax.ShapeDtypeStruct((T, D), F32), jax.ShapeDtypeStruct((1, D), F32)],
        compiler_params=_params(("arbitrary",)),
    )(*args)


def _mm_nn(a_list, b, res, name, comm=None):
    T = a_list[0].shape[0]
    K, N = b.shape
    tt = min(ROW_TILE, T)
    ks = [a.shape[1] for a in a_list]
    na = len(a_list)
    has_res = res is not None

    def body(*refs):
        a_refs = refs[:na]
        b_ref = refs[na]
        o_ref = refs[-1]
        acc = res_v = None
        off = 0
        for a_ref, k in zip(a_refs, ks):
            part = _dot(a_ref[...].astype(BF16), b_ref[off:off + k, :])
            acc = part if acc is None else acc + part
            off += k
        if has_res:
            acc = refs[na + 1][...] + acc
        o_ref[...] = acc

    in_specs = [pl.BlockSpec((tt, k), lambda t: (t, 0)) for k in ks] + [pl.BlockSpec((K, N), lambda t: (0, 0))]
    args = list(a_list) + [b]
    if has_res:
        in_specs.append(pl.BlockSpec((tt, N), lambda t: (t, 0)))
        args.append(res)
    return _pcall(
        body, comm=comm, name=name, grid=(T // tt,), in_specs=in_specs,
        out_specs=pl.BlockSpec((tt, N), lambda t: (t, 0)),
        out_shape=jax.ShapeDtypeStruct((T, N), F32),
        compiler_params=_params(("arbitrary",)),
    )(*args)


def _mm_nt(a_list, b, name, comm=None):
    T = a_list[0].shape[0]
    K, N = b.shape
    tt = min(ROW_TILE, T)
    ns = [a.shape[1] for a in a_list]
    na = len(a_list)

    def body(*refs):
        b_ref = refs[na]
        o_ref = refs[-1]
        acc = None
        off = 0
        for a_ref, n in zip(refs[:na], ns):
            part = _dot_nt(a_ref[...].astype(BF16), b_ref[:, off:off + n])
            acc = part if acc is None else acc + part
            off += n
        o_ref[...] = acc

    return _pcall(
        body, comm=comm, name=name, grid=(T // tt,),
        in_specs=[pl.BlockSpec((tt, n), lambda t: (t, 0)) for n in ns] + [pl.BlockSpec((K, N), lambda t: (0, 0))],
        out_specs=pl.BlockSpec((tt, K), lambda t: (t, 0)),
        out_shape=jax.ShapeDtypeStruct((T, K), F32),
        compiler_params=_params(("arbitrary",)),
    )(*a_list, b)


def _mm_tn(a_list, b_list, col_chunks, name, comm=None):
    T = a_list[0].shape[0]
    tt = min(ROW_TILE, T)
    nt = T // tt
    ms = [a.shape[1] for a in a_list]
    ns = [b.shape[1] for b in b_list]
    M, N = sum(ms), sum(ns)
    na, nb = len(a_list), len(b_list)
    cw = N // col_chunks

    def body(*refs):
        a_refs, b_refs = refs[:na], refs[na:na + nb]
        o_ref, acc = refs[na + nb], refs[na + nb + 1]
        t = pl.program_id(0)

        @pl.when(t == 0)
        def _():
            acc[...] = jnp.zeros_like(acc)

        ro = 0
        for a_ref, m in zip(a_refs, ms):
            av = a_ref[...].astype(BF16)
            co = 0
            for b_ref, n in zip(b_refs, ns):
                acc[ro:ro + m, co:co + n] += _dot_tn(av, b_ref[...].astype(BF16))
                co += n
            ro += m

        @pl.when(t == nt - 1)
        def _():
            if col_chunks == 1:
                o_ref[...] = acc[...].astype(BF16)
            else:
                for q in range(col_chunks):
                    o_ref[q] = acc[:, q * cw:(q + 1) * cw].astype(BF16)

    out_shape = (M, N) if col_chunks == 1 else (col_chunks, M, cw)
    return _pcall(
        body, comm=comm, name=name, grid=(nt,),
        in_specs=[pl.BlockSpec((tt, m), lambda t: (t, 0)) for m in ms]
        + [pl.BlockSpec((tt, n), lambda t: (t, 0)) for n in ns],
        out_specs=_full(out_shape),
        out_shape=jax.ShapeDtypeStruct(out_shape, BF16),
        scratch_shapes=[pltpu.VMEM((M, N), F32)],
        compiler_params=_params(("arbitrary",)),
    )(*a_list, *b_list)


def _head_masks():
    lane = lax.broadcasted_iota(jnp.int32, (1, LANES), 1)
    l64 = lane & (HEAD_DIM - 1)
    return lane < HEAD_DIM, l64 < ROPE_DIM // 2, l64 < ROPE_DIM


def _head_mean(v, lo):
    s_lo = jnp.sum(jnp.where(lo, v, 0.0), axis=-1, keepdims=True)
    s_hi = jnp.sum(jnp.where(lo, 0.0, v), axis=-1, keepdims=True)
    return jnp.where(lo, s_lo, s_hi) * (1.0 / HEAD_DIM)


def _rope_swap(v, first, rot):
    up = pltpu.roll(v, LANES - ROPE_DIM // 2, 1)
    down = pltpu.roll(v, ROPE_DIM // 2, 1)
    return jnp.where(first, up, jnp.where(rot, down, 0.0))


def _head_norm(x, g, lo):
    rstd = lax.rsqrt(_head_mean(x * x, lo) + EPS)
    return x * rstd * g


def _head_norm_bwd(x, g, dy, lo):
    rstd = lax.rsqrt(_head_mean(x * x, lo) + EPS)
    xhat = x * rstd
    gy = dy * g
    dx = rstd * (gy - xhat * _head_mean(gy * xhat, lo))
    return dx, dy * xhat


def _rope(xn, cos, sin, first, rot):
    return xn * cos + _rope_swap(xn, first, rot) * sin


def _rope_bwd(dy, cos, sin, first, rot):
    return dy * cos + _rope_swap(dy * sin, first, rot)


def _fold_heads(v):
    return v + pltpu.roll(v, HEAD_DIM, 1)


ATT_ROWS = 256


def _attn_prepare(q_ref, k_ref, v_ref, cos_ref, sin_ref, gq_ref, gk_ref, qs, ks, vs):
    T = q_ref.shape[0]
    lo, first, rot = _head_masks()
    ks[0:BLOCK, :] = jnp.zeros((BLOCK, KV_COLS), BF16)
    vs[0:BLOCK, :] = jnp.zeros((BLOCK, KV_COLS), BF16)

    def step(i, _):
        r0 = pl.multiple_of(i * ATT_ROWS, ATT_ROWS)
        rows = pl.ds(r0, ATT_ROWS)
        prow = pl.ds(r0 + BLOCK, ATT_ROWS)
        cos, sin = cos_ref[rows, :], sin_ref[rows, :]
        for p in range(Q_COLS // LANES):
            cols = slice(p * LANES, (p + 1) * LANES)
            xr = _rope(_head_norm(q_ref[rows, cols], gq_ref[...], lo), cos, sin, first, rot)
            qs[rows, cols] = (xr * (HEAD_DIM ** -0.5)).astype(BF16)
        kr = _rope(_head_norm(k_ref[rows, :], gk_ref[...], lo), cos, sin, first, rot)
        ks[prow, :] = kr.astype(BF16)
        vs[prow, :] = v_ref[rows, :].astype(BF16)
        return 0

    lax.fori_loop(0, T // ATT_ROWS, step, 0)


def _attn_scores(qh, kw, blk, sink):
    s = _dot_nt(qh, kw)
    qi = lax.broadcasted_iota(jnp.int32, (BLOCK, 2 * BLOCK), 0) + BLOCK
    ki = lax.broadcasted_iota(jnp.int32, (BLOCK, 2 * BLOCK), 1)
    rel = qi - ki
    valid = (rel >= 0) & (rel < BLOCK) & ((blk > 0) | (ki >= BLOCK))
    s = jnp.where(valid, s, NEG)
    m = jnp.maximum(jnp.max(s, axis=-1, keepdims=True), sink)
    p = jnp.exp(s - m)
    e_sink = jnp.exp(sink - m)
    inv = 1.0 / (jnp.sum(p, axis=-1, keepdims=True) + e_sink)
    return p * inv, e_sink * inv


def _attn_fwd(proj, cos, sin, gq2, gk2, sinks, name, comm=None):
    T = proj.shape[0]
    nb = T // BLOCK

    def body(q_ref, k_ref, v_ref, cos_ref, sin_ref, gq_ref, gk_ref, sink_ref, y_ref, qs, ks, vs):
        _attn_prepare(q_ref, k_ref, v_ref, cos_ref, sin_ref, gq_ref, gk_ref, qs, ks, vs)

        def blk_step(blk, _):
            r0 = pl.multiple_of(blk * BLOCK, BLOCK)
            for g in range(N_KV_HEADS):
                gc = slice(g * HEAD_DIM, (g + 1) * HEAD_DIM)
                kw = ks[pl.ds(r0, 2 * BLOCK), gc]
                vw = vs[pl.ds(r0, 2 * BLOCK), gc]
                for r in range(Q_PER_KV):
                    h = g * Q_PER_KV + r
                    hc = slice(h * HEAD_DIM, (h + 1) * HEAD_DIM)
                    w, _ws = _attn_scores(qs[pl.ds(r0, BLOCK), hc], kw, blk, sink_ref[0, h])
                    y_ref[pl.ds(r0, BLOCK), hc] = _dot(w.astype(BF16), vw).astype(BF16)
            return 0

        lax.fori_loop(0, nb, blk_step, 0)

    return _pcall(
        body, comm=comm, name=name, grid=(1,),
        in_specs=[
            pl.BlockSpec((T, Q_COLS), lambda i: (0, 0)),
            pl.BlockSpec((T, KV_COLS), lambda i: (0, Q_COLS // KV_COLS)),
            pl.BlockSpec((T, KV_COLS), lambda i: (0, Q_COLS // KV_COLS + 1)),
            _full((T, LANES)), _full((T, LANES)), _full((1, LANES)), _full((1, LANES)),
            pl.BlockSpec(memory_space=pltpu.SMEM),
        ],
        out_specs=_full((T, Q_COLS)),
        out_shape=jax.ShapeDtypeStruct((T, Q_COLS), BF16),
        scratch_shapes=[
            pltpu.VMEM((T, Q_COLS), BF16),
            pltpu.VMEM((T + BLOCK, KV_COLS), BF16),
            pltpu.VMEM((T + BLOCK, KV_COLS), BF16),
        ],
        compiler_params=_params(("arbitrary",)),
    )(proj, proj, proj, cos, sin, gq2, gk2, sinks)


def _attn_bwd(proj, cos, sin, gq2, gk2, sinks, dyc, name, comm=None):
    T = proj.shape[0]
    nb = T // BLOCK

    def body(q_ref, k_ref, v_ref, cos_ref, sin_ref, gq_ref, gk_ref, sink_ref, dy_ref,
             dq_ref, dk_ref, dv_ref, dgq_ref, dgk_ref, dsink_ref, qs, ks, vs, dqs, dks, dvs):
        _attn_prepare(q_ref, k_ref, v_ref, cos_ref, sin_ref, gq_ref, gk_ref, qs, ks, vs)
        dks[...] = jnp.zeros_like(dks)
        dvs[...] = jnp.zeros_like(dvs)
        lane = lax.broadcasted_iota(jnp.int32, (1, LANES), 1)

        def blk_step(blk, dsink):
            r0 = pl.multiple_of(blk * BLOCK, BLOCK)
            win = pl.ds(r0, 2 * BLOCK)
            for g in range(N_KV_HEADS):
                gc = slice(g * HEAD_DIM, (g + 1) * HEAD_DIM)
                kw = ks[win, gc]
                vw = vs[win, gc]
                for r in range(Q_PER_KV):
                    h = g * Q_PER_KV + r
                    hc = slice(h * HEAD_DIM, (h + 1) * HEAD_DIM)
                    qh = qs[pl.ds(r0, BLOCK), hc]
                    w, w_sink = _attn_scores(qh, kw, blk, sink_ref[0, h])
                    do = dy_ref[pl.ds(r0, BLOCK), hc].astype(BF16)
                    dvs[win, gc] += _dot_tn(w.astype(BF16), do)
                    dw = _dot_nt(do, vw)
                    delta = jnp.sum(w * dw, axis=-1, keepdims=True)
                    ds = (w * (dw - delta)).astype(BF16)
                    dsink = dsink + jnp.where(lane == h, -jnp.sum(w_sink * delta, axis=0, keepdims=True), 0.0)
                    dqs[pl.ds(r0, BLOCK), hc] = _dot(ds, kw)
                    dks[win, gc] += _dot_tn(ds, qh)
            return dsink

        dsink_ref[...] = lax.fori_loop(0, nb, blk_step, jnp.zeros((1, LANES), F32))

        lo, first, rot = _head_masks()

        def step(i, carry):
            dgq, dgk = carry
            r0 = pl.multiple_of(i * ATT_ROWS, ATT_ROWS)
            rows = pl.ds(r0, ATT_ROWS)
            prow = pl.ds(r0 + BLOCK, ATT_ROWS)
            cos, sin = cos_ref[rows, :], sin_ref[rows, :]
            for p in range(Q_COLS // LANES):
                cols = slice(p * LANES, (p + 1) * LANES)
                dxn = _rope_bwd(dqs[rows, cols] * (HEAD_DIM ** -0.5), cos, sin, first, rot)
                dx, dgp = _head_norm_bwd(q_ref[rows, cols], gq_ref[...], dxn, lo)
                dq_ref[rows, cols] = dx
                dgq = dgq + jnp.sum(dgp, axis=0, keepdims=True)
            dkn = _rope_bwd(dks[prow, :], cos, sin, first, rot)
            dx, dgp = _head_norm_bwd(k_ref[rows, :], gk_ref[...], dkn, lo)
            dk_ref[rows, :] = dx
            dgk = dgk + jnp.sum(dgp, axis=0, keepdims=True)
            dv_ref[rows, :] = dvs[prow, :]
            return dgq, dgk

        zero = jnp.zeros((1, LANES), F32)
        dgq, dgk = lax.fori_loop(0, T // ATT_ROWS, step, (zero, zero))
        dgq_ref[...] = _fold_heads(dgq)
        dgk_ref[...] = _fold_heads(dgk)

    vec = jax.ShapeDtypeStruct((1, LANES), F32)
    return _pcall(
        body, comm=comm, name=name, grid=(1,),
        in_specs=[
            pl.BlockSpec((T, Q_COLS), lambda i: (0, 0)),
            pl.BlockSpec((T, KV_COLS), lambda i: (0, Q_COLS // KV_COLS)),
            pl.BlockSpec((T, KV_COLS), lambda i: (0, Q_COLS // KV_COLS + 1)),
            _full((T, LANES)), _full((T, LANES)), _full((1, LANES)), _full((1, LANES)),
            pl.BlockSpec(memory_space=pltpu.SMEM),
            pl.BlockSpec((T, Q_COLS), lambda i: (0, 0)),
        ],
        out_specs=[_full((T, Q_COLS)), _full((T, KV_COLS)), _full((T, KV_COLS)),
                   _full((1, LANES)), _full((1, LANES)), _full((1, LANES))],
        out_shape=[jax.ShapeDtypeStruct((T, Q_COLS), F32), jax.ShapeDtypeStruct((T, KV_COLS), F32),
                   jax.ShapeDtypeStruct((T, KV_COLS), F32), vec, vec, vec],
        scratch_shapes=[
            pltpu.VMEM((T, Q_COLS), BF16),
            pltpu.VMEM((T + BLOCK, KV_COLS), BF16),
            pltpu.VMEM((T + BLOCK, KV_COLS), BF16),
            pltpu.VMEM((T, Q_COLS), F32),
            pltpu.VMEM((T + BLOCK, KV_COLS), F32),
            pltpu.VMEM((T + BLOCK, KV_COLS), F32),
        ],
        compiler_params=_params(("arbitrary",)),
    )(proj, proj, proj, cos, sin, gq2, gk2, sinks, dyc)


CONV_PAD = 32
CONV_ROWS = 256


def _conv_taps(src, w_ref, r0, first_off, step_sign):
    acc = None
    for i in range(CONV_WIDTH):
        term = w_ref[i:i + 1, :] * src[r0 + first_off + step_sign * i:r0 + first_off + step_sign * i + CONV_ROWS, :]
        acc = term if acc is None else acc + term
    return acc


def _conv_fwd(proj, w_dw, b_dw, g_ln, b_ln, name, comm=None):
    T = proj.shape[0]
    a_blk = (Q_COLS + 2 * KV_COLS) // CONV_CH

    def body(a_ref, gate_ref, w_ref, bdw_ref, g_ref, b_ref, y_ref, c_ref, pad):
        pad[0:CONV_PAD, :] = jnp.zeros((CONV_PAD, CONV_CH), F32)
        pad[CONV_PAD:, :] = a_ref[...] * _sigmoid(gate_ref[...])
        for n in range(T // CONV_ROWS):
            r0 = n * CONV_ROWS
            c = _conv_taps(pad, w_ref, r0, CONV_PAD - (CONV_WIDTH - 1), 1) + bdw_ref[...]
            c_ref[r0:r0 + CONV_ROWS, :] = c
            mu = jnp.mean(c, axis=-1, keepdims=True)
            cc = c - mu
            rstd = lax.rsqrt(jnp.mean(cc * cc, axis=-1, keepdims=True) + EPS)
            z = cc * rstd * g_ref[...] + b_ref[...]
            y_ref[r0:r0 + CONV_ROWS, :] = (z * _sigmoid(z)).astype(BF16)

    vec = _full((1, CONV_CH))
    return _pcall(
        body, comm=comm, name=name, grid=(1,),
        in_specs=[
            pl.BlockSpec((T, CONV_CH), lambda i: (0, a_blk)),
            pl.BlockSpec((T, CONV_CH), lambda i: (0, a_blk + 1)),
            _full((CONV_WIDTH, CONV_CH)), vec, vec, vec,
        ],
        out_specs=[_full((T, CONV_CH)), _full((T, CONV_CH))],
        out_shape=[jax.ShapeDtypeStruct((T, CONV_CH), BF16), jax.ShapeDtypeStruct((T, CONV_CH), F32)],
        scratch_shapes=[pltpu.VMEM((T + CONV_PAD, CONV_CH), F32)],
        compiler_params=_params(("arbitrary",)),
    )(proj, proj, w_dw, b_dw, g_ln, b_ln)


def _conv_bwd(proj, c, w_dw, g_ln, b_ln, dyc, name, comm=None):
    T = proj.shape[0]
    a_blk = (Q_COLS + 2 * KV_COLS) // CONV_CH
    y_blk = Q_COLS // CONV_CH

    def body(a_ref, gate_ref, c_ref, w_ref, g_ref, b_ref, dy_ref,
             da_ref, dgate_ref, dw_ref, dbdw_ref, dg_ref, db_ref, pad, dcp):
        pad[0:CONV_PAD, :] = jnp.zeros((CONV_PAD, CONV_CH), F32)
        sg = _sigmoid(gate_ref[...])
        pad[CONV_PAD:, :] = a_ref[...] * sg
        dcp[T:, :] = jnp.zeros((CONV_PAD, CONV_CH), F32)
        dg = db = dbdw = jnp.zeros((1, CONV_CH), F32)
        for n in range(T // CONV_ROWS):
            rows = slice(n * CONV_ROWS, (n + 1) * CONV_ROWS)
            cv = c_ref[rows, :]
            mu = jnp.mean(cv, axis=-1, keepdims=True)
            cc = cv - mu
            rstd = lax.rsqrt(jnp.mean(cc * cc, axis=-1, keepdims=True) + EPS)
            chat = cc * rstd
            z = chat * g_ref[...] + b_ref[...]
            sz = _sigmoid(z)
            dz = dy_ref[rows, :] * (sz * (1.0 + z * (1.0 - sz)))
            dg = dg + jnp.sum(dz * chat, axis=0, keepdims=True)
            db = db + jnp.sum(dz, axis=0, keepdims=True)
            dch = dz * g_ref[...]
            dc = rstd * (dch - jnp.mean(dch, axis=-1, keepdims=True)
                         - chat * jnp.mean(dch * chat, axis=-1, keepdims=True))
            dbdw = dbdw + jnp.sum(dc, axis=0, keepdims=True)
            dcp[rows, :] = dc
        dg_ref[...] = dg
        db_ref[...] = db
        dbdw_ref[...] = dbdw
        dw_ref[CONV_WIDTH:, :] = jnp.zeros((CONV_PAD - CONV_WIDTH, CONV_CH), F32)
        for i in range(CONV_WIDTH):
            off = CONV_PAD - (CONV_WIDTH - 1) + i
            acc = jnp.zeros((1, CONV_CH), F32)
            for n in range(T // CONV_ROWS):
                r0 = n * CONV_ROWS
                acc = acc + jnp.sum(dcp[r0:r0 + CONV_ROWS, :] * pad[r0 + off:r0 + off + CONV_ROWS, :],
                                    axis=0, keepdims=True)
            dw_ref[i:i + 1, :] = acc
        for n in range(T // CONV_ROWS):
            r0 = n * CONV_ROWS
            rows = slice(r0, r0 + CONV_ROWS)
            dhg = _conv_taps(dcp, w_ref, r0, CONV_WIDTH - 1, -1)
            sgv = sg[rows, :]
            da_ref[rows, :] = dhg * sgv
            dgate_ref[rows, :] = dhg * a_ref[rows, :] * sgv * (1.0 - sgv)

    vec = _full((1, CONV_CH))
    vshape = jax.ShapeDtypeStruct((1, CONV_CH), F32)
    return _pcall(
        body, comm=comm, name=name, grid=(1,),
        in_specs=[
            pl.BlockSpec((T, CONV_CH), lambda i: (0, a_blk)),
            pl.BlockSpec((T, CONV_CH), lambda i: (0, a_blk + 1)),
            _full((T, CONV_CH)), _full((CONV_WIDTH, CONV_CH)), vec, vec,
            pl.BlockSpec((T, CONV_CH), lambda i: (0, y_blk)),
        ],
        out_specs=[_full((T, CONV_CH)), _full((T, CONV_CH)), _full((CONV_PAD, CONV_CH)), vec, vec, vec],
        out_shape=[jax.ShapeDtypeStruct((T, CONV_CH), F32), jax.ShapeDtypeStruct((T, CONV_CH), F32),
                   jax.ShapeDtypeStruct((CONV_PAD, CONV_CH), F32), vshape, vshape, vshape],
        scratch_shapes=[pltpu.VMEM((T + CONV_PAD, CONV_CH), F32), pltpu.VMEM((T + CONV_PAD, CONV_CH), F32)],
        compiler_params=_params(("arbitrary",)),
    )(proj, proj, c, w_dw, g_ln, b_ln, dyc)


def _mem_kv(mkv_ref, gk_ref, lo, kn_s, vv_s):
    for p in range(MQ_COLS // LANES):
        cols = slice(p * LANES, (p + 1) * LANES)
        kn_s[:, cols] = _head_norm(mkv_ref[:, cols], gk_ref[...], lo).astype(BF16)
    vv_s[...] = mkv_ref[:, MQ_COLS:].astype(BF16)


def _mem_softmax(qh, kh):
    s = _dot_nt(qh, kh)
    m = jnp.max(s, axis=-1, keepdims=True)
    p = jnp.exp(s - m)
    return p / jnp.sum(p, axis=-1, keepdims=True)


def _mem_fwd(proj, mkv, gq2, gk2, name, comm=None):
    T = proj.shape[0]
    tt = ROW_TILE
    q_blk = (IN_COLS - MQ_COLS) // MQ_COLS

    def body(q_ref, mkv_ref, gq_ref, gk_ref, y_ref, kn_s, vv_s, qn_s):
        lo, _, _ = _head_masks()
        _mem_kv(mkv_ref, gk_ref, lo, kn_s, vv_s)
        for p in range(MQ_COLS // LANES):
            cols = slice(p * LANES, (p + 1) * LANES)
            qn_s[:, cols] = (_head_norm(q_ref[:, cols], gq_ref[...], lo) * (HEAD_DIM ** -0.5)).astype(BF16)
        for h in range(N_MEM_HEADS):
            hc = slice(h * HEAD_DIM, (h + 1) * HEAD_DIM)
            w = _mem_softmax(qn_s[:, hc], kn_s[:, hc])
            y_ref[:, hc] = _dot(w.astype(BF16), vv_s[:, hc]).astype(BF16)

    return _pcall(
        body, comm=comm, name=name, grid=(T // tt,),
        in_specs=[
            pl.BlockSpec((tt, MQ_COLS), lambda t: (t, q_blk)),
            pl.BlockSpec((MEM_LEN, 2 * MQ_COLS), lambda t: (0, 0)),
            pl.BlockSpec((1, LANES), lambda t: (0, 0)), pl.BlockSpec((1, LANES), lambda t: (0, 0)),
        ],
        out_specs=pl.BlockSpec((tt, MQ_COLS), lambda t: (t, 0)),
        out_shape=jax.ShapeDtypeStruct((T, MQ_COLS), BF16),
        scratch_shapes=[pltpu.VMEM((MEM_LEN, MQ_COLS), BF16), pltpu.VMEM((MEM_LEN, MQ_COLS), BF16),
                        pltpu.VMEM((tt, MQ_COLS), BF16)],
        compiler_params=_params(("arbitrary",)),
    )(proj, mkv, gq2, gk2)


def _mem_bwd(proj, mkv, gq2, gk2, dyc, name, comm=None):
    T = proj.shape[0]
    tt = ROW_TILE
    nt = T // tt
    q_blk = (IN_COLS - MQ_COLS) // MQ_COLS
    y_blk = (Q_COLS + CONV_CH) // MQ_COLS

    def body(q_ref, mkv_ref, gq_ref, gk_ref, dy_ref, dq_ref, dmkv_ref, dgq_ref, dgk_ref,
             kn_s, vv_s, qn_s, dqn_s, dkn_acc):
        t = pl.program_id(0)
        lo, _, _ = _head_masks()
        _mem_kv(mkv_ref, gk_ref, lo, kn_s, vv_s)

        @pl.when(t == 0)
        def _():
            dkn_acc[...] = jnp.zeros_like(dkn_acc)
            dmkv_ref[...] = jnp.zeros_like(dmkv_ref)
            dgq_ref[...] = jnp.zeros_like(dgq_ref)

        for p in range(MQ_COLS // LANES):
            cols = slice(p * LANES, (p + 1) * LANES)
            qn_s[:, cols] = (_head_norm(q_ref[:, cols], gq_ref[...], lo) * (HEAD_DIM ** -0.5)).astype(BF16)
        for h in range(N_MEM_HEADS):
            hc = slice(h * HEAD_DIM, (h + 1) * HEAD_DIM)
            vc = slice(MQ_COLS + h * HEAD_DIM, MQ_COLS + (h + 1) * HEAD_DIM)
            qh = qn_s[:, hc]
            w = _mem_softmax(qh, kn_s[:, hc])
            do = dy_ref[:, hc].astype(BF16)
            dmkv_ref[:, vc] += _dot_tn(w.astype(BF16), do)
            dw = _dot_nt(do, vv_s[:, hc])
            ds = (w * (dw - jnp.sum(w * dw, axis=-1, keepdims=True))).astype(BF16)
            dqn_s[:, hc] = _dot(ds, kn_s[:, hc])
            dkn_acc[:, hc] += _dot_tn(ds, qh)
        dgq = jnp.zeros((1, LANES), F32)
        for p in range(MQ_COLS // LANES):
            cols = slice(p * LANES, (p + 1) * LANES)
            dx, dgp = _head_norm_bwd(q_ref[:, cols], gq_ref[...], dqn_s[:, cols] * (HEAD_DIM ** -0.5), lo)
            dq_ref[:, cols] = dx
            dgq = dgq + jnp.sum(dgp, axis=0, keepdims=True)
        dgq_ref[...] += dgq

        @pl.when(t == nt - 1)
        def _():
            dgk = jnp.zeros((1, LANES), F32)
            for p in range(MQ_COLS // LANES):
                cols = slice(p * LANES, (p + 1) * LANES)
                dx, dgp = _head_norm_bwd(mkv_ref[:, cols], gk_ref[...], dkn_acc[:, cols], lo)
                dmkv_ref[:, cols] = dx
                dgk = dgk + jnp.sum(dgp, axis=0, keepdims=True)
            dgk_ref[...] = _fold_heads(dgk)
            dgq_ref[...] = _fold_heads(dgq_ref[...])

    vec = pl.BlockSpec((1, LANES), lambda t: (0, 0))
    vshape = jax.ShapeDtypeStruct((1, LANES), F32)
    return _pcall(
        body, comm=comm, name=name, grid=(nt,),
        in_specs=[
            pl.BlockSpec((tt, MQ_COLS), lambda t: (t, q_blk)),
            pl.BlockSpec((MEM_LEN, 2 * MQ_COLS), lambda t: (0, 0)),
            vec, vec,
            pl.BlockSpec((tt, MQ_COLS), lambda t: (t, y_blk)),
        ],
        out_specs=[pl.BlockSpec((tt, MQ_COLS), lambda t: (t, 0)),
                   pl.BlockSpec((MEM_LEN, 2 * MQ_COLS), lambda t: (0, 0)), vec, vec],
        out_shape=[jax.ShapeDtypeStruct((T, MQ_COLS), F32), jax.ShapeDtypeStruct((MEM_LEN, 2 * MQ_COLS), F32),
                   vshape, vshape],
        scratch_shapes=[pltpu.VMEM((MEM_LEN, MQ_COLS), BF16), pltpu.VMEM((MEM_LEN, MQ_COLS), BF16),
                        pltpu.VMEM((tt, MQ_COLS), BF16), pltpu.VMEM((tt, MQ_COLS), F32),
                        pltpu.VMEM((MEM_LEN, MQ_COLS), F32)],
        compiler_params=_params(("arbitrary",)),
    )(proj, mkv, gq2, gk2, dyc)


def _loss_head(y, target, name, comm=None):
    T, D = y.shape
    tt = ROW_TILE

    def body(y_ref, t_ref, dy_ref, loss_ref):
        t = pl.program_id(0)
        err = y_ref[...] - t_ref[...]
        dy_ref[...] = err * (1.0 / D)
        part = 0.5 * jnp.sum(jnp.mean(err * err, axis=-1, keepdims=True), axis=0, keepdims=True)

        @pl.when(t == 0)
        def _():
            loss_ref[...] = jnp.zeros_like(loss_ref)

        loss_ref[...] += jnp.broadcast_to(part, loss_ref.shape)

    tile = pl.BlockSpec((tt, D), lambda t: (t, 0))
    return _pcall(
        body, comm=comm, name=name, grid=(T // tt,),
        in_specs=[tile, tile],
        out_specs=[tile, pl.BlockSpec((1, LANES), lambda t: (0, 0))],
        out_shape=[jax.ShapeDtypeStruct((T, D), F32), jax.ShapeDtypeStruct((1, LANES), F32)],
        compiler_params=_params(("arbitrary",)),
    )(y, target)


def _adamw(w, g, m, v, name, comm=None):
    R, C = w.shape
    tr = next((r for r in (512, 352, 256, 128) if R % r == 0), R)

    def body(w_ref, g_ref, m_ref, v_ref, go_ref, d_ref, nm_ref, nv_ref):
        gv = g_ref[...]
        go_ref[...] = gv
        nm = ADAM_B1 * m_ref[...] + (1.0 - ADAM_B1) * gv
        nv = ADAM_B2 * v_ref[...] + (1.0 - ADAM_B2) * (gv * gv)
        m_hat = nm / (1.0 - ADAM_B1 ** ADAM_STEP)
        v_hat = nv / (1.0 - ADAM_B2 ** ADAM_STEP)
        d_ref[...] = -ADAM_LR * (m_hat / (jnp.sqrt(v_hat) + ADAM_EPS) + ADAM_WD * w_ref[...])
        nm_ref[...] = nm
        nv_ref[...] = nv

    tile = pl.BlockSpec((tr, C), lambda i: (i, 0))
    shape = jax.ShapeDtypeStruct((R, C), F32)
    return _pcall(
        body, comm=comm, name=name, grid=(R // tr,),
        in_specs=[tile] * 4, out_specs=[tile] * 4, out_shape=[shape] * 4,
        compiler_params=_params(("arbitrary",)),
    )(w, g, m, v)


def _mesh_pos():
    return lax.axis_index("x"), lax.axis_index("y"), lax.axis_index("c")


def _other_chips(x, y):
    return [(1 - x, y), (x, 1 - y), (1 - x, 1 - y)]


def _quarter(ref, layout, q, rows, cols):
    if layout == "cols":
        return ref.at[rows, pl.ds(pl.multiple_of(q * cols, LANES), cols)]
    return ref.at[q, rows, :]


def _gather_weights(shards, dtypes, layouts, name, comm=None):
    n = len(shards)
    all_rows = slice(None)
    remote = [i for i in range(n) if layouts[i] != "own"]
    split = [i for i in remote if layouts[i] != "whole"]

    def body(*refs):
        ins, outs = refs[:n], refs[n:2 * n]
        st32, st16 = refs[2 * n:3 * n], refs[3 * n:4 * n]
        in_sems, own_sems, send_sems, recv_sems, fwd_send_sems, fwd_recv_sems = refs[4 * n:]
        x, y, c = _mesh_pos()
        chip = 2 * x + y
        sibling = (x, y, 1 - c)
        chips = _other_chips(x, y)

        def half(i, which):
            if i not in split:
                return all_rows
            hr = shards[i].shape[0] // 2
            return pl.ds(pl.multiple_of(which * hr, 16), hr)

        def place(i, q, rows):
            if layouts[i] == "own":
                return outs[i]
            return _quarter(outs[i], layouts[i], q, rows, shards[i].shape[1])

        def ici(i, k, origin_chip, src):
            px, py = chips[k]
            return pltpu.make_async_remote_copy(
                src_ref=src, dst_ref=place(i, origin_chip, half(i, c)), send_sem=send_sems.at[i, k],
                recv_sem=recv_sems.at[i, k], device_id=(px, py, c), device_id_type=MESH)

        def forward(i, k, rows):
            px, py = chips[k]
            there = place(i, 2 * px + py, rows)
            return pltpu.make_async_remote_copy(
                src_ref=there, dst_ref=there, send_sem=fwd_send_sems.at[i, k],
                recv_sem=fwd_recv_sems.at[i, k], device_id=sibling, device_id_type=MESH)

        loads = [pltpu.make_async_copy(ins[i], st32[i], in_sems.at[i]) for i in range(n)]
        for cp in loads:
            cp.start()
        owns, sent = [], []
        for i in range(n):
            loads[i].wait()
            st16[i][...] = st32[i][...].astype(dtypes[i])
            own = pltpu.make_async_copy(st16[i], place(i, chip, all_rows), own_sems.at[i])
            own.start()
            owns.append(own)
            for k in range(3 if i in remote else 0):
                cp = ici(i, k, chip, st16[i].at[half(i, c)])
                cp.start()
                sent.append(cp)
        for i in remote:
            for k, (px, py) in enumerate(chips):
                ici(i, k, 2 * px + py, st16[i].at[half(i, c)]).wait_recv()
                if i in split:
                    cp = forward(i, k, half(i, c))
                    cp.start()
                    sent.append(cp)
        for i in split:
            for k in range(3):
                forward(i, k, half(i, 1 - c)).wait_recv()
        for cp in sent:
            cp.wait_send()
        for cp in owns:
            cp.wait()

    hbm = pl.BlockSpec(memory_space=pl.ANY)
    return _pcall(
        body, comm=comm, name=name,
        in_specs=[hbm] * n, out_specs=[hbm] * n,
        out_shape=[_gathered_shape(s.shape, d, lay) for s, d, lay in zip(shards, dtypes, layouts)],
        scratch_shapes=[pltpu.VMEM(s.shape, F32) for s in shards] + [pltpu.VMEM(s.shape, d) for s, d in zip(shards, dtypes)]
        + [pltpu.SemaphoreType.DMA((n,)), pltpu.SemaphoreType.DMA((n,)),
           pltpu.SemaphoreType.DMA((n, 3)), pltpu.SemaphoreType.DMA((n, 3)),
           pltpu.SemaphoreType.DMA((n, 3)), pltpu.SemaphoreType.DMA((n, 3))],
        compiler_params=pltpu.CompilerParams(vmem_limit_bytes=VMEM_LIMIT),
    )(*shards)


def _gathered_shape(quarter_shape, dtype, layout):
    R, C = quarter_shape
    shape = {"cols": (R, N_CHIPS * C), "own": (R, C)}.get(layout, (N_CHIPS, R, C))
    return jax.ShapeDtypeStruct(shape, dtype)


def _remote(src, dst, sems, j, device):
    return pltpu.make_async_remote_copy(src_ref=src, dst_ref=dst, send_sem=sems.at[2 * j], recv_sem=sems.at[2 * j + 1],
                                        device_id=device, device_id_type=MESH)


def _half_rows(rows, which):
    hr = rows // 2
    return pl.ds(pl.multiple_of(which * hr, 16), hr)


def _spread_plan(owns, layouts):
    def plan(ins, outs, sems, finishing):
        x, y, c = _mesh_pos()
        chip = 2 * x + y
        local, sends, recvs = [], [], []
        for i, (own, full) in enumerate(zip(ins, outs)):
            R, C = own.shape
            mine = _half_rows(R, c)
            local.append(pltpu.make_async_copy(own, _quarter(full, layouts[i], chip, slice(None), C), sems.at[8 * i + 6]))
            for k, (px, py) in enumerate(_other_chips(x, y)):
                src = own.at[mine, :]
                sends.append(_remote(src, _quarter(full, layouts[i], chip, mine, C), sems, 4 * i + k, (px, py, c)))
                if finishing:
                    recvs.append(_remote(src, _quarter(full, layouts[i], 2 * px + py, mine, C), sems, 4 * i + k,
                                         (px, py, c)))
        return local, sends, recvs

    shapes = [_gathered_shape(o.shape, o.dtype, lay) for o, lay in zip(owns, layouts)]
    return _Comm(owns, shapes, 8 * len(owns), plan)


def _forward_plan(fulls, quarter_shapes, layouts):
    def plan(ins, outs, sems, finishing):
        x, y, c = _mesh_pos()
        sends, recvs = [], []
        for i, full in enumerate(outs):
            R, C = quarter_shapes[i]
            for k, (px, py) in enumerate(_other_chips(x, y)):
                mine = _quarter(full, layouts[i], 2 * px + py, _half_rows(R, c), C)
                sends.append(_remote(mine, mine, sems, 3 * i + k, (x, y, 1 - c)))
                if finishing:
                    theirs = _quarter(full, layouts[i], 2 * px + py, _half_rows(R, 1 - c), C)
                    recvs.append(_remote(mine, theirs, sems, 3 * i + k, (x, y, 1 - c)))
        return [], sends, recvs

    shapes = [jax.ShapeDtypeStruct(f.shape, f.dtype) for f in fulls]
    return _Comm(fulls, shapes, 6 * len(fulls), plan, aliases={i: i for i in range(len(fulls))})


def _swap_plan(grads):
    def plan(ins, outs, sems, finishing):
        x, y, c = _mesh_pos()
        sends = [_remote(g.at[:, _half_rows(g.shape[1], 1 - c), :], sib, sems, i, (x, y, 1 - c))
                 for i, (g, sib) in enumerate(zip(ins, outs))]
        return [], sends, sends

    shapes = [jax.ShapeDtypeStruct((N_CHIPS, g.shape[1] // 2, g.shape[2]), BF16) for g in grads]
    return _Comm(grads, shapes, 2 * len(grads), plan)


def _pair_sums(gs, sibs, name, comm=None):
    n = len(gs)

    def body(*refs):
        c = lax.axis_index("c")
        for g_ref, sib_ref, o_ref in zip(refs[:n], refs[n:2 * n], refs[2 * n:]):
            mine = _half_rows(g_ref.shape[1], c)
            o_ref[0] = (g_ref[0, mine, :].astype(F32) + sib_ref[0].astype(F32)).astype(BF16)

    def chunk(shape):
        return pl.BlockSpec((1,) + shape[1:], lambda q: (q, 0, 0))

    return _pcall(
        body, comm=comm, name=name, grid=(N_CHIPS,),
        in_specs=[chunk(g.shape) for g in gs] + [chunk(s.shape) for s in sibs],
        out_specs=[chunk(s.shape) for s in sibs],
        out_shape=[jax.ShapeDtypeStruct(s.shape, BF16) for s in sibs],
        compiler_params=_params(("arbitrary",)),
    )(*gs, *sibs)


def _ici_plan(sums):
    def plan(ins, outs, sems, finishing):
        x, y, c = _mesh_pos()
        sends = []
        for i, (s, rcv) in enumerate(zip(ins, outs)):
            for k, (px, py) in enumerate(_other_chips(x, y)):
                sends.append(_remote(s.at[2 * px + py], rcv.at[k], sems, 3 * i + k, (px, py, c)))
        return [], sends, sends

    shapes = [jax.ShapeDtypeStruct((3,) + s.shape[1:], BF16) for s in sums]
    return _Comm(sums, shapes, 6 * len(sums), plan)


def _run_comm(comm, name):
    def body():
        pass

    _, landed = _pcall(body, comm=comm, name=name, grid=(1,), in_specs=[], out_specs=[], out_shape=[])()
    return landed


def _finish_quarters(ss, rcvs, name, comm=None):
    n = len(ss)

    def body(*refs):
        s_refs, rcv_refs, out_refs, sems = refs[:n], refs[n:2 * n], refs[2 * n:3 * n], refs[3 * n]
        x, y, c = _mesh_pos()
        swaps = []
        for i, (s_ref, rcv_ref, out_ref) in enumerate(zip(s_refs, rcv_refs, out_refs)):
            mine = _half_rows(out_ref.shape[0], c)
            acc = s_ref[2 * x + y].astype(F32)
            for k in range(3):
                acc = acc + rcv_ref[k].astype(F32)
            out_ref[mine, :] = acc
            back = _remote(out_ref.at[mine, :], out_ref.at[mine, :], sems, i, (x, y, 1 - c))
            back.start()
            swaps.append(back)
        for back in swaps:
            back.wait()

    vmem = pl.BlockSpec(memory_space=pltpu.VMEM)
    return _pcall(
        body, comm=comm, name=name, grid=(1,),
        in_specs=[vmem] * (2 * n), out_specs=[vmem] * n,
        out_shape=[jax.ShapeDtypeStruct((2 * s.shape[1], s.shape[2]), F32) for s in ss],
        scratch_shapes=[pltpu.SemaphoreType.DMA((2 * n,))],
        compiler_params=pltpu.CompilerParams(vmem_limit_bytes=VMEM_LIMIT),
    )(*ss, *rcvs)


def _allreduce_small(v, name, comm=None):
    R, C = v.shape
    n_dev = 8

    def body(v_ref, out_ref, buf, send_sems, recv_sems):
        x, y, c = _mesh_pos()
        me = 4 * x + 2 * y + c
        buf[me] = v_ref[...]
        peers = []
        for k in range(1, n_dev):
            kx, ky, kc = (k >> 2) & 1, (k >> 1) & 1, k & 1
            px = 1 - x if kx else x
            py = 1 - y if ky else y
            pc = 1 - c if kc else c
            peers.append((px, py, pc))
        sends = []
        for k, peer in enumerate(peers):
            cp = pltpu.make_async_remote_copy(
                src_ref=v_ref, dst_ref=buf.at[me], send_sem=send_sems.at[k], recv_sem=recv_sems.at[k],
                device_id=peer, device_id_type=MESH)
            cp.start()
            sends.append(cp)
        for k, (px, py, pc) in enumerate(peers):
            pltpu.make_async_remote_copy(
                src_ref=v_ref, dst_ref=buf.at[4 * px + 2 * py + pc], send_sem=send_sems.at[k],
                recv_sem=recv_sems.at[k], device_id=(px, py, pc), device_id_type=MESH).wait_recv()
        for cp in sends:
            cp.wait_send()
        acc = buf[0]
        for i in range(1, n_dev):
            acc = acc + buf[i]
        out_ref[...] = acc

    vmem = pl.BlockSpec(memory_space=pltpu.VMEM)
    return _pcall(
        body, comm=comm, name=name,
        in_specs=[vmem], out_specs=vmem,
        out_shape=jax.ShapeDtypeStruct((R, C), F32),
        scratch_shapes=[pltpu.VMEM((n_dev, R, C), F32),
                        pltpu.SemaphoreType.DMA((n_dev - 1,)), pltpu.SemaphoreType.DMA((n_dev - 1,))],
        compiler_params=pltpu.CompilerParams(vmem_limit_bytes=VMEM_LIMIT),
    )(v)


def _rope_tables(positions):
    half = ROPE_DIM // 2
    inv_freq = ROPE_THETA ** (-jnp.arange(half, dtype=F32) / half)
    ang = positions.astype(F32)[:, None] * inv_freq
    cos, sin = jnp.cos(ang), jnp.sin(ang)
    T = positions.shape[0]
    ones = jnp.ones((T, HEAD_DIM - ROPE_DIM), F32)
    c64 = jnp.concatenate([cos, cos, ones], axis=1)
    s64 = jnp.concatenate([-sin, sin, 0.0 * ones], axis=1)
    return jnp.tile(c64, (1, 2)), jnp.tile(s64, (1, 2))


def _local_step(x, mem, positions, target, small, first, own):
    cos, sin = _rope_tables(positions)
    two = lambda g: jnp.tile(g, (1, 2))
    gq2, gk2, gmq2, gmk2 = two(small["g_q"]), two(small["g_k"]), two(small["g_mq"]), two(small["g_mk"])
    mix_names = ["w_in", "w_mkv", "w_out"]
    mix_layouts = ["cols", "stack", "stack"]
    ffn2_names = ["wg2", "wu2", "wd2"]

    spread = _spread_plan([own[n] for n in mix_names] + [own["wg2"]], mix_layouts + ["stack"])
    (x1, h1, a1, b1), landed = _ffn_fwd(x, small["g_ffn1"], first["wg1"], first["wu1"], first["wd1"], "ffn1_fwd",
                                         comm=spread)
    half_wg2 = landed[3]
    passing = _forward_plan(landed[:3], [own[n].shape for n in mix_names], mix_layouts)
    hm, (w_in, w_mkv, w_out) = _rms_fwd(x1, small["g_mix"], "mix_norm", comm=passing)
    w_mkv = w_mkv.reshape(D_MODEL, 2 * MQ_COLS)
    w_out = w_out.reshape(D_MODEL, D_MODEL)
    proj = _mm_nn([hm], w_in, None, "in_proj")
    hmem = _rms_fwd(mem, small["g_mem"], "mem_norm")
    mkv = _mm_nn([hmem], w_mkv, None, "mem_proj")
    ya, (half_wu2,) = _attn_fwd(proj, cos, sin, gq2, gk2, small["sinks"], "swa_fwd",
                                comm=_spread_plan([own["wu2"]], ["stack"]))
    yc, cpre = _conv_fwd(proj, small["w_dw"], small["b_dw"], small["g_conv_ln"], small["b_conv_ln"], "conv_fwd")
    ym, (half_wd2,) = _mem_fwd(proj, mkv, gmq2, gmk2, "memattn_fwd", comm=_spread_plan([own["wd2"]], ["stack"]))
    passing = _forward_plan([half_wg2, half_wu2, half_wd2], [own[n].shape for n in ffn2_names], ["stack"] * 3)
    x2, (wg2, wu2, wd2) = _mm_nn([ya, yc, ym], w_out, x1, "out_proj", comm=passing)
    x3, h2, a2, b2 = _ffn_fwd(x2, small["g_ffn2"], wg2, wu2, wd2, "ffn2_fwd")
    dx3, loss = _loss_head(x3, target, "loss_head")

    dh2, dwg2, dwu2, dwd2 = _ffn_bwd(dx3, h2, a2, b2, wg2, wu2, wd2, "ffn2_bwd")
    (dx2, dg_ffn2), sibs = _rms_bwd(x2, small["g_ffn2"], dh2, dx3, "ffn2_norm_bwd", comm=_swap_plan([dwg2, dwu2, dwd2]))
    sums_ffn2 = _pair_sums([dwg2, dwu2, dwd2], sibs, "pair_sums_ffn2")
    dyc = _mm_nt([dx2], w_out, "out_proj_bwd")
    dw_out = _mm_tn([ya, yc, ym], [dx2], 1, "out_proj_wgrad").reshape(N_CHIPS, -1, D_MODEL)
    (dq, dk, dv, dgq, dgk, dsinks), rcv_ffn2 = _attn_bwd(proj, cos, sin, gq2, gk2, small["sinks"], dyc, "swa_bwd",
                                                          comm=_ici_plan(sums_ffn2))
    (da, dgate, dw_dw, db_dw, dg_ln, db_ln), sibs = _conv_bwd(
        proj, cpre, small["w_dw"], small["g_conv_ln"], small["b_conv_ln"], dyc, "conv_bwd", comm=_swap_plan([dw_out]))
    dmq, dmkv, dgmq, dgmk = _mem_bwd(proj, mkv, gmq2, gmk2, dyc, "memattn_bwd")
    sums_out = _pair_sums([dw_out], sibs, "pair_sums_out")
    pieces = [dq, dk, dv, da, dgate, dmq]
    dhm, rcv_out = _mm_nt(pieces, w_in, "in_proj_bwd", comm=_ici_plan(sums_out))
    dw_in = _mm_tn([hm], pieces, N_CHIPS, "in_proj_wgrad")
    dhmem = _mm_nt([dmkv], w_mkv, "mem_proj_bwd")
    dw_mkv = _mm_tn([hmem], [dmkv], 1, "mem_proj_wgrad").reshape(N_CHIPS, -1, 2 * MQ_COLS)
    _, dg_mem = _rms_bwd(mem, small["g_mem"], dhmem, None, "mem_norm_bwd")
    (dx1, dg_mix), sibs = _rms_bwd(x1, small["g_mix"], dhm, dx2, "mix_norm_bwd", comm=_swap_plan([dw_in, dw_mkv]))
    sums_in = _pair_sums([dw_in, dw_mkv], sibs, "pair_sums_in")
    (dh1, dwg1, dwu1, dwd1), rcv_in = _ffn_bwd(dx1, h1, a1, b1, first["wg1"], first["wu1"], first["wd1"], "ffn1_bwd",
                                                comm=_ici_plan(sums_in))
    (dx, dg_ffn1), sibs = _rms_bwd(x, small["g_ffn1"], dh1, dx1, "ffn1_norm_bwd", comm=_swap_plan([dwg1, dwu1, dwd1]))
    sums_ffn1 = _pair_sums([dwg1, dwu1, dwd1], sibs, "pair_sums_ffn1")
    g_ffn2, rcv_wg1 = _finish_quarters(sums_ffn2, rcv_ffn2, "finish_ffn2", comm=_ici_plan(sums_ffn1[:1]))
    g_mix, rcv_wu1 = _finish_quarters(sums_in + sums_out, rcv_in + rcv_out, "finish_mix", comm=_ici_plan(sums_ffn1[1:2]))
    rcv_wd1 = _run_comm(_ici_plan(sums_ffn1[2:]), "ffn1_grads_ici")
    g_ffn1 = _finish_quarters(sums_ffn1, rcv_wg1 + rcv_wu1 + rcv_wd1, "finish_ffn1")
    big_grads = dict(zip(["wg1", "wu1", "wd1", "w_in", "w_mkv", "w_out"] + ffn2_names, [*g_ffn1, *g_mix, *g_ffn2]))
    small_grads = dict(
        g_ffn1=dg_ffn1, g_mix=dg_mix, g_q=dgq[:, :HEAD_DIM], g_k=dgk[:, :HEAD_DIM], sinks=dsinks[:, :N_Q_HEADS],
        w_dw=dw_dw[:CONV_WIDTH], b_dw=db_dw, g_conv_ln=dg_ln, b_conv_ln=db_ln, g_mem=dg_mem,
        g_mq=dgmq[:, :HEAD_DIM], g_mk=dgmk[:, :HEAD_DIM], g_ffn2=dg_ffn2)
    return loss, dx, big_grads, small_grads


SMALL_NAMES = ["g_ffn1", "g_mix", "g_q", "g_k", "sinks", "b_dw", "g_conv_ln", "b_conv_ln", "g_mem", "g_mq", "g_mk",
               "g_ffn2"]
PACK_COLS = 1024


def _pack(parts):
    flat = [p.reshape(-1) for p in parts]
    offs, o = [], 0
    for f in flat:
        offs.append(o)
        o += f.shape[0]
    rows = -(-o // (8 * PACK_COLS)) * 8
    pad = jnp.zeros((rows * PACK_COLS - o,), F32)
    return jnp.concatenate(flat + [pad]).reshape(rows, PACK_COLS), offs


def _unpack(packed, offs, shapes):
    flat = packed.reshape(-1)
    return [flat[o:o + math.prod(s)].reshape(s) for o, s in zip(offs, shapes)]


def kernel(x, mem, positions, g_ffn1, w_ffn1_gate, w_ffn1_up, w_ffn1_down, g_mix, w_in, g_q, g_k, sinks, w_dw, b_dw, g_conv_ln, b_conv_ln, g_mem, w_mem_kv, g_mq, g_mk, w_out, g_ffn2, w_ffn2_gate, w_ffn2_up, w_ffn2_down, loss_target, m_g_ffn1, m_w_ffn1_gate, m_w_ffn1_up, m_w_ffn1_down, m_g_mix, m_w_in, m_g_q, m_g_k, m_sinks, m_w_dw, m_b_dw, m_g_conv_ln, m_b_conv_ln, m_g_mem, m_w_mem_kv, m_g_mq, m_g_mk, m_w_out, m_g_ffn2, m_w_ffn2_gate, m_w_ffn2_up, m_w_ffn2_down, v_g_ffn1, v_w_ffn1_gate, v_w_ffn1_up, v_w_ffn1_down, v_g_mix, v_w_in, v_g_q, v_g_k, v_sinks, v_w_dw, v_b_dw, v_g_conv_ln, v_b_conv_ln, v_g_mem, v_w_mem_kv, v_g_mq, v_g_mk, v_w_out, v_g_ffn2, v_w_ffn2_gate, v_w_ffn2_up, v_w_ffn2_down):
    args = dict(locals())
    weight_names = ["g_ffn1", "w_ffn1_gate", "w_ffn1_up", "w_ffn1_down", "g_mix", "w_in", "g_q", "g_k", "sinks",
                    "w_dw", "b_dw", "g_conv_ln", "b_conv_ln", "g_mem", "w_mem_kv", "g_mq", "g_mk", "w_out", "g_ffn2",
                    "w_ffn2_gate", "w_ffn2_up", "w_ffn2_down"]
    big_names = ["w_ffn1_gate", "w_ffn1_up", "w_ffn1_down", "w_in", "w_mem_kv", "w_out",
                 "w_ffn2_gate", "w_ffn2_up", "w_ffn2_down"]
    short = dict(w_ffn1_gate="wg1", w_ffn1_up="wu1", w_ffn1_down="wd1", w_in="w_in", w_mem_kv="w_mkv",
                 w_out="w_out", w_ffn2_gate="wg2", w_ffn2_up="wu2", w_ffn2_down="wd2")

    transposed = ("w_ffn1_gate", "w_ffn1_up", "w_ffn2_gate", "w_ffn2_up")

    def quarter(a, n):
        return jnp.swapaxes(a, 1, 2)[0] if n in transposed else a[0]

    def unquarter(a, n):
        return jnp.swapaxes(a[None], 1, 2) if n in transposed else a[None]

    shards = [quarter(args[n], n) for n in big_names]
    layouts = ["stack" if short[n] in ("wg1", "wu1", "wd1") else "own" for n in big_names]
    gathered = _gather_weights(shards + [w_dw[0]], [BF16] * len(shards) + [F32], layouts + ["whole"],
                               "gather_first")
    first = {short[n]: gathered[i] for i, n in enumerate(big_names) if layouts[i] == "stack"}
    own = {short[n]: gathered[i] for i, n in enumerate(big_names) if layouts[i] == "own"}
    small = {n: args[n] for n in SMALL_NAMES}
    small["w_dw"] = jnp.transpose(gathered[-1], (1, 0, 2)).reshape(CONV_WIDTH, CONV_CH)

    loss, dx, big_grads, small_grads = _local_step(x[0], mem[0], positions[0], loss_target[0], small, first, own)

    small_order = SMALL_NAMES + ["w_dw"]
    packed, offs = _pack([small_grads[n] for n in small_order] + [loss[:, :1]])
    total = _allreduce_small(packed, "allreduce_small")
    shapes = [small_grads[n].shape for n in small_order] + [(1, 1)]
    summed = dict(zip(small_order + ["loss"], _unpack(total, offs, shapes)))
    chip = 2 * lax.axis_index("x") + lax.axis_index("y")
    dw_dw_full = summed.pop("w_dw")
    loss_out = summed.pop("loss").reshape(())

    grads = {n: summed[n] for n in SMALL_NAMES}
    grads["w_dw"] = lax.dynamic_slice_in_dim(dw_dw_full, chip * (CONV_CH // N_CHIPS), CONV_CH // N_CHIPS, axis=1)
    for n in big_names:
        grads[n] = big_grads[short[n]]

    delta, new_m, new_v = {}, {}, {}
    for n in big_names:
        g, d, nm, nv = _adamw(quarter(args[n], n), grads[n], quarter(args["m_" + n], n), quarter(args["v_" + n], n),
                              "adamw_" + short[n])
        grads[n], delta[n], new_m[n], new_v[n] = (unquarter(a, n) for a in (g, d, nm, nv))
    tiny = SMALL_NAMES + ["w_dw"]
    pw, poffs = _pack([args[n] for n in tiny])
    pg, _ = _pack([grads[n] for n in tiny])
    pm, _ = _pack([args["m_" + n] for n in tiny])
    pv, _ = _pack([args["v_" + n] for n in tiny])
    _, pd, pnm, pnv = _adamw(pw, pg, pm, pv, "adamw_small")
    tshapes = [args[n].shape for n in tiny]
    for store, packed_out in ((delta, pd), (new_m, pnm), (new_v, pnv)):
        for n, val in zip(tiny, _unpack(packed_out, poffs, tshapes)):
            store[n] = val

    def shaped(n, v):
        return v.reshape(args[n].shape)

    return (loss_out, dx[None],
            *[shaped(n, grads[n]) for n in weight_names],
            *[shaped(n, delta[n]) for n in weight_names],
            *[shaped(n, new_m[n]) for n in weight_names],
            *[shaped(n, new_v[n]) for n in weight_names])
```

```python
import functools
import math

import jax
import jax.numpy as jnp
from jax import lax
from jax.experimental import pallas as pl
from jax.experimental.pallas import tpu as pltpu

F32 = jnp.float32
BF16 = jnp.bfloat16

D_MODEL = 1024
SEQ = 2048
MEM_LEN = 256
HEAD_DIM = 64
N_Q_HEADS = 8
N_KV_HEADS = 2
Q_PER_KV = 4
N_MEM_HEADS = 4
BLOCK = 128
CONV_CH = 256
CONV_WIDTH = 31
ROPE_THETA = 500000.0
ROPE_DIM = 16
D_FF = 2816
EPS = 1e-6
Q_COLS = 512
KV_COLS = 128
MQ_COLS = 256
IN_COLS = 1536

N_CHIPS = 4
FF_CHUNK = D_FF // N_CHIPS
IN_CHUNK = IN_COLS // N_CHIPS

ADAM_LR = 0.001
ADAM_B1 = 0.9
ADAM_B2 = 0.999
ADAM_EPS = 1e-08
ADAM_WD = 0.01
ADAM_STEP = 10

LANES = 128
VMEM_LIMIT = 56 * 1024 * 1024
ROW_TILE = 512
MESH = pl.DeviceIdType.MESH
NEG = -1e30


class _Comm:
    def __init__(self, ins, out_shapes, n_sems, plan, aliases=None):
        self.ins, self.out_shapes, self.n_sems, self.plan = list(ins), list(out_shapes), n_sems, plan
        self.aliases = aliases or {}


def _pcall(body, comm=None, **kw):
    if comm is None:
        return pl.pallas_call(body, **kw)
    grid = kw["grid"]
    in_specs = list(kw["in_specs"])
    single = not isinstance(kw["out_shape"], (list, tuple))
    out_specs = [kw["out_specs"]] if single else list(kw["out_specs"])
    out_shape = [kw["out_shape"]] if single else list(kw["out_shape"])
    scratch = list(kw.get("scratch_shapes", ()))
    n_in, n_out, n_scr = len(in_specs), len(out_shape), len(scratch)
    n_ci, n_co = len(comm.ins), len(comm.out_shapes)

    def wrapped(*refs):
        o = 0
        parts = []
        for cnt in (n_in, n_ci, n_out, n_co, n_scr):
            parts.append(refs[o:o + cnt])
            o += cnt
        ins, c_ins, outs, c_outs, scr = parts
        sems = refs[o]
        first = last = None
        for d, size in enumerate(grid):
            at0, at_end = pl.program_id(d) == 0, pl.program_id(d) == size - 1
            first = at0 if first is None else first & at0
            last = at_end if last is None else last & at_end

        @pl.when(first)
        def _():
            local, sends, _ = comm.plan(c_ins, c_outs, sems, False)
            for cp in sends + local:
                cp.start()

        body(*ins, *outs, *scr)

        @pl.when(last)
        def _():
            local, sends, recvs = comm.plan(c_ins, c_outs, sems, True)
            for cp in recvs:
                cp.wait_recv()
            for cp in sends:
                cp.wait_send()
            for cp in local:
                cp.wait()

    hbm = pl.BlockSpec(memory_space=pl.ANY)
    kw = dict(kw, in_specs=in_specs + [hbm] * n_ci, out_specs=out_specs + [hbm] * n_co,
              out_shape=out_shape + comm.out_shapes,
              scratch_shapes=scratch + [pltpu.SemaphoreType.DMA((comm.n_sems,))])
    if comm.aliases:
        kw["input_output_aliases"] = {n_in + i: n_out + o for i, o in comm.aliases.items()}
    call = pl.pallas_call(wrapped, **kw)

    def run(*args):
        res = call(*args, *comm.ins)
        return (res[0] if single else list(res[:n_out])), list(res[n_out:])

    return run


def _params(sem=None):
    return pltpu.CompilerParams(dimension_semantics=sem, vmem_limit_bytes=VMEM_LIMIT)


def _dot(a, b):
    return jnp.dot(a, b, preferred_element_type=F32)


def _dot_nt(a, b):
    return lax.dot_general(a, b, (((1,), (1,)), ((), ())), preferred_element_type=F32)


def _dot_tn(a, b):
    return lax.dot_general(a, b, (((0,), (0,)), ((), ())), preferred_element_type=F32)


def _sigmoid(x):
    return 1.0 / (1.0 + jnp.exp(-x))


def _full(shape):
    n = len(shape)
    return pl.BlockSpec(shape, lambda *_: (0,) * n)


def _ffn_fwd(x, g, wg, wu, wd, name, comm=None):
    T, D = x.shape
    nt = T // ROW_TILE

    def body(x_ref, g_ref, wg_ref, wu_ref, wd_ref, xo_ref, h_ref, a_ref, b_ref):
        j = pl.program_id(1)

        @pl.when(j == 0)
        def _():
            xv = x_ref[...]
            rstd = lax.rsqrt(jnp.mean(xv * xv, axis=-1, keepdims=True) + EPS)
            h_ref[...] = (xv * rstd * g_ref[...]).astype(BF16)
            xo_ref[...] = jnp.zeros_like(xo_ref)

        h = h_ref[...]
        a = _dot_nt(h, wg_ref[0])
        b = _dot_nt(h, wu_ref[0])
        a_ref[0] = a.astype(BF16)
        b_ref[0] = b.astype(BF16)
        s = (a * _sigmoid(a)) * b
        xo_ref[...] += _dot(s.astype(BF16), wd_ref[0])

        @pl.when(j == N_CHIPS - 1)
        def _():
            xo_ref[...] = x_ref[...] + 0.5 * xo_ref[...]

    return _pcall(
        body, comm=comm, name=name, grid=(nt, N_CHIPS),
        in_specs=[
            pl.BlockSpec((ROW_TILE, D), lambda t, j: (t, 0)),
            pl.BlockSpec((1, D), lambda t, j: (0, 0)),
            pl.BlockSpec((1, FF_CHUNK, D), lambda t, j: (j, 0, 0)),
            pl.BlockSpec((1, FF_CHUNK, D), lambda t, j: (j, 0, 0)),
            pl.BlockSpec((1, FF_CHUNK, D), lambda t, j: (j, 0, 0)),
        ],
        out_specs=[
            pl.BlockSpec((ROW_TILE, D), lambda t, j: (t, 0)),
            pl.BlockSpec((ROW_TILE, D), lambda t, j: (t, 0)),
            pl.BlockSpec((1, ROW_TILE, FF_CHUNK), lambda t, j: (j, t, 0)),
            pl.BlockSpec((1, ROW_TILE, FF_CHUNK), lambda t, j: (j, t, 0)),
        ],
        out_shape=[
            jax.ShapeDtypeStruct((T, D), F32),
            jax.ShapeDtypeStruct((T, D), BF16),
            jax.ShapeDtypeStruct((N_CHIPS, T, FF_CHUNK), BF16),
            jax.ShapeDtypeStruct((N_CHIPS, T, FF_CHUNK), BF16),
        ],
        compiler_params=_params(("arbitrary", "arbitrary")),
    )(x, g, wg, wu, wd)


def _ffn_bwd(dxo, h, a, b, wg, wu, wd, name, comm=None):
    T, D = dxo.shape
    tt = ROW_TILE
    nt = T // tt

    def body(dxo_ref, h_ref, a_ref, b_ref, wg_ref, wu_ref, wd_ref,
             dh_hbm, dwg_ref, dwu_ref, dwd_ref, dh_acc, acc_g, acc_u, acc_d):
        j = pl.program_id(0)
        t = pl.program_id(1)
        do = (0.5 * dxo_ref[...]).astype(BF16)
        av = a_ref[0].astype(F32)
        bv = b_ref[0].astype(F32)
        sig = _sigmoid(av)
        sa = av * sig
        ds = _dot_nt(do, wd_ref[0])
        da = (ds * bv * (sig * (1.0 + av * (1.0 - sig)))).astype(BF16)
        db = (ds * sa).astype(BF16)
        hv = h_ref[...]
        rows = pl.ds(pl.multiple_of(t * tt, tt), tt)

        @pl.when(j == 0)
        def _():
            dh_acc[rows, :] = jnp.zeros((tt, D), F32)

        @pl.when(t == 0)
        def _():
            acc_g[...] = jnp.zeros_like(acc_g)
            acc_u[...] = jnp.zeros_like(acc_u)
            acc_d[...] = jnp.zeros_like(acc_d)

        acc_d[...] += _dot_tn((sa * bv).astype(BF16), do)
        acc_g[...] += _dot_tn(da, hv)
        acc_u[...] += _dot_tn(db, hv)
        dh_acc[rows, :] += _dot(da, wg_ref[0]) + _dot(db, wu_ref[0])

        @pl.when(t == nt - 1)
        def _():
            dwg_ref[0] = acc_g[...].astype(BF16)
            dwu_ref[0] = acc_u[...].astype(BF16)
            dwd_ref[0] = acc_d[...].astype(BF16)

        @pl.when((t == nt - 1) & (j == N_CHIPS - 1))
        def _():
            pltpu.sync_copy(dh_acc, dh_hbm)

    return _pcall(
        body, comm=comm, name=name, grid=(N_CHIPS, nt),
        in_specs=[
            pl.BlockSpec((tt, D), lambda j, t: (t, 0)),
            pl.BlockSpec((tt, D), lambda j, t: (t, 0)),
            pl.BlockSpec((1, tt, FF_CHUNK), lambda j, t: (j, t, 0)),
            pl.BlockSpec((1, tt, FF_CHUNK), lambda j, t: (j, t, 0)),
            pl.BlockSpec((1, FF_CHUNK, D), lambda j, t: (j, 0, 0)),
            pl.BlockSpec((1, FF_CHUNK, D), lambda j, t: (j, 0, 0)),
            pl.BlockSpec((1, FF_CHUNK, D), lambda j, t: (j, 0, 0)),
        ],
        out_specs=[
            pl.BlockSpec(memory_space=pl.ANY),
            pl.BlockSpec((1, FF_CHUNK, D), lambda j, t: (j, 0, 0)),
            pl.BlockSpec((1, FF_CHUNK, D), lambda j, t: (j, 0, 0)),
            pl.BlockSpec((1, FF_CHUNK, D), lambda j, t: (j, 0, 0)),
        ],
        out_shape=[
            jax.ShapeDtypeStruct((T, D), F32),
            jax.ShapeDtypeStruct((N_CHIPS, FF_CHUNK, D), BF16),
            jax.ShapeDtypeStruct((N_CHIPS, FF_CHUNK, D), BF16),
            jax.ShapeDtypeStruct((N_CHIPS, FF_CHUNK, D), BF16),
        ],
        scratch_shapes=[
            pltpu.VMEM((T, D), F32),
            pltpu.VMEM((FF_CHUNK, D), F32),
            pltpu.VMEM((FF_CHUNK, D), F32),
            pltpu.VMEM((FF_CHUNK, D), F32),
        ],
        compiler_params=_params(("arbitrary", "arbitrary")),
    )(dxo, h, a, b, wg, wu, wd)


def _rms_fwd(x, g, name, comm=None):
    T, D = x.shape
    tt = min(ROW_TILE, T)

    def body(x_ref, g_ref, h_ref):
        xv = x_ref[...]
        rstd = lax.rsqrt(jnp.mean(xv * xv, axis=-1, keepdims=True) + EPS)
        h_ref[...] = (xv * rstd * g_ref[...]).astype(BF16)

    return _pcall(
        body, comm=comm, name=name, grid=(T // tt,),
        in_specs=[pl.BlockSpec((tt, D), lambda t: (t, 0)), pl.BlockSpec((1, D), lambda t: (0, 0))],
        out_specs=pl.BlockSpec((tt, D), lambda t: (t, 0)),
        out_shape=jax.ShapeDtypeStruct((T, D), BF16),
        compiler_params=_params(("arbitrary",)),
    )(x, g)


def _rms_bwd(x, g, dh, dres, name, comm=None):
    T, D = x.shape
    tt = min(ROW_TILE, T)
    has_res = dres is not None

    def body(*refs):
        if has_res:
            x_ref, g_ref, dh_ref, dres_ref, dx_ref, dg_ref = refs
        else:
            x_ref, g_ref, dh_ref, dx_ref, dg_ref = refs
        t = pl.program_id(0)
        xv = x_ref[...]
        rstd = lax.rsqrt(jnp.mean(xv * xv, axis=-1, keepdims=True) + EPS)
        xhat = xv * rstd
        dhv = dh_ref[...]
        gy = dhv * g_ref[...]
        dx = rstd * (gy - xhat * jnp.mean(gy * xhat, axis=-1, keepdims=True))
        if has_res:
            dx = dx + dres_ref[...]
        dx_ref[...] = dx
        part = jnp.sum(dhv * xhat, axis=0, keepdims=True)

        @pl.when(t == 0)
        def _():
            dg_ref[...] = part

        @pl.when(t > 0)
        def _():
            dg_ref[...] += part

    tile = pl.BlockSpec((tt, D), lambda t: (t, 0))
    vec = pl.BlockSpec((1, D), lambda t: (0, 0))
    args = [x, g, dh] + ([dres] if has_res else [])
    return _pcall(
        body, comm=comm, name=name, grid=(T // tt,),
        in_specs=[tile, vec, tile] + ([tile] if has_res else []),
        out_specs=[tile, vec],
        out_shape=[jax.ShapeDtypeStruct((T, D), F32), jax.ShapeDtypeStruct((1, D), F32)],
        compiler_params=_params(("arbitrary",)),
    )(*args)


def _mm_nn(a_list, b, res, name, comm=None):
    T = a_list[0].shape[0]
    K, N = b.shape
    tt = min(ROW_TILE, T)
    ks = [a.shape[1] for a in a_list]
    na = len(a_list)
    has_res = res is not None

    def body(*refs):
        a_refs = refs[:na]
        b_ref = refs[na]
        o_ref = refs[-1]
        acc = res_v = None
        off = 0
        for a_ref, k in zip(a_refs, ks):
            part = _dot(a_ref[...].astype(BF16), b_ref[off:off + k, :])
            acc = part if acc is None else acc + part
            off += k
        if has_res:
            acc = refs[na + 1][...] + acc
        o_ref[...] = acc

    in_specs = [pl.BlockSpec((tt, k), lambda t: (t, 0)) for k in ks] + [pl.BlockSpec((K, N), lambda t: (0, 0))]
    args = list(a_list) + [b]
    if has_res:
        in_specs.append(pl.BlockSpec((tt, N), lambda t: (t, 0)))
        args.append(res)
    return _pcall(
        body, comm=comm, name=name, grid=(T // tt,), in_specs=in_specs,
        out_specs=pl.BlockSpec((tt, N), lambda t: (t, 0)),
        out_shape=jax.ShapeDtypeStruct((T, N), F32),
        compiler_params=_params(("arbitrary",)),
    )(*args)


def _mm_nt(a_list, b, name, comm=None):
    T = a_list[0].shape[0]
    K, N = b.shape
    tt = min(ROW_TILE, T)
    ns = [a.shape[1] for a in a_list]
    na = len(a_list)

    def body(*refs):
        b_ref = refs[na]
        o_ref = refs[-1]
        acc = None
        off = 0
        for a_ref, n in zip(refs[:na], ns):
            part = _dot_nt(a_ref[...].astype(BF16), b_ref[:, off:off + n])
            acc = part if acc is None else acc + part
            off += n
        o_ref[...] = acc

    return _pcall(
        body, comm=comm, name=name, grid=(T // tt,),
        in_specs=[pl.BlockSpec((tt, n), lambda t: (t, 0)) for n in ns] + [pl.BlockSpec((K, N), lambda t: (0, 0))],
        out_specs=pl.BlockSpec((tt, K), lambda t: (t, 0)),
        out_shape=jax.ShapeDtypeStruct((T, K), F32),
        compiler_params=_params(("arbitrary",)),
    )(*a_list, b)


def _mm_tn(a_list, b_list, col_chunks, name, comm=None):
    T = a_list[0].shape[0]
    tt = min(ROW_TILE, T)
    nt = T // tt
    ms = [a.shape[1] for a in a_list]
    ns = [b.shape[1] for b in b_list]
    M, N = sum(ms), sum(ns)
    na, nb = len(a_list), len(b_list)
    cw = N // col_chunks

    def body(*refs):
        a_refs, b_refs = refs[:na], refs[na:na + nb]
        o_ref, acc = refs[na + nb], refs[na + nb + 1]
        t = pl.program_id(0)

        @pl.when(t == 0)
        def _():
            acc[...] = jnp.zeros_like(acc)

        ro = 0
        for a_ref, m in zip(a_refs, ms):
            av = a_ref[...].astype(BF16)
            co = 0
            for b_ref, n in zip(b_refs, ns):
                acc[ro:ro + m, co:co + n] += _dot_tn(av, b_ref[...].astype(BF16))
                co += n
            ro += m

        @pl.when(t == nt - 1)
        def _():
            if col_chunks == 1:
                o_ref[...] = acc[...].astype(BF16)
            else:
                for q in range(col_chunks):
                    o_ref[q] = acc[:, q * cw:(q + 1) * cw].astype(BF16)

    out_shape = (M, N) if col_chunks == 1 else (col_chunks, M, cw)
    return _pcall(
        body, comm=comm, name=name, grid=(nt,),
        in_specs=[pl.BlockSpec((tt, m), lambda t: (t, 0)) for m in ms]
        + [pl.BlockSpec((tt, n), lambda t: (t, 0)) for n in ns],
        out_specs=_full(out_shape),
        out_shape=jax.ShapeDtypeStruct(out_shape, BF16),
        scratch_shapes=[pltpu.VMEM((M, N), F32)],
        compiler_params=_params(("arbitrary",)),
    )(*a_list, *b_list)


def _head_masks():
    lane = lax.broadcasted_iota(jnp.int32, (1, LANES), 1)
    l64 = lane & (HEAD_DIM - 1)
    return lane < HEAD_DIM, l64 < ROPE_DIM // 2, l64 < ROPE_DIM


def _head_mean(v, lo):
    s_lo = jnp.sum(jnp.where(lo, v, 0.0), axis=-1, keepdims=True)
    s_hi = jnp.sum(jnp.where(lo, 0.0, v), axis=-1, keepdims=True)
    return jnp.where(lo, s_lo, s_hi) * (1.0 / HEAD_DIM)


def _rope_swap(v, first, rot):
    up = pltpu.roll(v, LANES - ROPE_DIM // 2, 1)
    down = pltpu.roll(v, ROPE_DIM // 2, 1)
    return jnp.where(first, up, jnp.where(rot, down, 0.0))


def _head_norm(x, g, lo):
    rstd = lax.rsqrt(_head_mean(x * x, lo) + EPS)
    return x * rstd * g


def _head_norm_bwd(x, g, dy, lo):
    rstd = lax.rsqrt(_head_mean(x * x, lo) + EPS)
    xhat = x * rstd
    gy = dy * g
    dx = rstd * (gy - xhat * _head_mean(gy * xhat, lo))
    return dx, dy * xhat


def _rope(xn, cos, sin, first, rot):
    return xn * cos + _rope_swap(xn, first, rot) * sin


def _rope_bwd(dy, cos, sin, first, rot):
    return dy * cos + _rope_swap(dy * sin, first, rot)


def _fold_heads(v):
    return v + pltpu.roll(v, HEAD_DIM, 1)


ATT_ROWS = 256


def _attn_prepare(q_ref, k_ref, v_ref, cos_ref, sin_ref, gq_ref, gk_ref, qs, ks, vs):
    T = q_ref.shape[0]
    lo, first, rot = _head_masks()
    ks[0:BLOCK, :] = jnp.zeros((BLOCK, KV_COLS), BF16)
    vs[0:BLOCK, :] = jnp.zeros((BLOCK, KV_COLS), BF16)

    def step(i, _):
        r0 = pl.multiple_of(i * ATT_ROWS, ATT_ROWS)
        rows = pl.ds(r0, ATT_ROWS)
        prow = pl.ds(r0 + BLOCK, ATT_ROWS)
        cos, sin = cos_ref[rows, :], sin_ref[rows, :]
        for p in range(Q_COLS // LANES):
            cols = slice(p * LANES, (p + 1) * LANES)
            xr = _rope(_head_norm(q_ref[rows, cols], gq_ref[...], lo), cos, sin, first, rot)
            qs[rows, cols] = (xr * (HEAD_DIM ** -0.5)).astype(BF16)
        kr = _rope(_head_norm(k_ref[rows, :], gk_ref[...], lo), cos, sin, first, rot)
        ks[prow, :] = kr.astype(BF16)
        vs[prow, :] = v_ref[rows, :].astype(BF16)
        return 0

    lax.fori_loop(0, T // ATT_ROWS, step, 0)


def _attn_scores(qh, kw, blk, sink):
    s = _dot_nt(qh, kw)
    qi = lax.broadcasted_iota(jnp.int32, (BLOCK, 2 * BLOCK), 0) + BLOCK
    ki = lax.broadcasted_iota(jnp.int32, (BLOCK, 2 * BLOCK), 1)
    rel = qi - ki
    valid = (rel >= 0) & (rel < BLOCK) & ((blk > 0) | (ki >= BLOCK))
    s = jnp.where(valid, s, NEG)
    m = jnp.maximum(jnp.max(s, axis=-1, keepdims=True), sink)
    p = jnp.exp(s - m)
    e_sink = jnp.exp(sink - m)
    inv = 1.0 / (jnp.sum(p, axis=-1, keepdims=True) + e_sink)
    return p * inv, e_sink * inv


def _attn_fwd(proj, cos, sin, gq2, gk2, sinks, name, comm=None):
    T = proj.shape[0]
    nb = T // BLOCK

    def body(q_ref, k_ref, v_ref, cos_ref, sin_ref, gq_ref, gk_ref, sink_ref, y_ref, qs, ks, vs):
        _attn_prepare(q_ref, k_ref, v_ref, cos_ref, sin_ref, gq_ref, gk_ref, qs, ks, vs)

        def blk_step(blk, _):
            r0 = pl.multiple_of(blk * BLOCK, BLOCK)
            for g in range(N_KV_HEADS):
                gc = slice(g * HEAD_DIM, (g + 1) * HEAD_DIM)
                kw = ks[pl.ds(r0, 2 * BLOCK), gc]
                vw = vs[pl.ds(r0, 2 * BLOCK), gc]
                for r in range(Q_PER_KV):
                    h = g * Q_PER_KV + r
                    hc = slice(h * HEAD_DIM, (h + 1) * HEAD_DIM)
                    w, _ws = _attn_scores(qs[pl.ds(r0, BLOCK), hc], kw, blk, sink_ref[0, h])
                    y_ref[pl.ds(r0, BLOCK), hc] = _dot(w.astype(BF16), vw).astype(BF16)
            return 0

        lax.fori_loop(0, nb, blk_step, 0)

    return _pcall(
        body, comm=comm, name=name, grid=(1,),
        in_specs=[
            pl.BlockSpec((T, Q_COLS), lambda i: (0, 0)),
            pl.BlockSpec((T, KV_COLS), lambda i: (0, Q_COLS // KV_COLS)),
            pl.BlockSpec((T, KV_COLS), lambda i: (0, Q_COLS // KV_COLS + 1)),
            _full((T, LANES)), _full((T, LANES)), _full((1, LANES)), _full((1, LANES)),
            pl.BlockSpec(memory_space=pltpu.SMEM),
        ],
        out_specs=_full((T, Q_COLS)),
        out_shape=jax.ShapeDtypeStruct((T, Q_COLS), BF16),
        scratch_shapes=[
            pltpu.VMEM((T, Q_COLS), BF16),
            pltpu.VMEM((T + BLOCK, KV_COLS), BF16),
            pltpu.VMEM((T + BLOCK, KV_COLS), BF16),
        ],
        compiler_params=_params(("arbitrary",)),
    )(proj, proj, proj, cos, sin, gq2, gk2, sinks)


def _attn_bwd(proj, cos, sin, gq2, gk2, sinks, dyc, name, comm=None):
    T = proj.shape[0]
    nb = T // BLOCK

    def body(q_ref, k_ref, v_ref, cos_ref, sin_ref, gq_ref, gk_ref, sink_ref, dy_ref,
             dq_ref, dk_ref, dv_ref, dgq_ref, dgk_ref, dsink_ref, qs, ks, vs, dqs, dks, dvs):
        _attn_prepare(q_ref, k_ref, v_ref, cos_ref, sin_ref, gq_ref, gk_ref, qs, ks, vs)
        dks[...] = jnp.zeros_like(dks)
        dvs[...] = jnp.zeros_like(dvs)
        lane = lax.broadcasted_iota(jnp.int32, (1, LANES), 1)

        def blk_step(blk, dsink):
            r0 = pl.multiple_of(blk * BLOCK, BLOCK)
            win = pl.ds(r0, 2 * BLOCK)
            for g in range(N_KV_HEADS):
                gc = slice(g * HEAD_DIM, (g + 1) * HEAD_DIM)
                kw = ks[win, gc]
                vw = vs[win, gc]
                for r in range(Q_PER_KV):
                    h = g * Q_PER_KV + r
                    hc = slice(h * HEAD_DIM, (h + 1) * HEAD_DIM)
                    qh = qs[pl.ds(r0, BLOCK), hc]
                    w, w_sink = _attn_scores(qh, kw, blk, sink_ref[0, h])
                    do = dy_ref[pl.ds(r0, BLOCK), hc].astype(BF16)
                    dvs[win, gc] += _dot_tn(w.astype(BF16), do)
                    dw = _dot_nt(do, vw)
                    delta = jnp.sum(w * dw, axis=-1, keepdims=True)
                    ds = (w * (dw - delta)).astype(BF16)
                    dsink = dsink + jnp.where(lane == h, -jnp.sum(w_sink * delta, axis=0, keepdims=True), 0.0)
                    dqs[pl.ds(r0, BLOCK), hc] = _dot(ds, kw)
                    dks[win, gc] += _dot_tn(ds, qh)
            return dsink

        dsink_ref[...] = lax.fori_loop(0, nb, blk_step, jnp.zeros((1, LANES), F32))

        lo, first, rot = _head_masks()

        def step(i, carry):
            dgq, dgk = carry
            r0 = pl.multiple_of(i * ATT_ROWS, ATT_ROWS)
            rows = pl.ds(r0, ATT_ROWS)
            prow = pl.ds(r0 + BLOCK, ATT_ROWS)
            cos, sin = cos_ref[rows, :], sin_ref[rows, :]
            for p in range(Q_COLS // LANES):
                cols = slice(p * LANES, (p + 1) * LANES)
                dxn = _rope_bwd(dqs[rows, cols] * (HEAD_DIM ** -0.5), cos, sin, first, rot)
                dx, dgp = _head_norm_bwd(q_ref[rows, cols], gq_ref[...], dxn, lo)
                dq_ref[rows, cols] = dx
                dgq = dgq + jnp.sum(dgp, axis=0, keepdims=True)
            dkn = _rope_bwd(dks[prow, :], cos, sin, first, rot)
            dx, dgp = _head_norm_bwd(k_ref[rows, :], gk_ref[...], dkn, lo)
            dk_ref[rows, :] = dx
            dgk = dgk + jnp.sum(dgp, axis=0, keepdims=True)
            dv_ref[rows, :] = dvs[prow, :]
            return dgq, dgk

        zero = jnp.zeros((1, LANES), F32)
        dgq, dgk = lax.fori_loop(0, T // ATT_ROWS, step, (zero, zero))
        dgq_ref[...] = _fold_heads(dgq)
        dgk_ref[...] = _fold_heads(dgk)

    vec = jax.ShapeDtypeStruct((1, LANES), F32)
    return _pcall(
        body, comm=comm, name=name, grid=(1,),
        in_specs=[
            pl.BlockSpec((T, Q_COLS), lambda i: (0, 0)),
            pl.BlockSpec((T, KV_COLS), lambda i: (0, Q_COLS // KV_COLS)),
            pl.BlockSpec((T, KV_COLS), lambda i: (0, Q_COLS // KV_COLS + 1)),
            _full((T, LANES)), _full((T, LANES)), _full((1, LANES)), _full((1, LANES)),
            pl.BlockSpec(memory_space=pltpu.SMEM),
            pl.BlockSpec((T, Q_COLS), lambda i: (0, 0)),
        ],
        out_specs=[_full((T, Q_COLS)), _full((T, KV_COLS)), _full((T, KV_COLS)),
                   _full((1, LANES)), _full((1, LANES)), _full((1, LANES))],
        out_shape=[jax.ShapeDtypeStruct((T, Q_COLS), F32), jax.ShapeDtypeStruct((T, KV_COLS), F32),
                   jax.ShapeDtypeStruct((T, KV_COLS), F32), vec, vec, vec],
        scratch_shapes=[
            pltpu.VMEM((T, Q_COLS), BF16),
            pltpu.VMEM((T + BLOCK, KV_COLS), BF16),
            pltpu.VMEM((T + BLOCK, KV_COLS), BF16),
            pltpu.VMEM((T, Q_COLS), F32),
            pltpu.VMEM((T + BLOCK, KV_COLS), F32),
            pltpu.VMEM((T + BLOCK, KV_COLS), F32),
        ],
        compiler_params=_params(("arbitrary",)),
    )(proj, proj, proj, cos, sin, gq2, gk2, sinks, dyc)


CONV_PAD = 32
CONV_ROWS = 256


def _conv_taps(src, w_ref, r0, first_off, step_sign):
    acc = None
    for i in range(CONV_WIDTH):
        term = w_ref[i:i + 1, :] * src[r0 + first_off + step_sign * i:r0 + first_off + step_sign * i + CONV_ROWS, :]
        acc = term if acc is None else acc + term
    return acc


def _conv_fwd(proj, w_dw, b_dw, g_ln, b_ln, name, comm=None):
    T = proj.shape[0]
    a_blk = (Q_COLS + 2 * KV_COLS) // CONV_CH

    def body(a_ref, gate_ref, w_ref, bdw_ref, g_ref, b_ref, y_ref, c_ref, pad):
        pad[0:CONV_PAD, :] = jnp.zeros((CONV_PAD, CONV_CH), F32)
        pad[CONV_PAD:, :] = a_ref[...] * _sigmoid(gate_ref[...])
        for n in range(T // CONV_ROWS):
            r0 = n * CONV_ROWS
            c = _conv_taps(pad, w_ref, r0, CONV_PAD - (CONV_WIDTH - 1), 1) + bdw_ref[...]
            c_ref[r0:r0 + CONV_ROWS, :] = c
            mu = jnp.mean(c, axis=-1, keepdims=True)
            cc = c - mu
            rstd = lax.rsqrt(jnp.mean(cc * cc, axis=-1, keepdims=True) + EPS)
            z = cc * rstd * g_ref[...] + b_ref[...]
            y_ref[r0:r0 + CONV_ROWS, :] = (z * _sigmoid(z)).astype(BF16)

    vec = _full((1, CONV_CH))
    return _pcall(
        body, comm=comm, name=name, grid=(1,),
        in_specs=[
            pl.BlockSpec((T, CONV_CH), lambda i: (0, a_blk)),
            pl.BlockSpec((T, CONV_CH), lambda i: (0, a_blk + 1)),
            _full((CONV_WIDTH, CONV_CH)), vec, vec, vec,
        ],
        out_specs=[_full((T, CONV_CH)), _full((T, CONV_CH))],
        out_shape=[jax.ShapeDtypeStruct((T, CONV_CH), BF16), jax.ShapeDtypeStruct((T, CONV_CH), F32)],
        scratch_shapes=[pltpu.VMEM((T + CONV_PAD, CONV_CH), F32)],
        compiler_params=_params(("arbitrary",)),
    )(proj, proj, w_dw, b_dw, g_ln, b_ln)


def _conv_bwd(proj, c, w_dw, g_ln, b_ln, dyc, name, comm=None):
    T = proj.shape[0]
    a_blk = (Q_COLS + 2 * KV_COLS) // CONV_CH
    y_blk = Q_COLS // CONV_CH

    def body(a_ref, gate_ref, c_ref, w_ref, g_ref, b_ref, dy_ref,
             da_ref, dgate_ref, dw_ref, dbdw_ref, dg_ref, db_ref, pad, dcp):
        pad[0:CONV_PAD, :] = jnp.zeros((CONV_PAD, CONV_CH), F32)
        sg = _sigmoid(gate_ref[...])
        pad[CONV_PAD:, :] = a_ref[...] * sg
        dcp[T:, :] = jnp.zeros((CONV_PAD, CONV_CH), F32)
        dg = db = dbdw = jnp.zeros((1, CONV_CH), F32)
        for n in range(T // CONV_ROWS):
            rows = slice(n * CONV_ROWS, (n + 1) * CONV_ROWS)
            cv = c_ref[rows, :]
            mu = jnp.mean(cv, axis=-1, keepdims=True)
            cc = cv - mu
            rstd = lax.rsqrt(jnp.mean(cc * cc, axis=-1, keepdims=True) + EPS)
            chat = cc * rstd
            z = chat * g_ref[...] + b_ref[...]
            sz = _sigmoid(z)
            dz = dy_ref[rows, :] * (sz * (1.0 + z * (1.0 - sz)))
            dg = dg + jnp.sum(dz * chat, axis=0, keepdims=True)
            db = db + jnp.sum(dz, axis=0, keepdims=True)
            dch = dz * g_ref[...]
            dc = rstd * (dch - jnp.mean(dch, axis=-1, keepdims=True)
                         - chat * jnp.mean(dch * chat, axis=-1, keepdims=True))
            dbdw = dbdw + jnp.sum(dc, axis=0, keepdims=True)
            dcp[rows, :] = dc
        dg_ref[...] = dg
        db_ref[...] = db
        dbdw_ref[...] = dbdw
        dw_ref[CONV_WIDTH:, :] = jnp.zeros((CONV_PAD - CONV_WIDTH, CONV_CH), F32)
        for i in range(CONV_WIDTH):
            off = CONV_PAD - (CONV_WIDTH - 1) + i
            acc = jnp.zeros((1, CONV_CH), F32)
            for n in range(T // CONV_ROWS):
                r0 = n * CONV_ROWS
                acc = acc + jnp.sum(dcp[r0:r0 + CONV_ROWS, :] * pad[r0 + off:r0 + off + CONV_ROWS, :],
                                    axis=0, keepdims=True)
            dw_ref[i:i + 1, :] = acc
        for n in range(T // CONV_ROWS):
            r0 = n * CONV_ROWS
            rows = slice(r0, r0 + CONV_ROWS)
            dhg = _conv_taps(dcp, w_ref, r0, CONV_WIDTH - 1, -1)
            sgv = sg[rows, :]
            da_ref[rows, :] = dhg * sgv
            dgate_ref[rows, :] = dhg * a_ref[rows, :] * sgv * (1.0 - sgv)

    vec = _full((1, CONV_CH))
    vshape = jax.ShapeDtypeStruct((1, CONV_CH), F32)
    return _pcall(
        body, comm=comm, name=name, grid=(1,),
        in_specs=[
            pl.BlockSpec((T, CONV_CH), lambda i: (0, a_blk)),
            pl.BlockSpec((T, CONV_CH), lambda i: (0, a_blk + 1)),
            _full((T, CONV_CH)), _full((CONV_WIDTH, CONV_CH)), vec, vec,
            pl.BlockSpec((T, CONV_CH), lambda i: (0, y_blk)),
        ],
        out_specs=[_full((T, CONV_CH)), _full((T, CONV_CH)), _full((CONV_PAD, CONV_CH)), vec, vec, vec],
        out_shape=[jax.ShapeDtypeStruct((T, CONV_CH), F32), jax.ShapeDtypeStruct((T, CONV_CH), F32),
                   jax.ShapeDtypeStruct((CONV_PAD, CONV_CH), F32), vshape, vshape, vshape],
        scratch_shapes=[pltpu.VMEM((T + CONV_PAD, CONV_CH), F32), pltpu.VMEM((T + CONV_PAD, CONV_CH), F32)],
        compiler_params=_params(("arbitrary",)),
    )(proj, proj, c, w_dw, g_ln, b_ln, dyc)


def _mem_kv(mkv_ref, gk_ref, lo, kn_s, vv_s):
    for p in range(MQ_COLS // LANES):
        cols = slice(p * LANES, (p + 1) * LANES)
        kn_s[:, cols] = _head_norm(mkv_ref[:, cols], gk_ref[...], lo).astype(BF16)
    vv_s[...] = mkv_ref[:, MQ_COLS:].astype(BF16)


def _mem_softmax(qh, kh):
    s = _dot_nt(qh, kh)
    m = jnp.max(s, axis=-1, keepdims=True)
    p = jnp.exp(s - m)
    return p / jnp.sum(p, axis=-1, keepdims=True)


def _mem_fwd(proj, mkv, gq2, gk2, name, comm=None):
    T = proj.shape[0]
    tt = ROW_TILE
    q_blk = (IN_COLS - MQ_COLS) // MQ_COLS

    def body(q_ref, mkv_ref, gq_ref, gk_ref, y_ref, kn_s, vv_s, qn_s):
        lo, _, _ = _head_masks()
        _mem_kv(mkv_ref, gk_ref, lo, kn_s, vv_s)
        for p in range(MQ_COLS // LANES):
            cols = slice(p * LANES, (p + 1) * LANES)
            qn_s[:, cols] = (_head_norm(q_ref[:, cols], gq_ref[...], lo) * (HEAD_DIM ** -0.5)).astype(BF16)
        for h in range(N_MEM_HEADS):
            hc = slice(h * HEAD_DIM, (h + 1) * HEAD_DIM)
            w = _mem_softmax(qn_s[:, hc], kn_s[:, hc])
            y_ref[:, hc] = _dot(w.astype(BF16), vv_s[:, hc]).astype(BF16)

    return _pcall(
        body, comm=comm, name=name, grid=(T // tt,),
        in_specs=[
            pl.BlockSpec((tt, MQ_COLS), lambda t: (t, q_blk)),
            pl.BlockSpec((MEM_LEN, 2 * MQ_COLS), lambda t: (0, 0)),
            pl.BlockSpec((1, LANES), lambda t: (0, 0)), pl.BlockSpec((1, LANES), lambda t: (0, 0)),
        ],
        out_specs=pl.BlockSpec((tt, MQ_COLS), lambda t: (t, 0)),
        out_shape=jax.ShapeDtypeStruct((T, MQ_COLS), BF16),
        scratch_shapes=[pltpu.VMEM((MEM_LEN, MQ_COLS), BF16), pltpu.VMEM((MEM_LEN, MQ_COLS), BF16),
                        pltpu.VMEM((tt, MQ_COLS), BF16)],
        compiler_params=_params(("arbitrary",)),
    )(proj, mkv, gq2, gk2)


def _mem_bwd(proj, mkv, gq2, gk2, dyc, name, comm=None):
    T = proj.shape[0]
    tt = ROW_TILE
    nt = T // tt
    q_blk = (IN_COLS - MQ_COLS) // MQ_COLS
    y_blk = (Q_COLS + CONV_CH) // MQ_COLS

    def body(q_ref, mkv_ref, gq_ref, gk_ref, dy_ref, dq_ref, dmkv_ref, dgq_ref, dgk_ref,
             kn_s, vv_s, qn_s, dqn_s, dkn_acc):
        t = pl.program_id(0)
        lo, _, _ = _head_masks()
        _mem_kv(mkv_ref, gk_ref, lo, kn_s, vv_s)

        @pl.when(t == 0)
        def _():
            dkn_acc[...] = jnp.zeros_like(dkn_acc)
            dmkv_ref[...] = jnp.zeros_like(dmkv_ref)
            dgq_ref[...] = jnp.zeros_like(dgq_ref)

        for p in range(MQ_COLS // LANES):
            cols = slice(p * LANES, (p + 1) * LANES)
            qn_s[:, cols] = (_head_norm(q_ref[:, cols], gq_ref[...], lo) * (HEAD_DIM ** -0.5)).astype(BF16)
        for h in range(N_MEM_HEADS):
            hc = slice(h * HEAD_DIM, (h + 1) * HEAD_DIM)
            vc = slice(MQ_COLS + h * HEAD_DIM, MQ_COLS + (h + 1) * HEAD_DIM)
            qh = qn_s[:, hc]
            w = _mem_softmax(qh, kn_s[:, hc])
            do = dy_ref[:, hc].astype(BF16)
            dmkv_ref[:, vc] += _dot_tn(w.astype(BF16), do)
            dw = _dot_nt(do, vv_s[:, hc])
            ds = (w * (dw - jnp.sum(w * dw, axis=-1, keepdims=True))).astype(BF16)
            dqn_s[:, hc] = _dot(ds, kn_s[:, hc])
            dkn_acc[:, hc] += _dot_tn(ds, qh)
        dgq = jnp.zeros((1, LANES), F32)
        for p in range(MQ_COLS // LANES):
            cols = slice(p * LANES, (p + 1) * LANES)
            dx, dgp = _head_norm_bwd(q_ref[:, cols], gq_ref[...], dqn_s[:, cols] * (HEAD_DIM ** -0.5), lo)
            dq_ref[:, cols] = dx
            dgq = dgq + jnp.sum(dgp, axis=0, keepdims=True)
        dgq_ref[...] += dgq

        @pl.when(t == nt - 1)
        def _():
            dgk = jnp.zeros((1, LANES), F32)
            for p in range(MQ_COLS // LANES):
                cols = slice(p * LANES, (p + 1) * LANES)
                dx, dgp = _head_norm_bwd(mkv_ref[:, cols], gk_ref[...], dkn_acc[:, cols], lo)
                dmkv_ref[:, cols] = dx
                dgk = dgk + jnp.sum(dgp, axis=0, keepdims=True)
            dgk_ref[...] = _fold_heads(dgk)
            dgq_ref[...] = _fold_heads(dgq_ref[...])

    vec = pl.BlockSpec((1, LANES), lambda t: (0, 0))
    vshape = jax.ShapeDtypeStruct((1, LANES), F32)
    return _pcall(
        body, comm=comm, name=name, grid=(nt,),
        in_specs=[
            pl.BlockSpec((tt, MQ_COLS), lambda t: (t, q_blk)),
            pl.BlockSpec((MEM_LEN, 2 * MQ_COLS), lambda t: (0, 0)),
            vec, vec,
            pl.BlockSpec((tt, MQ_COLS), lambda t: (t, y_blk)),
        ],
        out_specs=[pl.BlockSpec((tt, MQ_COLS), lambda t: (t, 0)),
                   pl.BlockSpec((MEM_LEN, 2 * MQ_COLS), lambda t: (0, 0)), vec, vec],
        out_shape=[jax.ShapeDtypeStruct((T, MQ_COLS), F32), jax.ShapeDtypeStruct((MEM_LEN, 2 * MQ_COLS), F32),
                   vshape, vshape],
        scratch_shapes=[pltpu.VMEM((MEM_LEN, MQ_COLS), BF16), pltpu.VMEM((MEM_LEN, MQ_COLS), BF16),
                        pltpu.VMEM((tt, MQ_COLS), BF16), pltpu.VMEM((tt, MQ_COLS), F32),
                        pltpu.VMEM((MEM_LEN, MQ_COLS), F32)],
        compiler_params=_params(("arbitrary",)),
    )(proj, mkv, gq2, gk2, dyc)


def _loss_head(y, target, name, comm=None):
    T, D = y.shape
    tt = ROW_TILE

    def body(y_ref, t_ref, dy_ref, loss_ref):
        t = pl.program_id(0)
        err = y_ref[...] - t_ref[...]
        dy_ref[...] = err * (1.0 / D)
        part = 0.5 * jnp.sum(jnp.mean(err * err, axis=-1, keepdims=True), axis=0, keepdims=True)

        @pl.when(t == 0)
        def _():
            loss_ref[...] = jnp.zeros_like(loss_ref)

        loss_ref[...] += jnp.broadcast_to(part, loss_ref.shape)

    tile = pl.BlockSpec((tt, D), lambda t: (t, 0))
    return _pcall(
        body, comm=comm, name=name, grid=(T // tt,),
        in_specs=[tile, tile],
        out_specs=[tile, pl.BlockSpec((1, LANES), lambda t: (0, 0))],
        out_shape=[jax.ShapeDtypeStruct((T, D), F32), jax.ShapeDtypeStruct((1, LANES), F32)],
        compiler_params=_params(("arbitrary",)),
    )(y, target)


def _adamw(w, g, m, v, name, comm=None):
    R, C = w.shape
    tr = next((r for r in (512, 352, 256, 128) if R % r == 0), R)

    def body(w_ref, g_ref, m_ref, v_ref, go_ref, d_ref, nm_ref, nv_ref):
        gv = g_ref[...]
        go_ref[...] = gv
        nm = ADAM_B1 * m_ref[...] + (1.0 - ADAM_B1) * gv
        nv = ADAM_B2 * v_ref[...] + (1.0 - ADAM_B2) * (gv * gv)
        m_hat = nm / (1.0 - ADAM_B1 ** ADAM_STEP)
        v_hat = nv / (1.0 - ADAM_B2 ** ADAM_STEP)
        d_ref[...] = -ADAM_LR * (m_hat / (jnp.sqrt(v_hat) + ADAM_EPS) + ADAM_WD * w_ref[...])
        nm_ref[...] = nm
        nv_ref[...] = nv

    tile = pl.BlockSpec((tr, C), lambda i: (i, 0))
    shape = jax.ShapeDtypeStruct((R, C), F32)
    return _pcall(
        body, comm=comm, name=name, grid=(R // tr,),
        in_specs=[tile] * 4, out_specs=[tile] * 4, out_shape=[shape] * 4,
        compiler_params=_params(("arbitrary",)),
    )(w, g, m, v)


def _mesh_pos():
    return lax.axis_index("x"), lax.axis_index("y"), lax.axis_index("c")


def _other_chips(x, y):
    return [(1 - x, y), (x, 1 - y), (1 - x, 1 - y)]


def _quarter(ref, layout, q, rows, cols):
    if layout == "cols":
        return ref.at[rows, pl.ds(pl.multiple_of(q * cols, LANES), cols)]
    return ref.at[q, rows, :]


def _gather_weights(shards, dtypes, layouts, later, name, comm=None):
    n = len(shards)
    all_rows = slice(None)
    remote = [i for i in range(n) if i not in later]
    split = [i for i in remote if layouts[i] != "whole"]

    def body(*refs):
        ins, outs = refs[:n], refs[n:2 * n]
        st32, st16 = refs[2 * n:3 * n], refs[3 * n:4 * n]
        in_sems, own_sems, send_sems, recv_sems, fwd_send_sems, fwd_recv_sems = refs[4 * n:]
        x, y, c = _mesh_pos()
        chip = 2 * x + y
        sibling = (x, y, 1 - c)
        chips = _other_chips(x, y)

        def half(i, which):
            if i not in split:
                return all_rows
            hr = shards[i].shape[0] // 2
            return pl.ds(pl.multiple_of(which * hr, 16), hr)

        def place(i, q, rows):
            return _quarter(outs[i], layouts[i], q, rows, shards[i].shape[1])

        def ici(i, k, origin_chip, src):
            px, py = chips[k]
            return pltpu.make_async_remote_copy(
                src_ref=src, dst_ref=place(i, origin_chip, half(i, c)), send_sem=send_sems.at[i, k],
                recv_sem=recv_sems.at[i, k], device_id=(px, py, c), device_id_type=MESH)

        def forward(i, k, rows):
            px, py = chips[k]
            there = place(i, 2 * px + py, rows)
            return pltpu.make_async_remote_copy(
                src_ref=there, dst_ref=there, send_sem=fwd_send_sems.at[i, k],
                recv_sem=fwd_recv_sems.at[i, k], device_id=sibling, device_id_type=MESH)

        loads = [pltpu.make_async_copy(ins[i], st32[i], in_sems.at[i]) for i in range(n)]
        for cp in loads:
            cp.start()
        owns, sent = [], []
        for i in range(n):
            loads[i].wait()
            st16[i][...] = st32[i][...].astype(dtypes[i])
            own = pltpu.make_async_copy(st16[i], place(i, chip, all_rows), own_sems.at[i])
            own.start()
            owns.append(own)
            for k in range(3 if i in remote else 0):
                cp = ici(i, k, chip, st16[i].at[half(i, c)])
                cp.start()
                sent.append(cp)
        for i in remote:
            for k, (px, py) in enumerate(chips):
                ici(i, k, 2 * px + py, st16[i].at[half(i, c)]).wait_recv()
                if i in split:
                    cp = forward(i, k, half(i, c))
                    cp.start()
                    sent.append(cp)
        for i in split:
            for k in range(3):
                forward(i, k, half(i, 1 - c)).wait_recv()
        for cp in sent:
            cp.wait_send()
        for cp in owns:
            cp.wait()

    hbm = pl.BlockSpec(memory_space=pl.ANY)
    return _pcall(
        body, comm=comm, name=name,
        in_specs=[hbm] * n, out_specs=[hbm] * n,
        out_shape=[_gathered_shape(s.shape, d, lay) for s, d, lay in zip(shards, dtypes, layouts)],
        scratch_shapes=[pltpu.VMEM(s.shape, F32) for s in shards] + [pltpu.VMEM(s.shape, d) for s, d in zip(shards, dtypes)]
        + [pltpu.SemaphoreType.DMA((n,)), pltpu.SemaphoreType.DMA((n,)),
           pltpu.SemaphoreType.DMA((n, 3)), pltpu.SemaphoreType.DMA((n, 3)),
           pltpu.SemaphoreType.DMA((n, 3)), pltpu.SemaphoreType.DMA((n, 3))],
        compiler_params=pltpu.CompilerParams(vmem_limit_bytes=VMEM_LIMIT),
    )(*shards)


def _gathered_shape(quarter_shape, dtype, layout):
    R, C = quarter_shape
    return jax.ShapeDtypeStruct((R, N_CHIPS * C) if layout == "cols" else (N_CHIPS, R, C), dtype)


def _remote(src, dst, sems, j, device):
    return pltpu.make_async_remote_copy(src_ref=src, dst_ref=dst, send_sem=sems.at[2 * j], recv_sem=sems.at[2 * j + 1],
                                        device_id=device, device_id_type=MESH)


class _SemWindow:
    def __init__(self, sems, offset):
        self.sems, self.offset = sems, offset

    @property
    def at(self):
        return self

    def __getitem__(self, i):
        return self.sems.at[self.offset + i]


def _join_plans(a, b):
    n_ai, n_ao = len(a.ins), len(a.out_shapes)

    def plan(ins, outs, sems, finishing):
        first = a.plan(ins[:n_ai], outs[:n_ao], sems, finishing)
        second = b.plan(ins[n_ai:], outs[n_ao:], _SemWindow(sems, a.n_sems), finishing)
        return tuple(p + q for p, q in zip(first, second))

    aliases = {**a.aliases, **{n_ai + i: n_ao + o for i, o in b.aliases.items()}}
    return _Comm(a.ins + b.ins, a.out_shapes + b.out_shapes, a.n_sems + b.n_sems, plan, aliases)


def _half_rows(rows, which):
    hr = rows // 2
    return pl.ds(pl.multiple_of(which * hr, 16), hr)


ICI_PARTS = 1


def _row_parts(rows):
    tiles = rows // 16
    sizes = [16 * (tiles // ICI_PARTS + (p < tiles % ICI_PARTS)) for p in range(ICI_PARTS)]
    return [(sum(sizes[:p]), size) for p, size in enumerate(sizes) if size]


def _half_parts(rows, which):
    hr = rows // 2
    return [pl.ds(pl.multiple_of(which * hr + start, 16), size) for start, size in _row_parts(hr)]


def _spread_plan(fulls, quarter_shapes, layouts, peers=(0, 1, 2)):
    def plan(ins, outs, sems, finishing):
        x, y, c = _mesh_pos()
        chip = 2 * x + y
        sends, recvs = [], []
        for i, full in enumerate(outs):
            R, C = quarter_shapes[i]
            for k, (px, py) in enumerate(_other_chips(x, y)):
                if k not in peers:
                    continue
                for p, part in enumerate(_half_parts(R, c)):
                    j = ICI_PARTS * (3 * i + k) + p
                    mine = _quarter(full, layouts[i], chip, part, C)
                    sends.append(_remote(mine, mine, sems, j, (px, py, c)))
                    if finishing:
                        recvs.append(_remote(mine, _quarter(full, layouts[i], 2 * px + py, part, C), sems, j, (px, py, c)))
        return [], sends, recvs

    shapes = [jax.ShapeDtypeStruct(f.shape, f.dtype) for f in fulls]
    return _Comm(fulls, shapes, 2 * 3 * ICI_PARTS * len(fulls), plan, aliases={i: i for i in range(len(fulls))})


def _forward_plan(fulls, quarter_shapes, layouts):
    def plan(ins, outs, sems, finishing):
        x, y, c = _mesh_pos()
        sends, recvs = [], []
        for i, full in enumerate(outs):
            R, C = quarter_shapes[i]
            for k, (px, py) in enumerate(_other_chips(x, y)):
                mine = _quarter(full, layouts[i], 2 * px + py, _half_rows(R, c), C)
                sends.append(_remote(mine, mine, sems, 3 * i + k, (x, y, 1 - c)))
                if finishing:
                    theirs = _quarter(full, layouts[i], 2 * px + py, _half_rows(R, 1 - c), C)
                    recvs.append(_remote(mine, theirs, sems, 3 * i + k, (x, y, 1 - c)))
        return [], sends, recvs

    shapes = [jax.ShapeDtypeStruct(f.shape, f.dtype) for f in fulls]
    return _Comm(fulls, shapes, 6 * len(fulls), plan, aliases={i: i for i in range(len(fulls))})


def _swap_plan(grads):
    def plan(ins, outs, sems, finishing):
        x, y, c = _mesh_pos()
        sends = [_remote(g.at[:, _half_rows(g.shape[1], 1 - c), :], sib, sems, i, (x, y, 1 - c))
                 for i, (g, sib) in enumerate(zip(ins, outs))]
        return [], sends, sends

    shapes = [jax.ShapeDtypeStruct((N_CHIPS, g.shape[1] // 2, g.shape[2]), BF16) for g in grads]
    return _Comm(grads, shapes, 2 * len(grads), plan)


def _pair_sums(gs, sibs, name, comm=None):
    n = len(gs)

    def body(*refs):
        c = lax.axis_index("c")
        for g_ref, sib_ref, o_ref in zip(refs[:n], refs[n:2 * n], refs[2 * n:]):
            mine = _half_rows(g_ref.shape[1], c)
            o_ref[0] = (g_ref[0, mine, :].astype(F32) + sib_ref[0].astype(F32)).astype(BF16)

    def chunk(shape):
        return pl.BlockSpec((1,) + shape[1:], lambda q: (q, 0, 0))

    return _pcall(
        body, comm=comm, name=name, grid=(N_CHIPS,),
        in_specs=[chunk(g.shape) for g in gs] + [chunk(s.shape) for s in sibs],
        out_specs=[chunk(s.shape) for s in sibs],
        out_shape=[jax.ShapeDtypeStruct(s.shape, BF16) for s in sibs],
        compiler_params=_params(("arbitrary",)),
    )(*gs, *sibs)


def _ici_plan(sums):
    def plan(ins, outs, sems, finishing):
        x, y, c = _mesh_pos()
        sends = []
        for i, (s, rcv) in enumerate(zip(ins, outs)):
            for k, (px, py) in enumerate(_other_chips(x, y)):
                for p, (start, size) in enumerate(_row_parts(s.shape[1])):
                    rows = pl.ds(start, size)
                    sends.append(_remote(s.at[2 * px + py, rows, :], rcv.at[k, rows, :], sems,
                                         ICI_PARTS * (3 * i + k) + p, (px, py, c)))
        return [], sends, sends

    shapes = [jax.ShapeDtypeStruct((3,) + s.shape[1:], BF16) for s in sums]
    return _Comm(sums, shapes, 2 * 3 * ICI_PARTS * len(sums), plan)


def _run_comm(comm, name):
    def body():
        pass

    _, landed = _pcall(body, comm=comm, name=name, grid=(1,), in_specs=[], out_specs=[], out_shape=[])()
    return landed


def _finish_quarters(ss, rcvs, name, comm=None):
    n = len(ss)

    def body(*refs):
        s_refs, rcv_refs, out_refs, sems = refs[:n], refs[n:2 * n], refs[2 * n:3 * n], refs[3 * n]
        x, y, c = _mesh_pos()
        swaps = []
        for i, (s_ref, rcv_ref, out_ref) in enumerate(zip(s_refs, rcv_refs, out_refs)):
            mine = _half_rows(out_ref.shape[0], c)
            acc = s_ref[2 * x + y].astype(F32)
            for k in range(3):
                acc = acc + rcv_ref[k].astype(F32)
            out_ref[mine, :] = acc
            back = _remote(out_ref.at[mine, :], out_ref.at[mine, :], sems, i, (x, y, 1 - c))
            back.start()
            swaps.append(back)
        for back in swaps:
            back.wait()

    vmem = pl.BlockSpec(memory_space=pltpu.VMEM)
    return _pcall(
        body, comm=comm, name=name, grid=(1,),
        in_specs=[vmem] * (2 * n), out_specs=[vmem] * n,
        out_shape=[jax.ShapeDtypeStruct((2 * s.shape[1], s.shape[2]), F32) for s in ss],
        scratch_shapes=[pltpu.SemaphoreType.DMA((2 * n,))],
        compiler_params=pltpu.CompilerParams(vmem_limit_bytes=VMEM_LIMIT),
    )(*ss, *rcvs)


def _allreduce_small(v, name, comm=None):
    R, C = v.shape
    n_dev = 8

    def body(v_ref, out_ref, buf, send_sems, recv_sems):
        x, y, c = _mesh_pos()
        me = 4 * x + 2 * y + c
        buf[me] = v_ref[...]
        peers = []
        for k in range(1, n_dev):
            kx, ky, kc = (k >> 2) & 1, (k >> 1) & 1, k & 1
            px = 1 - x if kx else x
            py = 1 - y if ky else y
            pc = 1 - c if kc else c
            peers.append((px, py, pc))
        sends = []
        for k, peer in enumerate(peers):
            cp = pltpu.make_async_remote_copy(
                src_ref=v_ref, dst_ref=buf.at[me], send_sem=send_sems.at[k], recv_sem=recv_sems.at[k],
                device_id=peer, device_id_type=MESH)
            cp.start()
            sends.append(cp)
        for k, (px, py, pc) in enumerate(peers):
            pltpu.make_async_remote_copy(
                src_ref=v_ref, dst_ref=buf.at[4 * px + 2 * py + pc], send_sem=send_sems.at[k],
                recv_sem=recv_sems.at[k], device_id=(px, py, pc), device_id_type=MESH).wait_recv()
        for cp in sends:
            cp.wait_send()
        acc = buf[0]
        for i in range(1, n_dev):
            acc = acc + buf[i]
        out_ref[...] = acc

    vmem = pl.BlockSpec(memory_space=pltpu.VMEM)
    return _pcall(
        body, comm=comm, name=name,
        in_specs=[vmem], out_specs=vmem,
        out_shape=jax.ShapeDtypeStruct((R, C), F32),
        scratch_shapes=[pltpu.VMEM((n_dev, R, C), F32),
                        pltpu.SemaphoreType.DMA((n_dev - 1,)), pltpu.SemaphoreType.DMA((n_dev - 1,))],
        compiler_params=pltpu.CompilerParams(vmem_limit_bytes=VMEM_LIMIT),
    )(v)


def _rope_tables(positions):
    half = ROPE_DIM // 2
    inv_freq = ROPE_THETA ** (-jnp.arange(half, dtype=F32) / half)
    ang = positions.astype(F32)[:, None] * inv_freq
    cos, sin = jnp.cos(ang), jnp.sin(ang)
    T = positions.shape[0]
    ones = jnp.ones((T, HEAD_DIM - ROPE_DIM), F32)
    c64 = jnp.concatenate([cos, cos, ones], axis=1)
    s64 = jnp.concatenate([-sin, sin, 0.0 * ones], axis=1)
    return jnp.tile(c64, (1, 2)), jnp.tile(s64, (1, 2))


def _local_step(x, mem, positions, target, small, first, own):
    cos, sin = _rope_tables(positions)
    two = lambda g: jnp.tile(g, (1, 2))
    gq2, gk2, gmq2, gmk2 = two(small["g_q"]), two(small["g_k"]), two(small["g_mq"]), two(small["g_mk"])
    mix_names = ["w_in", "w_mkv", "w_out"]
    mix_layouts = ["cols", "stack", "stack"]
    mix_quarters = [(D_MODEL, IN_CHUNK), (D_MODEL // N_CHIPS, 2 * MQ_COLS), (D_MODEL // N_CHIPS, D_MODEL)]
    ffn2_names = ["wg2", "wu2", "wd2"]
    ffn_quarter = (FF_CHUNK, D_MODEL)

    spread = _spread_plan([own["w_in"], own["w_mkv"], own["wg2"]], mix_quarters[:2] + [ffn_quarter], ["cols", "stack", "stack"])
    (x1, h1, a1, b1), (half_in, half_mkv, half_wg2) = _ffn_fwd(
        x, small["g_ffn1"], first["wg1"], first["wu1"], first["wd1"], "ffn1_fwd", comm=spread)
    hm, (w_in, w_mkv) = _rms_fwd(x1, small["g_mix"], "mix_norm",
                                 comm=_forward_plan([half_in, half_mkv], mix_quarters[:2], mix_layouts[:2]))
    w_mkv = w_mkv.reshape(D_MODEL, 2 * MQ_COLS)
    proj, (half_out,) = _mm_nn([hm], w_in, None, "in_proj",
                               comm=_spread_plan([own["w_out"]], mix_quarters[2:], mix_layouts[2:]))
    hmem = _rms_fwd(mem, small["g_mem"], "mem_norm")
    mkv = _mm_nn([hmem], w_mkv, None, "mem_proj")
    ya, (half_wu2,) = _attn_fwd(proj, cos, sin, gq2, gk2, small["sinks"], "swa_fwd",
                                comm=_spread_plan([own["wu2"]], [ffn_quarter], ["stack"]))
    (yc, cpre), (near_wd2, w_out) = _conv_fwd(
        proj, small["w_dw"], small["b_dw"], small["g_conv_ln"], small["b_conv_ln"], "conv_fwd",
        comm=_join_plans(_spread_plan([own["wd2"]], [ffn_quarter], ["stack"], peers=(0, 1)),
                         _forward_plan([half_out], mix_quarters[2:], mix_layouts[2:])))
    w_out = w_out.reshape(D_MODEL, D_MODEL)
    ym, (half_wd2,) = _mem_fwd(proj, mkv, gmq2, gmk2, "memattn_fwd",
                               comm=_spread_plan([near_wd2], [ffn_quarter], ["stack"], peers=(2,)))
    passing = _forward_plan([half_wg2, half_wu2, half_wd2], [ffn_quarter] * 3, ["stack"] * 3)
    x2, (wg2, wu2, wd2) = _mm_nn([ya, yc, ym], w_out, x1, "out_proj", comm=passing)
    x3, h2, a2, b2 = _ffn_fwd(x2, small["g_ffn2"], wg2, wu2, wd2, "ffn2_fwd")
    dx3, loss = _loss_head(x3, target, "loss_head")

    dh2, dwg2, dwu2, dwd2 = _ffn_bwd(dx3, h2, a2, b2, wg2, wu2, wd2, "ffn2_bwd")
    (dx2, dg_ffn2), sibs = _rms_bwd(x2, small["g_ffn2"], dh2, dx3, "ffn2_norm_bwd", comm=_swap_plan([dwg2, dwu2, dwd2]))
    sums_ffn2 = _pair_sums([dwg2, dwu2, dwd2], sibs, "pair_sums_ffn2")
    dyc = _mm_nt([dx2], w_out, "out_proj_bwd")
    dw_out = _mm_tn([ya, yc, ym], [dx2], 1, "out_proj_wgrad").reshape(N_CHIPS, -1, D_MODEL)
    (dq, dk, dv, dgq, dgk, dsinks), rcv_ffn2 = _attn_bwd(proj, cos, sin, gq2, gk2, small["sinks"], dyc, "swa_bwd",
                                                          comm=_ici_plan(sums_ffn2))
    (da, dgate, dw_dw, db_dw, dg_ln, db_ln), sibs = _conv_bwd(
        proj, cpre, small["w_dw"], small["g_conv_ln"], small["b_conv_ln"], dyc, "conv_bwd", comm=_swap_plan([dw_out]))
    dmq, dmkv, dgmq, dgmk = _mem_bwd(proj, mkv, gmq2, gmk2, dyc, "memattn_bwd")
    sums_out = _pair_sums([dw_out], sibs, "pair_sums_out")
    pieces = [dq, dk, dv, da, dgate, dmq]
    dhm, rcv_out = _mm_nt(pieces, w_in, "in_proj_bwd", comm=_ici_plan(sums_out))
    dw_in = _mm_tn([hm], pieces, N_CHIPS, "in_proj_wgrad")
    dhmem = _mm_nt([dmkv], w_mkv, "mem_proj_bwd")
    dw_mkv = _mm_tn([hmem], [dmkv], 1, "mem_proj_wgrad").reshape(N_CHIPS, -1, 2 * MQ_COLS)
    _, dg_mem = _rms_bwd(mem, small["g_mem"], dhmem, None, "mem_norm_bwd")
    (dx1, dg_mix), sibs = _rms_bwd(x1, small["g_mix"], dhm, dx2, "mix_norm_bwd", comm=_swap_plan([dw_in, dw_mkv]))
    sums_in = _pair_sums([dw_in, dw_mkv], sibs, "pair_sums_in")
    (dh1, dwg1, dwu1, dwd1), rcv_in = _ffn_bwd(dx1, h1, a1, b1, first["wg1"], first["wu1"], first["wd1"], "ffn1_bwd",
                                                comm=_ici_plan(sums_in))
    (dx, dg_ffn1), sibs = _rms_bwd(x, small["g_ffn1"], dh1, dx1, "ffn1_norm_bwd", comm=_swap_plan([dwg1, dwu1, dwd1]))
    sums_ffn1 = _pair_sums([dwg1, dwu1, dwd1], sibs, "pair_sums_ffn1")
    g_ffn2, rcv_wg1 = _finish_quarters(sums_ffn2, rcv_ffn2, "finish_ffn2", comm=_ici_plan(sums_ffn1[:1]))
    g_mix, rcv_wu1 = _finish_quarters(sums_in + sums_out, rcv_in + rcv_out, "finish_mix", comm=_ici_plan(sums_ffn1[1:2]))
    rcv_wd1 = _run_comm(_ici_plan(sums_ffn1[2:]), "ffn1_grads_ici")
    g_ffn1 = _finish_quarters(sums_ffn1, rcv_wg1 + rcv_wu1 + rcv_wd1, "finish_ffn1")
    big_grads = dict(zip(["wg1", "wu1", "wd1", "w_in", "w_mkv", "w_out"] + ffn2_names, [*g_ffn1, *g_mix, *g_ffn2]))
    small_grads = dict(
        g_ffn1=dg_ffn1, g_mix=dg_mix, g_q=dgq[:, :HEAD_DIM], g_k=dgk[:, :HEAD_DIM], sinks=dsinks[:, :N_Q_HEADS],
        w_dw=dw_dw[:CONV_WIDTH], b_dw=db_dw, g_conv_ln=dg_ln, b_conv_ln=db_ln, g_mem=dg_mem,
        g_mq=dgmq[:, :HEAD_DIM], g_mk=dgmk[:, :HEAD_DIM], g_ffn2=dg_ffn2)
    return loss, dx, big_grads, small_grads


SMALL_NAMES = ["g_ffn1", "g_mix", "g_q", "g_k", "sinks", "b_dw", "g_conv_ln", "b_conv_ln", "g_mem", "g_mq", "g_mk",
               "g_ffn2"]
PACK_COLS = 1024


def _pack(parts):
    flat = [p.reshape(-1) for p in parts]
    offs, o = [], 0
    for f in flat:
        offs.append(o)
        o += f.shape[0]
    rows = -(-o // (8 * PACK_COLS)) * 8
    pad = jnp.zeros((rows * PACK_COLS - o,), F32)
    return jnp.concatenate(flat + [pad]).reshape(rows, PACK_COLS), offs


def _unpack(packed, offs, shapes):
    flat = packed.reshape(-1)
    return [flat[o:o + math.prod(s)].reshape(s) for o, s in zip(offs, shapes)]


def kernel(x, mem, positions, g_ffn1, w_ffn1_gate, w_ffn1_up, w_ffn1_down, g_mix, w_in, g_q, g_k, sinks, w_dw, b_dw, g_conv_ln, b_conv_ln, g_mem, w_mem_kv, g_mq, g_mk, w_out, g_ffn2, w_ffn2_gate, w_ffn2_up, w_ffn2_down, loss_target, m_g_ffn1, m_w_ffn1_gate, m_w_ffn1_up, m_w_ffn1_down, m_g_mix, m_w_in, m_g_q, m_g_k, m_sinks, m_w_dw, m_b_dw, m_g_conv_ln, m_b_conv_ln, m_g_mem, m_w_mem_kv, m_g_mq, m_g_mk, m_w_out, m_g_ffn2, m_w_ffn2_gate, m_w_ffn2_up, m_w_ffn2_down, v_g_ffn1, v_w_ffn1_gate, v_w_ffn1_up, v_w_ffn1_down, v_g_mix, v_w_in, v_g_q, v_g_k, v_sinks, v_w_dw, v_b_dw, v_g_conv_ln, v_b_conv_ln, v_g_mem, v_w_mem_kv, v_g_mq, v_g_mk, v_w_out, v_g_ffn2, v_w_ffn2_gate, v_w_ffn2_up, v_w_ffn2_down):
    args = dict(locals())
    weight_names = ["g_ffn1", "w_ffn1_gate", "w_ffn1_up", "w_ffn1_down", "g_mix", "w_in", "g_q", "g_k", "sinks",
                    "w_dw", "b_dw", "g_conv_ln", "b_conv_ln", "g_mem", "w_mem_kv", "g_mq", "g_mk", "w_out", "g_ffn2",
                    "w_ffn2_gate", "w_ffn2_up", "w_ffn2_down"]
    big_names = ["w_ffn1_gate", "w_ffn1_up", "w_ffn1_down", "w_in", "w_mem_kv", "w_out",
                 "w_ffn2_gate", "w_ffn2_up", "w_ffn2_down"]
    short = dict(w_ffn1_gate="wg1", w_ffn1_up="wu1", w_ffn1_down="wd1", w_in="w_in", w_mem_kv="w_mkv",
                 w_out="w_out", w_ffn2_gate="wg2", w_ffn2_up="wu2", w_ffn2_down="wd2")

    transposed = ("w_ffn1_gate", "w_ffn1_up", "w_ffn2_gate", "w_ffn2_up")

    def quarter(a, n):
        return jnp.swapaxes(a, 1, 2)[0] if n in transposed else a[0]

    def unquarter(a, n):
        return jnp.swapaxes(a[None], 1, 2) if n in transposed else a[None]

    shards = [quarter(args[n], n) for n in big_names]
    layouts = ["cols" if n == "w_in" else "stack" for n in big_names]
    later = [i for i, n in enumerate(big_names) if short[n] not in ("wg1", "wu1", "wd1")]
    gathered = _gather_weights(shards + [w_dw[0]], [BF16] * len(shards) + [F32], layouts + ["whole"], later,
                               "gather_first")
    first = {short[n]: gathered[i] for i, n in enumerate(big_names) if i not in later}
    own = {short[n]: gathered[i] for i, n in enumerate(big_names) if i in later}
    small = {n: args[n] for n in SMALL_NAMES}
    small["w_dw"] = jnp.transpose(gathered[-1], (1, 0, 2)).reshape(CONV_WIDTH, CONV_CH)

    loss, dx, big_grads, small_grads = _local_step(x[0], mem[0], positions[0], loss_target[0], small, first, own)

    small_order = SMALL_NAMES + ["w_dw"]
    packed, offs = _pack([small_grads[n] for n in small_order] + [loss[:, :1]])
    total = _allreduce_small(packed, "allreduce_small")
    shapes = [small_grads[n].shape for n in small_order] + [(1, 1)]
    summed = dict(zip(small_order + ["loss"], _unpack(total, offs, shapes)))
    chip = 2 * lax.axis_index("x") + lax.axis_index("y")
    dw_dw_full = summed.pop("w_dw")
    loss_out = summed.pop("loss").reshape(())

    grads = {n: summed[n] for n in SMALL_NAMES}
    grads["w_dw"] = lax.dynamic_slice_in_dim(dw_dw_full, chip * (CONV_CH // N_CHIPS), CONV_CH // N_CHIPS, axis=1)
    for n in big_names:
        grads[n] = big_grads[short[n]]

    delta, new_m, new_v = {}, {}, {}
    for n in big_names:
        g, d, nm, nv = _adamw(quarter(args[n], n), grads[n], quarter(args["m_" + n], n), quarter(args["v_" + n], n),
                              "adamw_" + short[n])
        grads[n], delta[n], new_m[n], new_v[n] = (unquarter(a, n) for a in (g, d, nm, nv))
    tiny = SMALL_NAMES + ["w_dw"]
    pw, poffs = _pack([args[n] for n in tiny])
    pg, _ = _pack([grads[n] for n in tiny])
    pm, _ = _pack([args["m_" + n] for n in tiny])
    pv, _ = _pack([args["v_" + n] for n in tiny])
    _, pd, pnm, pnv = _adamw(pw, pg, pm, pv, "adamw_small")
    tshapes = [args[n].shape for n in tiny]
    for store, packed_out in ((delta, pd), (new_m, pnm), (new_v, pnv)):
        for n, val in zip(tiny, _unpack(packed_out, poffs, tshapes)):
            store[n] = val

    def shaped(n, v):
        return v.reshape(args[n].shape)

    return (loss_out, dx[None],
            *[shaped(n, grads[n]) for n in weight_names],
            *[shaped(n, delta[n]) for n in weight_names],
            *[shaped(n, new_m[n]) for n in weight_names],
            *[shaped(n, new_v[n]) for n in weight_names])
```

```python
import functools
import math

import jax
import jax.numpy as jnp
from jax import lax
from jax.experimental import pallas as pl
from jax.experimental.pallas import tpu as pltpu

F32 = jnp.float32
BF16 = jnp.bfloat16

D_MODEL = 1024
SEQ = 2048
MEM_LEN = 256
HEAD_DIM = 64
N_Q_HEADS = 8
N_KV_HEADS = 2
Q_PER_KV = 4
N_MEM_HEADS = 4
BLOCK = 128
CONV_CH = 256
CONV_WIDTH = 31
ROPE_THETA = 500000.0
ROPE_DIM = 16
D_FF = 2816
EPS = 1e-6
Q_COLS = 512
KV_COLS = 128
MQ_COLS = 256
IN_COLS = 1536

N_CHIPS = 4
FF_CHUNK = D_FF // N_CHIPS
IN_CHUNK = IN_COLS // N_CHIPS

ADAM_LR = 0.001
ADAM_B1 = 0.9
ADAM_B2 = 0.999
ADAM_EPS = 1e-08
ADAM_WD = 0.01
ADAM_STEP = 10

LANES = 128
VMEM_LIMIT = 56 * 1024 * 1024
ROW_TILE = 512
MESH = pl.DeviceIdType.MESH
NEG = -1e30


class _Comm:
    def __init__(self, ins, out_shapes, n_sems, plan, aliases=None):
        self.ins, self.out_shapes, self.n_sems, self.plan = list(ins), list(out_shapes), n_sems, plan
        self.aliases = aliases or {}


def _pcall(body, comm=None, **kw):
    if comm is None:
        return pl.pallas_call(body, **kw)
    grid = kw["grid"]
    in_specs = list(kw["in_specs"])
    single = not isinstance(kw["out_shape"], (list, tuple))
    out_specs = [kw["out_specs"]] if single else list(kw["out_specs"])
    out_shape = [kw["out_shape"]] if single else list(kw["out_shape"])
    scratch = list(kw.get("scratch_shapes", ()))
    n_in, n_out, n_scr = len(in_specs), len(out_shape), len(scratch)
    n_ci, n_co = len(comm.ins), len(comm.out_shapes)

    def wrapped(*refs):
        o = 0
        parts = []
        for cnt in (n_in, n_ci, n_out, n_co, n_scr):
            parts.append(refs[o:o + cnt])
            o += cnt
        ins, c_ins, outs, c_outs, scr = parts
        sems = refs[o]
        first = last = None
        for d, size in enumerate(grid):
            at0, at_end = pl.program_id(d) == 0, pl.program_id(d) == size - 1
            first = at0 if first is None else first & at0
            last = at_end if last is None else last & at_end

        @pl.when(first)
        def _():
            local, sends, _ = comm.plan(c_ins, c_outs, sems, False)
            for cp in sends + local:
                cp.start()

        body(*ins, *outs, *scr)

        @pl.when(last)
        def _():
            local, sends, recvs = comm.plan(c_ins, c_outs, sems, True)
            for cp in recvs:
                cp.wait_recv()
            for cp in sends:
                cp.wait_send()
            for cp in local:
                cp.wait()

    hbm = pl.BlockSpec(memory_space=pl.ANY)
    kw = dict(kw, in_specs=in_specs + [hbm] * n_ci, out_specs=out_specs + [hbm] * n_co,
              out_shape=out_shape + comm.out_shapes,
              scratch_shapes=scratch + [pltpu.SemaphoreType.DMA((comm.n_sems,))])
    if comm.aliases:
        kw["input_output_aliases"] = {n_in + i: n_out + o for i, o in comm.aliases.items()}
    call = pl.pallas_call(wrapped, **kw)

    def run(*args):
        res = call(*args, *comm.ins)
        return (res[0] if single else list(res[:n_out])), list(res[n_out:])

    return run


def _params(sem=None):
    return pltpu.CompilerParams(dimension_semantics=sem, vmem_limit_bytes=VMEM_LIMIT)


def _dot(a, b):
    return jnp.dot(a, b, preferred_element_type=F32)


def _dot_nt(a, b):
    return lax.dot_general(a, b, (((1,), (1,)), ((), ())), preferred_element_type=F32)


def _dot_tn(a, b):
    return lax.dot_general(a, b, (((0,), (0,)), ((), ())), preferred_element_type=F32)


def _sigmoid(x):
    return 1.0 / (1.0 + jnp.exp(-x))


def _full(shape):
    n = len(shape)
    return pl.BlockSpec(shape, lambda *_: (0,) * n)


def _ffn_fwd(x, g, wg, wu, wd, name, comm=None):
    T, D = x.shape
    nt = T // ROW_TILE

    def body(x_ref, g_ref, wg_ref, wu_ref, wd_ref, xo_ref, h_ref, a_ref, b_ref):
        j = pl.program_id(1)

        @pl.when(j == 0)
        def _():
            xv = x_ref[...]
            rstd = lax.rsqrt(jnp.mean(xv * xv, axis=-1, keepdims=True) + EPS)
            h_ref[...] = (xv * rstd * g_ref[...]).astype(BF16)
            xo_ref[...] = jnp.zeros_like(xo_ref)

        h = h_ref[...]
        a = _dot_nt(h, wg_ref[0])
        b = _dot_nt(h, wu_ref[0])
        a_ref[0] = a.astype(BF16)
        b_ref[0] = b.astype(BF16)
        s = (a * _sigmoid(a)) * b
        xo_ref[...] += _dot(s.astype(BF16), wd_ref[0])

        @pl.when(j == N_CHIPS - 1)
        def _():
            xo_ref[...] = x_ref[...] + 0.5 * xo_ref[...]

    return _pcall(
        body, comm=comm, name=name, grid=(nt, N_CHIPS),
        in_specs=[
            pl.BlockSpec((ROW_TILE, D), lambda t, j: (t, 0)),
            pl.BlockSpec((1, D), lambda t, j: (0, 0)),
            pl.BlockSpec((1, FF_CHUNK, D), lambda t, j: (j, 0, 0)),
            pl.BlockSpec((1, FF_CHUNK, D), lambda t, j: (j, 0, 0)),
            pl.BlockSpec((1, FF_CHUNK, D), lambda t, j: (j, 0, 0)),
        ],
        out_specs=[
            pl.BlockSpec((ROW_TILE, D), lambda t, j: (t, 0)),
            pl.BlockSpec((ROW_TILE, D), lambda t, j: (t, 0)),
            pl.BlockSpec((1, ROW_TILE, FF_CHUNK), lambda t, j: (j, t, 0)),
            pl.BlockSpec((1, ROW_TILE, FF_CHUNK), lambda t, j: (j, t, 0)),
        ],
        out_shape=[
            jax.ShapeDtypeStruct((T, D), F32),
            jax.ShapeDtypeStruct((T, D), BF16),
            jax.ShapeDtypeStruct((N_CHIPS, T, FF_CHUNK), BF16),
            jax.ShapeDtypeStruct((N_CHIPS, T, FF_CHUNK), BF16),
        ],
        compiler_params=_params(("arbitrary", "arbitrary")),
    )(x, g, wg, wu, wd)


def _ffn_bwd(dxo, h, a, b, wg, wu, wd, name, comm=None):
    T, D = dxo.shape
    tt = ROW_TILE
    nt = T // tt

    def body(dxo_ref, h_ref, a_ref, b_ref, wg_ref, wu_ref, wd_ref,
             dh_hbm, dwg_ref, dwu_ref, dwd_ref, dh_acc, acc_g, acc_u, acc_d):
        j = pl.program_id(0)
        t = pl.program_id(1)
        do = (0.5 * dxo_ref[...]).astype(BF16)
        av = a_ref[0].astype(F32)
        bv = b_ref[0].astype(F32)
        sig = _sigmoid(av)
        sa = av * sig
        ds = _dot_nt(do, wd_ref[0])
        da = (ds * bv * (sig * (1.0 + av * (1.0 - sig)))).astype(BF16)
        db = (ds * sa).astype(BF16)
        hv = h_ref[...]
        rows = pl.ds(pl.multiple_of(t * tt, tt), tt)

        @pl.when(j == 0)
        def _():
            dh_acc[rows, :] = jnp.zeros((tt, D), F32)

        @pl.when(t == 0)
        def _():
            acc_g[...] = jnp.zeros_like(acc_g)
            acc_u[...] = jnp.zeros_like(acc_u)
            acc_d[...] = jnp.zeros_like(acc_d)

        acc_d[...] += _dot_tn((sa * bv).astype(BF16), do)
        acc_g[...] += _dot_tn(da, hv)
        acc_u[...] += _dot_tn(db, hv)
        dh_acc[rows, :] += _dot(da, wg_ref[0]) + _dot(db, wu_ref[0])

        @pl.when(t == nt - 1)
        def _():
            dwg_ref[0] = acc_g[...].astype(BF16)
            dwu_ref[0] = acc_u[...].astype(BF16)
            dwd_ref[0] = acc_d[...].astype(BF16)

        @pl.when((t == nt - 1) & (j == N_CHIPS - 1))
        def _():
            pltpu.sync_copy(dh_acc, dh_hbm)

    return _pcall(
        body, comm=comm, name=name, grid=(N_CHIPS, nt),
        in_specs=[
            pl.BlockSpec((tt, D), lambda j, t: (t, 0)),
            pl.BlockSpec((tt, D), lambda j, t: (t, 0)),
            pl.BlockSpec((1, tt, FF_CHUNK), lambda j, t: (j, t, 0)),
            pl.BlockSpec((1, tt, FF_CHUNK), lambda j, t: (j, t, 0)),
            pl.BlockSpec((1, FF_CHUNK, D), lambda j, t: (j, 0, 0)),
            pl.BlockSpec((1, FF_CHUNK, D), lambda j, t: (j, 0, 0)),
            pl.BlockSpec((1, FF_CHUNK, D), lambda j, t: (j, 0, 0)),
        ],
        out_specs=[
            pl.BlockSpec(memory_space=pl.ANY),
            pl.BlockSpec((1, FF_CHUNK, D), lambda j, t: (j, 0, 0)),
            pl.BlockSpec((1, FF_CHUNK, D), lambda j, t: (j, 0, 0)),
            pl.BlockSpec((1, FF_CHUNK, D), lambda j, t: (j, 0, 0)),
        ],
        out_shape=[
            jax.ShapeDtypeStruct((T, D), F32),
            jax.ShapeDtypeStruct((N_CHIPS, FF_CHUNK, D), BF16),
            jax.ShapeDtypeStruct((N_CHIPS, FF_CHUNK, D), BF16),
            jax.ShapeDtypeStruct((N_CHIPS, FF_CHUNK, D), BF16),
        ],
        scratch_shapes=[
            pltpu.VMEM((T, D), F32),
            pltpu.VMEM((FF_CHUNK, D), F32),
            pltpu.VMEM((FF_CHUNK, D), F32),
            pltpu.VMEM((FF_CHUNK, D), F32),
        ],
        compiler_params=_params(("arbitrary", "arbitrary")),
    )(dxo, h, a, b, wg, wu, wd)


def _rms_fwd(x, g, name, comm=None):
    T, D = x.shape
    tt = min(ROW_TILE, T)

    def body(x_ref, g_ref, h_ref):
        xv = x_ref[...]
        rstd = lax.rsqrt(jnp.mean(xv * xv, axis=-1, keepdims=True) + EPS)
        h_ref[...] = (xv * rstd * g_ref[...]).astype(BF16)

    return _pcall(
        body, comm=comm, name=name, grid=(T // tt,),
        in_specs=[pl.BlockSpec((tt, D), lambda t: (t, 0)), pl.BlockSpec((1, D), lambda t: (0, 0))],
        out_specs=pl.BlockSpec((tt, D), lambda t: (t, 0)),
        out_shape=jax.ShapeDtypeStruct((T, D), BF16),
        compiler_params=_params(("arbitrary",)),
    )(x, g)


def _rms_bwd(x, g, dh, dres, name, comm=None):
    T, D = x.shape
    tt = min(ROW_TILE, T)
    has_res = dres is not None

    def body(*refs):
        if has_res:
            x_ref, g_ref, dh_ref, dres_ref, dx_ref, dg_ref = refs
        else:
            x_ref, g_ref, dh_ref, dx_ref, dg_ref = refs
        t = pl.program_id(0)
        xv = x_ref[...]
        rstd = lax.rsqrt(jnp.mean(xv * xv, axis=-1, keepdims=True) + EPS)
        xhat = xv * rstd
        dhv = dh_ref[...]
        gy = dhv * g_ref[...]
        dx = rstd * (gy - xhat * jnp.mean(gy * xhat, axis=-1, keepdims=True))
        if has_res:
            dx = dx + dres_ref[...]
        dx_ref[...] = dx
        part = jnp.sum(dhv * xhat, axis=0, keepdims=True)

        @pl.when(t == 0)
        def _():
            dg_ref[...] = part

        @pl.when(t > 0)
        def _():
            dg_ref[...] += part

    tile = pl.BlockSpec((tt, D), lambda t: (t, 0))
    vec = pl.BlockSpec((1, D), lambda t: (0, 0))
    args = [x, g, dh] + ([dres] if has_res else [])
    return _pcall(
        body, comm=comm, name=name, grid=(T // tt,),
        in_specs=[tile, vec, tile] + ([tile] if has_res else []),
        out_specs=[tile, vec],
        out_shape=[jax.ShapeDtypeStruct((T, D), F32), jax.ShapeDtypeStruct((1, D), F32)],
        compiler_params=_params(("arbitrary",)),
    )(*args)


def _mm_nn(a_list, b, res, name, comm=None):
    T = a_list[0].shape[0]
    K, N = b.shape
    tt = min(ROW_TILE, T)
    ks = [a.shape[1] for a in a_list]
    na = len(a_list)
    has_res = res is not None

    def body(*refs):
        a_refs = refs[:na]
        b_ref = refs[na]
        o_ref = refs[-1]
        acc = res_v = None
        off = 0
        for a_ref, k in zip(a_refs, ks):
            part = _dot(a_ref[...].astype(BF16), b_ref[off:off + k, :])
            acc = part if acc is None else acc + part
            off += k
        if has_res:
            acc = refs[na + 1][...] + acc
        o_ref[...] = acc

    in_specs = [pl.BlockSpec((tt, k), lambda t: (t, 0)) for k in ks] + [pl.BlockSpec((K, N), lambda t: (0, 0))]
    args = list(a_list) + [b]
    if has_res:
        in_specs.append(pl.BlockSpec((tt, N), lambda t: (t, 0)))
        args.append(res)
    return _pcall(
        body, comm=comm, name=name, grid=(T // tt,), in_specs=in_specs,
        out_specs=pl.BlockSpec((tt, N), lambda t: (t, 0)),
        out_shape=jax.ShapeDtypeStruct((T, N), F32),
        compiler_params=_params(("arbitrary",)),
    )(*args)


def _mm_nt(a_list, b, name, comm=None):
    T = a_list[0].shape[0]
    K, N = b.shape
    tt = min(ROW_TILE, T)
    ns = [a.shape[1] for a in a_list]
    na = len(a_list)

    def body(*refs):
        b_ref = refs[na]
        o_ref = refs[-1]
        acc = None
        off = 0
        for a_ref, n in zip(refs[:na], ns):
            part = _dot_nt(a_ref[...].astype(BF16), b_ref[:, off:off + n])
            acc = part if acc is None else acc + part
            off += n
        o_ref[...] = acc

    return _pcall(
        body, comm=comm, name=name, grid=(T // tt,),
        in_specs=[pl.BlockSpec((tt, n), lambda t: (t, 0)) for n in ns] + [pl.BlockSpec((K, N), lambda t: (0, 0))],
        out_specs=pl.BlockSpec((tt, K), lambda t: (t, 0)),
        out_shape=jax.ShapeDtypeStruct((T, K), F32),
        compiler_params=_params(("arbitrary",)),
    )(*a_list, b)


def _mm_tn(a_list, b_list, col_chunks, name, comm=None):
    T = a_list[0].shape[0]
    tt = min(ROW_TILE, T)
    nt = T // tt
    ms = [a.shape[1] for a in a_list]
    ns = [b.shape[1] for b in b_list]
    M, N = sum(ms), sum(ns)
    na, nb = len(a_list), len(b_list)
    cw = N // col_chunks

    def body(*refs):
        a_refs, b_refs = refs[:na], refs[na:na + nb]
        o_ref, acc = refs[na + nb], refs[na + nb + 1]
        t = pl.program_id(0)

        @pl.when(t == 0)
        def _():
            acc[...] = jnp.zeros_like(acc)

        ro = 0
        for a_ref, m in zip(a_refs, ms):
            av = a_ref[...].astype(BF16)
            co = 0
            for b_ref, n in zip(b_refs, ns):
                acc[ro:ro + m, co:co + n] += _dot_tn(av, b_ref[...].astype(BF16))
                co += n
            ro += m

        @pl.when(t == nt - 1)
        def _():
            if col_chunks == 1:
                o_ref[...] = acc[...].astype(BF16)
            else:
                for q in range(col_chunks):
                    o_ref[q] = acc[:, q * cw:(q + 1) * cw].astype(BF16)

    out_shape = (M, N) if col_chunks == 1 else (col_chunks, M, cw)
    return _pcall(
        body, comm=comm, name=name, grid=(nt,),
        in_specs=[pl.BlockSpec((tt, m), lambda t: (t, 0)) for m in ms]
        + [pl.BlockSpec((tt, n), lambda t: (t, 0)) for n in ns],
        out_specs=_full(out_shape),
        out_shape=jax.ShapeDtypeStruct(out_shape, BF16),
        scratch_shapes=[pltpu.VMEM((M, N), F32)],
        compiler_params=_params(("arbitrary",)),
    )(*a_list, *b_list)


def _head_masks():
    lane = lax.broadcasted_iota(jnp.int32, (1, LANES), 1)
    l64 = lane & (HEAD_DIM - 1)
    return lane < HEAD_DIM, l64 < ROPE_DIM // 2, l64 < ROPE_DIM


def _head_mean(v, lo):
    s_lo = jnp.sum(jnp.where(lo, v, 0.0), axis=-1, keepdims=True)
    s_hi = jnp.sum(jnp.where(lo, 0.0, v), axis=-1, keepdims=True)
    return jnp.where(lo, s_lo, s_hi) * (1.0 / HEAD_DIM)


def _rope_swap(v, first, rot):
    up = pltpu.roll(v, LANES - ROPE_DIM // 2, 1)
    down = pltpu.roll(v, ROPE_DIM // 2, 1)
    return jnp.where(first, up, jnp.where(rot, down, 0.0))


def _head_norm(x, g, lo):
    rstd = lax.rsqrt(_head_mean(x * x, lo) + EPS)
    return x * rstd * g


def _head_norm_bwd(x, g, dy, lo):
    rstd = lax.rsqrt(_head_mean(x * x, lo) + EPS)
    xhat = x * rstd
    gy = dy * g
    dx = rstd * (gy - xhat * _head_mean(gy * xhat, lo))
    return dx, dy * xhat


def _rope(xn, cos, sin, first, rot):
    return xn * cos + _rope_swap(xn, first, rot) * sin


def _rope_bwd(dy, cos, sin, first, rot):
    return dy * cos + _rope_swap(dy * sin, first, rot)


def _fold_heads(v):
    return v + pltpu.roll(v, HEAD_DIM, 1)


ATT_ROWS = 256


def _attn_prepare(q_ref, k_ref, v_ref, cos_ref, sin_ref, gq_ref, gk_ref, qs, ks, vs):
    T = q_ref.shape[0]
    lo, first, rot = _head_masks()
    ks[0:BLOCK, :] = jnp.zeros((BLOCK, KV_COLS), BF16)
    vs[0:BLOCK, :] = jnp.zeros((BLOCK, KV_COLS), BF16)

    def step(i, _):
        r0 = pl.multiple_of(i * ATT_ROWS, ATT_ROWS)
        rows = pl.ds(r0, ATT_ROWS)
        prow = pl.ds(r0 + BLOCK, ATT_ROWS)
        cos, sin = cos_ref[rows, :], sin_ref[rows, :]
        for p in range(Q_COLS // LANES):
            cols = slice(p * LANES, (p + 1) * LANES)
            xr = _rope(_head_norm(q_ref[rows, cols], gq_ref[...], lo), cos, sin, first, rot)
            qs[rows, cols] = (xr * (HEAD_DIM ** -0.5)).astype(BF16)
        kr = _rope(_head_norm(k_ref[rows, :], gk_ref[...], lo), cos, sin, first, rot)
        ks[prow, :] = kr.astype(BF16)
        vs[prow, :] = v_ref[rows, :].astype(BF16)
        return 0

    lax.fori_loop(0, T // ATT_ROWS, step, 0)


GROUP_ROWS = Q_PER_KV * BLOCK


def _group_rows(ref, r0, g, cast=None):
    parts = []
    for r in range(Q_PER_KV):
        h = g * Q_PER_KV + r
        part = ref[pl.ds(r0, BLOCK), h * HEAD_DIM:(h + 1) * HEAD_DIM]
        parts.append(part if cast is None else part.astype(cast))
    return jnp.concatenate(parts, axis=0)


def _group_sinks(sink_ref, g):
    row = lax.broadcasted_iota(jnp.int32, (GROUP_ROWS, 1), 0)
    col = jnp.full((GROUP_ROWS, 1), sink_ref[0, g * Q_PER_KV], F32)
    for r in range(1, Q_PER_KV):
        col = jnp.where(row >= r * BLOCK, sink_ref[0, g * Q_PER_KV + r], col)
    return col


def _attn_scores(qg, kw, blk, sink):
    s = _dot_nt(qg, kw)
    qi = (lax.broadcasted_iota(jnp.int32, (GROUP_ROWS, 2 * BLOCK), 0) & (BLOCK - 1)) + BLOCK
    ki = lax.broadcasted_iota(jnp.int32, (GROUP_ROWS, 2 * BLOCK), 1)
    rel = qi - ki
    valid = (rel >= 0) & (rel < BLOCK) & ((blk > 0) | (ki >= BLOCK))
    s = jnp.where(valid, s, NEG)
    m = jnp.maximum(jnp.max(s, axis=-1, keepdims=True), sink)
    p = jnp.exp(s - m)
    e_sink = jnp.exp(sink - m)
    inv = 1.0 / (jnp.sum(p, axis=-1, keepdims=True) + e_sink)
    return p * inv, e_sink * inv


def _attn_fwd(proj, cos, sin, gq2, gk2, sinks, name, comm=None):
    T = proj.shape[0]
    nb = T // BLOCK

    def body(q_ref, k_ref, v_ref, cos_ref, sin_ref, gq_ref, gk_ref, sink_ref, y_ref, qs, ks, vs):
        _attn_prepare(q_ref, k_ref, v_ref, cos_ref, sin_ref, gq_ref, gk_ref, qs, ks, vs)

        def blk_step(blk, _):
            r0 = pl.multiple_of(blk * BLOCK, BLOCK)
            for g in range(N_KV_HEADS):
                gc = slice(g * HEAD_DIM, (g + 1) * HEAD_DIM)
                kw = ks[pl.ds(r0, 2 * BLOCK), gc]
                vw = vs[pl.ds(r0, 2 * BLOCK), gc]
                w, _ws = _attn_scores(_group_rows(qs, r0, g), kw, blk, _group_sinks(sink_ref, g))
                o = _dot(w.astype(BF16), vw).astype(BF16)
                for r in range(Q_PER_KV):
                    h = g * Q_PER_KV + r
                    y_ref[pl.ds(r0, BLOCK), h * HEAD_DIM:(h + 1) * HEAD_DIM] = o[r * BLOCK:(r + 1) * BLOCK, :]
            return 0

        lax.fori_loop(0, nb, blk_step, 0)

    return _pcall(
        body, comm=comm, name=name, grid=(1,),
        in_specs=[
            pl.BlockSpec((T, Q_COLS), lambda i: (0, 0)),
            pl.BlockSpec((T, KV_COLS), lambda i: (0, Q_COLS // KV_COLS)),
            pl.BlockSpec((T, KV_COLS), lambda i: (0, Q_COLS // KV_COLS + 1)),
            _full((T, LANES)), _full((T, LANES)), _full((1, LANES)), _full((1, LANES)),
            pl.BlockSpec(memory_space=pltpu.SMEM),
        ],
        out_specs=_full((T, Q_COLS)),
        out_shape=jax.ShapeDtypeStruct((T, Q_COLS), BF16),
        scratch_shapes=[
            pltpu.VMEM((T, Q_COLS), BF16),
            pltpu.VMEM((T + BLOCK, KV_COLS), BF16),
            pltpu.VMEM((T + BLOCK, KV_COLS), BF16),
        ],
        compiler_params=_params(("arbitrary",)),
    )(proj, proj, proj, cos, sin, gq2, gk2, sinks)


def _attn_bwd(proj, cos, sin, gq2, gk2, sinks, dyc, name, comm=None):
    T = proj.shape[0]
    nb = T // BLOCK

    def body(q_ref, k_ref, v_ref, cos_ref, sin_ref, gq_ref, gk_ref, sink_ref, dy_ref,
             dq_ref, dk_ref, dv_ref, dgq_ref, dgk_ref, dsink_ref, qs, ks, vs, dqs, dks, dvs):
        _attn_prepare(q_ref, k_ref, v_ref, cos_ref, sin_ref, gq_ref, gk_ref, qs, ks, vs)
        dks[...] = jnp.zeros_like(dks)
        dvs[...] = jnp.zeros_like(dvs)
        lane = lax.broadcasted_iota(jnp.int32, (1, LANES), 1)

        def blk_step(blk, dsink):
            r0 = pl.multiple_of(blk * BLOCK, BLOCK)
            win = pl.ds(r0, 2 * BLOCK)
            for g in range(N_KV_HEADS):
                gc = slice(g * HEAD_DIM, (g + 1) * HEAD_DIM)
                kw = ks[win, gc]
                vw = vs[win, gc]
                qg = _group_rows(qs, r0, g)
                w, w_sink = _attn_scores(qg, kw, blk, _group_sinks(sink_ref, g))
                do = _group_rows(dy_ref, r0, g, cast=BF16)
                dvs[win, gc] += _dot_tn(w.astype(BF16), do)
                dw = _dot_nt(do, vw)
                delta = jnp.sum(w * dw, axis=-1, keepdims=True)
                ds = (w * (dw - delta)).astype(BF16)
                sink_part = w_sink * delta
                dq = _dot(ds, kw)
                for r in range(Q_PER_KV):
                    h = g * Q_PER_KV + r
                    slab = slice(r * BLOCK, (r + 1) * BLOCK)
                    dsink = dsink + jnp.where(lane == h, -jnp.sum(sink_part[slab, :], axis=0, keepdims=True), 0.0)
                    dqs[pl.ds(r0, BLOCK), h * HEAD_DIM:(h + 1) * HEAD_DIM] = dq[slab, :]
                dks[win, gc] += _dot_tn(ds, qg)
            return dsink

        dsink_ref[...] = lax.fori_loop(0, nb, blk_step, jnp.zeros((1, LANES), F32))

        lo, first, rot = _head_masks()

        def step(i, carry):
            dgq, dgk = carry
            r0 = pl.multiple_of(i * ATT_ROWS, ATT_ROWS)
            rows = pl.ds(r0, ATT_ROWS)
            prow = pl.ds(r0 + BLOCK, ATT_ROWS)
            cos, sin = cos_ref[rows, :], sin_ref[rows, :]
            for p in range(Q_COLS // LANES):
                cols = slice(p * LANES, (p + 1) * LANES)
                dxn = _rope_bwd(dqs[rows, cols] * (HEAD_DIM ** -0.5), cos, sin, first, rot)
                dx, dgp = _head_norm_bwd(q_ref[rows, cols], gq_ref[...], dxn, lo)
                dq_ref[rows, cols] = dx
                dgq = dgq + jnp.sum(dgp, axis=0, keepdims=True)
            dkn = _rope_bwd(dks[prow, :], cos, sin, first, rot)
            dx, dgp = _head_norm_bwd(k_ref[rows, :], gk_ref[...], dkn, lo)
            dk_ref[rows, :] = dx
            dgk = dgk + jnp.sum(dgp, axis=0, keepdims=True)
            dv_ref[rows, :] = dvs[prow, :]
            return dgq, dgk

        zero = jnp.zeros((1, LANES), F32)
        dgq, dgk = lax.fori_loop(0, T // ATT_ROWS, step, (zero, zero))
        dgq_ref[...] = _fold_heads(dgq)
        dgk_ref[...] = _fold_heads(dgk)

    vec = jax.ShapeDtypeStruct((1, LANES), F32)
    return _pcall(
        body, comm=comm, name=name, grid=(1,),
        in_specs=[
            pl.BlockSpec((T, Q_COLS), lambda i: (0, 0)),
            pl.BlockSpec((T, KV_COLS), lambda i: (0, Q_COLS // KV_COLS)),
            pl.BlockSpec((T, KV_COLS), lambda i: (0, Q_COLS // KV_COLS + 1)),
            _full((T, LANES)), _full((T, LANES)), _full((1, LANES)), _full((1, LANES)),
            pl.BlockSpec(memory_space=pltpu.SMEM),
            pl.BlockSpec((T, Q_COLS), lambda i: (0, 0)),
        ],
        out_specs=[_full((T, Q_COLS)), _full((T, KV_COLS)), _full((T, KV_COLS)),
                   _full((1, LANES)), _full((1, LANES)), _full((1, LANES))],
        out_shape=[jax.ShapeDtypeStruct((T, Q_COLS), F32), jax.ShapeDtypeStruct((T, KV_COLS), F32),
                   jax.ShapeDtypeStruct((T, KV_COLS), F32), vec, vec, vec],
        scratch_shapes=[
            pltpu.VMEM((T, Q_COLS), BF16),
            pltpu.VMEM((T + BLOCK, KV_COLS), BF16),
            pltpu.VMEM((T + BLOCK, KV_COLS), BF16),
            pltpu.VMEM((T, Q_COLS), F32),
            pltpu.VMEM((T + BLOCK, KV_COLS), F32),
            pltpu.VMEM((T + BLOCK, KV_COLS), F32),
        ],
        compiler_params=_params(("arbitrary",)),
    )(proj, proj, proj, cos, sin, gq2, gk2, sinks, dyc)


CONV_PAD = 32
CONV_ROWS = 256


def _conv_taps(src, w_ref, r0, first_off, step_sign):
    acc = None
    for i in range(CONV_WIDTH):
        term = w_ref[i:i + 1, :] * src[r0 + first_off + step_sign * i:r0 + first_off + step_sign * i + CONV_ROWS, :]
        acc = term if acc is None else acc + term
    return acc


def _conv_fwd(proj, w_dw, b_dw, g_ln, b_ln, name, comm=None):
    T = proj.shape[0]
    a_blk = (Q_COLS + 2 * KV_COLS) // CONV_CH

    def body(a_ref, gate_ref, w_ref, bdw_ref, g_ref, b_ref, y_ref, c_ref, pad):
        pad[0:CONV_PAD, :] = jnp.zeros((CONV_PAD, CONV_CH), F32)
        pad[CONV_PAD:, :] = a_ref[...] * _sigmoid(gate_ref[...])
        for n in range(T // CONV_ROWS):
            r0 = n * CONV_ROWS
            c = _conv_taps(pad, w_ref, r0, CONV_PAD - (CONV_WIDTH - 1), 1) + bdw_ref[...]
            c_ref[r0:r0 + CONV_ROWS, :] = c
            mu = jnp.mean(c, axis=-1, keepdims=True)
            cc = c - mu
            rstd = lax.rsqrt(jnp.mean(cc * cc, axis=-1, keepdims=True) + EPS)
            z = cc * rstd * g_ref[...] + b_ref[...]
            y_ref[r0:r0 + CONV_ROWS, :] = (z * _sigmoid(z)).astype(BF16)

    vec = _full((1, CONV_CH))
    return _pcall(
        body, comm=comm, name=name, grid=(1,),
        in_specs=[
            pl.BlockSpec((T, CONV_CH), lambda i: (0, a_blk)),
            pl.BlockSpec((T, CONV_CH), lambda i: (0, a_blk + 1)),
            _full((CONV_WIDTH, CONV_CH)), vec, vec, vec,
        ],
        out_specs=[_full((T, CONV_CH)), _full((T, CONV_CH))],
        out_shape=[jax.ShapeDtypeStruct((T, CONV_CH), BF16), jax.ShapeDtypeStruct((T, CONV_CH), F32)],
        scratch_shapes=[pltpu.VMEM((T + CONV_PAD, CONV_CH), F32)],
        compiler_params=_params(("arbitrary",)),
    )(proj, proj, w_dw, b_dw, g_ln, b_ln)


def _conv_bwd(proj, c, w_dw, g_ln, b_ln, dyc, name, comm=None):
    T = proj.shape[0]
    a_blk = (Q_COLS + 2 * KV_COLS) // CONV_CH
    y_blk = Q_COLS // CONV_CH

    def body(a_ref, gate_ref, c_ref, w_ref, g_ref, b_ref, dy_ref,
             da_ref, dgate_ref, dw_ref, dbdw_ref, dg_ref, db_ref, pad, dcp):
        pad[0:CONV_PAD, :] = jnp.zeros((CONV_PAD, CONV_CH), F32)
        sg = _sigmoid(gate_ref[...])
        pad[CONV_PAD:, :] = a_ref[...] * sg
        dcp[T:, :] = jnp.zeros((CONV_PAD, CONV_CH), F32)
        dg = db = dbdw = jnp.zeros((1, CONV_CH), F32)
        for n in range(T // CONV_ROWS):
            rows = slice(n * CONV_ROWS, (n + 1) * CONV_ROWS)
            cv = c_ref[rows, :]
            mu = jnp.mean(cv, axis=-1, keepdims=True)
            cc = cv - mu
            rstd = lax.rsqrt(jnp.mean(cc * cc, axis=-1, keepdims=True) + EPS)
            chat = cc * rstd
            z = chat * g_ref[...] + b_ref[...]
            sz = _sigmoid(z)
            dz = dy_ref[rows, :] * (sz * (1.0 + z * (1.0 - sz)))
            dg = dg + jnp.sum(dz * chat, axis=0, keepdims=True)
            db = db + jnp.sum(dz, axis=0, keepdims=True)
            dch = dz * g_ref[...]
            dc = rstd * (dch - jnp.mean(dch, axis=-1, keepdims=True)
                         - chat * jnp.mean(dch * chat, axis=-1, keepdims=True))
            dbdw = dbdw + jnp.sum(dc, axis=0, keepdims=True)
            dcp[rows, :] = dc
        dg_ref[...] = dg
        db_ref[...] = db
        dbdw_ref[...] = dbdw
        dw_ref[CONV_WIDTH:, :] = jnp.zeros((CONV_PAD - CONV_WIDTH, CONV_CH), F32)
        for i in range(CONV_WIDTH):
            off = CONV_PAD - (CONV_WIDTH - 1) + i
            acc = jnp.zeros((1, CONV_CH), F32)
            for n in range(T // CONV_ROWS):
                r0 = n * CONV_ROWS
                acc = acc + jnp.sum(dcp[r0:r0 + CONV_ROWS, :] * pad[r0 + off:r0 + off + CONV_ROWS, :],
                                    axis=0, keepdims=True)
            dw_ref[i:i + 1, :] = acc
        for n in range(T // CONV_ROWS):
            r0 = n * CONV_ROWS
            rows = slice(r0, r0 + CONV_ROWS)
            dhg = _conv_taps(dcp, w_ref, r0, CONV_WIDTH - 1, -1)
            sgv = sg[rows, :]
            da_ref[rows, :] = dhg * sgv
            dgate_ref[rows, :] = dhg * a_ref[rows, :] * sgv * (1.0 - sgv)

    vec = _full((1, CONV_CH))
    vshape = jax.ShapeDtypeStruct((1, CONV_CH), F32)
    return _pcall(
        body, comm=comm, name=name, grid=(1,),
        in_specs=[
            pl.BlockSpec((T, CONV_CH), lambda i: (0, a_blk)),
            pl.BlockSpec((T, CONV_CH), lambda i: (0, a_blk + 1)),
            _full((T, CONV_CH)), _full((CONV_WIDTH, CONV_CH)), vec, vec,
            pl.BlockSpec((T, CONV_CH), lambda i: (0, y_blk)),
        ],
        out_specs=[_full((T, CONV_CH)), _full((T, CONV_CH)), _full((CONV_PAD, CONV_CH)), vec, vec, vec],
        out_shape=[jax.ShapeDtypeStruct((T, CONV_CH), F32), jax.ShapeDtypeStruct((T, CONV_CH), F32),
                   jax.ShapeDtypeStruct((CONV_PAD, CONV_CH), F32), vshape, vshape, vshape],
        scratch_shapes=[pltpu.VMEM((T + CONV_PAD, CONV_CH), F32), pltpu.VMEM((T + CONV_PAD, CONV_CH), F32)],
        compiler_params=_params(("arbitrary",)),
    )(proj, proj, c, w_dw, g_ln, b_ln, dyc)


def _mem_kv(mkv_ref, gk_ref, lo, kn_s, vv_s):
    for p in range(MQ_COLS // LANES):
        cols = slice(p * LANES, (p + 1) * LANES)
        kn_s[:, cols] = _head_norm(mkv_ref[:, cols], gk_ref[...], lo).astype(BF16)
    vv_s[...] = mkv_ref[:, MQ_COLS:].astype(BF16)


def _mem_softmax(qh, kh):
    s = _dot_nt(qh, kh)
    m = jnp.max(s, axis=-1, keepdims=True)
    p = jnp.exp(s - m)
    return p / jnp.sum(p, axis=-1, keepdims=True)


def _mem_fwd(proj, mkv, gq2, gk2, name, comm=None):
    T = proj.shape[0]
    tt = ROW_TILE
    q_blk = (IN_COLS - MQ_COLS) // MQ_COLS

    def body(q_ref, mkv_ref, gq_ref, gk_ref, y_ref, kn_s, vv_s, qn_s):
        lo, _, _ = _head_masks()
        _mem_kv(mkv_ref, gk_ref, lo, kn_s, vv_s)
        for p in range(MQ_COLS // LANES):
            cols = slice(p * LANES, (p + 1) * LANES)
            qn_s[:, cols] = (_head_norm(q_ref[:, cols], gq_ref[...], lo) * (HEAD_DIM ** -0.5)).astype(BF16)
        for h in range(N_MEM_HEADS):
            hc = slice(h * HEAD_DIM, (h + 1) * HEAD_DIM)
            w = _mem_softmax(qn_s[:, hc], kn_s[:, hc])
            y_ref[:, hc] = _dot(w.astype(BF16), vv_s[:, hc]).astype(BF16)

    return _pcall(
        body, comm=comm, name=name, grid=(T // tt,),
        in_specs=[
            pl.BlockSpec((tt, MQ_COLS), lambda t: (t, q_blk)),
            pl.BlockSpec((MEM_LEN, 2 * MQ_COLS), lambda t: (0, 0)),
            pl.BlockSpec((1, LANES), lambda t: (0, 0)), pl.BlockSpec((1, LANES), lambda t: (0, 0)),
        ],
        out_specs=pl.BlockSpec((tt, MQ_COLS), lambda t: (t, 0)),
        out_shape=jax.ShapeDtypeStruct((T, MQ_COLS), BF16),
        scratch_shapes=[pltpu.VMEM((MEM_LEN, MQ_COLS), BF16), pltpu.VMEM((MEM_LEN, MQ_COLS), BF16),
                        pltpu.VMEM((tt, MQ_COLS), BF16)],
        compiler_params=_params(("arbitrary",)),
    )(proj, mkv, gq2, gk2)


def _mem_bwd(proj, mkv, gq2, gk2, dyc, name, comm=None):
    T = proj.shape[0]
    tt = ROW_TILE
    nt = T // tt
    q_blk = (IN_COLS - MQ_COLS) // MQ_COLS
    y_blk = (Q_COLS + CONV_CH) // MQ_COLS

    def body(q_ref, mkv_ref, gq_ref, gk_ref, dy_ref, dq_ref, dmkv_ref, dgq_ref, dgk_ref,
             kn_s, vv_s, qn_s, dqn_s, dkn_acc):
        t = pl.program_id(0)
        lo, _, _ = _head_masks()
        _mem_kv(mkv_ref, gk_ref, lo, kn_s, vv_s)

        @pl.when(t == 0)
        def _():
            dkn_acc[...] = jnp.zeros_like(dkn_acc)
            dmkv_ref[...] = jnp.zeros_like(dmkv_ref)
            dgq_ref[...] = jnp.zeros_like(dgq_ref)

        for p in range(MQ_COLS // LANES):
            cols = slice(p * LANES, (p + 1) * LANES)
            qn_s[:, cols] = (_head_norm(q_ref[:, cols], gq_ref[...], lo) * (HEAD_DIM ** -0.5)).astype(BF16)
        for h in range(N_MEM_HEADS):
            hc = slice(h * HEAD_DIM, (h + 1) * HEAD_DIM)
            vc = slice(MQ_COLS + h * HEAD_DIM, MQ_COLS + (h + 1) * HEAD_DIM)
            qh = qn_s[:, hc]
            w = _mem_softmax(qh, kn_s[:, hc])
            do = dy_ref[:, hc].astype(BF16)
            dmkv_ref[:, vc] += _dot_tn(w.astype(BF16), do)
            dw = _dot_nt(do, vv_s[:, hc])
            ds = (w * (dw - jnp.sum(w * dw, axis=-1, keepdims=True))).astype(BF16)
            dqn_s[:, hc] = _dot(ds, kn_s[:, hc])
            dkn_acc[:, hc] += _dot_tn(ds, qh)
        dgq = jnp.zeros((1, LANES), F32)
        for p in range(MQ_COLS // LANES):
            cols = slice(p * LANES, (p + 1) * LANES)
            dx, dgp = _head_norm_bwd(q_ref[:, cols], gq_ref[...], dqn_s[:, cols] * (HEAD_DIM ** -0.5), lo)
            dq_ref[:, cols] = dx
            dgq = dgq + jnp.sum(dgp, axis=0, keepdims=True)
        dgq_ref[...] += dgq

        @pl.when(t == nt - 1)
        def _():
            dgk = jnp.zeros((1, LANES), F32)
            for p in range(MQ_COLS // LANES):
                cols = slice(p * LANES, (p + 1) * LANES)
                dx, dgp = _head_norm_bwd(mkv_ref[:, cols], gk_ref[...], dkn_acc[:, cols], lo)
                dmkv_ref[:, cols] = dx
                dgk = dgk + jnp.sum(dgp, axis=0, keepdims=True)
            dgk_ref[...] = _fold_heads(dgk)
            dgq_ref[...] = _fold_heads(dgq_ref[...])

    vec = pl.BlockSpec((1, LANES), lambda t: (0, 0))
    vshape = jax.ShapeDtypeStruct((1, LANES), F32)
    return _pcall(
        body, comm=comm, name=name, grid=(nt,),
        in_specs=[
            pl.BlockSpec((tt, MQ_COLS), lambda t: (t, q_blk)),
            pl.BlockSpec((MEM_LEN, 2 * MQ_COLS), lambda t: (0, 0)),
            vec, vec,
            pl.BlockSpec((tt, MQ_COLS), lambda t: (t, y_blk)),
        ],
        out_specs=[pl.BlockSpec((tt, MQ_COLS), lambda t: (t, 0)),
                   pl.BlockSpec((MEM_LEN, 2 * MQ_COLS), lambda t: (0, 0)), vec, vec],
        out_shape=[jax.ShapeDtypeStruct((T, MQ_COLS), F32), jax.ShapeDtypeStruct((MEM_LEN, 2 * MQ_COLS), F32),
                   vshape, vshape],
        scratch_shapes=[pltpu.VMEM((MEM_LEN, MQ_COLS), BF16), pltpu.VMEM((MEM_LEN, MQ_COLS), BF16),
                        pltpu.VMEM((tt, MQ_COLS), BF16), pltpu.VMEM((tt, MQ_COLS), F32),
                        pltpu.VMEM((MEM_LEN, MQ_COLS), F32)],
        compiler_params=_params(("arbitrary",)),
    )(proj, mkv, gq2, gk2, dyc)


def _loss_head(y, target, name, comm=None):
    T, D = y.shape
    tt = ROW_TILE

    def body(y_ref, t_ref, dy_ref, loss_ref):
        t = pl.program_id(0)
        err = y_ref[...] - t_ref[...]
        dy_ref[...] = err * (1.0 / D)
        part = 0.5 * jnp.sum(jnp.mean(err * err, axis=-1, keepdims=True), axis=0, keepdims=True)

        @pl.when(t == 0)
        def _():
            loss_ref[...] = jnp.zeros_like(loss_ref)

        loss_ref[...] += jnp.broadcast_to(part, loss_ref.shape)

    tile = pl.BlockSpec((tt, D), lambda t: (t, 0))
    return _pcall(
        body, comm=comm, name=name, grid=(T // tt,),
        in_specs=[tile, tile],
        out_specs=[tile, pl.BlockSpec((1, LANES), lambda t: (0, 0))],
        out_shape=[jax.ShapeDtypeStruct((T, D), F32), jax.ShapeDtypeStruct((1, LANES), F32)],
        compiler_params=_params(("arbitrary",)),
    )(y, target)


def _adamw(w, g, m, v, name, comm=None):
    R, C = w.shape
    tr = next((r for r in (512, 352, 256, 128) if R % r == 0), R)

    def body(w_ref, g_ref, m_ref, v_ref, go_ref, d_ref, nm_ref, nv_ref):
        gv = g_ref[...]
        go_ref[...] = gv
        nm = ADAM_B1 * m_ref[...] + (1.0 - ADAM_B1) * gv
        nv = ADAM_B2 * v_ref[...] + (1.0 - ADAM_B2) * (gv * gv)
        m_hat = nm / (1.0 - ADAM_B1 ** ADAM_STEP)
        v_hat = nv / (1.0 - ADAM_B2 ** ADAM_STEP)
        d_ref[...] = -ADAM_LR * (m_hat / (jnp.sqrt(v_hat) + ADAM_EPS) + ADAM_WD * w_ref[...])
        nm_ref[...] = nm
        nv_ref[...] = nv

    tile = pl.BlockSpec((tr, C), lambda i: (i, 0))
    shape = jax.ShapeDtypeStruct((R, C), F32)
    return _pcall(
        body, comm=comm, name=name, grid=(R // tr,),
        in_specs=[tile] * 4, out_specs=[tile] * 4, out_shape=[shape] * 4,
        compiler_params=_params(("arbitrary",)),
    )(w, g, m, v)


def _mesh_pos():
    return lax.axis_index("x"), lax.axis_index("y"), lax.axis_index("c")


def _other_chips(x, y):
    return [(1 - x, y), (x, 1 - y), (1 - x, 1 - y)]


def _quarter(ref, layout, q, rows, cols):
    if layout == "cols":
        return ref.at[rows, pl.ds(pl.multiple_of(q * cols, LANES), cols)]
    return ref.at[q, rows, :]


def _gather_weights(shards, dtypes, layouts, later, name, comm=None):
    n = len(shards)
    all_rows = slice(None)
    remote = [i for i in range(n) if i not in later]
    split = [i for i in remote if layouts[i] != "whole"]

    def body(*refs):
        ins, outs = refs[:n], refs[n:2 * n]
        st32, st16 = refs[2 * n:3 * n], refs[3 * n:4 * n]
        in_sems, own_sems, send_sems, recv_sems, fwd_send_sems, fwd_recv_sems = refs[4 * n:]
        x, y, c = _mesh_pos()
        chip = 2 * x + y
        sibling = (x, y, 1 - c)
        chips = _other_chips(x, y)

        def half(i, which):
            if i not in split:
                return all_rows
            hr = shards[i].shape[0] // 2
            return pl.ds(pl.multiple_of(which * hr, 16), hr)

        def place(i, q, rows):
            return _quarter(outs[i], layouts[i], q, rows, shards[i].shape[1])

        def ici(i, k, origin_chip, src):
            px, py = chips[k]
            return pltpu.make_async_remote_copy(
                src_ref=src, dst_ref=place(i, origin_chip, half(i, c)), send_sem=send_sems.at[i, k],
                recv_sem=recv_sems.at[i, k], device_id=(px, py, c), device_id_type=MESH)

        def forward(i, k, rows):
            px, py = chips[k]
            there = place(i, 2 * px + py, rows)
            return pltpu.make_async_remote_copy(
                src_ref=there, dst_ref=there, send_sem=fwd_send_sems.at[i, k],
                recv_sem=fwd_recv_sems.at[i, k], device_id=sibling, device_id_type=MESH)

        loads = [pltpu.make_async_copy(ins[i], st32[i], in_sems.at[i]) for i in range(n)]
        for cp in loads:
            cp.start()
        owns, sent = [], []
        for i in range(n):
            loads[i].wait()
            st16[i][...] = st32[i][...].astype(dtypes[i])
            own = pltpu.make_async_copy(st16[i], place(i, chip, all_rows), own_sems.at[i])
            own.start()
            owns.append(own)
            for k in range(3 if i in remote else 0):
                cp = ici(i, k, chip, st16[i].at[half(i, c)])
                cp.start()
                sent.append(cp)
        for i in remote:
            for k, (px, py) in enumerate(chips):
                ici(i, k, 2 * px + py, st16[i].at[half(i, c)]).wait_recv()
                if i in split:
                    cp = forward(i, k, half(i, c))
                    cp.start()
                    sent.append(cp)
        for i in split:
            for k in range(3):
                forward(i, k, half(i, 1 - c)).wait_recv()
        for cp in sent:
            cp.wait_send()
        for cp in owns:
            cp.wait()

    hbm = pl.BlockSpec(memory_space=pl.ANY)
    return _pcall(
        body, comm=comm, name=name,
        in_specs=[hbm] * n, out_specs=[hbm] * n,
        out_shape=[_gathered_shape(s.shape, d, lay) for s, d, lay in zip(shards, dtypes, layouts)],
        scratch_shapes=[pltpu.VMEM(s.shape, F32) for s in shards] + [pltpu.VMEM(s.shape, d) for s, d in zip(shards, dtypes)]
        + [pltpu.SemaphoreType.DMA((n,)), pltpu.SemaphoreType.DMA((n,)),
           pltpu.SemaphoreType.DMA((n, 3)), pltpu.SemaphoreType.DMA((n, 3)),
           pltpu.SemaphoreType.DMA((n, 3)), pltpu.SemaphoreType.DMA((n, 3))],
        compiler_params=pltpu.CompilerParams(vmem_limit_bytes=VMEM_LIMIT),
    )(*shards)


def _gathered_shape(quarter_shape, dtype, layout):
    R, C = quarter_shape
    return jax.ShapeDtypeStruct((R, N_CHIPS * C) if layout == "cols" else (N_CHIPS, R, C), dtype)


def _remote(src, dst, sems, j, device):
    return pltpu.make_async_remote_copy(src_ref=src, dst_ref=dst, send_sem=sems.at[2 * j], recv_sem=sems.at[2 * j + 1],
                                        device_id=device, device_id_type=MESH)


class _SemWindow:
    def __init__(self, sems, offset):
        self.sems, self.offset = sems, offset

    @property
    def at(self):
        return self

    def __getitem__(self, i):
        return self.sems.at[self.offset + i]


def _join_plans(a, b):
    n_ai, n_ao = len(a.ins), len(a.out_shapes)

    def plan(ins, outs, sems, finishing):
        first = a.plan(ins[:n_ai], outs[:n_ao], sems, finishing)
        second = b.plan(ins[n_ai:], outs[n_ao:], _SemWindow(sems, a.n_sems), finishing)
        return tuple(p + q for p, q in zip(first, second))

    aliases = {**a.aliases, **{n_ai + i: n_ao + o for i, o in b.aliases.items()}}
    return _Comm(a.ins + b.ins, a.out_shapes + b.out_shapes, a.n_sems + b.n_sems, plan, aliases)


def _half_rows(rows, which):
    hr = rows // 2
    return pl.ds(pl.multiple_of(which * hr, 16), hr)


ICI_PARTS = 1


def _row_parts(rows):
    tiles = rows // 16
    sizes = [16 * (tiles // ICI_PARTS + (p < tiles % ICI_PARTS)) for p in range(ICI_PARTS)]
    return [(sum(sizes[:p]), size) for p, size in enumerate(sizes) if size]


def _half_parts(rows, which):
    hr = rows // 2
    return [pl.ds(pl.multiple_of(which * hr + start, 16), size) for start, size in _row_parts(hr)]


def _spread_plan(fulls, quarter_shapes, layouts, peers=(0, 1, 2)):
    def plan(ins, outs, sems, finishing):
        x, y, c = _mesh_pos()
        chip = 2 * x + y
        sends, recvs = [], []
        for i, full in enumerate(outs):
            R, C = quarter_shapes[i]
            for k, (px, py) in enumerate(_other_chips(x, y)):
                if k not in peers:
                    continue
                for p, part in enumerate(_half_parts(R, c)):
                    j = ICI_PARTS * (3 * i + k) + p
                    mine = _quarter(full, layouts[i], chip, part, C)
                    sends.append(_remote(mine, mine, sems, j, (px, py, c)))
                    if finishing:
                        recvs.append(_remote(mine, _quarter(full, layouts[i], 2 * px + py, part, C), sems, j, (px, py, c)))
        return [], sends, recvs

    shapes = [jax.ShapeDtypeStruct(f.shape, f.dtype) for f in fulls]
    return _Comm(fulls, shapes, 2 * 3 * ICI_PARTS * len(fulls), plan, aliases={i: i for i in range(len(fulls))})


def _forward_plan(fulls, quarter_shapes, layouts):
    def plan(ins, outs, sems, finishing):
        x, y, c = _mesh_pos()
        sends, recvs = [], []
        for i, full in enumerate(outs):
            R, C = quarter_shapes[i]
            for k, (px, py) in enumerate(_other_chips(x, y)):
                mine = _quarter(full, layouts[i], 2 * px + py, _half_rows(R, c), C)
                sends.append(_remote(mine, mine, sems, 3 * i + k, (x, y, 1 - c)))
                if finishing:
                    theirs = _quarter(full, layouts[i], 2 * px + py, _half_rows(R, 1 - c), C)
                    recvs.append(_remote(mine, theirs, sems, 3 * i + k, (x, y, 1 - c)))
        return [], sends, recvs

    shapes = [jax.ShapeDtypeStruct(f.shape, f.dtype) for f in fulls]
    return _Comm(fulls, shapes, 6 * len(fulls), plan, aliases={i: i for i in range(len(fulls))})


def _swap_plan(grads):
    def plan(ins, outs, sems, finishing):
        x, y, c = _mesh_pos()
        sends = [_remote(g.at[:, _half_rows(g.shape[1], 1 - c), :], sib, sems, i, (x, y, 1 - c))
                 for i, (g, sib) in enumerate(zip(ins, outs))]
        return [], sends, sends

    shapes = [jax.ShapeDtypeStruct((N_CHIPS, g.shape[1] // 2, g.shape[2]), BF16) for g in grads]
    return _Comm(grads, shapes, 2 * len(grads), plan)


def _pair_sums(gs, sibs, name, comm=None):
    n = len(gs)

    def body(*refs):
        c = lax.axis_index("c")
        for g_ref, sib_ref, o_ref in zip(refs[:n], refs[n:2 * n], refs[2 * n:]):
            mine = _half_rows(g_ref.shape[1], c)
            o_ref[0] = (g_ref[0, mine, :].astype(F32) + sib_ref[0].astype(F32)).astype(BF16)

    def chunk(shape):
        return pl.BlockSpec((1,) + shape[1:], lambda q: (q, 0, 0))

    return _pcall(
        body, comm=comm, name=name, grid=(N_CHIPS,),
        in_specs=[chunk(g.shape) for g in gs] + [chunk(s.shape) for s in sibs],
        out_specs=[chunk(s.shape) for s in sibs],
        out_shape=[jax.ShapeDtypeStruct(s.shape, BF16) for s in sibs],
        compiler_params=_params(("arbitrary",)),
    )(*gs, *sibs)


def _ici_plan(sums):
    def plan(ins, outs, sems, finishing):
        x, y, c = _mesh_pos()
        sends = []
        for i, (s, rcv) in enumerate(zip(ins, outs)):
            for k, (px, py) in enumerate(_other_chips(x, y)):
                for p, (start, size) in enumerate(_row_parts(s.shape[1])):
                    rows = pl.ds(start, size)
                    sends.append(_remote(s.at[2 * px + py, rows, :], rcv.at[k, rows, :], sems,
                                         ICI_PARTS * (3 * i + k) + p, (px, py, c)))
        return [], sends, sends

    shapes = [jax.ShapeDtypeStruct((3,) + s.shape[1:], BF16) for s in sums]
    return _Comm(sums, shapes, 2 * 3 * ICI_PARTS * len(sums), plan)


def _run_comm(comm, name):
    def body():
        pass

    _, landed = _pcall(body, comm=comm, name=name, grid=(1,), in_specs=[], out_specs=[], out_shape=[])()
    return landed


def _finish_quarters(ss, rcvs, name, comm=None):
    n = len(ss)

    def body(*refs):
        s_refs, rcv_refs, out_refs, sems = refs[:n], refs[n:2 * n], refs[2 * n:3 * n], refs[3 * n]
        x, y, c = _mesh_pos()
        swaps = []
        for i, (s_ref, rcv_ref, out_ref) in enumerate(zip(s_refs, rcv_refs, out_refs)):
            mine = _half_rows(out_ref.shape[0], c)
            acc = s_ref[2 * x + y].astype(F32)
            for k in range(3):
                acc = acc + rcv_ref[k].astype(F32)
            out_ref[mine, :] = acc
            back = _remote(out_ref.at[mine, :], out_ref.at[mine, :], sems, i, (x, y, 1 - c))
            back.start()
            swaps.append(back)
        for back in swaps:
            back.wait()

    vmem = pl.BlockSpec(memory_space=pltpu.VMEM)
    return _pcall(
        body, comm=comm, name=name, grid=(1,),
        in_specs=[vmem] * (2 * n), out_specs=[vmem] * n,
        out_shape=[jax.ShapeDtypeStruct((2 * s.shape[1], s.shape[2]), F32) for s in ss],
        scratch_shapes=[pltpu.SemaphoreType.DMA((2 * n,))],
        compiler_params=pltpu.CompilerParams(vmem_limit_bytes=VMEM_LIMIT),
    )(*ss, *rcvs)


def _allreduce_small(v, name, comm=None):
    R, C = v.shape
    n_dev = 8

    def body(v_ref, out_ref, buf, send_sems, recv_sems):
        x, y, c = _mesh_pos()
        me = 4 * x + 2 * y + c
        buf[me] = v_ref[...]
        peers = []
        for k in range(1, n_dev):
            kx, ky, kc = (k >> 2) & 1, (k >> 1) & 1, k & 1
            px = 1 - x if kx else x
            py = 1 - y if ky else y
            pc = 1 - c if kc else c
            peers.append((px, py, pc))
        sends = []
        for k, peer in enumerate(peers):
            cp = pltpu.make_async_remote_copy(
                src_ref=v_ref, dst_ref=buf.at[me], send_sem=send_sems.at[k], recv_sem=recv_sems.at[k],
                device_id=peer, device_id_type=MESH)
            cp.start()
            sends.append(cp)
        for k, (px, py, pc) in enumerate(peers):
            pltpu.make_async_remote_copy(
                src_ref=v_ref, dst_ref=buf.at[4 * px + 2 * py + pc], send_sem=send_sems.at[k],
                recv_sem=recv_sems.at[k], device_id=(px, py, pc), device_id_type=MESH).wait_recv()
        for cp in sends:
            cp.wait_send()
        acc = buf[0]
        for i in range(1, n_dev):
            acc = acc + buf[i]
        out_ref[...] = acc

    vmem = pl.BlockSpec(memory_space=pltpu.VMEM)
    return _pcall(
        body, comm=comm, name=name, grid=(1,),
        in_specs=[vmem], out_specs=vmem,
        out_shape=jax.ShapeDtypeStruct((R, C), F32),
        scratch_shapes=[pltpu.VMEM((n_dev, R, C), F32),
                        pltpu.SemaphoreType.DMA((n_dev - 1,)), pltpu.SemaphoreType.DMA((n_dev - 1,))],
        compiler_params=pltpu.CompilerParams(vmem_limit_bytes=VMEM_LIMIT),
    )(v)


def _rope_tables(positions):
    half = ROPE_DIM // 2
    inv_freq = ROPE_THETA ** (-jnp.arange(half, dtype=F32) / half)
    ang = positions.astype(F32)[:, None] * inv_freq
    cos, sin = jnp.cos(ang), jnp.sin(ang)
    T = positions.shape[0]
    ones = jnp.ones((T, HEAD_DIM - ROPE_DIM), F32)
    c64 = jnp.concatenate([cos, cos, ones], axis=1)
    s64 = jnp.concatenate([-sin, sin, 0.0 * ones], axis=1)
    return jnp.tile(c64, (1, 2)), jnp.tile(s64, (1, 2))


def _local_step(x, mem, positions, target, small, first, own):
    cos, sin = _rope_tables(positions)
    two = lambda g: jnp.tile(g, (1, 2))
    gq2, gk2, gmq2, gmk2 = two(small["g_q"]), two(small["g_k"]), two(small["g_mq"]), two(small["g_mk"])
    mix_names = ["w_in", "w_mkv", "w_out"]
    mix_layouts = ["cols", "stack", "stack"]
    mix_quarters = [(D_MODEL, IN_CHUNK), (D_MODEL // N_CHIPS, 2 * MQ_COLS), (D_MODEL // N_CHIPS, D_MODEL)]
    ffn2_names = ["wg2", "wu2", "wd2"]
    ffn_quarter = (FF_CHUNK, D_MODEL)

    spread = _spread_plan([own["w_in"], own["w_mkv"], own["wg2"]], mix_quarters[:2] + [ffn_quarter], ["cols", "stack", "stack"])
    (x1, h1, a1, b1), (half_in, half_mkv, half_wg2) = _ffn_fwd(
        x, small["g_ffn1"], first["wg1"], first["wu1"], first["wd1"], "ffn1_fwd", comm=spread)
    hm, (w_in, w_mkv) = _rms_fwd(x1, small["g_mix"], "mix_norm",
                                 comm=_forward_plan([half_in, half_mkv], mix_quarters[:2], mix_layouts[:2]))
    w_mkv = w_mkv.reshape(D_MODEL, 2 * MQ_COLS)
    proj, (half_out,) = _mm_nn([hm], w_in, None, "in_proj",
                               comm=_spread_plan([own["w_out"]], mix_quarters[2:], mix_layouts[2:]))
    hmem = _rms_fwd(mem, small["g_mem"], "mem_norm")
    mkv = _mm_nn([hmem], w_mkv, None, "mem_proj")
    ya, (half_wu2,) = _attn_fwd(proj, cos, sin, gq2, gk2, small["sinks"], "swa_fwd",
                                comm=_spread_plan([own["wu2"]], [ffn_quarter], ["stack"]))
    (yc, cpre), (near_wd2, w_out) = _conv_fwd(
        proj, small["w_dw"], small["b_dw"], small["g_conv_ln"], small["b_conv_ln"], "conv_fwd",
        comm=_join_plans(_spread_plan([own["wd2"]], [ffn_quarter], ["stack"], peers=(0, 1)),
                         _forward_plan([half_out], mix_quarters[2:], mix_layouts[2:])))
    w_out = w_out.reshape(D_MODEL, D_MODEL)
    ym, (half_wd2,) = _mem_fwd(proj, mkv, gmq2, gmk2, "memattn_fwd",
                               comm=_spread_plan([near_wd2], [ffn_quarter], ["stack"], peers=(2,)))
    passing = _forward_plan([half_wg2, half_wu2, half_wd2], [ffn_quarter] * 3, ["stack"] * 3)
    x2, (wg2, wu2, wd2) = _mm_nn([ya, yc, ym], w_out, x1, "out_proj", comm=passing)
    x3, h2, a2, b2 = _ffn_fwd(x2, small["g_ffn2"], wg2, wu2, wd2, "ffn2_fwd")
    dx3, loss = _loss_head(x3, target, "loss_head")

    dh2, dwg2, dwu2, dwd2 = _ffn_bwd(dx3, h2, a2, b2, wg2, wu2, wd2, "ffn2_bwd")
    (dx2, dg_ffn2), sibs = _rms_bwd(x2, small["g_ffn2"], dh2, dx3, "ffn2_norm_bwd", comm=_swap_plan([dwg2, dwu2, dwd2]))
    sums_ffn2 = _pair_sums([dwg2, dwu2, dwd2], sibs, "pair_sums_ffn2")
    dyc = _mm_nt([dx2], w_out, "out_proj_bwd")
    dw_out = _mm_tn([ya, yc, ym], [dx2], 1, "out_proj_wgrad").reshape(N_CHIPS, -1, D_MODEL)
    (dq, dk, dv, dgq, dgk, dsinks), rcv_wg2 = _attn_bwd(proj, cos, sin, gq2, gk2, small["sinks"], dyc, "swa_bwd",
                                                         comm=_ici_plan(sums_ffn2[:1]))
    (da, dgate, dw_dw, db_dw, dg_ln, db_ln), sibs = _conv_bwd(
        proj, cpre, small["w_dw"], small["g_conv_ln"], small["b_conv_ln"], dyc, "conv_bwd", comm=_swap_plan([dw_out]))
    dmq, dmkv, dgmq, dgmk = _mem_bwd(proj, mkv, gmq2, gmk2, dyc, "memattn_bwd")
    sums_out = _pair_sums([dw_out], sibs, "pair_sums_out")
    pieces = [dq, dk, dv, da, dgate, dmq]
    dhm, rcv_out = _mm_nt(pieces, w_in, "in_proj_bwd", comm=_ici_plan(sums_out))
    dw_in = _mm_tn([hm], pieces, N_CHIPS, "in_proj_wgrad")
    dhmem = _mm_nt([dmkv], w_mkv, "mem_proj_bwd")
    dw_mkv = _mm_tn([hmem], [dmkv], 1, "mem_proj_wgrad").reshape(N_CHIPS, -1, 2 * MQ_COLS)
    _, dg_mem = _rms_bwd(mem, small["g_mem"], dhmem, None, "mem_norm_bwd")
    (dx1, dg_mix), sibs = _rms_bwd(x1, small["g_mix"], dhm, dx2, "mix_norm_bwd", comm=_swap_plan([dw_in, dw_mkv]))
    sums_in = _pair_sums([dw_in, dw_mkv], sibs, "pair_sums_in")
    (dh1, dwg1, dwu1, dwd1), landed = _ffn_bwd(dx1, h1, a1, b1, first["wg1"], first["wu1"], first["wd1"], "ffn1_bwd",
                                                comm=_ici_plan([*sums_in, *sums_ffn2[1:]]))
    rcv_in, rcv_ffn2 = landed[:2], rcv_wg2 + landed[2:]
    (dx, dg_ffn1), sibs = _rms_bwd(x, small["g_ffn1"], dh1, dx1, "ffn1_norm_bwd", comm=_swap_plan([dwg1, dwu1, dwd1]))
    sums_ffn1 = _pair_sums([dwg1, dwu1, dwd1], sibs, "pair_sums_ffn1")
    g_ffn2, rcv_wg1 = _finish_quarters(sums_ffn2, rcv_ffn2, "finish_ffn2", comm=_ici_plan(sums_ffn1[:1]))
    g_mix, rcv_wu1 = _finish_quarters(sums_in + sums_out, rcv_in + rcv_out, "finish_mix", comm=_ici_plan(sums_ffn1[1:2]))
    small_grads = dict(
        g_ffn1=dg_ffn1, g_mix=dg_mix, g_q=dgq[:, :HEAD_DIM], g_k=dgk[:, :HEAD_DIM], sinks=dsinks[:, :N_Q_HEADS],
        w_dw=dw_dw[:CONV_WIDTH], b_dw=db_dw, g_conv_ln=dg_ln, b_conv_ln=db_ln, g_mem=dg_mem,
        g_mq=dgmq[:, :HEAD_DIM], g_mk=dgmk[:, :HEAD_DIM], g_ffn2=dg_ffn2, loss=loss[:, :1])
    names = list(small_grads)
    packed, offs = _pack([small_grads[n] for n in names])
    total, rcv_wd1 = _allreduce_small(packed, "allreduce_small", comm=_ici_plan(sums_ffn1[2:]))
    summed = dict(zip(names, _unpack(total, offs, [small_grads[n].shape for n in names])))
    g_ffn1 = _finish_quarters(sums_ffn1, rcv_wg1 + rcv_wu1 + rcv_wd1, "finish_ffn1")
    big_grads = dict(zip(["wg1", "wu1", "wd1", "w_in", "w_mkv", "w_out"] + ffn2_names, [*g_ffn1, *g_mix, *g_ffn2]))
    return dx, big_grads, summed


SMALL_NAMES = ["g_ffn1", "g_mix", "g_q", "g_k", "sinks", "b_dw", "g_conv_ln", "b_conv_ln", "g_mem", "g_mq", "g_mk",
               "g_ffn2"]
PACK_COLS = 1024


def _pack(parts):
    flat = [p.reshape(-1) for p in parts]
    offs, o = [], 0
    for f in flat:
        offs.append(o)
        o += f.shape[0]
    rows = -(-o // (8 * PACK_COLS)) * 8
    pad = jnp.zeros((rows * PACK_COLS - o,), F32)
    return jnp.concatenate(flat + [pad]).reshape(rows, PACK_COLS), offs


def _unpack(packed, offs, shapes):
    flat = packed.reshape(-1)
    return [flat[o:o + math.prod(s)].reshape(s) for o, s in zip(offs, shapes)]


def kernel(x, mem, positions, g_ffn1, w_ffn1_gate, w_ffn1_up, w_ffn1_down, g_mix, w_in, g_q, g_k, sinks, w_dw, b_dw, g_conv_ln, b_conv_ln, g_mem, w_mem_kv, g_mq, g_mk, w_out, g_ffn2, w_ffn2_gate, w_ffn2_up, w_ffn2_down, loss_target, m_g_ffn1, m_w_ffn1_gate, m_w_ffn1_up, m_w_ffn1_down, m_g_mix, m_w_in, m_g_q, m_g_k, m_sinks, m_w_dw, m_b_dw, m_g_conv_ln, m_b_conv_ln, m_g_mem, m_w_mem_kv, m_g_mq, m_g_mk, m_w_out, m_g_ffn2, m_w_ffn2_gate, m_w_ffn2_up, m_w_ffn2_down, v_g_ffn1, v_w_ffn1_gate, v_w_ffn1_up, v_w_ffn1_down, v_g_mix, v_w_in, v_g_q, v_g_k, v_sinks, v_w_dw, v_b_dw, v_g_conv_ln, v_b_conv_ln, v_g_mem, v_w_mem_kv, v_g_mq, v_g_mk, v_w_out, v_g_ffn2, v_w_ffn2_gate, v_w_ffn2_up, v_w_ffn2_down):
    args = dict(locals())
    weight_names = ["g_ffn1", "w_ffn1_gate", "w_ffn1_up", "w_ffn1_down", "g_mix", "w_in", "g_q", "g_k", "sinks",
                    "w_dw", "b_dw", "g_conv_ln", "b_conv_ln", "g_mem", "w_mem_kv", "g_mq", "g_mk", "w_out", "g_ffn2",
                    "w_ffn2_gate", "w_ffn2_up", "w_ffn2_down"]
    big_names = ["w_ffn1_gate", "w_ffn1_up", "w_ffn1_down", "w_in", "w_mem_kv", "w_out",
                 "w_ffn2_gate", "w_ffn2_up", "w_ffn2_down"]
    short = dict(w_ffn1_gate="wg1", w_ffn1_up="wu1", w_ffn1_down="wd1", w_in="w_in", w_mem_kv="w_mkv",
                 w_out="w_out", w_ffn2_gate="wg2", w_ffn2_up="wu2", w_ffn2_down="wd2")

    transposed = ("w_ffn1_gate", "w_ffn1_up", "w_ffn2_gate", "w_ffn2_up")

    def quarter(a, n):
        return jnp.swapaxes(a, 1, 2)[0] if n in transposed else a[0]

    def unquarter(a, n):
        return jnp.swapaxes(a[None], 1, 2) if n in transposed else a[None]

    shards = [quarter(args[n], n) for n in big_names]
    layouts = ["cols" if n == "w_in" else "stack" for n in big_names]
    later = [i for i, n in enumerate(big_names) if short[n] not in ("wg1", "wu1", "wd1")]
    gathered = _gather_weights(shards + [w_dw[0]], [BF16] * len(shards) + [F32], layouts + ["whole"], later,
                               "gather_first")
    first = {short[n]: gathered[i] for i, n in enumerate(big_names) if i not in later}
    own = {short[n]: gathered[i] for i, n in enumerate(big_names) if i in later}
    small = {n: args[n] for n in SMALL_NAMES}
    small["w_dw"] = jnp.transpose(gathered[-1], (1, 0, 2)).reshape(CONV_WIDTH, CONV_CH)

    dx, big_grads, summed = _local_step(x[0], mem[0], positions[0], loss_target[0], small, first, own)
    chip = 2 * lax.axis_index("x") + lax.axis_index("y")
    dw_dw_full = summed.pop("w_dw")
    loss_out = summed.pop("loss").reshape(())

    grads = {n: summed[n] for n in SMALL_NAMES}
    grads["w_dw"] = lax.dynamic_slice_in_dim(dw_dw_full, chip * (CONV_CH // N_CHIPS), CONV_CH // N_CHIPS, axis=1)
    for n in big_names:
        grads[n] = big_grads[short[n]]

    delta, new_m, new_v = {}, {}, {}
    for n in big_names:
        g, d, nm, nv = _adamw(quarter(args[n], n), grads[n], quarter(args["m_" + n], n), quarter(args["v_" + n], n),
                              "adamw_" + short[n])
        grads[n], delta[n], new_m[n], new_v[n] = (unquarter(a, n) for a in (g, d, nm, nv))
    tiny = SMALL_NAMES + ["w_dw"]
    pw, poffs = _pack([args[n] for n in tiny])
    pg, _ = _pack([grads[n] for n in tiny])
    pm, _ = _pack([args["m_" + n] for n in tiny])
    pv, _ = _pack([args["v_" + n] for n in tiny])
    _, pd, pnm, pnv = _adamw(pw, pg, pm, pv, "adamw_small")
    tshapes = [args[n].shape for n in tiny]
    for store, packed_out in ((delta, pd), (new_m, pnm), (new_v, pnv)):
        for n, val in zip(tiny, _unpack(packed_out, poffs, tshapes)):
            store[n] = val

    def shaped(n, v):
        return v.reshape(args[n].shape)

    return (loss_out, dx[None],
            *[shaped(n, grads[n]) for n in weight_names],
            *[shaped(n, delta[n]) for n in weight_names],
            *[shaped(n, new_m[n]) for n in weight_names],
            *[shaped(n, new_v[n]) for n in weight_names])
```

```python
import functools
import math

import jax
import jax.numpy as jnp
from jax import lax
from jax.experimental import pallas as pl
from jax.experimental.pallas import tpu as pltpu

F32 = jnp.float32
BF16 = jnp.bfloat16

D_MODEL = 1024
SEQ = 2048
MEM_LEN = 256
HEAD_DIM = 64
N_Q_HEADS = 8
N_KV_HEADS = 2
Q_PER_KV = 4
N_MEM_HEADS = 4
BLOCK = 128
CONV_CH = 256
CONV_WIDTH = 31
ROPE_THETA = 500000.0
ROPE_DIM = 16
D_FF = 2816
EPS = 1e-6
Q_COLS = 512
KV_COLS = 128
MQ_COLS = 256
IN_COLS = 1536

N_CHIPS = 4
FF_CHUNK = D_FF // N_CHIPS
IN_CHUNK = IN_COLS // N_CHIPS

ADAM_LR = 0.001
ADAM_B1 = 0.9
ADAM_B2 = 0.999
ADAM_EPS = 1e-08
ADAM_WD = 0.01
ADAM_STEP = 10

LANES = 128
VMEM_LIMIT = 56 * 1024 * 1024
ROW_TILE = 512
MESH = pl.DeviceIdType.MESH
NEG = -1e30


class _Comm:
    def __init__(self, ins, out_shapes, n_sems, plan, aliases=None):
        self.ins, self.out_shapes, self.n_sems, self.plan = list(ins), list(out_shapes), n_sems, plan
        self.aliases = aliases or {}


def _pcall(body, comm=None, **kw):
    if comm is None:
        return pl.pallas_call(body, **kw)
    grid = kw["grid"]
    in_specs = list(kw["in_specs"])
    single = not isinstance(kw["out_shape"], (list, tuple))
    out_specs = [kw["out_specs"]] if single else list(kw["out_specs"])
    out_shape = [kw["out_shape"]] if single else list(kw["out_shape"])
    scratch = list(kw.get("scratch_shapes", ()))
    n_in, n_out, n_scr = len(in_specs), len(out_shape), len(scratch)
    n_ci, n_co = len(comm.ins), len(comm.out_shapes)

    def wrapped(*refs):
        o = 0
        parts = []
        for cnt in (n_in, n_ci, n_out, n_co, n_scr):
            parts.append(refs[o:o + cnt])
            o += cnt
        ins, c_ins, outs, c_outs, scr = parts
        sems = refs[o]
        first = last = None
        for d, size in enumerate(grid):
            at0, at_end = pl.program_id(d) == 0, pl.program_id(d) == size - 1
            first = at0 if first is None else first & at0
            last = at_end if last is None else last & at_end

        @pl.when(first)
        def _():
            local, sends, _ = comm.plan(c_ins, c_outs, sems, False)
            for cp in sends + local:
                cp.start()

        body(*ins, *outs, *scr)

        @pl.when(last)
        def _():
            local, sends, recvs = comm.plan(c_ins, c_outs, sems, True)
            for cp in recvs:
                cp.wait_recv()
            for cp in sends:
                cp.wait_send()
            for cp in local:
                cp.wait()

    hbm = pl.BlockSpec(memory_space=pl.ANY)
    kw = dict(kw, in_specs=in_specs + [hbm] * n_ci, out_specs=out_specs + [hbm] * n_co,
              out_shape=out_shape + comm.out_shapes,
              scratch_shapes=scratch + [pltpu.SemaphoreType.DMA((comm.n_sems,))])
    if comm.aliases:
        kw["input_output_aliases"] = {n_in + i: n_out + o for i, o in comm.aliases.items()}
    call = pl.pallas_call(wrapped, **kw)

    def run(*args):
        res = call(*args, *comm.ins)
        return (res[0] if single else list(res[:n_out])), list(res[n_out:])

    return run


def _params(sem=None):
    return pltpu.CompilerParams(dimension_semantics=sem, vmem_limit_bytes=VMEM_LIMIT)


def _dot(a, b):
    return jnp.dot(a, b, preferred_element_type=F32)


def _dot_nt(a, b):
    return lax.dot_general(a, b, (((1,), (1,)), ((), ())), preferred_element_type=F32)


def _dot_tn(a, b):
    return lax.dot_general(a, b, (((0,), (0,)), ((), ())), preferred_element_type=F32)


def _sigmoid(x):
    return 1.0 / (1.0 + jnp.exp(-x))


def _full(shape):
    n = len(shape)
    return pl.BlockSpec(shape, lambda *_: (0,) * n)


def _ffn_fwd(x, g, wg, wu, wd, name, comm=None, target=None):
    T, D = x.shape
    nt = T // ROW_TILE
    with_loss = target is not None

    def body(*refs):
        if with_loss:
            x_ref, g_ref, wg_ref, wu_ref, wd_ref, t_ref, xo_ref, h_ref, a_ref, b_ref, loss_ref = refs
        else:
            x_ref, g_ref, wg_ref, wu_ref, wd_ref, xo_ref, h_ref, a_ref, b_ref = refs
        t = pl.program_id(0)
        j = pl.program_id(1)

        @pl.when(j == 0)
        def _():
            xv = x_ref[...]
            rstd = lax.rsqrt(jnp.mean(xv * xv, axis=-1, keepdims=True) + EPS)
            h_ref[...] = (xv * rstd * g_ref[...]).astype(BF16)
            xo_ref[...] = jnp.zeros_like(xo_ref)

        h = h_ref[...]
        a = _dot_nt(h, wg_ref[0])
        b = _dot_nt(h, wu_ref[0])
        a_ref[0] = a.astype(BF16)
        b_ref[0] = b.astype(BF16)
        s = (a * _sigmoid(a)) * b
        xo_ref[...] += _dot(s.astype(BF16), wd_ref[0])

        @pl.when(j == N_CHIPS - 1)
        def _():
            out = x_ref[...] + 0.5 * xo_ref[...]
            if with_loss:
                err = out - t_ref[...]
                xo_ref[...] = err * (1.0 / D)
                part = 0.5 * jnp.sum(jnp.mean(err * err, axis=-1, keepdims=True), axis=0, keepdims=True)

                @pl.when(t == 0)
                def _():
                    loss_ref[...] = jnp.zeros_like(loss_ref)

                loss_ref[...] += jnp.broadcast_to(part, loss_ref.shape)
            else:
                xo_ref[...] = out

    tile = pl.BlockSpec((ROW_TILE, D), lambda t, j: (t, 0))
    chunk = pl.BlockSpec((1, FF_CHUNK, D), lambda t, j: (j, 0, 0))
    act = pl.BlockSpec((1, ROW_TILE, FF_CHUNK), lambda t, j: (j, t, 0))
    act_shape = jax.ShapeDtypeStruct((N_CHIPS, T, FF_CHUNK), BF16)
    loss_spec = [pl.BlockSpec((1, LANES), lambda t, j: (0, 0))] if with_loss else []
    loss_shape = [jax.ShapeDtypeStruct((1, LANES), F32)] if with_loss else []
    return _pcall(
        body, comm=comm, name=name, grid=(nt, N_CHIPS),
        in_specs=[tile, pl.BlockSpec((1, D), lambda t, j: (0, 0)), chunk, chunk, chunk] + ([tile] if with_loss else []),
        out_specs=[tile, tile, act, act] + loss_spec,
        out_shape=[jax.ShapeDtypeStruct((T, D), F32), jax.ShapeDtypeStruct((T, D), BF16), act_shape, act_shape]
        + loss_shape,
        compiler_params=_params(("arbitrary", "arbitrary")),
    )(x, g, wg, wu, wd, *([target] if with_loss else []))


def _ffn_bwd(dxo, h, a, b, wg, wu, wd, name, comm=None):
    T, D = dxo.shape
    tt = ROW_TILE
    nt = T // tt

    def body(dxo_ref, h_ref, a_ref, b_ref, wg_ref, wu_ref, wd_ref,
             dh_hbm, dwg_ref, dwu_ref, dwd_ref, dh_acc, acc_g, acc_u, acc_d):
        j = pl.program_id(0)
        t = pl.program_id(1)
        do = (0.5 * dxo_ref[...]).astype(BF16)
        av = a_ref[0].astype(F32)
        bv = b_ref[0].astype(F32)
        sig = _sigmoid(av)
        sa = av * sig
        ds = _dot_nt(do, wd_ref[0])
        da = (ds * bv * (sig * (1.0 + av * (1.0 - sig)))).astype(BF16)
        db = (ds * sa).astype(BF16)
        hv = h_ref[...]
        rows = pl.ds(pl.multiple_of(t * tt, tt), tt)

        @pl.when(j == 0)
        def _():
            dh_acc[rows, :] = jnp.zeros((tt, D), F32)

        @pl.when(t == 0)
        def _():
            acc_g[...] = jnp.zeros_like(acc_g)
            acc_u[...] = jnp.zeros_like(acc_u)
            acc_d[...] = jnp.zeros_like(acc_d)

        acc_d[...] += _dot_tn((sa * bv).astype(BF16), do)
        acc_g[...] += _dot_tn(da, hv)
        acc_u[...] += _dot_tn(db, hv)
        dh_acc[rows, :] += _dot(da, wg_ref[0]) + _dot(db, wu_ref[0])

        @pl.when(t == nt - 1)
        def _():
            dwg_ref[0] = acc_g[...].astype(BF16)
            dwu_ref[0] = acc_u[...].astype(BF16)
            dwd_ref[0] = acc_d[...].astype(BF16)

        @pl.when((t == nt - 1) & (j == N_CHIPS - 1))
        def _():
            pltpu.sync_copy(dh_acc, dh_hbm)

    return _pcall(
        body, comm=comm, name=name, grid=(N_CHIPS, nt),
        in_specs=[
            pl.BlockSpec((tt, D), lambda j, t: (t, 0)),
            pl.BlockSpec((tt, D), lambda j, t: (t, 0)),
            pl.BlockSpec((1, tt, FF_CHUNK), lambda j, t: (j, t, 0)),
            pl.BlockSpec((1, tt, FF_CHUNK), lambda j, t: (j, t, 0)),
            pl.BlockSpec((1, FF_CHUNK, D), lambda j, t: (j, 0, 0)),
            pl.BlockSpec((1, FF_CHUNK, D), lambda j, t: (j, 0, 0)),
            pl.BlockSpec((1, FF_CHUNK, D), lambda j, t: (j, 0, 0)),
        ],
        out_specs=[
            pl.BlockSpec(memory_space=pl.ANY),
            pl.BlockSpec((1, FF_CHUNK, D), lambda j, t: (j, 0, 0)),
            pl.BlockSpec((1, FF_CHUNK, D), lambda j, t: (j, 0, 0)),
            pl.BlockSpec((1, FF_CHUNK, D), lambda j, t: (j, 0, 0)),
        ],
        out_shape=[
            jax.ShapeDtypeStruct((T, D), F32),
            jax.ShapeDtypeStruct((N_CHIPS, FF_CHUNK, D), BF16),
            jax.ShapeDtypeStruct((N_CHIPS, FF_CHUNK, D), BF16),
            jax.ShapeDtypeStruct((N_CHIPS, FF_CHUNK, D), BF16),
        ],
        scratch_shapes=[
            pltpu.VMEM((T, D), F32),
            pltpu.VMEM((FF_CHUNK, D), F32),
            pltpu.VMEM((FF_CHUNK, D), F32),
            pltpu.VMEM((FF_CHUNK, D), F32),
        ],
        compiler_params=_params(("arbitrary", "arbitrary")),
    )(dxo, h, a, b, wg, wu, wd)


def _rms_fwd(x, g, name, comm=None):
    T, D = x.shape
    tt = min(ROW_TILE, T)

    def body(x_ref, g_ref, h_ref):
        xv = x_ref[...]
        rstd = lax.rsqrt(jnp.mean(xv * xv, axis=-1, keepdims=True) + EPS)
        h_ref[...] = (xv * rstd * g_ref[...]).astype(BF16)

    return _pcall(
        body, comm=comm, name=name, grid=(T // tt,),
        in_specs=[pl.BlockSpec((tt, D), lambda t: (t, 0)), pl.BlockSpec((1, D), lambda t: (0, 0))],
        out_specs=pl.BlockSpec((tt, D), lambda t: (t, 0)),
        out_shape=jax.ShapeDtypeStruct((T, D), BF16),
        compiler_params=_params(("arbitrary",)),
    )(x, g)


def _rms_bwd(x, g, dh, dres, name, comm=None):
    T, D = x.shape
    tt = min(ROW_TILE, T)
    has_res = dres is not None

    def body(*refs):
        if has_res:
            x_ref, g_ref, dh_ref, dres_ref, dx_ref, dg_ref = refs
        else:
            x_ref, g_ref, dh_ref, dx_ref, dg_ref = refs
        t = pl.program_id(0)
        xv = x_ref[...]
        rstd = lax.rsqrt(jnp.mean(xv * xv, axis=-1, keepdims=True) + EPS)
        xhat = xv * rstd
        dhv = dh_ref[...]
        gy = dhv * g_ref[...]
        dx = rstd * (gy - xhat * jnp.mean(gy * xhat, axis=-1, keepdims=True))
        if has_res:
            dx = dx + dres_ref[...]
        dx_ref[...] = dx
        part = jnp.sum(dhv * xhat, axis=0, keepdims=True)

        @pl.when(t == 0)
        def _():
            dg_ref[...] = part

        @pl.when(t > 0)
        def _():
            dg_ref[...] += part

    tile = pl.BlockSpec((tt, D), lambda t: (t, 0))
    vec = pl.BlockSpec((1, D), lambda t: (0, 0))
    args = [x, g, dh] + ([dres] if has_res else [])
    return _pcall(
        body, comm=comm, name=name, grid=(T // tt,),
        in_specs=[tile, vec, tile] + ([tile] if has_res else []),
        out_specs=[tile, vec],
        out_shape=[jax.ShapeDtypeStruct((T, D), F32), jax.ShapeDtypeStruct((1, D), F32)],
        compiler_params=_params(("arbitrary",)),
    )(*args)


def _mm_nn(a_list, b, res, name, comm=None):
    T = a_list[0].shape[0]
    K, N = b.shape
    tt = min(ROW_TILE, T)
    ks = [a.shape[1] for a in a_list]
    na = len(a_list)
    has_res = res is not None

    def body(*refs):
        a_refs = refs[:na]
        b_ref = refs[na]
        o_ref = refs[-1]
        acc = res_v = None
        off = 0
        for a_ref, k in zip(a_refs, ks):
            part = _dot(a_ref[...].astype(BF16), b_ref[off:off + k, :])
            acc = part if acc is None else acc + part
            off += k
        if has_res:
            acc = refs[na + 1][...] + acc
        o_ref[...] = acc

    in_specs = [pl.BlockSpec((tt, k), lambda t: (t, 0)) for k in ks] + [pl.BlockSpec((K, N), lambda t: (0, 0))]
    args = list(a_list) + [b]
    if has_res:
        in_specs.append(pl.BlockSpec((tt, N), lambda t: (t, 0)))
        args.append(res)
    return _pcall(
        body, comm=comm, name=name, grid=(T // tt,), in_specs=in_specs,
        out_specs=pl.BlockSpec((tt, N), lambda t: (t, 0)),
        out_shape=jax.ShapeDtypeStruct((T, N), F32),
        compiler_params=_params(("arbitrary",)),
    )(*args)


def _mm_nt(a_list, b, name, comm=None):
    T = a_list[0].shape[0]
    K, N = b.shape
    tt = min(ROW_TILE, T)
    ns = [a.shape[1] for a in a_list]
    na = len(a_list)

    def body(*refs):
        b_ref = refs[na]
        o_ref = refs[-1]
        acc = None
        off = 0
        for a_ref, n in zip(refs[:na], ns):
            part = _dot_nt(a_ref[...].astype(BF16), b_ref[:, off:off + n])
            acc = part if acc is None else acc + part
            off += n
        o_ref[...] = acc

    return _pcall(
        body, comm=comm, name=name, grid=(T // tt,),
        in_specs=[pl.BlockSpec((tt, n), lambda t: (t, 0)) for n in ns] + [pl.BlockSpec((K, N), lambda t: (0, 0))],
        out_specs=pl.BlockSpec((tt, K), lambda t: (t, 0)),
        out_shape=jax.ShapeDtypeStruct((T, K), F32),
        compiler_params=_params(("arbitrary",)),
    )(*a_list, b)


def _mm_tn(a_list, b_list, col_chunks, name, comm=None):
    T = a_list[0].shape[0]
    tt = min(ROW_TILE, T)
    nt = T // tt
    ms = [a.shape[1] for a in a_list]
    ns = [b.shape[1] for b in b_list]
    M, N = sum(ms), sum(ns)
    na, nb = len(a_list), len(b_list)
    cw = N // col_chunks

    def body(*refs):
        a_refs, b_refs = refs[:na], refs[na:na + nb]
        o_ref, acc = refs[na + nb], refs[na + nb + 1]
        t = pl.program_id(0)

        @pl.when(t == 0)
        def _():
            acc[...] = jnp.zeros_like(acc)

        ro = 0
        for a_ref, m in zip(a_refs, ms):
            av = a_ref[...].astype(BF16)
            co = 0
            for b_ref, n in zip(b_refs, ns):
                acc[ro:ro + m, co:co + n] += _dot_tn(av, b_ref[...].astype(BF16))
                co += n
            ro += m

        @pl.when(t == nt - 1)
        def _():
            if col_chunks == 1:
                o_ref[...] = acc[...].astype(BF16)
            else:
                for q in range(col_chunks):
                    o_ref[q] = acc[:, q * cw:(q + 1) * cw].astype(BF16)

    out_shape = (M, N) if col_chunks == 1 else (col_chunks, M, cw)
    return _pcall(
        body, comm=comm, name=name, grid=(nt,),
        in_specs=[pl.BlockSpec((tt, m), lambda t: (t, 0)) for m in ms]
        + [pl.BlockSpec((tt, n), lambda t: (t, 0)) for n in ns],
        out_specs=_full(out_shape),
        out_shape=jax.ShapeDtypeStruct(out_shape, BF16),
        scratch_shapes=[pltpu.VMEM((M, N), F32)],
        compiler_params=_params(("arbitrary",)),
    )(*a_list, *b_list)


def _head_masks():
    lane = lax.broadcasted_iota(jnp.int32, (1, LANES), 1)
    l64 = lane & (HEAD_DIM - 1)
    return lane < HEAD_DIM, l64 < ROPE_DIM // 2, l64 < ROPE_DIM


def _head_mean(v, lo):
    s_lo = jnp.sum(jnp.where(lo, v, 0.0), axis=-1, keepdims=True)
    s_hi = jnp.sum(jnp.where(lo, 0.0, v), axis=-1, keepdims=True)
    return jnp.where(lo, s_lo, s_hi) * (1.0 / HEAD_DIM)


def _rope_swap(v, first, rot):
    up = pltpu.roll(v, LANES - ROPE_DIM // 2, 1)
    down = pltpu.roll(v, ROPE_DIM // 2, 1)
    return jnp.where(first, up, jnp.where(rot, down, 0.0))


def _head_norm(x, g, lo):
    rstd = lax.rsqrt(_head_mean(x * x, lo) + EPS)
    return x * rstd * g


def _head_norm_bwd(x, g, dy, lo):
    rstd = lax.rsqrt(_head_mean(x * x, lo) + EPS)
    xhat = x * rstd
    gy = dy * g
    dx = rstd * (gy - xhat * _head_mean(gy * xhat, lo))
    return dx, dy * xhat


def _rope(xn, cos, sin, first, rot):
    return xn * cos + _rope_swap(xn, first, rot) * sin


def _rope_bwd(dy, cos, sin, first, rot):
    return dy * cos + _rope_swap(dy * sin, first, rot)


def _fold_heads(v):
    return v + pltpu.roll(v, HEAD_DIM, 1)


ATT_ROWS = 256


def _attn_prepare(q_ref, k_ref, v_ref, cos_ref, sin_ref, gq_ref, gk_ref, qs, ks, vs):
    T = q_ref.shape[0]
    lo, first, rot = _head_masks()
    ks[0:BLOCK, :] = jnp.zeros((BLOCK, KV_COLS), BF16)
    vs[0:BLOCK, :] = jnp.zeros((BLOCK, KV_COLS), BF16)

    def step(i, _):
        r0 = pl.multiple_of(i * ATT_ROWS, ATT_ROWS)
        rows = pl.ds(r0, ATT_ROWS)
        prow = pl.ds(r0 + BLOCK, ATT_ROWS)
        cos, sin = cos_ref[rows, :], sin_ref[rows, :]
        for p in range(Q_COLS // LANES):
            cols = slice(p * LANES, (p + 1) * LANES)
            xr = _rope(_head_norm(q_ref[rows, cols], gq_ref[...], lo), cos, sin, first, rot)
            qs[rows, cols] = (xr * (HEAD_DIM ** -0.5)).astype(BF16)
        kr = _rope(_head_norm(k_ref[rows, :], gk_ref[...], lo), cos, sin, first, rot)
        ks[prow, :] = kr.astype(BF16)
        vs[prow, :] = v_ref[rows, :].astype(BF16)
        return 0

    lax.fori_loop(0, T // ATT_ROWS, step, 0)


GROUP_ROWS = Q_PER_KV * BLOCK


def _group_rows(ref, r0, g, cast=None):
    parts = []
    for r in range(Q_PER_KV):
        h = g * Q_PER_KV + r
        part = ref[pl.ds(r0, BLOCK), h * HEAD_DIM:(h + 1) * HEAD_DIM]
        parts.append(part if cast is None else part.astype(cast))
    return jnp.concatenate(parts, axis=0)


def _group_sinks(sink_ref, g):
    row = lax.broadcasted_iota(jnp.int32, (GROUP_ROWS, 1), 0)
    col = jnp.full((GROUP_ROWS, 1), sink_ref[0, g * Q_PER_KV], F32)
    for r in range(1, Q_PER_KV):
        col = jnp.where(row >= r * BLOCK, sink_ref[0, g * Q_PER_KV + r], col)
    return col


def _attn_scores(qg, kw, blk, sink):
    s = _dot_nt(qg, kw)
    qi = (lax.broadcasted_iota(jnp.int32, (GROUP_ROWS, 2 * BLOCK), 0) & (BLOCK - 1)) + BLOCK
    ki = lax.broadcasted_iota(jnp.int32, (GROUP_ROWS, 2 * BLOCK), 1)
    rel = qi - ki
    valid = (rel >= 0) & (rel < BLOCK) & ((blk > 0) | (ki >= BLOCK))
    s = jnp.where(valid, s, NEG)
    m = jnp.maximum(jnp.max(s, axis=-1, keepdims=True), sink)
    p = jnp.exp(s - m)
    e_sink = jnp.exp(sink - m)
    inv = 1.0 / (jnp.sum(p, axis=-1, keepdims=True) + e_sink)
    return p * inv, e_sink * inv


def _attn_fwd(proj, cos, sin, gq2, gk2, sinks, name, comm=None):
    T = proj.shape[0]
    nb = T // BLOCK

    def body(q_ref, k_ref, v_ref, cos_ref, sin_ref, gq_ref, gk_ref, sink_ref, y_ref, qs, ks, vs):
        _attn_prepare(q_ref, k_ref, v_ref, cos_ref, sin_ref, gq_ref, gk_ref, qs, ks, vs)

        def blk_step(blk, _):
            r0 = pl.multiple_of(blk * BLOCK, BLOCK)
            for g in range(N_KV_HEADS):
                gc = slice(g * HEAD_DIM, (g + 1) * HEAD_DIM)
                kw = ks[pl.ds(r0, 2 * BLOCK), gc]
                vw = vs[pl.ds(r0, 2 * BLOCK), gc]
                w, _ws = _attn_scores(_group_rows(qs, r0, g), kw, blk, _group_sinks(sink_ref, g))
                o = _dot(w.astype(BF16), vw).astype(BF16)
                for r in range(Q_PER_KV):
                    h = g * Q_PER_KV + r
                    y_ref[pl.ds(r0, BLOCK), h * HEAD_DIM:(h + 1) * HEAD_DIM] = o[r * BLOCK:(r + 1) * BLOCK, :]
            return 0

        lax.fori_loop(0, nb, blk_step, 0)

    return _pcall(
        body, comm=comm, name=name, grid=(1,),
        in_specs=[
            pl.BlockSpec((T, Q_COLS), lambda i: (0, 0)),
            pl.BlockSpec((T, KV_COLS), lambda i: (0, Q_COLS // KV_COLS)),
            pl.BlockSpec((T, KV_COLS), lambda i: (0, Q_COLS // KV_COLS + 1)),
            _full((T, LANES)), _full((T, LANES)), _full((1, LANES)), _full((1, LANES)),
            pl.BlockSpec(memory_space=pltpu.SMEM),
        ],
        out_specs=_full((T, Q_COLS)),
        out_shape=jax.ShapeDtypeStruct((T, Q_COLS), BF16),
        scratch_shapes=[
            pltpu.VMEM((T, Q_COLS), BF16),
            pltpu.VMEM((T + BLOCK, KV_COLS), BF16),
            pltpu.VMEM((T + BLOCK, KV_COLS), BF16),
        ],
        compiler_params=_params(("arbitrary",)),
    )(proj, proj, proj, cos, sin, gq2, gk2, sinks)


def _attn_bwd(proj, cos, sin, gq2, gk2, sinks, dyc, name, comm=None):
    T = proj.shape[0]
    nb = T // BLOCK

    def body(q_ref, k_ref, v_ref, cos_ref, sin_ref, gq_ref, gk_ref, sink_ref, dy_ref,
             dq_ref, dk_ref, dv_ref, dgq_ref, dgk_ref, dsink_ref, qs, ks, vs, dqs, dks, dvs):
        _attn_prepare(q_ref, k_ref, v_ref, cos_ref, sin_ref, gq_ref, gk_ref, qs, ks, vs)
        dks[...] = jnp.zeros_like(dks)
        dvs[...] = jnp.zeros_like(dvs)
        lane = lax.broadcasted_iota(jnp.int32, (1, LANES), 1)

        def blk_step(blk, dsink):
            r0 = pl.multiple_of(blk * BLOCK, BLOCK)
            win = pl.ds(r0, 2 * BLOCK)
            for g in range(N_KV_HEADS):
                gc = slice(g * HEAD_DIM, (g + 1) * HEAD_DIM)
                kw = ks[win, gc]
                vw = vs[win, gc]
                qg = _group_rows(qs, r0, g)
                w, w_sink = _attn_scores(qg, kw, blk, _group_sinks(sink_ref, g))
                do = _group_rows(dy_ref, r0, g, cast=BF16)
                dvs[win, gc] += _dot_tn(w.astype(BF16), do)
                dw = _dot_nt(do, vw)
                delta = jnp.sum(w * dw, axis=-1, keepdims=True)
                ds = (w * (dw - delta)).astype(BF16)
                sink_part = w_sink * delta
                dq = _dot(ds, kw)
                for r in range(Q_PER_KV):
                    h = g * Q_PER_KV + r
                    slab = slice(r * BLOCK, (r + 1) * BLOCK)
                    dsink = dsink + jnp.where(lane == h, -jnp.sum(sink_part[slab, :], axis=0, keepdims=True), 0.0)
                    dqs[pl.ds(r0, BLOCK), h * HEAD_DIM:(h + 1) * HEAD_DIM] = dq[slab, :]
                dks[win, gc] += _dot_tn(ds, qg)
            return dsink

        dsink_ref[...] = lax.fori_loop(0, nb, blk_step, jnp.zeros((1, LANES), F32))

        lo, first, rot = _head_masks()

        def step(i, carry):
            dgq, dgk = carry
            r0 = pl.multiple_of(i * ATT_ROWS, ATT_ROWS)
            rows = pl.ds(r0, ATT_ROWS)
            prow = pl.ds(r0 + BLOCK, ATT_ROWS)
            cos, sin = cos_ref[rows, :], sin_ref[rows, :]
            for p in range(Q_COLS // LANES):
                cols = slice(p * LANES, (p + 1) * LANES)
                dxn = _rope_bwd(dqs[rows, cols] * (HEAD_DIM ** -0.5), cos, sin, first, rot)
                dx, dgp = _head_norm_bwd(q_ref[rows, cols], gq_ref[...], dxn, lo)
                dq_ref[rows, cols] = dx
                dgq = dgq + jnp.sum(dgp, axis=0, keepdims=True)
            dkn = _rope_bwd(dks[prow, :], cos, sin, first, rot)
            dx, dgp = _head_norm_bwd(k_ref[rows, :], gk_ref[...], dkn, lo)
            dk_ref[rows, :] = dx
            dgk = dgk + jnp.sum(dgp, axis=0, keepdims=True)
            dv_ref[rows, :] = dvs[prow, :]
            return dgq, dgk

        zero = jnp.zeros((1, LANES), F32)
        dgq, dgk = lax.fori_loop(0, T // ATT_ROWS, step, (zero, zero))
        dgq_ref[...] = _fold_heads(dgq)
        dgk_ref[...] = _fold_heads(dgk)

    vec = jax.ShapeDtypeStruct((1, LANES), F32)
    return _pcall(
        body, comm=comm, name=name, grid=(1,),
        in_specs=[
            pl.BlockSpec((T, Q_COLS), lambda i: (0, 0)),
            pl.BlockSpec((T, KV_COLS), lambda i: (0, Q_COLS // KV_COLS)),
            pl.BlockSpec((T, KV_COLS), lambda i: (0, Q_COLS // KV_COLS + 1)),
            _full((T, LANES)), _full((T, LANES)), _full((1, LANES)), _full((1, LANES)),
            pl.BlockSpec(memory_space=pltpu.SMEM),
            pl.BlockSpec((T, Q_COLS), lambda i: (0, 0)),
        ],
        out_specs=[_full((T, Q_COLS)), _full((T, KV_COLS)), _full((T, KV_COLS)),
                   _full((1, LANES)), _full((1, LANES)), _full((1, LANES))],
        out_shape=[jax.ShapeDtypeStruct((T, Q_COLS), F32), jax.ShapeDtypeStruct((T, KV_COLS), F32),
                   jax.ShapeDtypeStruct((T, KV_COLS), F32), vec, vec, vec],
        scratch_shapes=[
            pltpu.VMEM((T, Q_COLS), BF16),
            pltpu.VMEM((T + BLOCK, KV_COLS), BF16),
            pltpu.VMEM((T + BLOCK, KV_COLS), BF16),
            pltpu.VMEM((T, Q_COLS), F32),
            pltpu.VMEM((T + BLOCK, KV_COLS), F32),
            pltpu.VMEM((T + BLOCK, KV_COLS), F32),
        ],
        compiler_params=_params(("arbitrary",)),
    )(proj, proj, proj, cos, sin, gq2, gk2, sinks, dyc)


CONV_PAD = 32
CONV_ROWS = 256


def _conv_taps(src, w_ref, r0, first_off, step_sign):
    acc = None
    for i in range(CONV_WIDTH):
        term = w_ref[i:i + 1, :] * src[r0 + first_off + step_sign * i:r0 + first_off + step_sign * i + CONV_ROWS, :]
        acc = term if acc is None else acc + term
    return acc


def _conv_fwd(proj, w_dw, b_dw, g_ln, b_ln, name, comm=None):
    T = proj.shape[0]
    a_blk = (Q_COLS + 2 * KV_COLS) // CONV_CH

    def body(a_ref, gate_ref, w_ref, bdw_ref, g_ref, b_ref, y_ref, c_ref, pad):
        pad[0:CONV_PAD, :] = jnp.zeros((CONV_PAD, CONV_CH), F32)
        pad[CONV_PAD:, :] = a_ref[...] * _sigmoid(gate_ref[...])
        for n in range(T // CONV_ROWS):
            r0 = n * CONV_ROWS
            c = _conv_taps(pad, w_ref, r0, CONV_PAD - (CONV_WIDTH - 1), 1) + bdw_ref[...]
            c_ref[r0:r0 + CONV_ROWS, :] = c
            mu = jnp.mean(c, axis=-1, keepdims=True)
            cc = c - mu
            rstd = lax.rsqrt(jnp.mean(cc * cc, axis=-1, keepdims=True) + EPS)
            z = cc * rstd * g_ref[...] + b_ref[...]
            y_ref[r0:r0 + CONV_ROWS, :] = (z * _sigmoid(z)).astype(BF16)

    vec = _full((1, CONV_CH))
    return _pcall(
        body, comm=comm, name=name, grid=(1,),
        in_specs=[
            pl.BlockSpec((T, CONV_CH), lambda i: (0, a_blk)),
            pl.BlockSpec((T, CONV_CH), lambda i: (0, a_blk + 1)),
            _full((CONV_WIDTH, CONV_CH)), vec, vec, vec,
        ],
        out_specs=[_full((T, CONV_CH)), _full((T, CONV_CH))],
        out_shape=[jax.ShapeDtypeStruct((T, CONV_CH), BF16), jax.ShapeDtypeStruct((T, CONV_CH), F32)],
        scratch_shapes=[pltpu.VMEM((T + CONV_PAD, CONV_CH), F32)],
        compiler_params=_params(("arbitrary",)),
    )(proj, proj, w_dw, b_dw, g_ln, b_ln)


def _conv_bwd(proj, c, w_dw, g_ln, b_ln, dyc, name, comm=None):
    T = proj.shape[0]
    a_blk = (Q_COLS + 2 * KV_COLS) // CONV_CH
    y_blk = Q_COLS // CONV_CH

    def body(a_ref, gate_ref, c_ref, w_ref, g_ref, b_ref, dy_ref,
             da_ref, dgate_ref, dw_ref, dbdw_ref, dg_ref, db_ref, pad, dcp):
        pad[0:CONV_PAD, :] = jnp.zeros((CONV_PAD, CONV_CH), F32)
        sg = _sigmoid(gate_ref[...])
        pad[CONV_PAD:, :] = a_ref[...] * sg
        dcp[T:, :] = jnp.zeros((CONV_PAD, CONV_CH), F32)
        dg = db = dbdw = jnp.zeros((1, CONV_CH), F32)
        for n in range(T // CONV_ROWS):
            rows = slice(n * CONV_ROWS, (n + 1) * CONV_ROWS)
            cv = c_ref[rows, :]
            mu = jnp.mean(cv, axis=-1, keepdims=True)
            cc = cv - mu
            rstd = lax.rsqrt(jnp.mean(cc * cc, axis=-1, keepdims=True) + EPS)
            chat = cc * rstd
            z = chat * g_ref[...] + b_ref[...]
            sz = _sigmoid(z)
            dz = dy_ref[rows, :] * (sz * (1.0 + z * (1.0 - sz)))
            dg = dg + jnp.sum(dz * chat, axis=0, keepdims=True)
            db = db + jnp.sum(dz, axis=0, keepdims=True)
            dch = dz * g_ref[...]
            dc = rstd * (dch - jnp.mean(dch, axis=-1, keepdims=True)
                         - chat * jnp.mean(dch * chat, axis=-1, keepdims=True))
            dbdw = dbdw + jnp.sum(dc, axis=0, keepdims=True)
            dcp[rows, :] = dc
        dg_ref[...] = dg
        db_ref[...] = db
        dbdw_ref[...] = dbdw
        dw_ref[CONV_WIDTH:, :] = jnp.zeros((CONV_PAD - CONV_WIDTH, CONV_CH), F32)
        for i in range(CONV_WIDTH):
            off = CONV_PAD - (CONV_WIDTH - 1) + i
            acc = jnp.zeros((1, CONV_CH), F32)
            for n in range(T // CONV_ROWS):
                r0 = n * CONV_ROWS
                acc = acc + jnp.sum(dcp[r0:r0 + CONV_ROWS, :] * pad[r0 + off:r0 + off + CONV_ROWS, :],
                                    axis=0, keepdims=True)
            dw_ref[i:i + 1, :] = acc
        for n in range(T // CONV_ROWS):
            r0 = n * CONV_ROWS
            rows = slice(r0, r0 + CONV_ROWS)
            dhg = _conv_taps(dcp, w_ref, r0, CONV_WIDTH - 1, -1)
            sgv = sg[rows, :]
            da_ref[rows, :] = dhg * sgv
            dgate_ref[rows, :] = dhg * a_ref[rows, :] * sgv * (1.0 - sgv)

    vec = _full((1, CONV_CH))
    vshape = jax.ShapeDtypeStruct((1, CONV_CH), F32)
    return _pcall(
        body, comm=comm, name=name, grid=(1,),
        in_specs=[
            pl.BlockSpec((T, CONV_CH), lambda i: (0, a_blk)),
            pl.BlockSpec((T, CONV_CH), lambda i: (0, a_blk + 1)),
            _full((T, CONV_CH)), _full((CONV_WIDTH, CONV_CH)), vec, vec,
            pl.BlockSpec((T, CONV_CH), lambda i: (0, y_blk)),
        ],
        out_specs=[_full((T, CONV_CH)), _full((T, CONV_CH)), _full((CONV_PAD, CONV_CH)), vec, vec, vec],
        out_shape=[jax.ShapeDtypeStruct((T, CONV_CH), F32), jax.ShapeDtypeStruct((T, CONV_CH), F32),
                   jax.ShapeDtypeStruct((CONV_PAD, CONV_CH), F32), vshape, vshape, vshape],
        scratch_shapes=[pltpu.VMEM((T + CONV_PAD, CONV_CH), F32), pltpu.VMEM((T + CONV_PAD, CONV_CH), F32)],
        compiler_params=_params(("arbitrary",)),
    )(proj, proj, c, w_dw, g_ln, b_ln, dyc)


def _mem_kv(mkv_ref, gk_ref, lo, kn_s, vv_s):
    for p in range(MQ_COLS // LANES):
        cols = slice(p * LANES, (p + 1) * LANES)
        kn_s[:, cols] = _head_norm(mkv_ref[:, cols], gk_ref[...], lo).astype(BF16)
    vv_s[...] = mkv_ref[:, MQ_COLS:].astype(BF16)


def _mem_softmax(qh, kh):
    s = _dot_nt(qh, kh)
    m = jnp.max(s, axis=-1, keepdims=True)
    p = jnp.exp(s - m)
    return p / jnp.sum(p, axis=-1, keepdims=True)


def _mem_fwd(proj, mkv, gq2, gk2, name, comm=None):
    T = proj.shape[0]
    tt = ROW_TILE
    q_blk = (IN_COLS - MQ_COLS) // MQ_COLS

    def body(q_ref, mkv_ref, gq_ref, gk_ref, y_ref, kn_s, vv_s, qn_s):
        lo, _, _ = _head_masks()
        _mem_kv(mkv_ref, gk_ref, lo, kn_s, vv_s)
        for p in range(MQ_COLS // LANES):
            cols = slice(p * LANES, (p + 1) * LANES)
            qn_s[:, cols] = (_head_norm(q_ref[:, cols], gq_ref[...], lo) * (HEAD_DIM ** -0.5)).astype(BF16)
        for h in range(N_MEM_HEADS):
            hc = slice(h * HEAD_DIM, (h + 1) * HEAD_DIM)
            w = _mem_softmax(qn_s[:, hc], kn_s[:, hc])
            y_ref[:, hc] = _dot(w.astype(BF16), vv_s[:, hc]).astype(BF16)

    return _pcall(
        body, comm=comm, name=name, grid=(T // tt,),
        in_specs=[
            pl.BlockSpec((tt, MQ_COLS), lambda t: (t, q_blk)),
            pl.BlockSpec((MEM_LEN, 2 * MQ_COLS), lambda t: (0, 0)),
            pl.BlockSpec((1, LANES), lambda t: (0, 0)), pl.BlockSpec((1, LANES), lambda t: (0, 0)),
        ],
        out_specs=pl.BlockSpec((tt, MQ_COLS), lambda t: (t, 0)),
        out_shape=jax.ShapeDtypeStruct((T, MQ_COLS), BF16),
        scratch_shapes=[pltpu.VMEM((MEM_LEN, MQ_COLS), BF16), pltpu.VMEM((MEM_LEN, MQ_COLS), BF16),
                        pltpu.VMEM((tt, MQ_COLS), BF16)],
        compiler_params=_params(("arbitrary",)),
    )(proj, mkv, gq2, gk2)


def _mem_bwd(proj, mkv, gq2, gk2, dyc, name, comm=None):
    T = proj.shape[0]
    tt = ROW_TILE
    nt = T // tt
    q_blk = (IN_COLS - MQ_COLS) // MQ_COLS
    y_blk = (Q_COLS + CONV_CH) // MQ_COLS

    def body(q_ref, mkv_ref, gq_ref, gk_ref, dy_ref, dq_ref, dmkv_ref, dgq_ref, dgk_ref,
             kn_s, vv_s, qn_s, dqn_s, dkn_acc):
        t = pl.program_id(0)
        lo, _, _ = _head_masks()
        _mem_kv(mkv_ref, gk_ref, lo, kn_s, vv_s)

        @pl.when(t == 0)
        def _():
            dkn_acc[...] = jnp.zeros_like(dkn_acc)
            dmkv_ref[...] = jnp.zeros_like(dmkv_ref)
            dgq_ref[...] = jnp.zeros_like(dgq_ref)

        for p in range(MQ_COLS // LANES):
            cols = slice(p * LANES, (p + 1) * LANES)
            qn_s[:, cols] = (_head_norm(q_ref[:, cols], gq_ref[...], lo) * (HEAD_DIM ** -0.5)).astype(BF16)
        for h in range(N_MEM_HEADS):
            hc = slice(h * HEAD_DIM, (h + 1) * HEAD_DIM)
            vc = slice(MQ_COLS + h * HEAD_DIM, MQ_COLS + (h + 1) * HEAD_DIM)
            qh = qn_s[:, hc]
            w = _mem_softmax(qh, kn_s[:, hc])
            do = dy_ref[:, hc].astype(BF16)
            dmkv_ref[:, vc] += _dot_tn(w.astype(BF16), do)
            dw = _dot_nt(do, vv_s[:, hc])
            ds = (w * (dw - jnp.sum(w * dw, axis=-1, keepdims=True))).astype(BF16)
            dqn_s[:, hc] = _dot(ds, kn_s[:, hc])
            dkn_acc[:, hc] += _dot_tn(ds, qh)
        dgq = jnp.zeros((1, LANES), F32)
        for p in range(MQ_COLS // LANES):
            cols = slice(p * LANES, (p + 1) * LANES)
            dx, dgp = _head_norm_bwd(q_ref[:, cols], gq_ref[...], dqn_s[:, cols] * (HEAD_DIM ** -0.5), lo)
            dq_ref[:, cols] = dx
            dgq = dgq + jnp.sum(dgp, axis=0, keepdims=True)
        dgq_ref[...] += dgq

        @pl.when(t == nt - 1)
        def _():
            dgk = jnp.zeros((1, LANES), F32)
            for p in range(MQ_COLS // LANES):
                cols = slice(p * LANES, (p + 1) * LANES)
                dx, dgp = _head_norm_bwd(mkv_ref[:, cols], gk_ref[...], dkn_acc[:, cols], lo)
                dmkv_ref[:, cols] = dx
                dgk = dgk + jnp.sum(dgp, axis=0, keepdims=True)
            dgk_ref[...] = _fold_heads(dgk)
            dgq_ref[...] = _fold_heads(dgq_ref[...])

    vec = pl.BlockSpec((1, LANES), lambda t: (0, 0))
    vshape = jax.ShapeDtypeStruct((1, LANES), F32)
    return _pcall(
        body, comm=comm, name=name, grid=(nt,),
        in_specs=[
            pl.BlockSpec((tt, MQ_COLS), lambda t: (t, q_blk)),
            pl.BlockSpec((MEM_LEN, 2 * MQ_COLS), lambda t: (0, 0)),
            vec, vec,
            pl.BlockSpec((tt, MQ_COLS), lambda t: (t, y_blk)),
        ],
        out_specs=[pl.BlockSpec((tt, MQ_COLS), lambda t: (t, 0)),
                   pl.BlockSpec((MEM_LEN, 2 * MQ_COLS), lambda t: (0, 0)), vec, vec],
        out_shape=[jax.ShapeDtypeStruct((T, MQ_COLS), F32), jax.ShapeDtypeStruct((MEM_LEN, 2 * MQ_COLS), F32),
                   vshape, vshape],
        scratch_shapes=[pltpu.VMEM((MEM_LEN, MQ_COLS), BF16), pltpu.VMEM((MEM_LEN, MQ_COLS), BF16),
                        pltpu.VMEM((tt, MQ_COLS), BF16), pltpu.VMEM((tt, MQ_COLS), F32),
                        pltpu.VMEM((MEM_LEN, MQ_COLS), F32)],
        compiler_params=_params(("arbitrary",)),
    )(proj, mkv, gq2, gk2, dyc)


ADAMW_TILE_BYTES = 5 << 19


def _adamw(ws, gs, ms, vs, name, comm=None):
    n = len(ws)
    R, C = ws[0].shape
    budget = ADAMW_TILE_BYTES // (4 * C * n)
    tr = next((r for r in (512, 352, 256, 176, 128, 88, 64, 32, 16, 8) if R % r == 0 and r <= max(budget, 8)), R)

    def body(*refs):
        ins, outs = refs[:4 * n], refs[4 * n:]
        for i in range(n):
            w_ref, g_ref, m_ref, v_ref = ins[i], ins[n + i], ins[2 * n + i], ins[3 * n + i]
            go_ref, d_ref, nm_ref, nv_ref = outs[4 * i:4 * i + 4]
            gv = g_ref[...]
            go_ref[...] = gv
            nm = ADAM_B1 * m_ref[...] + (1.0 - ADAM_B1) * gv
            nv = ADAM_B2 * v_ref[...] + (1.0 - ADAM_B2) * (gv * gv)
            m_hat = nm / (1.0 - ADAM_B1 ** ADAM_STEP)
            v_hat = nv / (1.0 - ADAM_B2 ** ADAM_STEP)
            d_ref[...] = -ADAM_LR * (m_hat / (jnp.sqrt(v_hat) + ADAM_EPS) + ADAM_WD * w_ref[...])
            nm_ref[...] = nm
            nv_ref[...] = nv

    tile = pl.BlockSpec((tr, C), lambda i: (i, 0))
    shape = jax.ShapeDtypeStruct((R, C), F32)
    res = _pcall(
        body, comm=comm, name=name, grid=(R // tr,),
        in_specs=[tile] * (4 * n), out_specs=[tile] * (4 * n), out_shape=[shape] * (4 * n),
        compiler_params=_params(("arbitrary",)),
    )(*ws, *gs, *ms, *vs)
    return [tuple(res[4 * i:4 * i + 4]) for i in range(n)]


def _mesh_pos():
    return lax.axis_index("x"), lax.axis_index("y"), lax.axis_index("c")


def _other_chips(x, y):
    return [(1 - x, y), (x, 1 - y), (1 - x, 1 - y)]


def _quarter(ref, layout, q, rows, cols):
    if layout == "cols":
        return ref.at[rows, pl.ds(pl.multiple_of(q * cols, LANES), cols)]
    return ref.at[q, rows, :]


def _gather_weights(shards, dtypes, layouts, later, name, comm=None):
    n = len(shards)
    all_rows = slice(None)
    remote = [i for i in range(n) if i not in later]
    split = [i for i in remote if layouts[i] != "whole"]

    def body(*refs):
        ins, outs = refs[:n], refs[n:2 * n]
        st32, st16 = refs[2 * n:3 * n], refs[3 * n:4 * n]
        in_sems, own_sems, send_sems, recv_sems, fwd_send_sems, fwd_recv_sems = refs[4 * n:]
        x, y, c = _mesh_pos()
        chip = 2 * x + y
        sibling = (x, y, 1 - c)
        chips = _other_chips(x, y)

        def half(i, which):
            if i not in split:
                return all_rows
            hr = shards[i].shape[0] // 2
            return pl.ds(pl.multiple_of(which * hr, 16), hr)

        def place(i, q, rows):
            return _quarter(outs[i], layouts[i], q, rows, shards[i].shape[1])

        def ici(i, k, origin_chip, src):
            px, py = chips[k]
            return pltpu.make_async_remote_copy(
                src_ref=src, dst_ref=place(i, origin_chip, half(i, c)), send_sem=send_sems.at[i, k],
                recv_sem=recv_sems.at[i, k], device_id=(px, py, c), device_id_type=MESH)

        def forward(i, k, rows):
            px, py = chips[k]
            there = place(i, 2 * px + py, rows)
            return pltpu.make_async_remote_copy(
                src_ref=there, dst_ref=there, send_sem=fwd_send_sems.at[i, k],
                recv_sem=fwd_recv_sems.at[i, k], device_id=sibling, device_id_type=MESH)

        loads = [pltpu.make_async_copy(ins[i], st32[i], in_sems.at[i]) for i in range(n)]
        for cp in loads:
            cp.start()
        owns, sent = [], []
        for i in range(n):
            loads[i].wait()
            st16[i][...] = st32[i][...].astype(dtypes[i])
            own = pltpu.make_async_copy(st16[i], place(i, chip, all_rows), own_sems.at[i])
            own.start()
            owns.append(own)
            for k in range(3 if i in remote else 0):
                cp = ici(i, k, chip, st16[i].at[half(i, c)])
                cp.start()
                sent.append(cp)
        for i in remote:
            for k, (px, py) in enumerate(chips):
                ici(i, k, 2 * px + py, st16[i].at[half(i, c)]).wait_recv()
                if i in split:
                    cp = forward(i, k, half(i, c))
                    cp.start()
                    sent.append(cp)
        for i in split:
            for k in range(3):
                forward(i, k, half(i, 1 - c)).wait_recv()
        for cp in sent:
            cp.wait_send()
        for cp in owns:
            cp.wait()

    hbm = pl.BlockSpec(memory_space=pl.ANY)
    return _pcall(
        body, comm=comm, name=name,
        in_specs=[hbm] * n, out_specs=[hbm] * n,
        out_shape=[_gathered_shape(s.shape, d, lay) for s, d, lay in zip(shards, dtypes, layouts)],
        scratch_shapes=[pltpu.VMEM(s.shape, F32) for s in shards] + [pltpu.VMEM(s.shape, d) for s, d in zip(shards, dtypes)]
        + [pltpu.SemaphoreType.DMA((n,)), pltpu.SemaphoreType.DMA((n,)),
           pltpu.SemaphoreType.DMA((n, 3)), pltpu.SemaphoreType.DMA((n, 3)),
           pltpu.SemaphoreType.DMA((n, 3)), pltpu.SemaphoreType.DMA((n, 3))],
        compiler_params=pltpu.CompilerParams(vmem_limit_bytes=VMEM_LIMIT),
    )(*shards)


def _gathered_shape(quarter_shape, dtype, layout):
    R, C = quarter_shape
    return jax.ShapeDtypeStruct((R, N_CHIPS * C) if layout == "cols" else (N_CHIPS, R, C), dtype)


def _remote(src, dst, sems, j, device):
    return pltpu.make_async_remote_copy(src_ref=src, dst_ref=dst, send_sem=sems.at[2 * j], recv_sem=sems.at[2 * j + 1],
                                        device_id=device, device_id_type=MESH)


class _SemWindow:
    def __init__(self, sems, offset):
        self.sems, self.offset = sems, offset

    @property
    def at(self):
        return self

    def __getitem__(self, i):
        return self.sems.at[self.offset + i]


def _join_plans(a, b):
    n_ai, n_ao = len(a.ins), len(a.out_shapes)

    def plan(ins, outs, sems, finishing):
        first = a.plan(ins[:n_ai], outs[:n_ao], sems, finishing)
        second = b.plan(ins[n_ai:], outs[n_ao:], _SemWindow(sems, a.n_sems), finishing)
        return tuple(p + q for p, q in zip(first, second))

    aliases = {**a.aliases, **{n_ai + i: n_ao + o for i, o in b.aliases.items()}}
    return _Comm(a.ins + b.ins, a.out_shapes + b.out_shapes, a.n_sems + b.n_sems, plan, aliases)


def _half_rows(rows, which):
    hr = rows // 2
    return pl.ds(pl.multiple_of(which * hr, 16), hr)


ICI_PARTS = 1


def _row_parts(rows):
    tiles = rows // 16
    sizes = [16 * (tiles // ICI_PARTS + (p < tiles % ICI_PARTS)) for p in range(ICI_PARTS)]
    return [(sum(sizes[:p]), size) for p, size in enumerate(sizes) if size]


def _half_parts(rows, which):
    hr = rows // 2
    return [pl.ds(pl.multiple_of(which * hr + start, 16), size) for start, size in _row_parts(hr)]


def _spread_plan(fulls, quarter_shapes, layouts, peers=(0, 1, 2)):
    def plan(ins, outs, sems, finishing):
        x, y, c = _mesh_pos()
        chip = 2 * x + y
        sends, recvs = [], []
        for i, full in enumerate(outs):
            R, C = quarter_shapes[i]
            for k, (px, py) in enumerate(_other_chips(x, y)):
                if k not in peers:
                    continue
                for p, part in enumerate(_half_parts(R, c)):
                    j = ICI_PARTS * (3 * i + k) + p
                    mine = _quarter(full, layouts[i], chip, part, C)
                    sends.append(_remote(mine, mine, sems, j, (px, py, c)))
                    if finishing:
                        recvs.append(_remote(mine, _quarter(full, layouts[i], 2 * px + py, part, C), sems, j, (px, py, c)))
        return [], sends, recvs

    shapes = [jax.ShapeDtypeStruct(f.shape, f.dtype) for f in fulls]
    return _Comm(fulls, shapes, 2 * 3 * ICI_PARTS * len(fulls), plan, aliases={i: i for i in range(len(fulls))})


def _forward_plan(fulls, quarter_shapes, layouts):
    def plan(ins, outs, sems, finishing):
        x, y, c = _mesh_pos()
        sends, recvs = [], []
        for i, full in enumerate(outs):
            R, C = quarter_shapes[i]
            for k, (px, py) in enumerate(_other_chips(x, y)):
                mine = _quarter(full, layouts[i], 2 * px + py, _half_rows(R, c), C)
                sends.append(_remote(mine, mine, sems, 3 * i + k, (x, y, 1 - c)))
                if finishing:
                    theirs = _quarter(full, layouts[i], 2 * px + py, _half_rows(R, 1 - c), C)
                    recvs.append(_remote(mine, theirs, sems, 3 * i + k, (x, y, 1 - c)))
        return [], sends, recvs

    shapes = [jax.ShapeDtypeStruct(f.shape, f.dtype) for f in fulls]
    return _Comm(fulls, shapes, 6 * len(fulls), plan, aliases={i: i for i in range(len(fulls))})


def _swap_plan(grads):
    def plan(ins, outs, sems, finishing):
        x, y, c = _mesh_pos()
        sends = [_remote(g.at[:, _half_rows(g.shape[1], 1 - c), :], sib, sems, i, (x, y, 1 - c))
                 for i, (g, sib) in enumerate(zip(ins, outs))]
        return [], sends, sends

    shapes = [jax.ShapeDtypeStruct((N_CHIPS, g.shape[1] // 2, g.shape[2]), BF16) for g in grads]
    return _Comm(grads, shapes, 2 * len(grads), plan)


def _pair_sums(gs, sibs, name, comm=None):
    n = len(gs)

    def body(*refs):
        c = lax.axis_index("c")
        for g_ref, sib_ref, o_ref in zip(refs[:n], refs[n:2 * n], refs[2 * n:]):
            mine = _half_rows(g_ref.shape[1], c)
            o_ref[0] = (g_ref[0, mine, :].astype(F32) + sib_ref[0].astype(F32)).astype(BF16)

    def chunk(shape):
        return pl.BlockSpec((1,) + shape[1:], lambda q: (q, 0, 0))

    return _pcall(
        body, comm=comm, name=name, grid=(N_CHIPS,),
        in_specs=[chunk(g.shape) for g in gs] + [chunk(s.shape) for s in sibs],
        out_specs=[chunk(s.shape) for s in sibs],
        out_shape=[jax.ShapeDtypeStruct(s.shape, BF16) for s in sibs],
        compiler_params=_params(("arbitrary",)),
    )(*gs, *sibs)


def _ici_plan(sums):
    def plan(ins, outs, sems, finishing):
        x, y, c = _mesh_pos()
        sends = []
        for i, (s, rcv) in enumerate(zip(ins, outs)):
            for k, (px, py) in enumerate(_other_chips(x, y)):
                for p, (start, size) in enumerate(_row_parts(s.shape[1])):
                    rows = pl.ds(start, size)
                    sends.append(_remote(s.at[2 * px + py, rows, :], rcv.at[k, rows, :], sems,
                                         ICI_PARTS * (3 * i + k) + p, (px, py, c)))
        return [], sends, sends

    shapes = [jax.ShapeDtypeStruct((3,) + s.shape[1:], BF16) for s in sums]
    return _Comm(sums, shapes, 2 * 3 * ICI_PARTS * len(sums), plan)


def _run_comm(comm, name):
    def body():
        pass

    _, landed = _pcall(body, comm=comm, name=name, grid=(1,), in_specs=[], out_specs=[], out_shape=[])()
    return landed


def _finish_quarters(ss, rcvs, name, comm=None):
    n = len(ss)

    def body(*refs):
        s_refs, rcv_refs, out_refs, sems = refs[:n], refs[n:2 * n], refs[2 * n:3 * n], refs[3 * n]
        x, y, c = _mesh_pos()
        swaps = []
        for i, (s_ref, rcv_ref, out_ref) in enumerate(zip(s_refs, rcv_refs, out_refs)):
            mine = _half_rows(out_ref.shape[0], c)
            acc = s_ref[2 * x + y].astype(F32)
            for k in range(3):
                acc = acc + rcv_ref[k].astype(F32)
            out_ref[mine, :] = acc
            back = _remote(out_ref.at[mine, :], out_ref.at[mine, :], sems, i, (x, y, 1 - c))
            back.start()
            swaps.append(back)
        for back in swaps:
            back.wait()

    vmem = pl.BlockSpec(memory_space=pltpu.VMEM)
    return _pcall(
        body, comm=comm, name=name, grid=(1,),
        in_specs=[vmem] * (2 * n), out_specs=[vmem] * n,
        out_shape=[jax.ShapeDtypeStruct((2 * s.shape[1], s.shape[2]), F32) for s in ss],
        scratch_shapes=[pltpu.SemaphoreType.DMA((2 * n,))],
        compiler_params=pltpu.CompilerParams(vmem_limit_bytes=VMEM_LIMIT),
    )(*ss, *rcvs)


def _allreduce_small(v, name, comm=None):
    R, C = v.shape
    n_dev = 8

    def body(v_ref, out_ref, buf, send_sems, recv_sems):
        x, y, c = _mesh_pos()
        me = 4 * x + 2 * y + c
        buf[me] = v_ref[...]
        peers = []
        for k in range(1, n_dev):
            kx, ky, kc = (k >> 2) & 1, (k >> 1) & 1, k & 1
            px = 1 - x if kx else x
            py = 1 - y if ky else y
            pc = 1 - c if kc else c
            peers.append((px, py, pc))
        sends = []
        for k, peer in enumerate(peers):
            cp = pltpu.make_async_remote_copy(
                src_ref=v_ref, dst_ref=buf.at[me], send_sem=send_sems.at[k], recv_sem=recv_sems.at[k],
                device_id=peer, device_id_type=MESH)
            cp.start()
            sends.append(cp)
        for k, (px, py, pc) in enumerate(peers):
            pltpu.make_async_remote_copy(
                src_ref=v_ref, dst_ref=buf.at[4 * px + 2 * py + pc], send_sem=send_sems.at[k],
                recv_sem=recv_sems.at[k], device_id=(px, py, pc), device_id_type=MESH).wait_recv()
        for cp in sends:
            cp.wait_send()
        acc = buf[0]
        for i in range(1, n_dev):
            acc = acc + buf[i]
        out_ref[...] = acc

    vmem = pl.BlockSpec(memory_space=pltpu.VMEM)
    return _pcall(
        body, comm=comm, name=name, grid=(1,),
        in_specs=[vmem], out_specs=vmem,
        out_shape=jax.ShapeDtypeStruct((R, C), F32),
        scratch_shapes=[pltpu.VMEM((n_dev, R, C), F32),
                        pltpu.SemaphoreType.DMA((n_dev - 1,)), pltpu.SemaphoreType.DMA((n_dev - 1,))],
        compiler_params=pltpu.CompilerParams(vmem_limit_bytes=VMEM_LIMIT),
    )(v)


def _rope_tables(positions):
    half = ROPE_DIM // 2
    inv_freq = ROPE_THETA ** (-jnp.arange(half, dtype=F32) / half)
    ang = positions.astype(F32)[:, None] * inv_freq
    cos, sin = jnp.cos(ang), jnp.sin(ang)
    T = positions.shape[0]
    ones = jnp.ones((T, HEAD_DIM - ROPE_DIM), F32)
    c64 = jnp.concatenate([cos, cos, ones], axis=1)
    s64 = jnp.concatenate([-sin, sin, 0.0 * ones], axis=1)
    return jnp.tile(c64, (1, 2)), jnp.tile(s64, (1, 2))


def _local_step(x, mem, positions, target, small, first, own):
    cos, sin = _rope_tables(positions)
    two = lambda g: jnp.tile(g, (1, 2))
    gq2, gk2, gmq2, gmk2 = two(small["g_q"]), two(small["g_k"]), two(small["g_mq"]), two(small["g_mk"])
    mix_names = ["w_in", "w_mkv", "w_out"]
    mix_layouts = ["cols", "stack", "stack"]
    mix_quarters = [(D_MODEL, IN_CHUNK), (D_MODEL // N_CHIPS, 2 * MQ_COLS), (D_MODEL // N_CHIPS, D_MODEL)]
    ffn2_names = ["wg2", "wu2", "wd2"]
    ffn_quarter = (FF_CHUNK, D_MODEL)

    spread = _spread_plan([own["w_in"], own["w_mkv"], own["wg2"]], mix_quarters[:2] + [ffn_quarter], ["cols", "stack", "stack"])
    (x1, h1, a1, b1), (half_in, half_mkv, half_wg2) = _ffn_fwd(
        x, small["g_ffn1"], first["wg1"], first["wu1"], first["wd1"], "ffn1_fwd", comm=spread)
    hm, (w_in, w_mkv) = _rms_fwd(x1, small["g_mix"], "mix_norm",
                                 comm=_forward_plan([half_in, half_mkv], mix_quarters[:2], mix_layouts[:2]))
    w_mkv = w_mkv.reshape(D_MODEL, 2 * MQ_COLS)
    proj, (half_out,) = _mm_nn([hm], w_in, None, "in_proj",
                               comm=_spread_plan([own["w_out"]], mix_quarters[2:], mix_layouts[2:]))
    hmem = _rms_fwd(mem, small["g_mem"], "mem_norm")
    mkv = _mm_nn([hmem], w_mkv, None, "mem_proj")
    ya, (half_wu2,) = _attn_fwd(proj, cos, sin, gq2, gk2, small["sinks"], "swa_fwd",
                                comm=_spread_plan([own["wu2"]], [ffn_quarter], ["stack"]))
    (yc, cpre), (near_wd2, w_out) = _conv_fwd(
        proj, small["w_dw"], small["b_dw"], small["g_conv_ln"], small["b_conv_ln"], "conv_fwd",
        comm=_join_plans(_spread_plan([own["wd2"]], [ffn_quarter], ["stack"], peers=(0, 1)),
                         _forward_plan([half_out], mix_quarters[2:], mix_layouts[2:])))
    w_out = w_out.reshape(D_MODEL, D_MODEL)
    ym, (half_wd2,) = _mem_fwd(proj, mkv, gmq2, gmk2, "memattn_fwd",
                               comm=_spread_plan([near_wd2], [ffn_quarter], ["stack"], peers=(2,)))
    passing = _forward_plan([half_wg2, half_wu2, half_wd2], [ffn_quarter] * 3, ["stack"] * 3)
    x2, (wg2, wu2, wd2) = _mm_nn([ya, yc, ym], w_out, x1, "out_proj", comm=passing)
    dx3, h2, a2, b2, loss = _ffn_fwd(x2, small["g_ffn2"], wg2, wu2, wd2, "ffn2_fwd", target=target)

    dh2, dwg2, dwu2, dwd2 = _ffn_bwd(dx3, h2, a2, b2, wg2, wu2, wd2, "ffn2_bwd")
    (dx2, dg_ffn2), sibs = _rms_bwd(x2, small["g_ffn2"], dh2, dx3, "ffn2_norm_bwd", comm=_swap_plan([dwg2, dwu2, dwd2]))
    sums_ffn2 = _pair_sums([dwg2, dwu2, dwd2], sibs, "pair_sums_ffn2")
    dyc = _mm_nt([dx2], w_out, "out_proj_bwd")
    dw_out = _mm_tn([ya, yc, ym], [dx2], 1, "out_proj_wgrad").reshape(N_CHIPS, -1, D_MODEL)
    (dq, dk, dv, dgq, dgk, dsinks), rcv_wg2 = _attn_bwd(proj, cos, sin, gq2, gk2, small["sinks"], dyc, "swa_bwd",
                                                         comm=_ici_plan(sums_ffn2[:1]))
    (da, dgate, dw_dw, db_dw, dg_ln, db_ln), sibs = _conv_bwd(
        proj, cpre, small["w_dw"], small["g_conv_ln"], small["b_conv_ln"], dyc, "conv_bwd", comm=_swap_plan([dw_out]))
    dmq, dmkv, dgmq, dgmk = _mem_bwd(proj, mkv, gmq2, gmk2, dyc, "memattn_bwd")
    sums_out = _pair_sums([dw_out], sibs, "pair_sums_out")
    pieces = [dq, dk, dv, da, dgate, dmq]
    dhm, rcv_out = _mm_nt(pieces, w_in, "in_proj_bwd", comm=_ici_plan(sums_out))
    dw_in = _mm_tn([hm], pieces, N_CHIPS, "in_proj_wgrad")
    dhmem = _mm_nt([dmkv], w_mkv, "mem_proj_bwd")
    dw_mkv = _mm_tn([hmem], [dmkv], 1, "mem_proj_wgrad").reshape(N_CHIPS, -1, 2 * MQ_COLS)
    _, dg_mem = _rms_bwd(mem, small["g_mem"], dhmem, None, "mem_norm_bwd")
    (dx1, dg_mix), sibs = _rms_bwd(x1, small["g_mix"], dhm, dx2, "mix_norm_bwd", comm=_swap_plan([dw_in, dw_mkv]))
    sums_in = _pair_sums([dw_in, dw_mkv], sibs, "pair_sums_in")
    (dh1, dwg1, dwu1, dwd1), landed = _ffn_bwd(dx1, h1, a1, b1, first["wg1"], first["wu1"], first["wd1"], "ffn1_bwd",
                                                comm=_ici_plan([*sums_in, *sums_ffn2[1:]]))
    rcv_in, rcv_ffn2 = landed[:2], rcv_wg2 + landed[2:]
    (dx, dg_ffn1), sibs = _rms_bwd(x, small["g_ffn1"], dh1, dx1, "ffn1_norm_bwd", comm=_swap_plan([dwg1, dwu1, dwd1]))
    sums_ffn1 = _pair_sums([dwg1, dwu1, dwd1], sibs, "pair_sums_ffn1")
    g_ffn2, rcv_wg1 = _finish_quarters(sums_ffn2, rcv_ffn2, "finish_ffn2", comm=_ici_plan(sums_ffn1[:1]))
    g_mix, rcv_wu1 = _finish_quarters(sums_in + sums_out, rcv_in + rcv_out, "finish_mix", comm=_ici_plan(sums_ffn1[1:2]))
    small_grads = dict(
        g_ffn1=dg_ffn1, g_mix=dg_mix, g_q=dgq[:, :HEAD_DIM], g_k=dgk[:, :HEAD_DIM], sinks=dsinks[:, :N_Q_HEADS],
        w_dw=dw_dw[:CONV_WIDTH], b_dw=db_dw, g_conv_ln=dg_ln, b_conv_ln=db_ln, g_mem=dg_mem,
        g_mq=dgmq[:, :HEAD_DIM], g_mk=dgmk[:, :HEAD_DIM], g_ffn2=dg_ffn2, loss=loss[:, :1])
    names = list(small_grads)
    packed, offs = _pack([small_grads[n] for n in names])
    total, rcv_wd1 = _allreduce_small(packed, "allreduce_small", comm=_ici_plan(sums_ffn1[2:]))
    summed = dict(zip(names, _unpack(total, offs, [small_grads[n].shape for n in names])))
    g_ffn1 = _finish_quarters(sums_ffn1, rcv_wg1 + rcv_wu1 + rcv_wd1, "finish_ffn1")
    big_grads = dict(zip(["wg1", "wu1", "wd1", "w_in", "w_mkv", "w_out"] + ffn2_names, [*g_ffn1, *g_mix, *g_ffn2]))
    return dx, big_grads, summed


SMALL_NAMES = ["g_ffn1", "g_mix", "g_q", "g_k", "sinks", "b_dw", "g_conv_ln", "b_conv_ln", "g_mem", "g_mq", "g_mk",
               "g_ffn2"]
PACK_COLS = 1024


def _pack(parts):
    flat = [p.reshape(-1) for p in parts]
    offs, o = [], 0
    for f in flat:
        offs.append(o)
        o += f.shape[0]
    rows = -(-o // (8 * PACK_COLS)) * 8
    pad = jnp.zeros((rows * PACK_COLS - o,), F32)
    return jnp.concatenate(flat + [pad]).reshape(rows, PACK_COLS), offs


def _unpack(packed, offs, shapes):
    flat = packed.reshape(-1)
    return [flat[o:o + math.prod(s)].reshape(s) for o, s in zip(offs, shapes)]


def kernel(x, mem, positions, g_ffn1, w_ffn1_gate, w_ffn1_up, w_ffn1_down, g_mix, w_in, g_q, g_k, sinks, w_dw, b_dw, g_conv_ln, b_conv_ln, g_mem, w_mem_kv, g_mq, g_mk, w_out, g_ffn2, w_ffn2_gate, w_ffn2_up, w_ffn2_down, loss_target, m_g_ffn1, m_w_ffn1_gate, m_w_ffn1_up, m_w_ffn1_down, m_g_mix, m_w_in, m_g_q, m_g_k, m_sinks, m_w_dw, m_b_dw, m_g_conv_ln, m_b_conv_ln, m_g_mem, m_w_mem_kv, m_g_mq, m_g_mk, m_w_out, m_g_ffn2, m_w_ffn2_gate, m_w_ffn2_up, m_w_ffn2_down, v_g_ffn1, v_w_ffn1_gate, v_w_ffn1_up, v_w_ffn1_down, v_g_mix, v_w_in, v_g_q, v_g_k, v_sinks, v_w_dw, v_b_dw, v_g_conv_ln, v_b_conv_ln, v_g_mem, v_w_mem_kv, v_g_mq, v_g_mk, v_w_out, v_g_ffn2, v_w_ffn2_gate, v_w_ffn2_up, v_w_ffn2_down):
    args = dict(locals())
    weight_names = ["g_ffn1", "w_ffn1_gate", "w_ffn1_up", "w_ffn1_down", "g_mix", "w_in", "g_q", "g_k", "sinks",
                    "w_dw", "b_dw", "g_conv_ln", "b_conv_ln", "g_mem", "w_mem_kv", "g_mq", "g_mk", "w_out", "g_ffn2",
                    "w_ffn2_gate", "w_ffn2_up", "w_ffn2_down"]
    big_names = ["w_ffn1_gate", "w_ffn1_up", "w_ffn1_down", "w_in", "w_mem_kv", "w_out",
                 "w_ffn2_gate", "w_ffn2_up", "w_ffn2_down"]
    short = dict(w_ffn1_gate="wg1", w_ffn1_up="wu1", w_ffn1_down="wd1", w_in="w_in", w_mem_kv="w_mkv",
                 w_out="w_out", w_ffn2_gate="wg2", w_ffn2_up="wu2", w_ffn2_down="wd2")

    transposed = ("w_ffn1_gate", "w_ffn1_up", "w_ffn2_gate", "w_ffn2_up")

    def quarter(a, n):
        return jnp.swapaxes(a, 1, 2)[0] if n in transposed else a[0]

    def unquarter(a, n):
        return jnp.swapaxes(a[None], 1, 2) if n in transposed else a[None]

    shards = [quarter(args[n], n) for n in big_names]
    layouts = ["cols" if n == "w_in" else "stack" for n in big_names]
    later = [i for i, n in enumerate(big_names) if short[n] not in ("wg1", "wu1", "wd1")]
    gathered = _gather_weights(shards + [w_dw[0]], [BF16] * len(shards) + [F32], layouts + ["whole"], later,
                               "gather_first")
    first = {short[n]: gathered[i] for i, n in enumerate(big_names) if i not in later}
    own = {short[n]: gathered[i] for i, n in enumerate(big_names) if i in later}
    small = {n: args[n] for n in SMALL_NAMES}
    small["w_dw"] = jnp.transpose(gathered[-1], (1, 0, 2)).reshape(CONV_WIDTH, CONV_CH)

    dx, big_grads, summed = _local_step(x[0], mem[0], positions[0], loss_target[0], small, first, own)
    chip = 2 * lax.axis_index("x") + lax.axis_index("y")
    dw_dw_full = summed.pop("w_dw")
    loss_out = summed.pop("loss").reshape(())

    grads = {n: summed[n] for n in SMALL_NAMES}
    grads["w_dw"] = lax.dynamic_slice_in_dim(dw_dw_full, chip * (CONV_CH // N_CHIPS), CONV_CH // N_CHIPS, axis=1)
    for n in big_names:
        grads[n] = big_grads[short[n]]

    delta, new_m, new_v = {}, {}, {}
    ffn_names = [n for n in big_names if "ffn" in n]
    for group, label in [(ffn_names, "adamw_ffn")] + [([n], "adamw_" + short[n]) for n in big_names if "ffn" not in n]:
        results = _adamw([quarter(args[n], n) for n in group], [grads[n] for n in group],
                         [quarter(args["m_" + n], n) for n in group], [quarter(args["v_" + n], n) for n in group], label)
        for n, (g, d, nm, nv) in zip(group, results):
            grads[n], delta[n], new_m[n], new_v[n] = (unquarter(a, n) for a in (g, d, nm, nv))
    tiny = SMALL_NAMES + ["w_dw"]
    pw, poffs = _pack([args[n] for n in tiny])
    pg, _ = _pack([grads[n] for n in tiny])
    pm, _ = _pack([args["m_" + n] for n in tiny])
    pv, _ = _pack([args["v_" + n] for n in tiny])
    ((_, pd, pnm, pnv),) = _adamw([pw], [pg], [pm], [pv], "adamw_small")
    tshapes = [args[n].shape for n in tiny]
    for store, packed_out in ((delta, pd), (new_m, pnm), (new_v, pnv)):
        for n, val in zip(tiny, _unpack(packed_out, poffs, tshapes)):
            store[n] = val

    def shaped(n, v):
        return v.reshape(args[n].shape)

    return (loss_out, dx[None],
            *[shaped(n, grads[n]) for n in weight_names],
            *[shaped(n, delta[n]) for n in weight_names],
            *[shaped(n, new_m[n]) for n in weight_names],
            *[shaped(n, new_v[n]) for n in weight_names])
```

```python
import functools
import math

import jax
import jax.numpy as jnp
from jax import lax
from jax.experimental import pallas as pl
from jax.experimental.pallas import tpu as pltpu

F32 = jnp.float32
BF16 = jnp.bfloat16

D_MODEL = 1024
SEQ = 2048
MEM_LEN = 256
HEAD_DIM = 64
N_Q_HEADS = 8
N_KV_HEADS = 2
Q_PER_KV = 4
N_MEM_HEADS = 4
BLOCK = 128
CONV_CH = 256
CONV_WIDTH = 31
ROPE_THETA = 500000.0
ROPE_DIM = 16
D_FF = 2816
EPS = 1e-6
Q_COLS = 512
KV_COLS = 128
MQ_COLS = 256
IN_COLS = 1536

N_CHIPS = 4
FF_CHUNK = D_FF // N_CHIPS
IN_CHUNK = IN_COLS // N_CHIPS

ADAM_LR = 0.001
ADAM_B1 = 0.9
ADAM_B2 = 0.999
ADAM_EPS = 1e-08
ADAM_WD = 0.01
ADAM_STEP = 10

LANES = 128
VMEM_LIMIT = 56 * 1024 * 1024
ROW_TILE = 512
MESH = pl.DeviceIdType.MESH
NEG = -1e30


class _Comm:
    def __init__(self, ins, out_shapes, n_sems, plan, aliases=None):
        self.ins, self.out_shapes, self.n_sems, self.plan = list(ins), list(out_shapes), n_sems, plan
        self.aliases = aliases or {}


def _pcall(body, comm=None, **kw):
    if comm is None:
        return pl.pallas_call(body, **kw)
    grid = kw["grid"]
    in_specs = list(kw["in_specs"])
    single = not isinstance(kw["out_shape"], (list, tuple))
    out_specs = [kw["out_specs"]] if single else list(kw["out_specs"])
    out_shape = [kw["out_shape"]] if single else list(kw["out_shape"])
    scratch = list(kw.get("scratch_shapes", ()))
    n_in, n_out, n_scr = len(in_specs), len(out_shape), len(scratch)
    n_ci, n_co = len(comm.ins), len(comm.out_shapes)

    def wrapped(*refs):
        o = 0
        parts = []
        for cnt in (n_in, n_ci, n_out, n_co, n_scr):
            parts.append(refs[o:o + cnt])
            o += cnt
        ins, c_ins, outs, c_outs, scr = parts
        sems = refs[o]
        first = last = None
        for d, size in enumerate(grid):
            at0, at_end = pl.program_id(d) == 0, pl.program_id(d) == size - 1
            first = at0 if first is None else first & at0
            last = at_end if last is None else last & at_end

        @pl.when(first)
        def _():
            local, sends, _ = comm.plan(c_ins, c_outs, sems, False)
            for cp in sends + local:
                cp.start()

        body(*ins, *outs, *scr)

        @pl.when(last)
        def _():
            local, sends, recvs = comm.plan(c_ins, c_outs, sems, True)
            for cp in recvs:
                cp.wait_recv()
            for cp in sends:
                cp.wait_send()
            for cp in local:
                cp.wait()

    hbm = pl.BlockSpec(memory_space=pl.ANY)
    kw = dict(kw, in_specs=in_specs + [hbm] * n_ci, out_specs=out_specs + [hbm] * n_co,
              out_shape=out_shape + comm.out_shapes,
              scratch_shapes=scratch + [pltpu.SemaphoreType.DMA((comm.n_sems,))])
    if comm.aliases:
        kw["input_output_aliases"] = {n_in + i: n_out + o for i, o in comm.aliases.items()}
    call = pl.pallas_call(wrapped, **kw)

    def run(*args):
        res = call(*args, *comm.ins)
        return (res[0] if single else list(res[:n_out])), list(res[n_out:])

    return run


def _params(sem=None):
    return pltpu.CompilerParams(dimension_semantics=sem, vmem_limit_bytes=VMEM_LIMIT)


def _dot(a, b):
    return jnp.dot(a, b, preferred_element_type=F32)


def _dot_nt(a, b):
    return lax.dot_general(a, b, (((1,), (1,)), ((), ())), preferred_element_type=F32)


def _dot_tn(a, b):
    return lax.dot_general(a, b, (((0,), (0,)), ((), ())), preferred_element_type=F32)


def _sigmoid(x):
    return 1.0 / (1.0 + jnp.exp(-x))


def _full(shape):
    n = len(shape)
    return pl.BlockSpec(shape, lambda *_: (0,) * n)


def _ffn_fwd(x, g, wg, wu, wd, name, comm=None, target=None, rows=ROW_TILE):
    T, D = x.shape
    tt = rows
    nt = T // tt
    with_loss = target is not None

    def body(*refs):
        if with_loss:
            x_ref, g_ref, wg_ref, wu_ref, wd_ref, t_ref, xo_ref, h_ref, a_ref, b_ref, loss_ref = refs
        else:
            x_ref, g_ref, wg_ref, wu_ref, wd_ref, xo_ref, h_ref, a_ref, b_ref = refs
        t = pl.program_id(0)
        j = pl.program_id(1)

        @pl.when(j == 0)
        def _():
            xv = x_ref[...]
            rstd = lax.rsqrt(jnp.mean(xv * xv, axis=-1, keepdims=True) + EPS)
            h_ref[...] = (xv * rstd * g_ref[...]).astype(BF16)
            xo_ref[...] = jnp.zeros_like(xo_ref)

        h = h_ref[...]
        a = _dot_nt(h, wg_ref[0])
        b = _dot_nt(h, wu_ref[0])
        a_ref[0] = a.astype(BF16)
        b_ref[0] = b.astype(BF16)
        s = (a * _sigmoid(a)) * b
        xo_ref[...] += _dot(s.astype(BF16), wd_ref[0])

        @pl.when(j == N_CHIPS - 1)
        def _():
            out = x_ref[...] + 0.5 * xo_ref[...]
            if with_loss:
                err = out - t_ref[...]
                xo_ref[...] = err * (1.0 / D)
                part = 0.5 * jnp.sum(jnp.mean(err * err, axis=-1, keepdims=True), axis=0, keepdims=True)

                @pl.when(t == 0)
                def _():
                    loss_ref[...] = jnp.zeros_like(loss_ref)

                loss_ref[...] += jnp.broadcast_to(part, loss_ref.shape)
            else:
                xo_ref[...] = out

    tile = pl.BlockSpec((tt, D), lambda t, j: (t, 0))
    chunk = pl.BlockSpec((1, FF_CHUNK, D), lambda t, j: (j, 0, 0))
    act = pl.BlockSpec((1, tt, FF_CHUNK), lambda t, j: (j, t, 0))
    act_shape = jax.ShapeDtypeStruct((N_CHIPS, T, FF_CHUNK), BF16)
    loss_spec = [pl.BlockSpec((1, LANES), lambda t, j: (0, 0))] if with_loss else []
    loss_shape = [jax.ShapeDtypeStruct((1, LANES), F32)] if with_loss else []
    return _pcall(
        body, comm=comm, name=name, grid=(nt, N_CHIPS),
        in_specs=[tile, pl.BlockSpec((1, D), lambda t, j: (0, 0)), chunk, chunk, chunk] + ([tile] if with_loss else []),
        out_specs=[tile, tile, act, act] + loss_spec,
        out_shape=[jax.ShapeDtypeStruct((T, D), F32), jax.ShapeDtypeStruct((T, D), BF16), act_shape, act_shape]
        + loss_shape,
        compiler_params=_params(("arbitrary", "arbitrary")),
    )(x, g, wg, wu, wd, *([target] if with_loss else []))


def _ffn_bwd(dxo, h, a, b, wg, wu, wd, name, comm=None):
    T, D = dxo.shape
    tt = ROW_TILE
    nt = T // tt

    def body(dxo_ref, h_ref, a_ref, b_ref, wg_ref, wu_ref, wd_ref,
             dh_hbm, dwg_ref, dwu_ref, dwd_ref, dh_acc, acc_g, acc_u, acc_d):
        j = pl.program_id(0)
        t = pl.program_id(1)
        do = (0.5 * dxo_ref[...]).astype(BF16)
        av = a_ref[0].astype(F32)
        bv = b_ref[0].astype(F32)
        sig = _sigmoid(av)
        sa = av * sig
        ds = _dot_nt(do, wd_ref[0])
        da = (ds * bv * (sig * (1.0 + av * (1.0 - sig)))).astype(BF16)
        db = (ds * sa).astype(BF16)
        hv = h_ref[...]
        rows = pl.ds(pl.multiple_of(t * tt, tt), tt)

        @pl.when(j == 0)
        def _():
            dh_acc[rows, :] = jnp.zeros((tt, D), F32)

        @pl.when(t == 0)
        def _():
            acc_g[...] = jnp.zeros_like(acc_g)
            acc_u[...] = jnp.zeros_like(acc_u)
            acc_d[...] = jnp.zeros_like(acc_d)

        acc_d[...] += _dot_tn((sa * bv).astype(BF16), do)
        acc_g[...] += _dot_tn(da, hv)
        acc_u[...] += _dot_tn(db, hv)
        dh_acc[rows, :] += _dot(da, wg_ref[0]) + _dot(db, wu_ref[0])

        @pl.when(t == nt - 1)
        def _():
            dwg_ref[0] = acc_g[...].astype(BF16)
            dwu_ref[0] = acc_u[...].astype(BF16)
            dwd_ref[0] = acc_d[...].astype(BF16)

        @pl.when((t == nt - 1) & (j == N_CHIPS - 1))
        def _():
            pltpu.sync_copy(dh_acc, dh_hbm)

    return _pcall(
        body, comm=comm, name=name, grid=(N_CHIPS, nt),
        in_specs=[
            pl.BlockSpec((tt, D), lambda j, t: (t, 0)),
            pl.BlockSpec((tt, D), lambda j, t: (t, 0)),
            pl.BlockSpec((1, tt, FF_CHUNK), lambda j, t: (j, t, 0)),
            pl.BlockSpec((1, tt, FF_CHUNK), lambda j, t: (j, t, 0)),
            pl.BlockSpec((1, FF_CHUNK, D), lambda j, t: (j, 0, 0)),
            pl.BlockSpec((1, FF_CHUNK, D), lambda j, t: (j, 0, 0)),
            pl.BlockSpec((1, FF_CHUNK, D), lambda j, t: (j, 0, 0)),
        ],
        out_specs=[
            pl.BlockSpec(memory_space=pl.ANY),
            pl.BlockSpec((1, FF_CHUNK, D), lambda j, t: (j, 0, 0)),
            pl.BlockSpec((1, FF_CHUNK, D), lambda j, t: (j, 0, 0)),
            pl.BlockSpec((1, FF_CHUNK, D), lambda j, t: (j, 0, 0)),
        ],
        out_shape=[
            jax.ShapeDtypeStruct((T, D), F32),
            jax.ShapeDtypeStruct((N_CHIPS, FF_CHUNK, D), BF16),
            jax.ShapeDtypeStruct((N_CHIPS, FF_CHUNK, D), BF16),
            jax.ShapeDtypeStruct((N_CHIPS, FF_CHUNK, D), BF16),
        ],
        scratch_shapes=[
            pltpu.VMEM((T, D), F32),
            pltpu.VMEM((FF_CHUNK, D), F32),
            pltpu.VMEM((FF_CHUNK, D), F32),
            pltpu.VMEM((FF_CHUNK, D), F32),
        ],
        compiler_params=_params(("arbitrary", "arbitrary")),
    )(dxo, h, a, b, wg, wu, wd)


def _rms_fwd(x, g, name, comm=None):
    T, D = x.shape
    tt = min(ROW_TILE, T)

    def body(x_ref, g_ref, h_ref):
        xv = x_ref[...]
        rstd = lax.rsqrt(jnp.mean(xv * xv, axis=-1, keepdims=True) + EPS)
        h_ref[...] = (xv * rstd * g_ref[...]).astype(BF16)

    return _pcall(
        body, comm=comm, name=name, grid=(T // tt,),
        in_specs=[pl.BlockSpec((tt, D), lambda t: (t, 0)), pl.BlockSpec((1, D), lambda t: (0, 0))],
        out_specs=pl.BlockSpec((tt, D), lambda t: (t, 0)),
        out_shape=jax.ShapeDtypeStruct((T, D), BF16),
        compiler_params=_params(("arbitrary",)),
    )(x, g)


def _rms_bwd(x, g, dh, dres, name, comm=None):
    T, D = x.shape
    tt = min(ROW_TILE, T)
    has_res = dres is not None

    def body(*refs):
        if has_res:
            x_ref, g_ref, dh_ref, dres_ref, dx_ref, dg_ref = refs
        else:
            x_ref, g_ref, dh_ref, dx_ref, dg_ref = refs
        t = pl.program_id(0)
        xv = x_ref[...]
        rstd = lax.rsqrt(jnp.mean(xv * xv, axis=-1, keepdims=True) + EPS)
        xhat = xv * rstd
        dhv = dh_ref[...]
        gy = dhv * g_ref[...]
        dx = rstd * (gy - xhat * jnp.mean(gy * xhat, axis=-1, keepdims=True))
        if has_res:
            dx = dx + dres_ref[...]
        dx_ref[...] = dx
        part = jnp.sum(dhv * xhat, axis=0, keepdims=True)

        @pl.when(t == 0)
        def _():
            dg_ref[...] = part

        @pl.when(t > 0)
        def _():
            dg_ref[...] += part

    tile = pl.BlockSpec((tt, D), lambda t: (t, 0))
    vec = pl.BlockSpec((1, D), lambda t: (0, 0))
    args = [x, g, dh] + ([dres] if has_res else [])
    return _pcall(
        body, comm=comm, name=name, grid=(T // tt,),
        in_specs=[tile, vec, tile] + ([tile] if has_res else []),
        out_specs=[tile, vec],
        out_shape=[jax.ShapeDtypeStruct((T, D), F32), jax.ShapeDtypeStruct((1, D), F32)],
        compiler_params=_params(("arbitrary",)),
    )(*args)


def _mm_nn(a_list, b, res, name, comm=None):
    T = a_list[0].shape[0]
    K, N = b.shape
    tt = min(ROW_TILE, T)
    ks = [a.shape[1] for a in a_list]
    na = len(a_list)
    has_res = res is not None

    def body(*refs):
        a_refs = refs[:na]
        b_ref = refs[na]
        o_ref = refs[-1]
        acc = res_v = None
        off = 0
        for a_ref, k in zip(a_refs, ks):
            part = _dot(a_ref[...].astype(BF16), b_ref[off:off + k, :])
            acc = part if acc is None else acc + part
            off += k
        if has_res:
            acc = refs[na + 1][...] + acc
        o_ref[...] = acc

    in_specs = [pl.BlockSpec((tt, k), lambda t: (t, 0)) for k in ks] + [pl.BlockSpec((K, N), lambda t: (0, 0))]
    args = list(a_list) + [b]
    if has_res:
        in_specs.append(pl.BlockSpec((tt, N), lambda t: (t, 0)))
        args.append(res)
    return _pcall(
        body, comm=comm, name=name, grid=(T // tt,), in_specs=in_specs,
        out_specs=pl.BlockSpec((tt, N), lambda t: (t, 0)),
        out_shape=jax.ShapeDtypeStruct((T, N), F32),
        compiler_params=_params(("arbitrary",)),
    )(*args)


def _mm_nt(a_list, b, name, comm=None):
    T = a_list[0].shape[0]
    K, N = b.shape
    tt = min(ROW_TILE, T)
    ns = [a.shape[1] for a in a_list]
    na = len(a_list)

    def body(*refs):
        b_ref = refs[na]
        o_ref = refs[-1]
        acc = None
        off = 0
        for a_ref, n in zip(refs[:na], ns):
            part = _dot_nt(a_ref[...].astype(BF16), b_ref[:, off:off + n])
            acc = part if acc is None else acc + part
            off += n
        o_ref[...] = acc

    return _pcall(
        body, comm=comm, name=name, grid=(T // tt,),
        in_specs=[pl.BlockSpec((tt, n), lambda t: (t, 0)) for n in ns] + [pl.BlockSpec((K, N), lambda t: (0, 0))],
        out_specs=pl.BlockSpec((tt, K), lambda t: (t, 0)),
        out_shape=jax.ShapeDtypeStruct((T, K), F32),
        compiler_params=_params(("arbitrary",)),
    )(*a_list, b)


def _mm_tn(a_list, b_list, col_chunks, name, comm=None):
    T = a_list[0].shape[0]
    tt = min(ROW_TILE, T)
    nt = T // tt
    ms = [a.shape[1] for a in a_list]
    ns = [b.shape[1] for b in b_list]
    M, N = sum(ms), sum(ns)
    na, nb = len(a_list), len(b_list)
    cw = N // col_chunks

    def body(*refs):
        a_refs, b_refs = refs[:na], refs[na:na + nb]
        o_ref, acc = refs[na + nb], refs[na + nb + 1]
        t = pl.program_id(0)

        @pl.when(t == 0)
        def _():
            acc[...] = jnp.zeros_like(acc)

        ro = 0
        for a_ref, m in zip(a_refs, ms):
            av = a_ref[...].astype(BF16)
            co = 0
            for b_ref, n in zip(b_refs, ns):
                acc[ro:ro + m, co:co + n] += _dot_tn(av, b_ref[...].astype(BF16))
                co += n
            ro += m

        @pl.when(t == nt - 1)
        def _():
            if col_chunks == 1:
                o_ref[...] = acc[...].astype(BF16)
            else:
                for q in range(col_chunks):
                    o_ref[q] = acc[:, q * cw:(q + 1) * cw].astype(BF16)

    out_shape = (M, N) if col_chunks == 1 else (col_chunks, M, cw)
    return _pcall(
        body, comm=comm, name=name, grid=(nt,),
        in_specs=[pl.BlockSpec((tt, m), lambda t: (t, 0)) for m in ms]
        + [pl.BlockSpec((tt, n), lambda t: (t, 0)) for n in ns],
        out_specs=_full(out_shape),
        out_shape=jax.ShapeDtypeStruct(out_shape, BF16),
        scratch_shapes=[pltpu.VMEM((M, N), F32)],
        compiler_params=_params(("arbitrary",)),
    )(*a_list, *b_list)


def _head_masks():
    lane = lax.broadcasted_iota(jnp.int32, (1, LANES), 1)
    l64 = lane & (HEAD_DIM - 1)
    return lane < HEAD_DIM, l64 < ROPE_DIM // 2, l64 < ROPE_DIM


def _head_mean(v, lo):
    s_lo = jnp.sum(jnp.where(lo, v, 0.0), axis=-1, keepdims=True)
    s_hi = jnp.sum(jnp.where(lo, 0.0, v), axis=-1, keepdims=True)
    return jnp.where(lo, s_lo, s_hi) * (1.0 / HEAD_DIM)


def _rope_swap(v, first, rot):
    up = pltpu.roll(v, LANES - ROPE_DIM // 2, 1)
    down = pltpu.roll(v, ROPE_DIM // 2, 1)
    return jnp.where(first, up, jnp.where(rot, down, 0.0))


def _head_norm(x, g, lo):
    rstd = lax.rsqrt(_head_mean(x * x, lo) + EPS)
    return x * rstd * g


def _head_norm_bwd(x, g, dy, lo):
    rstd = lax.rsqrt(_head_mean(x * x, lo) + EPS)
    xhat = x * rstd
    gy = dy * g
    dx = rstd * (gy - xhat * _head_mean(gy * xhat, lo))
    return dx, dy * xhat


def _rope(xn, cos, sin, first, rot):
    return xn * cos + _rope_swap(xn, first, rot) * sin


def _rope_bwd(dy, cos, sin, first, rot):
    return dy * cos + _rope_swap(dy * sin, first, rot)


def _fold_heads(v):
    return v + pltpu.roll(v, HEAD_DIM, 1)


ATT_ROWS = 256


def _attn_prepare(q_ref, k_ref, v_ref, cos_ref, sin_ref, gq_ref, gk_ref, qs, ks, vs):
    T = q_ref.shape[0]
    lo, first, rot = _head_masks()
    ks[0:BLOCK, :] = jnp.zeros((BLOCK, KV_COLS), BF16)
    vs[0:BLOCK, :] = jnp.zeros((BLOCK, KV_COLS), BF16)

    def step(i, _):
        r0 = pl.multiple_of(i * ATT_ROWS, ATT_ROWS)
        rows = pl.ds(r0, ATT_ROWS)
        prow = pl.ds(r0 + BLOCK, ATT_ROWS)
        cos, sin = cos_ref[rows, :], sin_ref[rows, :]
        for p in range(Q_COLS // LANES):
            cols = slice(p * LANES, (p + 1) * LANES)
            xr = _rope(_head_norm(q_ref[rows, cols], gq_ref[...], lo), cos, sin, first, rot)
            qs[rows, cols] = (xr * (HEAD_DIM ** -0.5)).astype(BF16)
        kr = _rope(_head_norm(k_ref[rows, :], gk_ref[...], lo), cos, sin, first, rot)
        ks[prow, :] = kr.astype(BF16)
        vs[prow, :] = v_ref[rows, :].astype(BF16)
        return 0

    lax.fori_loop(0, T // ATT_ROWS, step, 0)


GROUP_ROWS = Q_PER_KV * BLOCK


def _group_rows(ref, r0, g, cast=None):
    parts = []
    for r in range(Q_PER_KV):
        h = g * Q_PER_KV + r
        part = ref[pl.ds(r0, BLOCK), h * HEAD_DIM:(h + 1) * HEAD_DIM]
        parts.append(part if cast is None else part.astype(cast))
    return jnp.concatenate(parts, axis=0)


def _group_sinks(sink_ref, g):
    row = lax.broadcasted_iota(jnp.int32, (GROUP_ROWS, 1), 0)
    col = jnp.full((GROUP_ROWS, 1), sink_ref[0, g * Q_PER_KV], F32)
    for r in range(1, Q_PER_KV):
        col = jnp.where(row >= r * BLOCK, sink_ref[0, g * Q_PER_KV + r], col)
    return col


def _attn_scores(qg, kw, blk, sink):
    s = _dot_nt(qg, kw)
    qi = (lax.broadcasted_iota(jnp.int32, (GROUP_ROWS, 2 * BLOCK), 0) & (BLOCK - 1)) + BLOCK
    ki = lax.broadcasted_iota(jnp.int32, (GROUP_ROWS, 2 * BLOCK), 1)
    rel = qi - ki
    valid = (rel >= 0) & (rel < BLOCK) & ((blk > 0) | (ki >= BLOCK))
    s = jnp.where(valid, s, NEG)
    m = jnp.maximum(jnp.max(s, axis=-1, keepdims=True), sink)
    p = jnp.exp(s - m)
    e_sink = jnp.exp(sink - m)
    inv = 1.0 / (jnp.sum(p, axis=-1, keepdims=True) + e_sink)
    return p * inv, e_sink * inv


def _attn_fwd(proj, cos, sin, gq2, gk2, sinks, name, comm=None):
    T = proj.shape[0]
    nb = T // BLOCK

    def body(q_ref, k_ref, v_ref, cos_ref, sin_ref, gq_ref, gk_ref, sink_ref, y_ref, qs, ks, vs):
        _attn_prepare(q_ref, k_ref, v_ref, cos_ref, sin_ref, gq_ref, gk_ref, qs, ks, vs)

        def blk_step(blk, _):
            r0 = pl.multiple_of(blk * BLOCK, BLOCK)
            for g in range(N_KV_HEADS):
                gc = slice(g * HEAD_DIM, (g + 1) * HEAD_DIM)
                kw = ks[pl.ds(r0, 2 * BLOCK), gc]
                vw = vs[pl.ds(r0, 2 * BLOCK), gc]
                w, _ws = _attn_scores(_group_rows(qs, r0, g), kw, blk, _group_sinks(sink_ref, g))
                o = _dot(w.astype(BF16), vw).astype(BF16)
                for r in range(Q_PER_KV):
                    h = g * Q_PER_KV + r
                    y_ref[pl.ds(r0, BLOCK), h * HEAD_DIM:(h + 1) * HEAD_DIM] = o[r * BLOCK:(r + 1) * BLOCK, :]
            return 0

        lax.fori_loop(0, nb, blk_step, 0)

    return _pcall(
        body, comm=comm, name=name, grid=(1,),
        in_specs=[
            pl.BlockSpec((T, Q_COLS), lambda i: (0, 0)),
            pl.BlockSpec((T, KV_COLS), lambda i: (0, Q_COLS // KV_COLS)),
            pl.BlockSpec((T, KV_COLS), lambda i: (0, Q_COLS // KV_COLS + 1)),
            _full((T, LANES)), _full((T, LANES)), _full((1, LANES)), _full((1, LANES)),
            pl.BlockSpec(memory_space=pltpu.SMEM),
        ],
        out_specs=_full((T, Q_COLS)),
        out_shape=jax.ShapeDtypeStruct((T, Q_COLS), BF16),
        scratch_shapes=[
            pltpu.VMEM((T, Q_COLS), BF16),
            pltpu.VMEM((T + BLOCK, KV_COLS), BF16),
            pltpu.VMEM((T + BLOCK, KV_COLS), BF16),
        ],
        compiler_params=_params(("arbitrary",)),
    )(proj, proj, proj, cos, sin, gq2, gk2, sinks)


def _attn_bwd(proj, cos, sin, gq2, gk2, sinks, dyc, name, comm=None):
    T = proj.shape[0]
    nb = T // BLOCK

    def body(q_ref, k_ref, v_ref, cos_ref, sin_ref, gq_ref, gk_ref, sink_ref, dy_ref,
             dq_ref, dk_ref, dv_ref, dgq_ref, dgk_ref, dsink_ref, qs, ks, vs, dqs, dks, dvs):
        _attn_prepare(q_ref, k_ref, v_ref, cos_ref, sin_ref, gq_ref, gk_ref, qs, ks, vs)
        dks[...] = jnp.zeros_like(dks)
        dvs[...] = jnp.zeros_like(dvs)
        lane = lax.broadcasted_iota(jnp.int32, (1, LANES), 1)

        def blk_step(blk, dsink):
            r0 = pl.multiple_of(blk * BLOCK, BLOCK)
            win = pl.ds(r0, 2 * BLOCK)
            for g in range(N_KV_HEADS):
                gc = slice(g * HEAD_DIM, (g + 1) * HEAD_DIM)
                kw = ks[win, gc]
                vw = vs[win, gc]
                qg = _group_rows(qs, r0, g)
                w, w_sink = _attn_scores(qg, kw, blk, _group_sinks(sink_ref, g))
                do = _group_rows(dy_ref, r0, g, cast=BF16)
                dvs[win, gc] += _dot_tn(w.astype(BF16), do)
                dw = _dot_nt(do, vw)
                delta = jnp.sum(w * dw, axis=-1, keepdims=True)
                ds = (w * (dw - delta)).astype(BF16)
                sink_part = w_sink * delta
                dq = _dot(ds, kw)
                for r in range(Q_PER_KV):
                    h = g * Q_PER_KV + r
                    slab = slice(r * BLOCK, (r + 1) * BLOCK)
                    dsink = dsink + jnp.where(lane == h, -jnp.sum(sink_part[slab, :], axis=0, keepdims=True), 0.0)
                    dqs[pl.ds(r0, BLOCK), h * HEAD_DIM:(h + 1) * HEAD_DIM] = dq[slab, :]
                dks[win, gc] += _dot_tn(ds, qg)
            return dsink

        dsink_ref[...] = lax.fori_loop(0, nb, blk_step, jnp.zeros((1, LANES), F32))

        lo, first, rot = _head_masks()

        def step(i, carry):
            dgq, dgk = carry
            r0 = pl.multiple_of(i * ATT_ROWS, ATT_ROWS)
            rows = pl.ds(r0, ATT_ROWS)
            prow = pl.ds(r0 + BLOCK, ATT_ROWS)
            cos, sin = cos_ref[rows, :], sin_ref[rows, :]
            for p in range(Q_COLS // LANES):
                cols = slice(p * LANES, (p + 1) * LANES)
                dxn = _rope_bwd(dqs[rows, cols] * (HEAD_DIM ** -0.5), cos, sin, first, rot)
                dx, dgp = _head_norm_bwd(q_ref[rows, cols], gq_ref[...], dxn, lo)
                dq_ref[rows, cols] = dx
                dgq = dgq + jnp.sum(dgp, axis=0, keepdims=True)
            dkn = _rope_bwd(dks[prow, :], cos, sin, first, rot)
            dx, dgp = _head_norm_bwd(k_ref[rows, :], gk_ref[...], dkn, lo)
            dk_ref[rows, :] = dx
            dgk = dgk + jnp.sum(dgp, axis=0, keepdims=True)
            dv_ref[rows, :] = dvs[prow, :]
            return dgq, dgk

        zero = jnp.zeros((1, LANES), F32)
        dgq, dgk = lax.fori_loop(0, T // ATT_ROWS, step, (zero, zero))
        dgq_ref[...] = _fold_heads(dgq)
        dgk_ref[...] = _fold_heads(dgk)

    vec = jax.ShapeDtypeStruct((1, LANES), F32)
    return _pcall(
        body, comm=comm, name=name, grid=(1,),
        in_specs=[
            pl.BlockSpec((T, Q_COLS), lambda i: (0, 0)),
            pl.BlockSpec((T, KV_COLS), lambda i: (0, Q_COLS // KV_COLS)),
            pl.BlockSpec((T, KV_COLS), lambda i: (0, Q_COLS // KV_COLS + 1)),
            _full((T, LANES)), _full((T, LANES)), _full((1, LANES)), _full((1, LANES)),
            pl.BlockSpec(memory_space=pltpu.SMEM),
            pl.BlockSpec((T, Q_COLS), lambda i: (0, 0)),
        ],
        out_specs=[_full((T, Q_COLS)), _full((T, KV_COLS)), _full((T, KV_COLS)),
                   _full((1, LANES)), _full((1, LANES)), _full((1, LANES))],
        out_shape=[jax.ShapeDtypeStruct((T, Q_COLS), F32), jax.ShapeDtypeStruct((T, KV_COLS), F32),
                   jax.ShapeDtypeStruct((T, KV_COLS), F32), vec, vec, vec],
        scratch_shapes=[
            pltpu.VMEM((T, Q_COLS), BF16),
            pltpu.VMEM((T + BLOCK, KV_COLS), BF16),
            pltpu.VMEM((T + BLOCK, KV_COLS), BF16),
            pltpu.VMEM((T, Q_COLS), F32),
            pltpu.VMEM((T + BLOCK, KV_COLS), F32),
            pltpu.VMEM((T + BLOCK, KV_COLS), F32),
        ],
        compiler_params=_params(("arbitrary",)),
    )(proj, proj, proj, cos, sin, gq2, gk2, sinks, dyc)


CONV_PAD = 32
CONV_ROWS = 256


def _conv_taps(src, w_ref, r0, first_off, step_sign):
    acc = None
    for i in range(CONV_WIDTH):
        term = w_ref[i:i + 1, :] * src[r0 + first_off + step_sign * i:r0 + first_off + step_sign * i + CONV_ROWS, :]
        acc = term if acc is None else acc + term
    return acc


def _conv_fwd(proj, w_dw, b_dw, g_ln, b_ln, name, comm=None):
    T = proj.shape[0]
    a_blk = (Q_COLS + 2 * KV_COLS) // CONV_CH

    def body(a_ref, gate_ref, w_ref, bdw_ref, g_ref, b_ref, y_ref, c_ref, pad):
        pad[0:CONV_PAD, :] = jnp.zeros((CONV_PAD, CONV_CH), F32)
        pad[CONV_PAD:, :] = a_ref[...] * _sigmoid(gate_ref[...])
        for n in range(T // CONV_ROWS):
            r0 = n * CONV_ROWS
            c = _conv_taps(pad, w_ref, r0, CONV_PAD - (CONV_WIDTH - 1), 1) + bdw_ref[...]
            c_ref[r0:r0 + CONV_ROWS, :] = c
            mu = jnp.mean(c, axis=-1, keepdims=True)
            cc = c - mu
            rstd = lax.rsqrt(jnp.mean(cc * cc, axis=-1, keepdims=True) + EPS)
            z = cc * rstd * g_ref[...] + b_ref[...]
            y_ref[r0:r0 + CONV_ROWS, :] = (z * _sigmoid(z)).astype(BF16)

    vec = _full((1, CONV_CH))
    return _pcall(
        body, comm=comm, name=name, grid=(1,),
        in_specs=[
            pl.BlockSpec((T, CONV_CH), lambda i: (0, a_blk)),
            pl.BlockSpec((T, CONV_CH), lambda i: (0, a_blk + 1)),
            _full((CONV_WIDTH, CONV_CH)), vec, vec, vec,
        ],
        out_specs=[_full((T, CONV_CH)), _full((T, CONV_CH))],
        out_shape=[jax.ShapeDtypeStruct((T, CONV_CH), BF16), jax.ShapeDtypeStruct((T, CONV_CH), F32)],
        scratch_shapes=[pltpu.VMEM((T + CONV_PAD, CONV_CH), F32)],
        compiler_params=_params(("arbitrary",)),
    )(proj, proj, w_dw, b_dw, g_ln, b_ln)


def _conv_bwd(proj, c, w_dw, g_ln, b_ln, dyc, name, comm=None):
    T = proj.shape[0]
    a_blk = (Q_COLS + 2 * KV_COLS) // CONV_CH
    y_blk = Q_COLS // CONV_CH

    def body(a_ref, gate_ref, c_ref, w_ref, g_ref, b_ref, dy_ref,
             da_ref, dgate_ref, dw_ref, dbdw_ref, dg_ref, db_ref, pad, dcp):
        pad[0:CONV_PAD, :] = jnp.zeros((CONV_PAD, CONV_CH), F32)
        sg = _sigmoid(gate_ref[...])
        pad[CONV_PAD:, :] = a_ref[...] * sg
        dcp[T:, :] = jnp.zeros((CONV_PAD, CONV_CH), F32)
        dg = db = dbdw = jnp.zeros((1, CONV_CH), F32)
        for n in range(T // CONV_ROWS):
            rows = slice(n * CONV_ROWS, (n + 1) * CONV_ROWS)
            cv = c_ref[rows, :]
            mu = jnp.mean(cv, axis=-1, keepdims=True)
            cc = cv - mu
            rstd = lax.rsqrt(jnp.mean(cc * cc, axis=-1, keepdims=True) + EPS)
            chat = cc * rstd
            z = chat * g_ref[...] + b_ref[...]
            sz = _sigmoid(z)
            dz = dy_ref[rows, :] * (sz * (1.0 + z * (1.0 - sz)))
            dg = dg + jnp.sum(dz * chat, axis=0, keepdims=True)
            db = db + jnp.sum(dz, axis=0, keepdims=True)
            dch = dz * g_ref[...]
            dc = rstd * (dch - jnp.mean(dch, axis=-1, keepdims=True)
                         - chat * jnp.mean(dch * chat, axis=-1, keepdims=True))
            dbdw = dbdw + jnp.sum(dc, axis=0, keepdims=True)
            dcp[rows, :] = dc
        dg_ref[...] = dg
        db_ref[...] = db
        dbdw_ref[...] = dbdw
        dw_ref[CONV_WIDTH:, :] = jnp.zeros((CONV_PAD - CONV_WIDTH, CONV_CH), F32)
        for i in range(CONV_WIDTH):
            off = CONV_PAD - (CONV_WIDTH - 1) + i
            acc = jnp.zeros((1, CONV_CH), F32)
            for n in range(T // CONV_ROWS):
                r0 = n * CONV_ROWS
                acc = acc + jnp.sum(dcp[r0:r0 + CONV_ROWS, :] * pad[r0 + off:r0 + off + CONV_ROWS, :],
                                    axis=0, keepdims=True)
            dw_ref[i:i + 1, :] = acc
        for n in range(T // CONV_ROWS):
            r0 = n * CONV_ROWS
            rows = slice(r0, r0 + CONV_ROWS)
            dhg = _conv_taps(dcp, w_ref, r0, CONV_WIDTH - 1, -1)
            sgv = sg[rows, :]
            da_ref[rows, :] = dhg * sgv
            dgate_ref[rows, :] = dhg * a_ref[rows, :] * sgv * (1.0 - sgv)

    vec = _full((1, CONV_CH))
    vshape = jax.ShapeDtypeStruct((1, CONV_CH), F32)
    return _pcall(
        body, comm=comm, name=name, grid=(1,),
        in_specs=[
            pl.BlockSpec((T, CONV_CH), lambda i: (0, a_blk)),
            pl.BlockSpec((T, CONV_CH), lambda i: (0, a_blk + 1)),
            _full((T, CONV_CH)), _full((CONV_WIDTH, CONV_CH)), vec, vec,
            pl.BlockSpec((T, CONV_CH), lambda i: (0, y_blk)),
        ],
        out_specs=[_full((T, CONV_CH)), _full((T, CONV_CH)), _full((CONV_PAD, CONV_CH)), vec, vec, vec],
        out_shape=[jax.ShapeDtypeStruct((T, CONV_CH), F32), jax.ShapeDtypeStruct((T, CONV_CH), F32),
                   jax.ShapeDtypeStruct((CONV_PAD, CONV_CH), F32), vshape, vshape, vshape],
        scratch_shapes=[pltpu.VMEM((T + CONV_PAD, CONV_CH), F32), pltpu.VMEM((T + CONV_PAD, CONV_CH), F32)],
        compiler_params=_params(("arbitrary",)),
    )(proj, proj, c, w_dw, g_ln, b_ln, dyc)


def _mem_kv(mkv_ref, gk_ref, lo, kn_s, vv_s):
    for p in range(MQ_COLS // LANES):
        cols = slice(p * LANES, (p + 1) * LANES)
        kn_s[:, cols] = _head_norm(mkv_ref[:, cols], gk_ref[...], lo).astype(BF16)
    vv_s[...] = mkv_ref[:, MQ_COLS:].astype(BF16)


def _mem_softmax(qh, kh):
    s = _dot_nt(qh, kh)
    m = jnp.max(s, axis=-1, keepdims=True)
    p = jnp.exp(s - m)
    return p / jnp.sum(p, axis=-1, keepdims=True)


def _mem_fwd(proj, mkv, gq2, gk2, name, comm=None):
    T = proj.shape[0]
    tt = ROW_TILE
    q_blk = (IN_COLS - MQ_COLS) // MQ_COLS

    def body(q_ref, mkv_ref, gq_ref, gk_ref, y_ref, kn_s, vv_s, qn_s):
        lo, _, _ = _head_masks()
        _mem_kv(mkv_ref, gk_ref, lo, kn_s, vv_s)
        for p in range(MQ_COLS // LANES):
            cols = slice(p * LANES, (p + 1) * LANES)
            qn_s[:, cols] = (_head_norm(q_ref[:, cols], gq_ref[...], lo) * (HEAD_DIM ** -0.5)).astype(BF16)
        for h in range(N_MEM_HEADS):
            hc = slice(h * HEAD_DIM, (h + 1) * HEAD_DIM)
            w = _mem_softmax(qn_s[:, hc], kn_s[:, hc])
            y_ref[:, hc] = _dot(w.astype(BF16), vv_s[:, hc]).astype(BF16)

    return _pcall(
        body, comm=comm, name=name, grid=(T // tt,),
        in_specs=[
            pl.BlockSpec((tt, MQ_COLS), lambda t: (t, q_blk)),
            pl.BlockSpec((MEM_LEN, 2 * MQ_COLS), lambda t: (0, 0)),
            pl.BlockSpec((1, LANES), lambda t: (0, 0)), pl.BlockSpec((1, LANES), lambda t: (0, 0)),
        ],
        out_specs=pl.BlockSpec((tt, MQ_COLS), lambda t: (t, 0)),
        out_shape=jax.ShapeDtypeStruct((T, MQ_COLS), BF16),
        scratch_shapes=[pltpu.VMEM((MEM_LEN, MQ_COLS), BF16), pltpu.VMEM((MEM_LEN, MQ_COLS), BF16),
                        pltpu.VMEM((tt, MQ_COLS), BF16)],
        compiler_params=_params(("arbitrary",)),
    )(proj, mkv, gq2, gk2)


def _mem_bwd(proj, mkv, gq2, gk2, dyc, name, comm=None):
    T = proj.shape[0]
    tt = ROW_TILE
    nt = T // tt
    q_blk = (IN_COLS - MQ_COLS) // MQ_COLS
    y_blk = (Q_COLS + CONV_CH) // MQ_COLS

    def body(q_ref, mkv_ref, gq_ref, gk_ref, dy_ref, dq_ref, dmkv_ref, dgq_ref, dgk_ref,
             kn_s, vv_s, qn_s, dqn_s, dkn_acc):
        t = pl.program_id(0)
        lo, _, _ = _head_masks()
        _mem_kv(mkv_ref, gk_ref, lo, kn_s, vv_s)

        @pl.when(t == 0)
        def _():
            dkn_acc[...] = jnp.zeros_like(dkn_acc)
            dmkv_ref[...] = jnp.zeros_like(dmkv_ref)
            dgq_ref[...] = jnp.zeros_like(dgq_ref)

        for p in range(MQ_COLS // LANES):
            cols = slice(p * LANES, (p + 1) * LANES)
            qn_s[:, cols] = (_head_norm(q_ref[:, cols], gq_ref[...], lo) * (HEAD_DIM ** -0.5)).astype(BF16)
        for h in range(N_MEM_HEADS):
            hc = slice(h * HEAD_DIM, (h + 1) * HEAD_DIM)
            vc = slice(MQ_COLS + h * HEAD_DIM, MQ_COLS + (h + 1) * HEAD_DIM)
            qh = qn_s[:, hc]
            w = _mem_softmax(qh, kn_s[:, hc])
            do = dy_ref[:, hc].astype(BF16)
            dmkv_ref[:, vc] += _dot_tn(w.astype(BF16), do)
            dw = _dot_nt(do, vv_s[:, hc])
            ds = (w * (dw - jnp.sum(w * dw, axis=-1, keepdims=True))).astype(BF16)
            dqn_s[:, hc] = _dot(ds, kn_s[:, hc])
            dkn_acc[:, hc] += _dot_tn(ds, qh)
        dgq = jnp.zeros((1, LANES), F32)
        for p in range(MQ_COLS // LANES):
            cols = slice(p * LANES, (p + 1) * LANES)
            dx, dgp = _head_norm_bwd(q_ref[:, cols], gq_ref[...], dqn_s[:, cols] * (HEAD_DIM ** -0.5), lo)
            dq_ref[:, cols] = dx
            dgq = dgq + jnp.sum(dgp, axis=0, keepdims=True)
        dgq_ref[...] += dgq

        @pl.when(t == nt - 1)
        def _():
            dgk = jnp.zeros((1, LANES), F32)
            for p in range(MQ_COLS // LANES):
                cols = slice(p * LANES, (p + 1) * LANES)
                dx, dgp = _head_norm_bwd(mkv_ref[:, cols], gk_ref[...], dkn_acc[:, cols], lo)
                dmkv_ref[:, cols] = dx
                dgk = dgk + jnp.sum(dgp, axis=0, keepdims=True)
            dgk_ref[...] = _fold_heads(dgk)
            dgq_ref[...] = _fold_heads(dgq_ref[...])

    vec = pl.BlockSpec((1, LANES), lambda t: (0, 0))
    vshape = jax.ShapeDtypeStruct((1, LANES), F32)
    return _pcall(
        body, comm=comm, name=name, grid=(nt,),
        in_specs=[
            pl.BlockSpec((tt, MQ_COLS), lambda t: (t, q_blk)),
            pl.BlockSpec((MEM_LEN, 2 * MQ_COLS), lambda t: (0, 0)),
            vec, vec,
            pl.BlockSpec((tt, MQ_COLS), lambda t: (t, y_blk)),
        ],
        out_specs=[pl.BlockSpec((tt, MQ_COLS), lambda t: (t, 0)),
                   pl.BlockSpec((MEM_LEN, 2 * MQ_COLS), lambda t: (0, 0)), vec, vec],
        out_shape=[jax.ShapeDtypeStruct((T, MQ_COLS), F32), jax.ShapeDtypeStruct((MEM_LEN, 2 * MQ_COLS), F32),
                   vshape, vshape],
        scratch_shapes=[pltpu.VMEM((MEM_LEN, MQ_COLS), BF16), pltpu.VMEM((MEM_LEN, MQ_COLS), BF16),
                        pltpu.VMEM((tt, MQ_COLS), BF16), pltpu.VMEM((tt, MQ_COLS), F32),
                        pltpu.VMEM((MEM_LEN, MQ_COLS), F32)],
        compiler_params=_params(("arbitrary",)),
    )(proj, mkv, gq2, gk2, dyc)


ADAMW_TILE_BYTES = 5 << 19


def _adamw(ws, gs, ms, vs, name, comm=None):
    n = len(ws)
    R, C = ws[0].shape
    budget = ADAMW_TILE_BYTES // (4 * C * n)
    tr = next((r for r in (512, 352, 256, 176, 128, 88, 64, 32, 16, 8) if R % r == 0 and r <= max(budget, 8)), R)

    def body(*refs):
        ins, outs = refs[:4 * n], refs[4 * n:]
        for i in range(n):
            w_ref, g_ref, m_ref, v_ref = ins[i], ins[n + i], ins[2 * n + i], ins[3 * n + i]
            go_ref, d_ref, nm_ref, nv_ref = outs[4 * i:4 * i + 4]
            gv = g_ref[...]
            go_ref[...] = gv
            nm = ADAM_B1 * m_ref[...] + (1.0 - ADAM_B1) * gv
            nv = ADAM_B2 * v_ref[...] + (1.0 - ADAM_B2) * (gv * gv)
            m_hat = nm / (1.0 - ADAM_B1 ** ADAM_STEP)
            v_hat = nv / (1.0 - ADAM_B2 ** ADAM_STEP)
            d_ref[...] = -ADAM_LR * (m_hat / (jnp.sqrt(v_hat) + ADAM_EPS) + ADAM_WD * w_ref[...])
            nm_ref[...] = nm
            nv_ref[...] = nv

    tile = pl.BlockSpec((tr, C), lambda i: (i, 0))
    shape = jax.ShapeDtypeStruct((R, C), F32)
    res = _pcall(
        body, comm=comm, name=name, grid=(R // tr,),
        in_specs=[tile] * (4 * n), out_specs=[tile] * (4 * n), out_shape=[shape] * (4 * n),
        compiler_params=_params(("arbitrary",)),
    )(*ws, *gs, *ms, *vs)
    return [tuple(res[4 * i:4 * i + 4]) for i in range(n)]


def _mesh_pos():
    return lax.axis_index("x"), lax.axis_index("y"), lax.axis_index("c")


def _other_chips(x, y):
    return [(1 - x, y), (x, 1 - y), (1 - x, 1 - y)]


def _quarter(ref, layout, q, rows, cols):
    if layout == "cols":
        return ref.at[rows, pl.ds(pl.multiple_of(q * cols, LANES), cols)]
    return ref.at[q, rows, :]


def _gather_weights(shards, dtypes, layouts, later, name, comm=None):
    n = len(shards)
    all_rows = slice(None)
    remote = [i for i in range(n) if i not in later]
    split = [i for i in remote if layouts[i] != "whole"]

    def body(*refs):
        ins, outs = refs[:n], refs[n:2 * n]
        st32, st16 = refs[2 * n:3 * n], refs[3 * n:4 * n]
        in_sems, own_sems, send_sems, recv_sems, fwd_send_sems, fwd_recv_sems = refs[4 * n:]
        x, y, c = _mesh_pos()
        chip = 2 * x + y
        sibling = (x, y, 1 - c)
        chips = _other_chips(x, y)

        def half(i, which):
            if i not in split:
                return all_rows
            hr = shards[i].shape[0] // 2
            return pl.ds(pl.multiple_of(which * hr, 16), hr)

        def place(i, q, rows):
            return _quarter(outs[i], layouts[i], q, rows, shards[i].shape[1])

        def ici(i, k, origin_chip, src):
            px, py = chips[k]
            return pltpu.make_async_remote_copy(
                src_ref=src, dst_ref=place(i, origin_chip, half(i, c)), send_sem=send_sems.at[i, k],
                recv_sem=recv_sems.at[i, k], device_id=(px, py, c), device_id_type=MESH)

        def forward(i, k, rows):
            px, py = chips[k]
            there = place(i, 2 * px + py, rows)
            return pltpu.make_async_remote_copy(
                src_ref=there, dst_ref=there, send_sem=fwd_send_sems.at[i, k],
                recv_sem=fwd_recv_sems.at[i, k], device_id=sibling, device_id_type=MESH)

        loads = [pltpu.make_async_copy(ins[i], st32[i], in_sems.at[i]) for i in range(n)]
        for cp in loads:
            cp.start()
        owns, sent = [], []
        for i in range(n):
            loads[i].wait()
            st16[i][...] = st32[i][...].astype(dtypes[i])
            own = pltpu.make_async_copy(st16[i], place(i, chip, all_rows), own_sems.at[i])
            own.start()
            owns.append(own)
            for k in range(3 if i in remote else 0):
                cp = ici(i, k, chip, st16[i].at[half(i, c)])
                cp.start()
                sent.append(cp)
        for i in remote:
            for k, (px, py) in enumerate(chips):
                ici(i, k, 2 * px + py, st16[i].at[half(i, c)]).wait_recv()
                if i in split:
                    cp = forward(i, k, half(i, c))
                    cp.start()
                    sent.append(cp)
        for i in split:
            for k in range(3):
                forward(i, k, half(i, 1 - c)).wait_recv()
        for cp in sent:
            cp.wait_send()
        for cp in owns:
            cp.wait()

    hbm = pl.BlockSpec(memory_space=pl.ANY)
    return _pcall(
        body, comm=comm, name=name,
        in_specs=[hbm] * n, out_specs=[hbm] * n,
        out_shape=[_gathered_shape(s.shape, d, lay) for s, d, lay in zip(shards, dtypes, layouts)],
        scratch_shapes=[pltpu.VMEM(s.shape, F32) for s in shards] + [pltpu.VMEM(s.shape, d) for s, d in zip(shards, dtypes)]
        + [pltpu.SemaphoreType.DMA((n,)), pltpu.SemaphoreType.DMA((n,)),
           pltpu.SemaphoreType.DMA((n, 3)), pltpu.SemaphoreType.DMA((n, 3)),
           pltpu.SemaphoreType.DMA((n, 3)), pltpu.SemaphoreType.DMA((n, 3))],
        compiler_params=pltpu.CompilerParams(vmem_limit_bytes=VMEM_LIMIT),
    )(*shards)


def _gathered_shape(quarter_shape, dtype, layout):
    R, C = quarter_shape
    return jax.ShapeDtypeStruct((R, N_CHIPS * C) if layout == "cols" else (N_CHIPS, R, C), dtype)


def _remote(src, dst, sems, j, device):
    return pltpu.make_async_remote_copy(src_ref=src, dst_ref=dst, send_sem=sems.at[2 * j], recv_sem=sems.at[2 * j + 1],
                                        device_id=device, device_id_type=MESH)


class _SemWindow:
    def __init__(self, sems, offset):
        self.sems, self.offset = sems, offset

    @property
    def at(self):
        return self

    def __getitem__(self, i):
        return self.sems.at[self.offset + i]


def _join_plans(a, b):
    n_ai, n_ao = len(a.ins), len(a.out_shapes)

    def plan(ins, outs, sems, finishing):
        first = a.plan(ins[:n_ai], outs[:n_ao], sems, finishing)
        second = b.plan(ins[n_ai:], outs[n_ao:], _SemWindow(sems, a.n_sems), finishing)
        return tuple(p + q for p, q in zip(first, second))

    aliases = {**a.aliases, **{n_ai + i: n_ao + o for i, o in b.aliases.items()}}
    return _Comm(a.ins + b.ins, a.out_shapes + b.out_shapes, a.n_sems + b.n_sems, plan, aliases)


def _half_rows(rows, which):
    hr = rows // 2
    return pl.ds(pl.multiple_of(which * hr, 16), hr)


def _spread_plan(fulls, quarter_shapes, layouts, peers=(0, 1, 2)):
    def plan(ins, outs, sems, finishing):
        x, y, c = _mesh_pos()
        chip = 2 * x + y
        sends, recvs = [], []
        for i, full in enumerate(outs):
            R, C = quarter_shapes[i]
            for k, (px, py) in enumerate(_other_chips(x, y)):
                if k not in peers:
                    continue
                mine = _quarter(full, layouts[i], chip, _half_rows(R, c), C)
                sends.append(_remote(mine, mine, sems, 3 * i + k, (px, py, c)))
                if finishing:
                    theirs = _quarter(full, layouts[i], 2 * px + py, _half_rows(R, c), C)
                    recvs.append(_remote(mine, theirs, sems, 3 * i + k, (px, py, c)))
        return [], sends, recvs

    shapes = [jax.ShapeDtypeStruct(f.shape, f.dtype) for f in fulls]
    return _Comm(fulls, shapes, 6 * len(fulls), plan, aliases={i: i for i in range(len(fulls))})


def _forward_plan(fulls, quarter_shapes, layouts):
    def plan(ins, outs, sems, finishing):
        x, y, c = _mesh_pos()
        sends, recvs = [], []
        for i, full in enumerate(outs):
            R, C = quarter_shapes[i]
            for k, (px, py) in enumerate(_other_chips(x, y)):
                mine = _quarter(full, layouts[i], 2 * px + py, _half_rows(R, c), C)
                sends.append(_remote(mine, mine, sems, 3 * i + k, (x, y, 1 - c)))
                if finishing:
                    theirs = _quarter(full, layouts[i], 2 * px + py, _half_rows(R, 1 - c), C)
                    recvs.append(_remote(mine, theirs, sems, 3 * i + k, (x, y, 1 - c)))
        return [], sends, recvs

    shapes = [jax.ShapeDtypeStruct(f.shape, f.dtype) for f in fulls]
    return _Comm(fulls, shapes, 6 * len(fulls), plan, aliases={i: i for i in range(len(fulls))})


def _swap_plan(grads):
    def plan(ins, outs, sems, finishing):
        x, y, c = _mesh_pos()
        sends = [_remote(g.at[:, _half_rows(g.shape[1], 1 - c), :], sib, sems, i, (x, y, 1 - c))
                 for i, (g, sib) in enumerate(zip(ins, outs))]
        return [], sends, sends

    shapes = [jax.ShapeDtypeStruct((N_CHIPS, g.shape[1] // 2, g.shape[2]), BF16) for g in grads]
    return _Comm(grads, shapes, 2 * len(grads), plan)


def _pair_sums(gs, sibs, name, comm=None):
    n = len(gs)

    def body(*refs):
        c = lax.axis_index("c")
        for g_ref, sib_ref, o_ref in zip(refs[:n], refs[n:2 * n], refs[2 * n:]):
            mine = _half_rows(g_ref.shape[1], c)
            o_ref[0] = (g_ref[0, mine, :].astype(F32) + sib_ref[0].astype(F32)).astype(BF16)

    def chunk(shape):
        return pl.BlockSpec((1,) + shape[1:], lambda q: (q, 0, 0))

    return _pcall(
        body, comm=comm, name=name, grid=(N_CHIPS,),
        in_specs=[chunk(g.shape) for g in gs] + [chunk(s.shape) for s in sibs],
        out_specs=[chunk(s.shape) for s in sibs],
        out_shape=[jax.ShapeDtypeStruct(s.shape, BF16) for s in sibs],
        compiler_params=_params(("arbitrary",)),
    )(*gs, *sibs)


def _ici_plan(sums):
    def plan(ins, outs, sems, finishing):
        x, y, c = _mesh_pos()
        sends = []
        for i, (s, rcv) in enumerate(zip(ins, outs)):
            for k, (px, py) in enumerate(_other_chips(x, y)):
                sends.append(_remote(s.at[2 * px + py], rcv.at[k], sems, 3 * i + k, (px, py, c)))
        return [], sends, sends

    shapes = [jax.ShapeDtypeStruct((3,) + s.shape[1:], BF16) for s in sums]
    return _Comm(sums, shapes, 6 * len(sums), plan)


def _finish_quarters(ss, rcvs, name, comm=None):
    n = len(ss)

    def body(*refs):
        s_refs, rcv_refs, out_refs, sems = refs[:n], refs[n:2 * n], refs[2 * n:3 * n], refs[3 * n]
        x, y, c = _mesh_pos()
        swaps = []
        for i, (s_ref, rcv_ref, out_ref) in enumerate(zip(s_refs, rcv_refs, out_refs)):
            mine = _half_rows(out_ref.shape[0], c)
            acc = s_ref[2 * x + y].astype(F32)
            for k in range(3):
                acc = acc + rcv_ref[k].astype(F32)
            out_ref[mine, :] = acc
            back = _remote(out_ref.at[mine, :], out_ref.at[mine, :], sems, i, (x, y, 1 - c))
            back.start()
            swaps.append(back)
        for back in swaps:
            back.wait()

    vmem = pl.BlockSpec(memory_space=pltpu.VMEM)
    return _pcall(
        body, comm=comm, name=name, grid=(1,),
        in_specs=[vmem] * (2 * n), out_specs=[vmem] * n,
        out_shape=[jax.ShapeDtypeStruct((2 * s.shape[1], s.shape[2]), F32) for s in ss],
        scratch_shapes=[pltpu.SemaphoreType.DMA((2 * n,))],
        compiler_params=pltpu.CompilerParams(vmem_limit_bytes=VMEM_LIMIT),
    )(*ss, *rcvs)


def _allreduce_small(v, name, comm=None):
    R, C = v.shape
    n_dev = 8

    def body(v_ref, out_ref, buf, send_sems, recv_sems):
        x, y, c = _mesh_pos()
        me = 4 * x + 2 * y + c
        buf[me] = v_ref[...]
        peers = []
        for k in range(1, n_dev):
            kx, ky, kc = (k >> 2) & 1, (k >> 1) & 1, k & 1
            px = 1 - x if kx else x
            py = 1 - y if ky else y
            pc = 1 - c if kc else c
            peers.append((px, py, pc))
        sends = []
        for k, peer in enumerate(peers):
            cp = pltpu.make_async_remote_copy(
                src_ref=v_ref, dst_ref=buf.at[me], send_sem=send_sems.at[k], recv_sem=recv_sems.at[k],
                device_id=peer, device_id_type=MESH)
            cp.start()
            sends.append(cp)
        for k, (px, py, pc) in enumerate(peers):
            pltpu.make_async_remote_copy(
                src_ref=v_ref, dst_ref=buf.at[4 * px + 2 * py + pc], send_sem=send_sems.at[k],
                recv_sem=recv_sems.at[k], device_id=(px, py, pc), device_id_type=MESH).wait_recv()
        for cp in sends:
            cp.wait_send()
        acc = buf[0]
        for i in range(1, n_dev):
            acc = acc + buf[i]
        out_ref[...] = acc

    vmem = pl.BlockSpec(memory_space=pltpu.VMEM)
    return _pcall(
        body, comm=comm, name=name, grid=(1,),
        in_specs=[vmem], out_specs=vmem,
        out_shape=jax.ShapeDtypeStruct((R, C), F32),
        scratch_shapes=[pltpu.VMEM((n_dev, R, C), F32),
                        pltpu.SemaphoreType.DMA((n_dev - 1,)), pltpu.SemaphoreType.DMA((n_dev - 1,))],
        compiler_params=pltpu.CompilerParams(vmem_limit_bytes=VMEM_LIMIT),
    )(v)


def _rope_tables(positions):
    half = ROPE_DIM // 2
    inv_freq = ROPE_THETA ** (-jnp.arange(half, dtype=F32) / half)
    ang = positions.astype(F32)[:, None] * inv_freq
    cos, sin = jnp.cos(ang), jnp.sin(ang)
    T = positions.shape[0]
    ones = jnp.ones((T, HEAD_DIM - ROPE_DIM), F32)
    c64 = jnp.concatenate([cos, cos, ones], axis=1)
    s64 = jnp.concatenate([-sin, sin, 0.0 * ones], axis=1)
    return jnp.tile(c64, (1, 2)), jnp.tile(s64, (1, 2))


def _local_step(x, mem, positions, target, small, first, own):
    cos, sin = _rope_tables(positions)
    two = lambda g: jnp.tile(g, (1, 2))
    gq2, gk2, gmq2, gmk2 = two(small["g_q"]), two(small["g_k"]), two(small["g_mq"]), two(small["g_mk"])
    mix_names = ["w_in", "w_mkv", "w_out"]
    mix_layouts = ["cols", "stack", "stack"]
    mix_quarters = [(D_MODEL, IN_CHUNK), (D_MODEL // N_CHIPS, 2 * MQ_COLS), (D_MODEL // N_CHIPS, D_MODEL)]
    ffn2_names = ["wg2", "wu2", "wd2"]
    ffn_quarter = (FF_CHUNK, D_MODEL)

    spread = _spread_plan([own["w_in"], own["w_mkv"], own["wg2"]], mix_quarters[:2] + [ffn_quarter], ["cols", "stack", "stack"])
    (x1, h1, a1, b1), (half_in, half_mkv, half_wg2) = _ffn_fwd(
        x, small["g_ffn1"], first["wg1"], first["wu1"], first["wd1"], "ffn1_fwd", comm=spread)
    hm, (w_in, w_mkv) = _rms_fwd(x1, small["g_mix"], "mix_norm",
                                 comm=_forward_plan([half_in, half_mkv], mix_quarters[:2], mix_layouts[:2]))
    w_mkv = w_mkv.reshape(D_MODEL, 2 * MQ_COLS)
    proj, (half_out,) = _mm_nn([hm], w_in, None, "in_proj",
                               comm=_spread_plan([own["w_out"]], mix_quarters[2:], mix_layouts[2:]))
    hmem = _rms_fwd(mem, small["g_mem"], "mem_norm")
    mkv = _mm_nn([hmem], w_mkv, None, "mem_proj")
    ya, (half_wu2,) = _attn_fwd(proj, cos, sin, gq2, gk2, small["sinks"], "swa_fwd",
                                comm=_spread_plan([own["wu2"]], [ffn_quarter], ["stack"]))
    (yc, cpre), (near_wd2, w_out) = _conv_fwd(
        proj, small["w_dw"], small["b_dw"], small["g_conv_ln"], small["b_conv_ln"], "conv_fwd",
        comm=_join_plans(_spread_plan([own["wd2"]], [ffn_quarter], ["stack"], peers=(0, 1)),
                         _forward_plan([half_out], mix_quarters[2:], mix_layouts[2:])))
    w_out = w_out.reshape(D_MODEL, D_MODEL)
    ym, (half_wd2,) = _mem_fwd(proj, mkv, gmq2, gmk2, "memattn_fwd",
                               comm=_spread_plan([near_wd2], [ffn_quarter], ["stack"], peers=(2,)))
    passing = _forward_plan([half_wg2, half_wu2, half_wd2], [ffn_quarter] * 3, ["stack"] * 3)
    x2, (wg2, wu2, wd2) = _mm_nn([ya, yc, ym], w_out, x1, "out_proj", comm=passing)
    dx3, h2, a2, b2, loss = _ffn_fwd(x2, small["g_ffn2"], wg2, wu2, wd2, "ffn2_fwd", target=target, rows=2 * ROW_TILE)

    dh2, dwg2, dwu2, dwd2 = _ffn_bwd(dx3, h2, a2, b2, wg2, wu2, wd2, "ffn2_bwd")
    (dx2, dg_ffn2), sibs = _rms_bwd(x2, small["g_ffn2"], dh2, dx3, "ffn2_norm_bwd", comm=_swap_plan([dwg2, dwu2, dwd2]))
    sums_ffn2 = _pair_sums([dwg2, dwu2, dwd2], sibs, "pair_sums_ffn2")
    dyc = _mm_nt([dx2], w_out, "out_proj_bwd")
    dw_out = _mm_tn([ya, yc, ym], [dx2], 1, "out_proj_wgrad").reshape(N_CHIPS, -1, D_MODEL)
    (dq, dk, dv, dgq, dgk, dsinks), rcv_wg2 = _attn_bwd(proj, cos, sin, gq2, gk2, small["sinks"], dyc, "swa_bwd",
                                                         comm=_ici_plan(sums_ffn2[:1]))
    (da, dgate, dw_dw, db_dw, dg_ln, db_ln), sibs = _conv_bwd(
        proj, cpre, small["w_dw"], small["g_conv_ln"], small["b_conv_ln"], dyc, "conv_bwd", comm=_swap_plan([dw_out]))
    dmq, dmkv, dgmq, dgmk = _mem_bwd(proj, mkv, gmq2, gmk2, dyc, "memattn_bwd")
    sums_out = _pair_sums([dw_out], sibs, "pair_sums_out")
    pieces = [dq, dk, dv, da, dgate, dmq]
    dhm, rcv_out = _mm_nt(pieces, w_in, "in_proj_bwd", comm=_ici_plan(sums_out))
    dw_in = _mm_tn([hm], pieces, N_CHIPS, "in_proj_wgrad")
    dhmem = _mm_nt([dmkv], w_mkv, "mem_proj_bwd")
    dw_mkv = _mm_tn([hmem], [dmkv], 1, "mem_proj_wgrad").reshape(N_CHIPS, -1, 2 * MQ_COLS)
    _, dg_mem = _rms_bwd(mem, small["g_mem"], dhmem, None, "mem_norm_bwd")
    (dx1, dg_mix), sibs = _rms_bwd(x1, small["g_mix"], dhm, dx2, "mix_norm_bwd", comm=_swap_plan([dw_in, dw_mkv]))
    sums_in = _pair_sums([dw_in, dw_mkv], sibs, "pair_sums_in")
    (dh1, dwg1, dwu1, dwd1), landed = _ffn_bwd(dx1, h1, a1, b1, first["wg1"], first["wu1"], first["wd1"], "ffn1_bwd",
                                                comm=_ici_plan([*sums_in, *sums_ffn2[1:]]))
    rcv_in, rcv_ffn2 = landed[:2], rcv_wg2 + landed[2:]
    (dx, dg_ffn1), sibs = _rms_bwd(x, small["g_ffn1"], dh1, dx1, "ffn1_norm_bwd", comm=_swap_plan([dwg1, dwu1, dwd1]))
    sums_ffn1 = _pair_sums([dwg1, dwu1, dwd1], sibs, "pair_sums_ffn1")
    g_ffn2, rcv_wg1 = _finish_quarters(sums_ffn2, rcv_ffn2, "finish_ffn2", comm=_ici_plan(sums_ffn1[:1]))
    g_mix, rcv_wu1 = _finish_quarters(sums_in + sums_out, rcv_in + rcv_out, "finish_mix", comm=_ici_plan(sums_ffn1[1:2]))
    small_grads = dict(
        g_ffn1=dg_ffn1, g_mix=dg_mix, g_q=dgq[:, :HEAD_DIM], g_k=dgk[:, :HEAD_DIM], sinks=dsinks[:, :N_Q_HEADS],
        w_dw=dw_dw[:CONV_WIDTH], b_dw=db_dw, g_conv_ln=dg_ln, b_conv_ln=db_ln, g_mem=dg_mem,
        g_mq=dgmq[:, :HEAD_DIM], g_mk=dgmk[:, :HEAD_DIM], g_ffn2=dg_ffn2, loss=loss[:, :1])
    names = list(small_grads)
    packed, offs = _pack([small_grads[n] for n in names])
    total, rcv_wd1 = _allreduce_small(packed, "allreduce_small", comm=_ici_plan(sums_ffn1[2:]))
    summed = dict(zip(names, _unpack(total, offs, [small_grads[n].shape for n in names])))
    g_ffn1 = _finish_quarters(sums_ffn1, rcv_wg1 + rcv_wu1 + rcv_wd1, "finish_ffn1")
    big_grads = dict(zip(["wg1", "wu1", "wd1", "w_in", "w_mkv", "w_out"] + ffn2_names, [*g_ffn1, *g_mix, *g_ffn2]))
    return dx, big_grads, summed


SMALL_NAMES = ["g_ffn1", "g_mix", "g_q", "g_k", "sinks", "b_dw", "g_conv_ln", "b_conv_ln", "g_mem", "g_mq", "g_mk",
               "g_ffn2"]
PACK_COLS = 1024


def _pack(parts):
    flat = [p.reshape(-1) for p in parts]
    offs, o = [], 0
    for f in flat:
        offs.append(o)
        o += f.shape[0]
    rows = -(-o // (8 * PACK_COLS)) * 8
    pad = jnp.zeros((rows * PACK_COLS - o,), F32)
    return jnp.concatenate(flat + [pad]).reshape(rows, PACK_COLS), offs


def _unpack(packed, offs, shapes):
    flat = packed.reshape(-1)
    return [flat[o:o + math.prod(s)].reshape(s) for o, s in zip(offs, shapes)]


def kernel(x, mem, positions, g_ffn1, w_ffn1_gate, w_ffn1_up, w_ffn1_down, g_mix, w_in, g_q, g_k, sinks, w_dw, b_dw, g_conv_ln, b_conv_ln, g_mem, w_mem_kv, g_mq, g_mk, w_out, g_ffn2, w_ffn2_gate, w_ffn2_up, w_ffn2_down, loss_target, m_g_ffn1, m_w_ffn1_gate, m_w_ffn1_up, m_w_ffn1_down, m_g_mix, m_w_in, m_g_q, m_g_k, m_sinks, m_w_dw, m_b_dw, m_g_conv_ln, m_b_conv_ln, m_g_mem, m_w_mem_kv, m_g_mq, m_g_mk, m_w_out, m_g_ffn2, m_w_ffn2_gate, m_w_ffn2_up, m_w_ffn2_down, v_g_ffn1, v_w_ffn1_gate, v_w_ffn1_up, v_w_ffn1_down, v_g_mix, v_w_in, v_g_q, v_g_k, v_sinks, v_w_dw, v_b_dw, v_g_conv_ln, v_b_conv_ln, v_g_mem, v_w_mem_kv, v_g_mq, v_g_mk, v_w_out, v_g_ffn2, v_w_ffn2_gate, v_w_ffn2_up, v_w_ffn2_down):
    args = dict(locals())
    weight_names = ["g_ffn1", "w_ffn1_gate", "w_ffn1_up", "w_ffn1_down", "g_mix", "w_in", "g_q", "g_k", "sinks",
                    "w_dw", "b_dw", "g_conv_ln", "b_conv_ln", "g_mem", "w_mem_kv", "g_mq", "g_mk", "w_out", "g_ffn2",
                    "w_ffn2_gate", "w_ffn2_up", "w_ffn2_down"]
    big_names = ["w_ffn1_gate", "w_ffn1_up", "w_ffn1_down", "w_in", "w_mem_kv", "w_out",
                 "w_ffn2_gate", "w_ffn2_up", "w_ffn2_down"]
    short = dict(w_ffn1_gate="wg1", w_ffn1_up="wu1", w_ffn1_down="wd1", w_in="w_in", w_mem_kv="w_mkv",
                 w_out="w_out", w_ffn2_gate="wg2", w_ffn2_up="wu2", w_ffn2_down="wd2")

    transposed = ("w_ffn1_gate", "w_ffn1_up", "w_ffn2_gate", "w_ffn2_up")

    def quarter(a, n):
        return jnp.swapaxes(a, 1, 2)[0] if n in transposed else a[0]

    def unquarter(a, n):
        return jnp.swapaxes(a[None], 1, 2) if n in transposed else a[None]

    shards = [quarter(args[n], n) for n in big_names]
    layouts = ["cols" if n == "w_in" else "stack" for n in big_names]
    later = [i for i, n in enumerate(big_names) if short[n] not in ("wg1", "wu1", "wd1")]
    gathered = _gather_weights(shards + [w_dw[0]], [BF16] * len(shards) + [F32], layouts + ["whole"], later,
                               "gather_first")
    first = {short[n]: gathered[i] for i, n in enumerate(big_names) if i not in later}
    own = {short[n]: gathered[i] for i, n in enumerate(big_names) if i in later}
    small = {n: args[n] for n in SMALL_NAMES}
    small["w_dw"] = jnp.transpose(gathered[-1], (1, 0, 2)).reshape(CONV_WIDTH, CONV_CH)

    dx, big_grads, summed = _local_step(x[0], mem[0], positions[0], loss_target[0], small, first, own)
    chip = 2 * lax.axis_index("x") + lax.axis_index("y")
    dw_dw_full = summed.pop("w_dw")
    loss_out = summed.pop("loss").reshape(())

    grads = {n: summed[n] for n in SMALL_NAMES}
    grads["w_dw"] = lax.dynamic_slice_in_dim(dw_dw_full, chip * (CONV_CH // N_CHIPS), CONV_CH // N_CHIPS, axis=1)
    for n in big_names:
        grads[n] = big_grads[short[n]]

    delta, new_m, new_v = {}, {}, {}
    ffn_names = [n for n in big_names if "ffn" in n]
    for group, label in [(ffn_names, "adamw_ffn")] + [([n], "adamw_" + short[n]) for n in big_names if "ffn" not in n]:
        results = _adamw([quarter(args[n], n) for n in group], [grads[n] for n in group],
                         [quarter(args["m_" + n], n) for n in group], [quarter(args["v_" + n], n) for n in group], label)
        for n, (g, d, nm, nv) in zip(group, results):
            grads[n], delta[n], new_m[n], new_v[n] = (unquarter(a, n) for a in (g, d, nm, nv))
    tiny = SMALL_NAMES + ["w_dw"]
    pw, poffs = _pack([args[n] for n in tiny])
    pg, _ = _pack([grads[n] for n in tiny])
    pm, _ = _pack([args["m_" + n] for n in tiny])
    pv, _ = _pack([args["v_" + n] for n in tiny])
    ((_, pd, pnm, pnv),) = _adamw([pw], [pg], [pm], [pv], "adamw_small")
    tshapes = [args[n].shape for n in tiny]
    for store, packed_out in ((delta, pd), (new_m, pnm), (new_v, pnv)):
        for n, val in zip(tiny, _unpack(packed_out, poffs, tshapes)):
            store[n] = val

    def shaped(n, v):
        return v.reshape(args[n].shape)

    return (loss_out, dx[None],
            *[shaped(n, grads[n]) for n in weight_names],
            *[shaped(n, delta[n]) for n in weight_names],
            *[shaped(n, new_m[n]) for n in weight_names],
            *[shaped(n, new_v[n]) for n in weight_names])
```

```python
import functools
import math

import jax
import jax.numpy as jnp
from jax import lax
from jax.experimental import pallas as pl
from jax.experimental.pallas import tpu as pltpu

F32 = jnp.float32
BF16 = jnp.bfloat16

D_MODEL = 1024
SEQ = 2048
MEM_LEN = 256
HEAD_DIM = 64
N_Q_HEADS = 8
N_KV_HEADS = 2
Q_PER_KV = 4
N_MEM_HEADS = 4
BLOCK = 128
CONV_CH = 256
CONV_WIDTH = 31
ROPE_THETA = 500000.0
ROPE_DIM = 16
D_FF = 2816
EPS = 1e-6
Q_COLS = 512
KV_COLS = 128
MQ_COLS = 256
IN_COLS = 1536

N_CHIPS = 4
FF_CHUNK = D_FF // N_CHIPS
IN_CHUNK = IN_COLS // N_CHIPS

ADAM_LR = 0.001
ADAM_B1 = 0.9
ADAM_B2 = 0.999
ADAM_EPS = 1e-08
ADAM_WD = 0.01
ADAM_STEP = 10

LANES = 128
VMEM_LIMIT = 56 * 1024 * 1024
ROW_TILE = 512
MESH = pl.DeviceIdType.MESH
NEG = -1e30


class _Comm:
    def __init__(self, ins, out_shapes, n_sems, plan, aliases=None):
        self.ins, self.out_shapes, self.n_sems, self.plan = list(ins), list(out_shapes), n_sems, plan
        self.aliases = aliases or {}


def _pcall(body, comm=None, **kw):
    if comm is None:
        return pl.pallas_call(body, **kw)
    grid = kw["grid"]
    in_specs = list(kw["in_specs"])
    single = not isinstance(kw["out_shape"], (list, tuple))
    out_specs = [kw["out_specs"]] if single else list(kw["out_specs"])
    out_shape = [kw["out_shape"]] if single else list(kw["out_shape"])
    scratch = list(kw.get("scratch_shapes", ()))
    n_in, n_out, n_scr = len(in_specs), len(out_shape), len(scratch)
    n_ci, n_co = len(comm.ins), len(comm.out_shapes)

    def wrapped(*refs):
        o = 0
        parts = []
        for cnt in (n_in, n_ci, n_out, n_co, n_scr):
            parts.append(refs[o:o + cnt])
            o += cnt
        ins, c_ins, outs, c_outs, scr = parts
        sems = refs[o]
        first = last = None
        for d, size in enumerate(grid):
            at0, at_end = pl.program_id(d) == 0, pl.program_id(d) == size - 1
            first = at0 if first is None else first & at0
            last = at_end if last is None else last & at_end

        @pl.when(first)
        def _():
            local, sends, _ = comm.plan(c_ins, c_outs, sems, False)
            for cp in sends + local:
                cp.start()

        body(*ins, *outs, *scr)

        @pl.when(last)
        def _():
            local, sends, recvs = comm.plan(c_ins, c_outs, sems, True)
            for cp in recvs:
                cp.wait_recv()
            for cp in sends:
                cp.wait_send()
            for cp in local:
                cp.wait()

    hbm = pl.BlockSpec(memory_space=pl.ANY)
    kw = dict(kw, in_specs=in_specs + [hbm] * n_ci, out_specs=out_specs + [hbm] * n_co,
              out_shape=out_shape + comm.out_shapes,
              scratch_shapes=scratch + [pltpu.SemaphoreType.DMA((comm.n_sems,))])
    if comm.aliases:
        kw["input_output_aliases"] = {n_in + i: n_out + o for i, o in comm.aliases.items()}
    call = pl.pallas_call(wrapped, **kw)

    def run(*args):
        res = call(*args, *comm.ins)
        return (res[0] if single else list(res[:n_out])), list(res[n_out:])

    return run


def _params(sem=None):
    return pltpu.CompilerParams(dimension_semantics=sem, vmem_limit_bytes=VMEM_LIMIT)


def _dot(a, b):
    return jnp.dot(a, b, preferred_element_type=F32)


def _dot_nt(a, b):
    return lax.dot_general(a, b, (((1,), (1,)), ((), ())), preferred_element_type=F32)


def _dot_tn(a, b):
    return lax.dot_general(a, b, (((0,), (0,)), ((), ())), preferred_element_type=F32)


def _sigmoid(x):
    return 1.0 / (1.0 + jnp.exp(-x))


def _full(shape):
    n = len(shape)
    return pl.BlockSpec(shape, lambda *_: (0,) * n)


def _ffn_fwd(x, g, wg, wu, wd, name, comm=None, target=None, rows=ROW_TILE):
    T, D = x.shape
    tt = rows
    nt = T // tt
    with_loss = target is not None

    def body(*refs):
        if with_loss:
            x_ref, g_ref, wg_ref, wu_ref, wd_ref, t_ref, xo_ref, h_ref, a_ref, b_ref, loss_ref = refs
        else:
            x_ref, g_ref, wg_ref, wu_ref, wd_ref, xo_ref, h_ref, a_ref, b_ref = refs
        t = pl.program_id(0)
        j = pl.program_id(1)

        @pl.when(j == 0)
        def _():
            xv = x_ref[...]
            rstd = lax.rsqrt(jnp.mean(xv * xv, axis=-1, keepdims=True) + EPS)
            h_ref[...] = (xv * rstd * g_ref[...]).astype(BF16)
            xo_ref[...] = jnp.zeros_like(xo_ref)

        h = h_ref[...]
        a = _dot_nt(h, wg_ref[0])
        b = _dot_nt(h, wu_ref[0])
        a_ref[0] = a.astype(BF16)
        b_ref[0] = b.astype(BF16)
        s = (a * _sigmoid(a)) * b
        xo_ref[...] += _dot(s.astype(BF16), wd_ref[0])

        @pl.when(j == N_CHIPS - 1)
        def _():
            out = x_ref[...] + 0.5 * xo_ref[...]
            if with_loss:
                err = out - t_ref[...]
                xo_ref[...] = err * (1.0 / D)
                part = 0.5 * jnp.sum(jnp.mean(err * err, axis=-1, keepdims=True), axis=0, keepdims=True)

                @pl.when(t == 0)
                def _():
                    loss_ref[...] = jnp.zeros_like(loss_ref)

                loss_ref[...] += jnp.broadcast_to(part, loss_ref.shape)
            else:
                xo_ref[...] = out

    tile = pl.BlockSpec((tt, D), lambda t, j: (t, 0))
    chunk = pl.BlockSpec((1, FF_CHUNK, D), lambda t, j: (j, 0, 0))
    act = pl.BlockSpec((1, tt, FF_CHUNK), lambda t, j: (j, t, 0))
    act_shape = jax.ShapeDtypeStruct((N_CHIPS, T, FF_CHUNK), BF16)
    loss_spec = [pl.BlockSpec((1, LANES), lambda t, j: (0, 0))] if with_loss else []
    loss_shape = [jax.ShapeDtypeStruct((1, LANES), F32)] if with_loss else []
    return _pcall(
        body, comm=comm, name=name, grid=(nt, N_CHIPS),
        in_specs=[tile, pl.BlockSpec((1, D), lambda t, j: (0, 0)), chunk, chunk, chunk] + ([tile] if with_loss else []),
        out_specs=[tile, tile, act, act] + loss_spec,
        out_shape=[jax.ShapeDtypeStruct((T, D), F32), jax.ShapeDtypeStruct((T, D), BF16), act_shape, act_shape]
        + loss_shape,
        compiler_params=_params(("arbitrary", "arbitrary")),
    )(x, g, wg, wu, wd, *([target] if with_loss else []))


def _ffn_bwd(dxo, h, a, b, wg, wu, wd, name, comm=None):
    T, D = dxo.shape
    tt = ROW_TILE
    nt = T // tt

    def body(dxo_ref, h_ref, a_ref, b_ref, wg_ref, wu_ref, wd_ref,
             dh_hbm, dwg_ref, dwu_ref, dwd_ref, dh_acc, acc_g, acc_u, acc_d):
        j = pl.program_id(0)
        t = pl.program_id(1)
        do = (0.5 * dxo_ref[...]).astype(BF16)
        av = a_ref[0].astype(F32)
        bv = b_ref[0].astype(F32)
        sig = _sigmoid(av)
        sa = av * sig
        ds = _dot_nt(do, wd_ref[0])
        da = (ds * bv * (sig * (1.0 + av * (1.0 - sig)))).astype(BF16)
        db = (ds * sa).astype(BF16)
        hv = h_ref[...]
        rows = pl.ds(pl.multiple_of(t * tt, tt), tt)

        @pl.when(j == 0)
        def _():
            dh_acc[rows, :] = jnp.zeros((tt, D), F32)

        @pl.when(t == 0)
        def _():
            acc_g[...] = jnp.zeros_like(acc_g)
            acc_u[...] = jnp.zeros_like(acc_u)
            acc_d[...] = jnp.zeros_like(acc_d)

        acc_d[...] += _dot_tn((sa * bv).astype(BF16), do)
        acc_g[...] += _dot_tn(da, hv)
        acc_u[...] += _dot_tn(db, hv)
        dh_acc[rows, :] += _dot(da, wg_ref[0]) + _dot(db, wu_ref[0])

        @pl.when(t == nt - 1)
        def _():
            dwg_ref[0] = acc_g[...].astype(BF16)
            dwu_ref[0] = acc_u[...].astype(BF16)
            dwd_ref[0] = acc_d[...].astype(BF16)

        @pl.when((t == nt - 1) & (j == N_CHIPS - 1))
        def _():
            pltpu.sync_copy(dh_acc, dh_hbm)

    return _pcall(
        body, comm=comm, name=name, grid=(N_CHIPS, nt),
        in_specs=[
            pl.BlockSpec((tt, D), lambda j, t: (t, 0)),
            pl.BlockSpec((tt, D), lambda j, t: (t, 0)),
            pl.BlockSpec((1, tt, FF_CHUNK), lambda j, t: (j, t, 0)),
            pl.BlockSpec((1, tt, FF_CHUNK), lambda j, t: (j, t, 0)),
            pl.BlockSpec((1, FF_CHUNK, D), lambda j, t: (j, 0, 0)),
            pl.BlockSpec((1, FF_CHUNK, D), lambda j, t: (j, 0, 0)),
            pl.BlockSpec((1, FF_CHUNK, D), lambda j, t: (j, 0, 0)),
        ],
        out_specs=[
            pl.BlockSpec(memory_space=pl.ANY),
            pl.BlockSpec((1, FF_CHUNK, D), lambda j, t: (j, 0, 0)),
            pl.BlockSpec((1, FF_CHUNK, D), lambda j, t: (j, 0, 0)),
            pl.BlockSpec((1, FF_CHUNK, D), lambda j, t: (j, 0, 0)),
        ],
        out_shape=[
            jax.ShapeDtypeStruct((T, D), F32),
            jax.ShapeDtypeStruct((N_CHIPS, FF_CHUNK, D), BF16),
            jax.ShapeDtypeStruct((N_CHIPS, FF_CHUNK, D), BF16),
            jax.ShapeDtypeStruct((N_CHIPS, FF_CHUNK, D), BF16),
        ],
        scratch_shapes=[
            pltpu.VMEM((T, D), F32),
            pltpu.VMEM((FF_CHUNK, D), F32),
            pltpu.VMEM((FF_CHUNK, D), F32),
            pltpu.VMEM((FF_CHUNK, D), F32),
        ],
        compiler_params=_params(("arbitrary", "arbitrary")),
    )(dxo, h, a, b, wg, wu, wd)


def _rms_fwd(x, g, name, comm=None):
    T, D = x.shape
    tt = min(ROW_TILE, T)

    def body(x_ref, g_ref, h_ref):
        xv = x_ref[...]
        rstd = lax.rsqrt(jnp.mean(xv * xv, axis=-1, keepdims=True) + EPS)
        h_ref[...] = (xv * rstd * g_ref[...]).astype(BF16)

    return _pcall(
        body, comm=comm, name=name, grid=(T // tt,),
        in_specs=[pl.BlockSpec((tt, D), lambda t: (t, 0)), pl.BlockSpec((1, D), lambda t: (0, 0))],
        out_specs=pl.BlockSpec((tt, D), lambda t: (t, 0)),
        out_shape=jax.ShapeDtypeStruct((T, D), BF16),
        compiler_params=_params(("arbitrary",)),
    )(x, g)


def _rms_bwd(x, g, dh, dres, name, comm=None):
    T, D = x.shape
    tt = min(ROW_TILE, T)
    has_res = dres is not None

    def body(*refs):
        if has_res:
            x_ref, g_ref, dh_ref, dres_ref, dx_ref, dg_ref = refs
        else:
            x_ref, g_ref, dh_ref, dx_ref, dg_ref = refs
        t = pl.program_id(0)
        xv = x_ref[...]
        rstd = lax.rsqrt(jnp.mean(xv * xv, axis=-1, keepdims=True) + EPS)
        xhat = xv * rstd
        dhv = dh_ref[...]
        gy = dhv * g_ref[...]
        dx = rstd * (gy - xhat * jnp.mean(gy * xhat, axis=-1, keepdims=True))
        if has_res:
            dx = dx + dres_ref[...]
        dx_ref[...] = dx
        part = jnp.sum(dhv * xhat, axis=0, keepdims=True)

        @pl.when(t == 0)
        def _():
            dg_ref[...] = part

        @pl.when(t > 0)
        def _():
            dg_ref[...] += part

    tile = pl.BlockSpec((tt, D), lambda t: (t, 0))
    vec = pl.BlockSpec((1, D), lambda t: (0, 0))
    args = [x, g, dh] + ([dres] if has_res else [])
    return _pcall(
        body, comm=comm, name=name, grid=(T // tt,),
        in_specs=[tile, vec, tile] + ([tile] if has_res else []),
        out_specs=[tile, vec],
        out_shape=[jax.ShapeDtypeStruct((T, D), F32), jax.ShapeDtypeStruct((1, D), F32)],
        compiler_params=_params(("arbitrary",)),
    )(*args)


def _mm_nn(a_list, b, res, name, comm=None):
    T = a_list[0].shape[0]
    K, N = b.shape
    tt = min(ROW_TILE, T)
    ks = [a.shape[1] for a in a_list]
    na = len(a_list)
    has_res = res is not None

    def body(*refs):
        a_refs = refs[:na]
        b_ref = refs[na]
        o_ref = refs[-1]
        acc = res_v = None
        off = 0
        for a_ref, k in zip(a_refs, ks):
            part = _dot(a_ref[...].astype(BF16), b_ref[off:off + k, :])
            acc = part if acc is None else acc + part
            off += k
        if has_res:
            acc = refs[na + 1][...] + acc
        o_ref[...] = acc

    in_specs = [pl.BlockSpec((tt, k), lambda t: (t, 0)) for k in ks] + [pl.BlockSpec((K, N), lambda t: (0, 0))]
    args = list(a_list) + [b]
    if has_res:
        in_specs.append(pl.BlockSpec((tt, N), lambda t: (t, 0)))
        args.append(res)
    return _pcall(
        body, comm=comm, name=name, grid=(T // tt,), in_specs=in_specs,
        out_specs=pl.BlockSpec((tt, N), lambda t: (t, 0)),
        out_shape=jax.ShapeDtypeStruct((T, N), F32),
        compiler_params=_params(("arbitrary",)),
    )(*args)


def _mm_nt(a_list, b, name, comm=None):
    T = a_list[0].shape[0]
    K, N = b.shape
    tt = min(ROW_TILE, T)
    ns = [a.shape[1] for a in a_list]
    na = len(a_list)

    def body(*refs):
        b_ref = refs[na]
        o_ref = refs[-1]
        acc = None
        off = 0
        for a_ref, n in zip(refs[:na], ns):
            part = _dot_nt(a_ref[...].astype(BF16), b_ref[:, off:off + n])
            acc = part if acc is None else acc + part
            off += n
        o_ref[...] = acc

    return _pcall(
        body, comm=comm, name=name, grid=(T // tt,),
        in_specs=[pl.BlockSpec((tt, n), lambda t: (t, 0)) for n in ns] + [pl.BlockSpec((K, N), lambda t: (0, 0))],
        out_specs=pl.BlockSpec((tt, K), lambda t: (t, 0)),
        out_shape=jax.ShapeDtypeStruct((T, K), F32),
        compiler_params=_params(("arbitrary",)),
    )(*a_list, b)


def _mm_tn(a_list, b_list, col_chunks, name, comm=None):
    T = a_list[0].shape[0]
    tt = min(ROW_TILE, T)
    nt = T // tt
    ms = [a.shape[1] for a in a_list]
    ns = [b.shape[1] for b in b_list]
    M, N = sum(ms), sum(ns)
    na, nb = len(a_list), len(b_list)
    cw = N // col_chunks

    def body(*refs):
        a_refs, b_refs = refs[:na], refs[na:na + nb]
        o_ref, acc = refs[na + nb], refs[na + nb + 1]
        t = pl.program_id(0)

        @pl.when(t == 0)
        def _():
            acc[...] = jnp.zeros_like(acc)

        ro = 0
        for a_ref, m in zip(a_refs, ms):
            av = a_ref[...].astype(BF16)
            co = 0
            for b_ref, n in zip(b_refs, ns):
                acc[ro:ro + m, co:co + n] += _dot_tn(av, b_ref[...].astype(BF16))
                co += n
            ro += m

        @pl.when(t == nt - 1)
        def _():
            if col_chunks == 1:
                o_ref[...] = acc[...].astype(BF16)
            else:
                for q in range(col_chunks):
                    o_ref[q] = acc[:, q * cw:(q + 1) * cw].astype(BF16)

    out_shape = (M, N) if col_chunks == 1 else (col_chunks, M, cw)
    return _pcall(
        body, comm=comm, name=name, grid=(nt,),
        in_specs=[pl.BlockSpec((tt, m), lambda t: (t, 0)) for m in ms]
        + [pl.BlockSpec((tt, n), lambda t: (t, 0)) for n in ns],
        out_specs=_full(out_shape),
        out_shape=jax.ShapeDtypeStruct(out_shape, BF16),
        scratch_shapes=[pltpu.VMEM((M, N), F32)],
        compiler_params=_params(("arbitrary",)),
    )(*a_list, *b_list)


def _head_masks():
    lane = lax.broadcasted_iota(jnp.int32, (1, LANES), 1)
    l64 = lane & (HEAD_DIM - 1)
    return lane < HEAD_DIM, l64 < ROPE_DIM // 2, l64 < ROPE_DIM


def _head_mean(v, lo):
    s_lo = jnp.sum(jnp.where(lo, v, 0.0), axis=-1, keepdims=True)
    s_hi = jnp.sum(jnp.where(lo, 0.0, v), axis=-1, keepdims=True)
    return jnp.where(lo, s_lo, s_hi) * (1.0 / HEAD_DIM)


def _rope_swap(v, first, rot):
    up = pltpu.roll(v, LANES - ROPE_DIM // 2, 1)
    down = pltpu.roll(v, ROPE_DIM // 2, 1)
    return jnp.where(first, up, jnp.where(rot, down, 0.0))


def _head_norm(x, g, lo):
    rstd = lax.rsqrt(_head_mean(x * x, lo) + EPS)
    return x * rstd * g


def _head_norm_bwd(x, g, dy, lo):
    rstd = lax.rsqrt(_head_mean(x * x, lo) + EPS)
    xhat = x * rstd
    gy = dy * g
    dx = rstd * (gy - xhat * _head_mean(gy * xhat, lo))
    return dx, dy * xhat


def _rope(xn, cos, sin, first, rot):
    return xn * cos + _rope_swap(xn, first, rot) * sin


def _rope_bwd(dy, cos, sin, first, rot):
    return dy * cos + _rope_swap(dy * sin, first, rot)


def _fold_heads(v):
    return v + pltpu.roll(v, HEAD_DIM, 1)


ATT_ROWS = 512


def _attn_prepare(q_ref, k_ref, v_ref, cos_ref, sin_ref, gq_ref, gk_ref, qs, ks, vs):
    T = q_ref.shape[0]
    lo, first, rot = _head_masks()
    ks[0:BLOCK, :] = jnp.zeros((BLOCK, KV_COLS), BF16)
    vs[0:BLOCK, :] = jnp.zeros((BLOCK, KV_COLS), BF16)

    def step(i, _):
        r0 = pl.multiple_of(i * ATT_ROWS, ATT_ROWS)
        rows = pl.ds(r0, ATT_ROWS)
        prow = pl.ds(r0 + BLOCK, ATT_ROWS)
        cos, sin = cos_ref[rows, :], sin_ref[rows, :]
        for p in range(Q_COLS // LANES):
            cols = slice(p * LANES, (p + 1) * LANES)
            xr = _rope(_head_norm(q_ref[rows, cols], gq_ref[...], lo), cos, sin, first, rot)
            qs[rows, cols] = (xr * (HEAD_DIM ** -0.5)).astype(BF16)
        kr = _rope(_head_norm(k_ref[rows, :], gk_ref[...], lo), cos, sin, first, rot)
        ks[prow, :] = kr.astype(BF16)
        vs[prow, :] = v_ref[rows, :].astype(BF16)
        return 0

    lax.fori_loop(0, T // ATT_ROWS, step, 0)


GROUP_ROWS = Q_PER_KV * BLOCK


def _group_rows(ref, r0, g, cast=None):
    parts = []
    for r in range(Q_PER_KV):
        h = g * Q_PER_KV + r
        part = ref[pl.ds(r0, BLOCK), h * HEAD_DIM:(h + 1) * HEAD_DIM]
        parts.append(part if cast is None else part.astype(cast))
    return jnp.concatenate(parts, axis=0)


def _group_sinks(sink_ref, g):
    row = lax.broadcasted_iota(jnp.int32, (GROUP_ROWS, 1), 0)
    col = jnp.full((GROUP_ROWS, 1), sink_ref[0, g * Q_PER_KV], F32)
    for r in range(1, Q_PER_KV):
        col = jnp.where(row >= r * BLOCK, sink_ref[0, g * Q_PER_KV + r], col)
    return col


def _attn_scores(qg, kw, blk, sink):
    s = _dot_nt(qg, kw)
    qi = (lax.broadcasted_iota(jnp.int32, (GROUP_ROWS, 2 * BLOCK), 0) & (BLOCK - 1)) + BLOCK
    ki = lax.broadcasted_iota(jnp.int32, (GROUP_ROWS, 2 * BLOCK), 1)
    rel = qi - ki
    valid = (rel >= 0) & (rel < BLOCK) & ((blk > 0) | (ki >= BLOCK))
    s = jnp.where(valid, s, NEG)
    m = jnp.maximum(jnp.max(s, axis=-1, keepdims=True), sink)
    p = jnp.exp(s - m)
    e_sink = jnp.exp(sink - m)
    inv = 1.0 / (jnp.sum(p, axis=-1, keepdims=True) + e_sink)
    return p * inv, e_sink * inv


def _attn_fwd(proj, cos, sin, gq2, gk2, sinks, name, comm=None):
    T = proj.shape[0]
    nb = T // BLOCK

    def body(q_ref, k_ref, v_ref, cos_ref, sin_ref, gq_ref, gk_ref, sink_ref, y_ref, qs, ks, vs):
        _attn_prepare(q_ref, k_ref, v_ref, cos_ref, sin_ref, gq_ref, gk_ref, qs, ks, vs)

        def blk_step(blk, _):
            r0 = pl.multiple_of(blk * BLOCK, BLOCK)
            for g in range(N_KV_HEADS):
                gc = slice(g * HEAD_DIM, (g + 1) * HEAD_DIM)
                kw = ks[pl.ds(r0, 2 * BLOCK), gc]
                vw = vs[pl.ds(r0, 2 * BLOCK), gc]
                w, _ws = _attn_scores(_group_rows(qs, r0, g), kw, blk, _group_sinks(sink_ref, g))
                o = _dot(w.astype(BF16), vw).astype(BF16)
                for r in range(Q_PER_KV):
                    h = g * Q_PER_KV + r
                    y_ref[pl.ds(r0, BLOCK), h * HEAD_DIM:(h + 1) * HEAD_DIM] = o[r * BLOCK:(r + 1) * BLOCK, :]
            return 0

        lax.fori_loop(0, nb, blk_step, 0)

    return _pcall(
        body, comm=comm, name=name, grid=(1,),
        in_specs=[
            pl.BlockSpec((T, Q_COLS), lambda i: (0, 0)),
            pl.BlockSpec((T, KV_COLS), lambda i: (0, Q_COLS // KV_COLS)),
            pl.BlockSpec((T, KV_COLS), lambda i: (0, Q_COLS // KV_COLS + 1)),
            _full((T, LANES)), _full((T, LANES)), _full((1, LANES)), _full((1, LANES)),
            pl.BlockSpec(memory_space=pltpu.SMEM),
        ],
        out_specs=_full((T, Q_COLS)),
        out_shape=jax.ShapeDtypeStruct((T, Q_COLS), BF16),
        scratch_shapes=[
            pltpu.VMEM((T, Q_COLS), BF16),
            pltpu.VMEM((T + BLOCK, KV_COLS), BF16),
            pltpu.VMEM((T + BLOCK, KV_COLS), BF16),
        ],
        compiler_params=_params(("arbitrary",)),
    )(proj, proj, proj, cos, sin, gq2, gk2, sinks)


def _attn_bwd(proj, cos, sin, gq2, gk2, sinks, dyc, name, comm=None):
    T = proj.shape[0]
    nb = T // BLOCK

    def body(q_ref, k_ref, v_ref, cos_ref, sin_ref, gq_ref, gk_ref, sink_ref, dy_ref,
             dq_ref, dk_ref, dv_ref, dgq_ref, dgk_ref, dsink_ref, qs, ks, vs, dqs, dks, dvs):
        _attn_prepare(q_ref, k_ref, v_ref, cos_ref, sin_ref, gq_ref, gk_ref, qs, ks, vs)
        dks[...] = jnp.zeros_like(dks)
        dvs[...] = jnp.zeros_like(dvs)
        lane = lax.broadcasted_iota(jnp.int32, (1, LANES), 1)

        def blk_step(blk, dsink):
            r0 = pl.multiple_of(blk * BLOCK, BLOCK)
            win = pl.ds(r0, 2 * BLOCK)
            for g in range(N_KV_HEADS):
                gc = slice(g * HEAD_DIM, (g + 1) * HEAD_DIM)
                kw = ks[win, gc]
                vw = vs[win, gc]
                qg = _group_rows(qs, r0, g)
                w, w_sink = _attn_scores(qg, kw, blk, _group_sinks(sink_ref, g))
                do = _group_rows(dy_ref, r0, g, cast=BF16)
                dvs[win, gc] += _dot_tn(w.astype(BF16), do)
                dw = _dot_nt(do, vw)
                delta = jnp.sum(w * dw, axis=-1, keepdims=True)
                ds = (w * (dw - delta)).astype(BF16)
                sink_part = w_sink * delta
                dq = _dot(ds, kw)
                for r in range(Q_PER_KV):
                    h = g * Q_PER_KV + r
                    slab = slice(r * BLOCK, (r + 1) * BLOCK)
                    dsink = dsink + jnp.where(lane == h, -jnp.sum(sink_part[slab, :], axis=0, keepdims=True), 0.0)
                    dqs[pl.ds(r0, BLOCK), h * HEAD_DIM:(h + 1) * HEAD_DIM] = dq[slab, :]
                dks[win, gc] += _dot_tn(ds, qg)
            return dsink

        dsink_ref[...] = lax.fori_loop(0, nb, blk_step, jnp.zeros((1, LANES), F32))

        lo, first, rot = _head_masks()

        def step(i, carry):
            dgq, dgk = carry
            r0 = pl.multiple_of(i * ATT_ROWS, ATT_ROWS)
            rows = pl.ds(r0, ATT_ROWS)
            prow = pl.ds(r0 + BLOCK, ATT_ROWS)
            cos, sin = cos_ref[rows, :], sin_ref[rows, :]
            for p in range(Q_COLS // LANES):
                cols = slice(p * LANES, (p + 1) * LANES)
                dxn = _rope_bwd(dqs[rows, cols] * (HEAD_DIM ** -0.5), cos, sin, first, rot)
                dx, dgp = _head_norm_bwd(q_ref[rows, cols], gq_ref[...], dxn, lo)
                dq_ref[rows, cols] = dx
                dgq = dgq + jnp.sum(dgp, axis=0, keepdims=True)
            dkn = _rope_bwd(dks[prow, :], cos, sin, first, rot)
            dx, dgp = _head_norm_bwd(k_ref[rows, :], gk_ref[...], dkn, lo)
            dk_ref[rows, :] = dx
            dgk = dgk + jnp.sum(dgp, axis=0, keepdims=True)
            dv_ref[rows, :] = dvs[prow, :]
            return dgq, dgk

        zero = jnp.zeros((1, LANES), F32)
        dgq, dgk = lax.fori_loop(0, T // ATT_ROWS, step, (zero, zero))
        dgq_ref[...] = _fold_heads(dgq)
        dgk_ref[...] = _fold_heads(dgk)

    vec = jax.ShapeDtypeStruct((1, LANES), F32)
    return _pcall(
        body, comm=comm, name=name, grid=(1,),
        in_specs=[
            pl.BlockSpec((T, Q_COLS), lambda i: (0, 0)),
            pl.BlockSpec((T, KV_COLS), lambda i: (0, Q_COLS // KV_COLS)),
            pl.BlockSpec((T, KV_COLS), lambda i: (0, Q_COLS // KV_COLS + 1)),
            _full((T, LANES)), _full((T, LANES)), _full((1, LANES)), _full((1, LANES)),
            pl.BlockSpec(memory_space=pltpu.SMEM),
            pl.BlockSpec((T, Q_COLS), lambda i: (0, 0)),
        ],
        out_specs=[_full((T, Q_COLS)), _full((T, KV_COLS)), _full((T, KV_COLS)),
                   _full((1, LANES)), _full((1, LANES)), _full((1, LANES))],
        out_shape=[jax.ShapeDtypeStruct((T, Q_COLS), F32), jax.ShapeDtypeStruct((T, KV_COLS), F32),
                   jax.ShapeDtypeStruct((T, KV_COLS), F32), vec, vec, vec],
        scratch_shapes=[
            pltpu.VMEM((T, Q_COLS), BF16),
            pltpu.VMEM((T + BLOCK, KV_COLS), BF16),
            pltpu.VMEM((T + BLOCK, KV_COLS), BF16),
            pltpu.VMEM((T, Q_COLS), F32),
            pltpu.VMEM((T + BLOCK, KV_COLS), F32),
            pltpu.VMEM((T + BLOCK, KV_COLS), F32),
        ],
        compiler_params=_params(("arbitrary",)),
    )(proj, proj, proj, cos, sin, gq2, gk2, sinks, dyc)


CONV_PAD = 32
CONV_ROWS = 256


def _conv_taps(src, w_ref, r0, first_off, step_sign):
    acc = None
    for i in range(CONV_WIDTH):
        term = w_ref[i:i + 1, :] * src[r0 + first_off + step_sign * i:r0 + first_off + step_sign * i + CONV_ROWS, :]
        acc = term if acc is None else acc + term
    return acc


def _conv_fwd(proj, w_dw, b_dw, g_ln, b_ln, name, comm=None):
    T = proj.shape[0]
    a_blk = (Q_COLS + 2 * KV_COLS) // CONV_CH

    def body(a_ref, gate_ref, w_ref, bdw_ref, g_ref, b_ref, y_ref, c_ref, pad):
        pad[0:CONV_PAD, :] = jnp.zeros((CONV_PAD, CONV_CH), F32)
        pad[CONV_PAD:, :] = a_ref[...] * _sigmoid(gate_ref[...])
        for n in range(T // CONV_ROWS):
            r0 = n * CONV_ROWS
            c = _conv_taps(pad, w_ref, r0, CONV_PAD - (CONV_WIDTH - 1), 1) + bdw_ref[...]
            c_ref[r0:r0 + CONV_ROWS, :] = c
            mu = jnp.mean(c, axis=-1, keepdims=True)
            cc = c - mu
            rstd = lax.rsqrt(jnp.mean(cc * cc, axis=-1, keepdims=True) + EPS)
            z = cc * rstd * g_ref[...] + b_ref[...]
            y_ref[r0:r0 + CONV_ROWS, :] = (z * _sigmoid(z)).astype(BF16)

    vec = _full((1, CONV_CH))
    return _pcall(
        body, comm=comm, name=name, grid=(1,),
        in_specs=[
            pl.BlockSpec((T, CONV_CH), lambda i: (0, a_blk)),
            pl.BlockSpec((T, CONV_CH), lambda i: (0, a_blk + 1)),
            _full((CONV_WIDTH, CONV_CH)), vec, vec, vec,
        ],
        out_specs=[_full((T, CONV_CH)), _full((T, CONV_CH))],
        out_shape=[jax.ShapeDtypeStruct((T, CONV_CH), BF16), jax.ShapeDtypeStruct((T, CONV_CH), F32)],
        scratch_shapes=[pltpu.VMEM((T + CONV_PAD, CONV_CH), F32)],
        compiler_params=_params(("arbitrary",)),
    )(proj, proj, w_dw, b_dw, g_ln, b_ln)


def _conv_bwd(proj, c, w_dw, g_ln, b_ln, dyc, name, comm=None):
    T = proj.shape[0]
    a_blk = (Q_COLS + 2 * KV_COLS) // CONV_CH
    y_blk = Q_COLS // CONV_CH

    def body(a_ref, gate_ref, c_ref, w_ref, g_ref, b_ref, dy_ref,
             da_ref, dgate_ref, dw_ref, dbdw_ref, dg_ref, db_ref, pad, dcp):
        pad[0:CONV_PAD, :] = jnp.zeros((CONV_PAD, CONV_CH), F32)
        sg = _sigmoid(gate_ref[...])
        pad[CONV_PAD:, :] = a_ref[...] * sg
        dcp[T:, :] = jnp.zeros((CONV_PAD, CONV_CH), F32)
        dg = db = dbdw = jnp.zeros((1, CONV_CH), F32)
        for n in range(T // CONV_ROWS):
            rows = slice(n * CONV_ROWS, (n + 1) * CONV_ROWS)
            cv = c_ref[rows, :]
            mu = jnp.mean(cv, axis=-1, keepdims=True)
            cc = cv - mu
            rstd = lax.rsqrt(jnp.mean(cc * cc, axis=-1, keepdims=True) + EPS)
            chat = cc * rstd
            z = chat * g_ref[...] + b_ref[...]
            sz = _sigmoid(z)
            dz = dy_ref[rows, :] * (sz * (1.0 + z * (1.0 - sz)))
            dg = dg + jnp.sum(dz * chat, axis=0, keepdims=True)
            db = db + jnp.sum(dz, axis=0, keepdims=True)
            dch = dz * g_ref[...]
            dc = rstd * (dch - jnp.mean(dch, axis=-1, keepdims=True)
                         - chat * jnp.mean(dch * chat, axis=-1, keepdims=True))
            dbdw = dbdw + jnp.sum(dc, axis=0, keepdims=True)
            dcp[rows, :] = dc
        dg_ref[...] = dg
        db_ref[...] = db
        dbdw_ref[...] = dbdw
        dw_ref[CONV_WIDTH:, :] = jnp.zeros((CONV_PAD - CONV_WIDTH, CONV_CH), F32)
        for i in range(CONV_WIDTH):
            off = CONV_PAD - (CONV_WIDTH - 1) + i
            acc = jnp.zeros((1, CONV_CH), F32)
            for n in range(T // CONV_ROWS):
                r0 = n * CONV_ROWS
                acc = acc + jnp.sum(dcp[r0:r0 + CONV_ROWS, :] * pad[r0 + off:r0 + off + CONV_ROWS, :],
                                    axis=0, keepdims=True)
            dw_ref[i:i + 1, :] = acc
        for n in range(T // CONV_ROWS):
            r0 = n * CONV_ROWS
            rows = slice(r0, r0 + CONV_ROWS)
            dhg = _conv_taps(dcp, w_ref, r0, CONV_WIDTH - 1, -1)
            sgv = sg[rows, :]
            da_ref[rows, :] = dhg * sgv
            dgate_ref[rows, :] = dhg * a_ref[rows, :] * sgv * (1.0 - sgv)

    vec = _full((1, CONV_CH))
    vshape = jax.ShapeDtypeStruct((1, CONV_CH), F32)
    return _pcall(
        body, comm=comm, name=name, grid=(1,),
        in_specs=[
            pl.BlockSpec((T, CONV_CH), lambda i: (0, a_blk)),
            pl.BlockSpec((T, CONV_CH), lambda i: (0, a_blk + 1)),
            _full((T, CONV_CH)), _full((CONV_WIDTH, CONV_CH)), vec, vec,
            pl.BlockSpec((T, CONV_CH), lambda i: (0, y_blk)),
        ],
        out_specs=[_full((T, CONV_CH)), _full((T, CONV_CH)), _full((CONV_PAD, CONV_CH)), vec, vec, vec],
        out_shape=[jax.ShapeDtypeStruct((T, CONV_CH), F32), jax.ShapeDtypeStruct((T, CONV_CH), F32),
                   jax.ShapeDtypeStruct((CONV_PAD, CONV_CH), F32), vshape, vshape, vshape],
        scratch_shapes=[pltpu.VMEM((T + CONV_PAD, CONV_CH), F32), pltpu.VMEM((T + CONV_PAD, CONV_CH), F32)],
        compiler_params=_params(("arbitrary",)),
    )(proj, proj, c, w_dw, g_ln, b_ln, dyc)


def _mem_kv(mkv_ref, gk_ref, lo, kn_s, vv_s):
    for p in range(MQ_COLS // LANES):
        cols = slice(p * LANES, (p + 1) * LANES)
        kn_s[:, cols] = _head_norm(mkv_ref[:, cols], gk_ref[...], lo).astype(BF16)
    vv_s[...] = mkv_ref[:, MQ_COLS:].astype(BF16)


def _mem_softmax(qh, kh):
    s = _dot_nt(qh, kh)
    m = jnp.max(s, axis=-1, keepdims=True)
    p = jnp.exp(s - m)
    return p / jnp.sum(p, axis=-1, keepdims=True)


def _mem_fwd(proj, mkv, gq2, gk2, name, comm=None):
    T = proj.shape[0]
    tt = ROW_TILE
    q_blk = (IN_COLS - MQ_COLS) // MQ_COLS

    def body(q_ref, mkv_ref, gq_ref, gk_ref, y_ref, kn_s, vv_s, qn_s):
        lo, _, _ = _head_masks()
        _mem_kv(mkv_ref, gk_ref, lo, kn_s, vv_s)
        for p in range(MQ_COLS // LANES):
            cols = slice(p * LANES, (p + 1) * LANES)
            qn_s[:, cols] = (_head_norm(q_ref[:, cols], gq_ref[...], lo) * (HEAD_DIM ** -0.5)).astype(BF16)
        for h in range(N_MEM_HEADS):
            hc = slice(h * HEAD_DIM, (h + 1) * HEAD_DIM)
            w = _mem_softmax(qn_s[:, hc], kn_s[:, hc])
            y_ref[:, hc] = _dot(w.astype(BF16), vv_s[:, hc]).astype(BF16)

    return _pcall(
        body, comm=comm, name=name, grid=(T // tt,),
        in_specs=[
            pl.BlockSpec((tt, MQ_COLS), lambda t: (t, q_blk)),
            pl.BlockSpec((MEM_LEN, 2 * MQ_COLS), lambda t: (0, 0)),
            pl.BlockSpec((1, LANES), lambda t: (0, 0)), pl.BlockSpec((1, LANES), lambda t: (0, 0)),
        ],
        out_specs=pl.BlockSpec((tt, MQ_COLS), lambda t: (t, 0)),
        out_shape=jax.ShapeDtypeStruct((T, MQ_COLS), BF16),
        scratch_shapes=[pltpu.VMEM((MEM_LEN, MQ_COLS), BF16), pltpu.VMEM((MEM_LEN, MQ_COLS), BF16),
                        pltpu.VMEM((tt, MQ_COLS), BF16)],
        compiler_params=_params(("arbitrary",)),
    )(proj, mkv, gq2, gk2)


def _mem_bwd(proj, mkv, gq2, gk2, dyc, name, comm=None):
    T = proj.shape[0]
    tt = 2 * ROW_TILE
    nt = T // tt
    q_blk = (IN_COLS - MQ_COLS) // MQ_COLS
    y_blk = (Q_COLS + CONV_CH) // MQ_COLS

    def body(q_ref, mkv_ref, gq_ref, gk_ref, dy_ref, dq_ref, dmkv_ref, dgq_ref, dgk_ref,
             kn_s, vv_s, qn_s, dqn_s, dkn_acc):
        t = pl.program_id(0)
        lo, _, _ = _head_masks()
        _mem_kv(mkv_ref, gk_ref, lo, kn_s, vv_s)

        @pl.when(t == 0)
        def _():
            dkn_acc[...] = jnp.zeros_like(dkn_acc)
            dmkv_ref[...] = jnp.zeros_like(dmkv_ref)
            dgq_ref[...] = jnp.zeros_like(dgq_ref)

        for p in range(MQ_COLS // LANES):
            cols = slice(p * LANES, (p + 1) * LANES)
            qn_s[:, cols] = (_head_norm(q_ref[:, cols], gq_ref[...], lo) * (HEAD_DIM ** -0.5)).astype(BF16)
        for h in range(N_MEM_HEADS):
            hc = slice(h * HEAD_DIM, (h + 1) * HEAD_DIM)
            vc = slice(MQ_COLS + h * HEAD_DIM, MQ_COLS + (h + 1) * HEAD_DIM)
            qh = qn_s[:, hc]
            w = _mem_softmax(qh, kn_s[:, hc])
            do = dy_ref[:, hc].astype(BF16)
            dmkv_ref[:, vc] += _dot_tn(w.astype(BF16), do)
            dw = _dot_nt(do, vv_s[:, hc])
            ds = (w * (dw - jnp.sum(w * dw, axis=-1, keepdims=True))).astype(BF16)
            dqn_s[:, hc] = _dot(ds, kn_s[:, hc])
            dkn_acc[:, hc] += _dot_tn(ds, qh)
        dgq = jnp.zeros((1, LANES), F32)
        for p in range(MQ_COLS // LANES):
            cols = slice(p * LANES, (p + 1) * LANES)
            dx, dgp = _head_norm_bwd(q_ref[:, cols], gq_ref[...], dqn_s[:, cols] * (HEAD_DIM ** -0.5), lo)
            dq_ref[:, cols] = dx
            dgq = dgq + jnp.sum(dgp, axis=0, keepdims=True)
        dgq_ref[...] += dgq

        @pl.when(t == nt - 1)
        def _():
            dgk = jnp.zeros((1, LANES), F32)
            for p in range(MQ_COLS // LANES):
                cols = slice(p * LANES, (p + 1) * LANES)
                dx, dgp = _head_norm_bwd(mkv_ref[:, cols], gk_ref[...], dkn_acc[:, cols], lo)
                dmkv_ref[:, cols] = dx
                dgk = dgk + jnp.sum(dgp, axis=0, keepdims=True)
            dgk_ref[...] = _fold_heads(dgk)
            dgq_ref[...] = _fold_heads(dgq_ref[...])

    vec = pl.BlockSpec((1, LANES), lambda t: (0, 0))
    vshape = jax.ShapeDtypeStruct((1, LANES), F32)
    return _pcall(
        body, comm=comm, name=name, grid=(nt,),
        in_specs=[
            pl.BlockSpec((tt, MQ_COLS), lambda t: (t, q_blk)),
            pl.BlockSpec((MEM_LEN, 2 * MQ_COLS), lambda t: (0, 0)),
            vec, vec,
            pl.BlockSpec((tt, MQ_COLS), lambda t: (t, y_blk)),
        ],
        out_specs=[pl.BlockSpec((tt, MQ_COLS), lambda t: (t, 0)),
                   pl.BlockSpec((MEM_LEN, 2 * MQ_COLS), lambda t: (0, 0)), vec, vec],
        out_shape=[jax.ShapeDtypeStruct((T, MQ_COLS), F32), jax.ShapeDtypeStruct((MEM_LEN, 2 * MQ_COLS), F32),
                   vshape, vshape],
        scratch_shapes=[pltpu.VMEM((MEM_LEN, MQ_COLS), BF16), pltpu.VMEM((MEM_LEN, MQ_COLS), BF16),
                        pltpu.VMEM((tt, MQ_COLS), BF16), pltpu.VMEM((tt, MQ_COLS), F32),
                        pltpu.VMEM((MEM_LEN, MQ_COLS), F32)],
        compiler_params=_params(("arbitrary",)),
    )(proj, mkv, gq2, gk2, dyc)


ADAMW_TILE_BYTES = 5 << 19


def _adamw(ws, gs, ms, vs, name, comm=None):
    n = len(ws)
    R, C = ws[0].shape
    budget = ADAMW_TILE_BYTES // (4 * C * n)
    tr = next((r for r in (512, 352, 256, 176, 128, 88, 64, 32, 16, 8) if R % r == 0 and r <= max(budget, 8)), R)

    def body(*refs):
        ins, outs = refs[:4 * n], refs[4 * n:]
        for i in range(n):
            w_ref, g_ref, m_ref, v_ref = ins[i], ins[n + i], ins[2 * n + i], ins[3 * n + i]
            go_ref, d_ref, nm_ref, nv_ref = outs[4 * i:4 * i + 4]
            gv = g_ref[...]
            go_ref[...] = gv
            nm = ADAM_B1 * m_ref[...] + (1.0 - ADAM_B1) * gv
            nv = ADAM_B2 * v_ref[...] + (1.0 - ADAM_B2) * (gv * gv)
            m_hat = nm / (1.0 - ADAM_B1 ** ADAM_STEP)
            v_hat = nv / (1.0 - ADAM_B2 ** ADAM_STEP)
            d_ref[...] = -ADAM_LR * (m_hat / (jnp.sqrt(v_hat) + ADAM_EPS) + ADAM_WD * w_ref[...])
            nm_ref[...] = nm
            nv_ref[...] = nv

    tile = pl.BlockSpec((tr, C), lambda i: (i, 0))
    shape = jax.ShapeDtypeStruct((R, C), F32)
    res = _pcall(
        body, comm=comm, name=name, grid=(R // tr,),
        in_specs=[tile] * (4 * n), out_specs=[tile] * (4 * n), out_shape=[shape] * (4 * n),
        compiler_params=_params(("arbitrary",)),
    )(*ws, *gs, *ms, *vs)
    return [tuple(res[4 * i:4 * i + 4]) for i in range(n)]


def _mesh_pos():
    return lax.axis_index("x"), lax.axis_index("y"), lax.axis_index("c")


def _other_chips(x, y):
    return [(1 - x, y), (x, 1 - y), (1 - x, 1 - y)]


def _quarter(ref, layout, q, rows, cols):
    if layout == "cols":
        return ref.at[rows, pl.ds(pl.multiple_of(q * cols, LANES), cols)]
    return ref.at[q, rows, :]


def _gather_weights(shards, dtypes, layouts, later, name, comm=None):
    n = len(shards)
    all_rows = slice(None)
    remote = [i for i in range(n) if i not in later]
    split = [i for i in remote if layouts[i] != "whole"]

    def body(*refs):
        ins, outs = refs[:n], refs[n:2 * n]
        st32, st16 = refs[2 * n:3 * n], refs[3 * n:4 * n]
        in_sems, own_sems, send_sems, recv_sems, fwd_send_sems, fwd_recv_sems = refs[4 * n:]
        x, y, c = _mesh_pos()
        chip = 2 * x + y
        sibling = (x, y, 1 - c)
        chips = _other_chips(x, y)

        def half(i, which):
            if i not in split:
                return all_rows
            hr = shards[i].shape[0] // 2
            return pl.ds(pl.multiple_of(which * hr, 16), hr)

        def place(i, q, rows):
            return _quarter(outs[i], layouts[i], q, rows, shards[i].shape[1])

        def ici(i, k, origin_chip, src):
            px, py = chips[k]
            return pltpu.make_async_remote_copy(
                src_ref=src, dst_ref=place(i, origin_chip, half(i, c)), send_sem=send_sems.at[i, k],
                recv_sem=recv_sems.at[i, k], device_id=(px, py, c), device_id_type=MESH)

        def forward(i, k, rows):
            px, py = chips[k]
            there = place(i, 2 * px + py, rows)
            return pltpu.make_async_remote_copy(
                src_ref=there, dst_ref=there, send_sem=fwd_send_sems.at[i, k],
                recv_sem=fwd_recv_sems.at[i, k], device_id=sibling, device_id_type=MESH)

        loads = [pltpu.make_async_copy(ins[i], st32[i], in_sems.at[i]) for i in range(n)]
        for cp in loads:
            cp.start()
        owns, sent = [], []
        for i in range(n):
            loads[i].wait()
            st16[i][...] = st32[i][...].astype(dtypes[i])
            own = pltpu.make_async_copy(st16[i], place(i, chip, all_rows), own_sems.at[i])
            own.start()
            owns.append(own)
            for k in range(3 if i in remote else 0):
                cp = ici(i, k, chip, st16[i].at[half(i, c)])
                cp.start()
                sent.append(cp)
        for i in remote:
            for k, (px, py) in enumerate(chips):
                ici(i, k, 2 * px + py, st16[i].at[half(i, c)]).wait_recv()
                if i in split:
                    cp = forward(i, k, half(i, c))
                    cp.start()
                    sent.append(cp)
        for i in split:
            for k in range(3):
                forward(i, k, half(i, 1 - c)).wait_recv()
        for cp in sent:
            cp.wait_send()
        for cp in owns:
            cp.wait()

    hbm = pl.BlockSpec(memory_space=pl.ANY)
    return _pcall(
        body, comm=comm, name=name,
        in_specs=[hbm] * n, out_specs=[hbm] * n,
        out_shape=[_gathered_shape(s.shape, d, lay) for s, d, lay in zip(shards, dtypes, layouts)],
        scratch_shapes=[pltpu.VMEM(s.shape, F32) for s in shards] + [pltpu.VMEM(s.shape, d) for s, d in zip(shards, dtypes)]
        + [pltpu.SemaphoreType.DMA((n,)), pltpu.SemaphoreType.DMA((n,)),
           pltpu.SemaphoreType.DMA((n, 3)), pltpu.SemaphoreType.DMA((n, 3)),
           pltpu.SemaphoreType.DMA((n, 3)), pltpu.SemaphoreType.DMA((n, 3))],
        compiler_params=pltpu.CompilerParams(vmem_limit_bytes=VMEM_LIMIT),
    )(*shards)


def _gathered_shape(quarter_shape, dtype, layout):
    R, C = quarter_shape
    return jax.ShapeDtypeStruct((R, N_CHIPS * C) if layout == "cols" else (N_CHIPS, R, C), dtype)


def _remote(src, dst, sems, j, device):
    return pltpu.make_async_remote_copy(src_ref=src, dst_ref=dst, send_sem=sems.at[2 * j], recv_sem=sems.at[2 * j + 1],
                                        device_id=device, device_id_type=MESH)


class _SemWindow:
    def __init__(self, sems, offset):
        self.sems, self.offset = sems, offset

    @property
    def at(self):
        return self

    def __getitem__(self, i):
        return self.sems.at[self.offset + i]


def _join_plans(a, b):
    n_ai, n_ao = len(a.ins), len(a.out_shapes)

    def plan(ins, outs, sems, finishing):
        first = a.plan(ins[:n_ai], outs[:n_ao], sems, finishing)
        second = b.plan(ins[n_ai:], outs[n_ao:], _SemWindow(sems, a.n_sems), finishing)
        return tuple(p + q for p, q in zip(first, second))

    aliases = {**a.aliases, **{n_ai + i: n_ao + o for i, o in b.aliases.items()}}
    return _Comm(a.ins + b.ins, a.out_shapes + b.out_shapes, a.n_sems + b.n_sems, plan, aliases)


def _half_rows(rows, which):
    hr = rows // 2
    return pl.ds(pl.multiple_of(which * hr, 16), hr)


def _spread_plan(fulls, quarter_shapes, layouts, peers=(0, 1, 2)):
    def plan(ins, outs, sems, finishing):
        x, y, c = _mesh_pos()
        chip = 2 * x + y
        sends, recvs = [], []
        for i, full in enumerate(outs):
            R, C = quarter_shapes[i]
            for k, (px, py) in enumerate(_other_chips(x, y)):
                if k not in peers:
                    continue
                mine = _quarter(full, layouts[i], chip, _half_rows(R, c), C)
                sends.append(_remote(mine, mine, sems, 3 * i + k, (px, py, c)))
                if finishing:
                    theirs = _quarter(full, layouts[i], 2 * px + py, _half_rows(R, c), C)
                    recvs.append(_remote(mine, theirs, sems, 3 * i + k, (px, py, c)))
        return [], sends, recvs

    shapes = [jax.ShapeDtypeStruct(f.shape, f.dtype) for f in fulls]
    return _Comm(fulls, shapes, 6 * len(fulls), plan, aliases={i: i for i in range(len(fulls))})


def _forward_plan(fulls, quarter_shapes, layouts):
    def plan(ins, outs, sems, finishing):
        x, y, c = _mesh_pos()
        sends, recvs = [], []
        for i, full in enumerate(outs):
            R, C = quarter_shapes[i]
            for k, (px, py) in enumerate(_other_chips(x, y)):
                mine = _quarter(full, layouts[i], 2 * px + py, _half_rows(R, c), C)
                sends.append(_remote(mine, mine, sems, 3 * i + k, (x, y, 1 - c)))
                if finishing:
                    theirs = _quarter(full, layouts[i], 2 * px + py, _half_rows(R, 1 - c), C)
                    recvs.append(_remote(mine, theirs, sems, 3 * i + k, (x, y, 1 - c)))
        return [], sends, recvs

    shapes = [jax.ShapeDtypeStruct(f.shape, f.dtype) for f in fulls]
    return _Comm(fulls, shapes, 6 * len(fulls), plan, aliases={i: i for i in range(len(fulls))})


def _swap_plan(grads):
    def plan(ins, outs, sems, finishing):
        x, y, c = _mesh_pos()
        sends = [_remote(g.at[:, _half_rows(g.shape[1], 1 - c), :], sib, sems, i, (x, y, 1 - c))
                 for i, (g, sib) in enumerate(zip(ins, outs))]
        return [], sends, sends

    shapes = [jax.ShapeDtypeStruct((N_CHIPS, g.shape[1] // 2, g.shape[2]), BF16) for g in grads]
    return _Comm(grads, shapes, 2 * len(grads), plan)


def _pair_sums(gs, sibs, name, comm=None):
    n = len(gs)

    def body(*refs):
        c = lax.axis_index("c")
        for g_ref, sib_ref, o_ref in zip(refs[:n], refs[n:2 * n], refs[2 * n:]):
            mine = _half_rows(g_ref.shape[1], c)
            o_ref[0] = (g_ref[0, mine, :].astype(F32) + sib_ref[0].astype(F32)).astype(BF16)

    def chunk(shape):
        return pl.BlockSpec((1,) + shape[1:], lambda q: (q, 0, 0))

    return _pcall(
        body, comm=comm, name=name, grid=(N_CHIPS,),
        in_specs=[chunk(g.shape) for g in gs] + [chunk(s.shape) for s in sibs],
        out_specs=[chunk(s.shape) for s in sibs],
        out_shape=[jax.ShapeDtypeStruct(s.shape, BF16) for s in sibs],
        compiler_params=_params(("arbitrary",)),
    )(*gs, *sibs)


def _ici_plan(sums):
    def plan(ins, outs, sems, finishing):
        x, y, c = _mesh_pos()
        sends = []
        for i, (s, rcv) in enumerate(zip(ins, outs)):
            for k, (px, py) in enumerate(_other_chips(x, y)):
                sends.append(_remote(s.at[2 * px + py], rcv.at[k], sems, 3 * i + k, (px, py, c)))
        return [], sends, sends

    shapes = [jax.ShapeDtypeStruct((3,) + s.shape[1:], BF16) for s in sums]
    return _Comm(sums, shapes, 6 * len(sums), plan)


def _finish_quarters(ss, rcvs, name, comm=None):
    n = len(ss)

    def body(*refs):
        s_refs, rcv_refs, out_refs, sems = refs[:n], refs[n:2 * n], refs[2 * n:3 * n], refs[3 * n]
        x, y, c = _mesh_pos()
        swaps = []
        for i, (s_ref, rcv_ref, out_ref) in enumerate(zip(s_refs, rcv_refs, out_refs)):
            mine = _half_rows(out_ref.shape[0], c)
            acc = s_ref[2 * x + y].astype(F32)
            for k in range(3):
                acc = acc + rcv_ref[k].astype(F32)
            out_ref[mine, :] = acc
            back = _remote(out_ref.at[mine, :], out_ref.at[mine, :], sems, i, (x, y, 1 - c))
            back.start()
            swaps.append(back)
        for back in swaps:
            back.wait()

    vmem = pl.BlockSpec(memory_space=pltpu.VMEM)
    return _pcall(
        body, comm=comm, name=name, grid=(1,),
        in_specs=[vmem] * (2 * n), out_specs=[vmem] * n,
        out_shape=[jax.ShapeDtypeStruct((2 * s.shape[1], s.shape[2]), F32) for s in ss],
        scratch_shapes=[pltpu.SemaphoreType.DMA((2 * n,))],
        compiler_params=pltpu.CompilerParams(vmem_limit_bytes=VMEM_LIMIT),
    )(*ss, *rcvs)


def _allreduce_small(v, name, comm=None):
    R, C = v.shape
    n_dev = 8

    def body(v_ref, out_ref, buf, send_sems, recv_sems):
        x, y, c = _mesh_pos()
        me = 4 * x + 2 * y + c
        buf[me] = v_ref[...]
        peers = []
        for k in range(1, n_dev):
            kx, ky, kc = (k >> 2) & 1, (k >> 1) & 1, k & 1
            px = 1 - x if kx else x
            py = 1 - y if ky else y
            pc = 1 - c if kc else c
            peers.append((px, py, pc))
        sends = []
        for k, peer in enumerate(peers):
            cp = pltpu.make_async_remote_copy(
                src_ref=v_ref, dst_ref=buf.at[me], send_sem=send_sems.at[k], recv_sem=recv_sems.at[k],
                device_id=peer, device_id_type=MESH)
            cp.start()
            sends.append(cp)
        for k, (px, py, pc) in enumerate(peers):
            pltpu.make_async_remote_copy(
                src_ref=v_ref, dst_ref=buf.at[4 * px + 2 * py + pc], send_sem=send_sems.at[k],
                recv_sem=recv_sems.at[k], device_id=(px, py, pc), device_id_type=MESH).wait_recv()
        for cp in sends:
            cp.wait_send()
        acc = buf[0]
        for i in range(1, n_dev):
            acc = acc + buf[i]
        out_ref[...] = acc

    vmem = pl.BlockSpec(memory_space=pltpu.VMEM)
    return _pcall(
        body, comm=comm, name=name, grid=(1,),
        in_specs=[vmem], out_specs=vmem,
        out_shape=jax.ShapeDtypeStruct((R, C), F32),
        scratch_shapes=[pltpu.VMEM((n_dev, R, C), F32),
                        pltpu.SemaphoreType.DMA((n_dev - 1,)), pltpu.SemaphoreType.DMA((n_dev - 1,))],
        compiler_params=pltpu.CompilerParams(vmem_limit_bytes=VMEM_LIMIT),
    )(v)


def _rope_tables(positions):
    half = ROPE_DIM // 2
    inv_freq = ROPE_THETA ** (-jnp.arange(half, dtype=F32) / half)
    ang = positions.astype(F32)[:, None] * inv_freq
    cos, sin = jnp.cos(ang), jnp.sin(ang)
    T = positions.shape[0]
    ones = jnp.ones((T, HEAD_DIM - ROPE_DIM), F32)
    c64 = jnp.concatenate([cos, cos, ones], axis=1)
    s64 = jnp.concatenate([-sin, sin, 0.0 * ones], axis=1)
    return jnp.tile(c64, (1, 2)), jnp.tile(s64, (1, 2))


def _local_step(x, mem, positions, target, small, first, own):
    cos, sin = _rope_tables(positions)
    two = lambda g: jnp.tile(g, (1, 2))
    gq2, gk2, gmq2, gmk2 = two(small["g_q"]), two(small["g_k"]), two(small["g_mq"]), two(small["g_mk"])
    mix_names = ["w_in", "w_mkv", "w_out"]
    mix_layouts = ["cols", "stack", "stack"]
    mix_quarters = [(D_MODEL, IN_CHUNK), (D_MODEL // N_CHIPS, 2 * MQ_COLS), (D_MODEL // N_CHIPS, D_MODEL)]
    ffn2_names = ["wg2", "wu2", "wd2"]
    ffn_quarter = (FF_CHUNK, D_MODEL)

    spread = _spread_plan([own["w_in"], own["w_mkv"], own["wg2"]], mix_quarters[:2] + [ffn_quarter], ["cols", "stack", "stack"])
    (x1, h1, a1, b1), (half_in, half_mkv, half_wg2) = _ffn_fwd(
        x, small["g_ffn1"], first["wg1"], first["wu1"], first["wd1"], "ffn1_fwd", comm=spread)
    hm, (w_in, w_mkv) = _rms_fwd(x1, small["g_mix"], "mix_norm",
                                 comm=_forward_plan([half_in, half_mkv], mix_quarters[:2], mix_layouts[:2]))
    w_mkv = w_mkv.reshape(D_MODEL, 2 * MQ_COLS)
    proj, (half_out,) = _mm_nn([hm], w_in, None, "in_proj",
                               comm=_spread_plan([own["w_out"]], mix_quarters[2:], mix_layouts[2:]))
    hmem = _rms_fwd(mem, small["g_mem"], "mem_norm")
    mkv = _mm_nn([hmem], w_mkv, None, "mem_proj")
    ya, (half_wu2,) = _attn_fwd(proj, cos, sin, gq2, gk2, small["sinks"], "swa_fwd",
                                comm=_spread_plan([own["wu2"]], [ffn_quarter], ["stack"]))
    (yc, cpre), (near_wd2, w_out) = _conv_fwd(
        proj, small["w_dw"], small["b_dw"], small["g_conv_ln"], small["b_conv_ln"], "conv_fwd",
        comm=_join_plans(_spread_plan([own["wd2"]], [ffn_quarter], ["stack"], peers=(0, 1)),
                         _forward_plan([half_out], mix_quarters[2:], mix_layouts[2:])))
    w_out = w_out.reshape(D_MODEL, D_MODEL)
    ym, (half_wd2,) = _mem_fwd(proj, mkv, gmq2, gmk2, "memattn_fwd",
                               comm=_spread_plan([near_wd2], [ffn_quarter], ["stack"], peers=(2,)))
    passing = _forward_plan([half_wg2, half_wu2, half_wd2], [ffn_quarter] * 3, ["stack"] * 3)
    x2, (wg2, wu2, wd2) = _mm_nn([ya, yc, ym], w_out, x1, "out_proj", comm=passing)
    dx3, h2, a2, b2, loss = _ffn_fwd(x2, small["g_ffn2"], wg2, wu2, wd2, "ffn2_fwd", target=target, rows=2 * ROW_TILE)

    dh2, dwg2, dwu2, dwd2 = _ffn_bwd(dx3, h2, a2, b2, wg2, wu2, wd2, "ffn2_bwd")
    (dx2, dg_ffn2), sibs = _rms_bwd(x2, small["g_ffn2"], dh2, dx3, "ffn2_norm_bwd", comm=_swap_plan([dwg2, dwu2, dwd2]))
    sums_ffn2 = _pair_sums([dwg2, dwu2, dwd2], sibs, "pair_sums_ffn2")
    dyc = _mm_nt([dx2], w_out, "out_proj_bwd")
    dw_out = _mm_tn([ya, yc, ym], [dx2], 1, "out_proj_wgrad").reshape(N_CHIPS, -1, D_MODEL)
    (dq, dk, dv, dgq, dgk, dsinks), rcv_wg2 = _attn_bwd(proj, cos, sin, gq2, gk2, small["sinks"], dyc, "swa_bwd",
                                                         comm=_ici_plan(sums_ffn2[:1]))
    (da, dgate, dw_dw, db_dw, dg_ln, db_ln), sibs = _conv_bwd(
        proj, cpre, small["w_dw"], small["g_conv_ln"], small["b_conv_ln"], dyc, "conv_bwd", comm=_swap_plan([dw_out]))
    dmq, dmkv, dgmq, dgmk = _mem_bwd(proj, mkv, gmq2, gmk2, dyc, "memattn_bwd")
    sums_out = _pair_sums([dw_out], sibs, "pair_sums_out")
    pieces = [dq, dk, dv, da, dgate, dmq]
    dhm, rcv_out = _mm_nt(pieces, w_in, "in_proj_bwd", comm=_ici_plan(sums_out))
    dw_in = _mm_tn([hm], pieces, N_CHIPS, "in_proj_wgrad")
    dhmem = _mm_nt([dmkv], w_mkv, "mem_proj_bwd")
    dw_mkv = _mm_tn([hmem], [dmkv], 1, "mem_proj_wgrad").reshape(N_CHIPS, -1, 2 * MQ_COLS)
    _, dg_mem = _rms_bwd(mem, small["g_mem"], dhmem, None, "mem_norm_bwd")
    (dx1, dg_mix), sibs = _rms_bwd(x1, small["g_mix"], dhm, dx2, "mix_norm_bwd", comm=_swap_plan([dw_in, dw_mkv]))
    sums_in = _pair_sums([dw_in, dw_mkv], sibs, "pair_sums_in")
    (dh1, dwg1, dwu1, dwd1), landed = _ffn_bwd(dx1, h1, a1, b1, first["wg1"], first["wu1"], first["wd1"], "ffn1_bwd",
                                                comm=_ici_plan([*sums_in, *sums_ffn2[1:]]))
    rcv_in, rcv_ffn2 = landed[:2], rcv_wg2 + landed[2:]
    (dx, dg_ffn1), sibs = _rms_bwd(x, small["g_ffn1"], dh1, dx1, "ffn1_norm_bwd", comm=_swap_plan([dwg1, dwu1, dwd1]))
    sums_ffn1 = _pair_sums([dwg1, dwu1, dwd1], sibs, "pair_sums_ffn1")
    g_ffn2, rcv_wg1 = _finish_quarters(sums_ffn2, rcv_ffn2, "finish_ffn2", comm=_ici_plan(sums_ffn1[:1]))
    g_mix, rcv_wu1 = _finish_quarters(sums_in + sums_out, rcv_in + rcv_out, "finish_mix", comm=_ici_plan(sums_ffn1[1:2]))
    small_grads = dict(
        g_ffn1=dg_ffn1, g_mix=dg_mix, g_q=dgq[:, :HEAD_DIM], g_k=dgk[:, :HEAD_DIM], sinks=dsinks[:, :N_Q_HEADS],
        w_dw=dw_dw[:CONV_WIDTH], b_dw=db_dw, g_conv_ln=dg_ln, b_conv_ln=db_ln, g_mem=dg_mem,
        g_mq=dgmq[:, :HEAD_DIM], g_mk=dgmk[:, :HEAD_DIM], g_ffn2=dg_ffn2, loss=loss[:, :1])
    names = list(small_grads)
    packed, offs = _pack([small_grads[n] for n in names])
    total, rcv_wd1 = _allreduce_small(packed, "allreduce_small", comm=_ici_plan(sums_ffn1[2:]))
    summed = dict(zip(names, _unpack(total, offs, [small_grads[n].shape for n in names])))
    g_ffn1 = _finish_quarters(sums_ffn1, rcv_wg1 + rcv_wu1 + rcv_wd1, "finish_ffn1")
    big_grads = dict(zip(["wg1", "wu1", "wd1", "w_in", "w_mkv", "w_out"] + ffn2_names, [*g_ffn1, *g_mix, *g_ffn2]))
    return dx, big_grads, summed


SMALL_NAMES = ["g_ffn1", "g_mix", "g_q", "g_k", "sinks", "b_dw", "g_conv_ln", "b_conv_ln", "g_mem", "g_mq", "g_mk",
               "g_ffn2"]
PACK_COLS = 1024


def _pack(parts):
    flat = [p.reshape(-1) for p in parts]
    offs, o = [], 0
    for f in flat:
        offs.append(o)
        o += f.shape[0]
    rows = -(-o // (8 * PACK_COLS)) * 8
    pad = jnp.zeros((rows * PACK_COLS - o,), F32)
    return jnp.concatenate(flat + [pad]).reshape(rows, PACK_COLS), offs


def _unpack(packed, offs, shapes):
    flat = packed.reshape(-1)
    return [flat[o:o + math.prod(s)].reshape(s) for o, s in zip(offs, shapes)]


def kernel(x, mem, positions, g_ffn1, w_ffn1_gate, w_ffn1_up, w_ffn1_down, g_mix, w_in, g_q, g_k, sinks, w_dw, b_dw, g_conv_ln, b_conv_ln, g_mem, w_mem_kv, g_mq, g_mk, w_out, g_ffn2, w_ffn2_gate, w_ffn2_up, w_ffn2_down, loss_target, m_g_ffn1, m_w_ffn1_gate, m_w_ffn1_up, m_w_ffn1_down, m_g_mix, m_w_in, m_g_q, m_g_k, m_sinks, m_w_dw, m_b_dw, m_g_conv_ln, m_b_conv_ln, m_g_mem, m_w_mem_kv, m_g_mq, m_g_mk, m_w_out, m_g_ffn2, m_w_ffn2_gate, m_w_ffn2_up, m_w_ffn2_down, v_g_ffn1, v_w_ffn1_gate, v_w_ffn1_up, v_w_ffn1_down, v_g_mix, v_w_in, v_g_q, v_g_k, v_sinks, v_w_dw, v_b_dw, v_g_conv_ln, v_b_conv_ln, v_g_mem, v_w_mem_kv, v_g_mq, v_g_mk, v_w_out, v_g_ffn2, v_w_ffn2_gate, v_w_ffn2_up, v_w_ffn2_down):
    args = dict(locals())
    weight_names = ["g_ffn1", "w_ffn1_gate", "w_ffn1_up", "w_ffn1_down", "g_mix", "w_in", "g_q", "g_k", "sinks",
                    "w_dw", "b_dw", "g_conv_ln", "b_conv_ln", "g_mem", "w_mem_kv", "g_mq", "g_mk", "w_out", "g_ffn2",
                    "w_ffn2_gate", "w_ffn2_up", "w_ffn2_down"]
    big_names = ["w_ffn1_gate", "w_ffn1_up", "w_ffn1_down", "w_in", "w_mem_kv", "w_out",
                 "w_ffn2_gate", "w_ffn2_up", "w_ffn2_down"]
    short = dict(w_ffn1_gate="wg1", w_ffn1_up="wu1", w_ffn1_down="wd1", w_in="w_in", w_mem_kv="w_mkv",
                 w_out="w_out", w_ffn2_gate="wg2", w_ffn2_up="wu2", w_ffn2_down="wd2")

    transposed = ("w_ffn1_gate", "w_ffn1_up", "w_ffn2_gate", "w_ffn2_up")

    def quarter(a, n):
        return jnp.swapaxes(a, 1, 2)[0] if n in transposed else a[0]

    def unquarter(a, n):
        return jnp.swapaxes(a[None], 1, 2) if n in transposed else a[None]

    shards = [quarter(args[n], n) for n in big_names]
    layouts = ["cols" if n == "w_in" else "stack" for n in big_names]
    later = [i for i, n in enumerate(big_names) if short[n] not in ("wg1", "wu1", "wd1")]
    gathered = _gather_weights(shards + [w_dw[0]], [BF16] * len(shards) + [F32], layouts + ["whole"], later,
                               "gather_first")
    first = {short[n]: gathered[i] for i, n in enumerate(big_names) if i not in later}
    own = {short[n]: gathered[i] for i, n in enumerate(big_names) if i in later}
    small = {n: args[n] for n in SMALL_NAMES}
    small["w_dw"] = jnp.transpose(gathered[-1], (1, 0, 2)).reshape(CONV_WIDTH, CONV_CH)

    dx, big_grads, summed = _local_step(x[0], mem[0], positions[0], loss_target[0], small, first, own)
    chip = 2 * lax.axis_index("x") + lax.axis_index("y")
    dw_dw_full = summed.pop("w_dw")
    loss_out = summed.pop("loss").reshape(())

    grads = {n: summed[n] for n in SMALL_NAMES}
    grads["w_dw"] = lax.dynamic_slice_in_dim(dw_dw_full, chip * (CONV_CH // N_CHIPS), CONV_CH // N_CHIPS, axis=1)
    for n in big_names:
        grads[n] = big_grads[short[n]]

    delta, new_m, new_v = {}, {}, {}
    ffn_names = [n for n in big_names if "ffn" in n]
    for group, label in [(ffn_names, "adamw_ffn")] + [([n], "adamw_" + short[n]) for n in big_names if "ffn" not in n]:
        results = _adamw([quarter(args[n], n) for n in group], [grads[n] for n in group],
                         [quarter(args["m_" + n], n) for n in group], [quarter(args["v_" + n], n) for n in group], label)
        for n, (g, d, nm, nv) in zip(group, results):
            grads[n], delta[n], new_m[n], new_v[n] = (unquarter(a, n) for a in (g, d, nm, nv))
    tiny = SMALL_NAMES + ["w_dw"]
    pw, poffs = _pack([args[n] for n in tiny])
    pg, _ = _pack([grads[n] for n in tiny])
    pm, _ = _pack([args["m_" + n] for n in tiny])
    pv, _ = _pack([args["v_" + n] for n in tiny])
    ((_, pd, pnm, pnv),) = _adamw([pw], [pg], [pm], [pv], "adamw_small")
    tshapes = [args[n].shape for n in tiny]
    for store, packed_out in ((delta, pd), (new_m, pnm), (new_v, pnv)):
        for n, val in zip(tiny, _unpack(packed_out, poffs, tshapes)):
            store[n] = val

    def shaped(n, v):
        return v.reshape(args[n].shape)

    return (loss_out, dx[None],
            *[shaped(n, grads[n]) for n in weight_names],
            *[shaped(n, delta[n]) for n in weight_names],
            *[shaped(n, new_m[n]) for n in weight_names],
            *[shaped(n, new_v[n]) for n in weight_names])
```

```python
import functools
import math

import jax
import jax.numpy as jnp
from jax import lax
from jax.experimental import pallas as pl
from jax.experimental.pallas import tpu as pltpu

F32 = jnp.float32
BF16 = jnp.bfloat16

D_MODEL = 1024
SEQ = 2048
MEM_LEN = 256
HEAD_DIM = 64
N_Q_HEADS = 8
N_KV_HEADS = 2
Q_PER_KV = 4
N_MEM_HEADS = 4
BLOCK = 128
CONV_CH = 256
CONV_WIDTH = 31
ROPE_THETA = 500000.0
ROPE_DIM = 16
D_FF = 2816
EPS = 1e-6
Q_COLS = 512
KV_COLS = 128
MQ_COLS = 256
IN_COLS = 1536

N_CHIPS = 4
FF_CHUNK = D_FF // N_CHIPS
IN_CHUNK = IN_COLS // N_CHIPS

ADAM_LR = 0.001
ADAM_B1 = 0.9
ADAM_B2 = 0.999
ADAM_EPS = 1e-08
ADAM_WD = 0.01
ADAM_STEP = 10

LANES = 128
VMEM_LIMIT = 56 * 1024 * 1024
ROW_TILE = 512
MESH = pl.DeviceIdType.MESH
NEG = -1e30


class _Comm:
    def __init__(self, ins, out_shapes, n_sems, plan, aliases=None):
        self.ins, self.out_shapes, self.n_sems, self.plan = list(ins), list(out_shapes), n_sems, plan
        self.aliases = aliases or {}


def _pcall(body, comm=None, **kw):
    if comm is None:
        return pl.pallas_call(body, **kw)
    grid = kw["grid"]
    in_specs = list(kw["in_specs"])
    single = not isinstance(kw["out_shape"], (list, tuple))
    out_specs = [kw["out_specs"]] if single else list(kw["out_specs"])
    out_shape = [kw["out_shape"]] if single else list(kw["out_shape"])
    scratch = list(kw.get("scratch_shapes", ()))
    n_in, n_out, n_scr = len(in_specs), len(out_shape), len(scratch)
    n_ci, n_co = len(comm.ins), len(comm.out_shapes)

    def wrapped(*refs):
        o = 0
        parts = []
        for cnt in (n_in, n_ci, n_out, n_co, n_scr):
            parts.append(refs[o:o + cnt])
            o += cnt
        ins, c_ins, outs, c_outs, scr = parts
        sems = refs[o]
        first = last = None
        for d, size in enumerate(grid):
            at0, at_end = pl.program_id(d) == 0, pl.program_id(d) == size - 1
            first = at0 if first is None else first & at0
            last = at_end if last is None else last & at_end

        @pl.when(first)
        def _():
            local, sends, _ = comm.plan(c_ins, c_outs, sems, False)
            for cp in sends + local:
                cp.start()

        body(*ins, *outs, *scr)

        @pl.when(last)
        def _():
            local, sends, recvs = comm.plan(c_ins, c_outs, sems, True)
            for cp in recvs:
                cp.wait_recv()
            for cp in sends:
                cp.wait_send()
            for cp in local:
                cp.wait()

    hbm = pl.BlockSpec(memory_space=pl.ANY)
    kw = dict(kw, in_specs=in_specs + [hbm] * n_ci, out_specs=out_specs + [hbm] * n_co,
              out_shape=out_shape + comm.out_shapes,
              scratch_shapes=scratch + [pltpu.SemaphoreType.DMA((comm.n_sems,))])
    if comm.aliases:
        kw["input_output_aliases"] = {n_in + i: n_out + o for i, o in comm.aliases.items()}
    call = pl.pallas_call(wrapped, **kw)

    def run(*args):
        res = call(*args, *comm.ins)
        return (res[0] if single else list(res[:n_out])), list(res[n_out:])

    return run


def _params(sem=None):
    return pltpu.CompilerParams(dimension_semantics=sem, vmem_limit_bytes=VMEM_LIMIT)


def _dot(a, b):
    return jnp.dot(a, b, preferred_element_type=F32)


def _dot_nt(a, b):
    return lax.dot_general(a, b, (((1,), (1,)), ((), ())), preferred_element_type=F32)


def _dot_tn(a, b):
    return lax.dot_general(a, b, (((0,), (0,)), ((), ())), preferred_element_type=F32)


def _sigmoid(x):
    return 1.0 / (1.0 + jnp.exp(-x))


def _full(shape):
    n = len(shape)
    return pl.BlockSpec(shape, lambda *_: (0,) * n)


def _ffn_fwd(x, g, wg, wu, wd, name, comm=None, target=None, rows=ROW_TILE):
    T, D = x.shape
    tt = rows
    nt = T // tt
    with_loss = target is not None

    def body(*refs):
        if with_loss:
            x_ref, g_ref, wg_ref, wu_ref, wd_ref, t_ref, xo_ref, h_ref, a_ref, b_ref, loss_ref = refs
        else:
            x_ref, g_ref, wg_ref, wu_ref, wd_ref, xo_ref, h_ref, a_ref, b_ref = refs
        t = pl.program_id(0)
        j = pl.program_id(1)

        @pl.when(j == 0)
        def _():
            xv = x_ref[...]
            rstd = lax.rsqrt(jnp.mean(xv * xv, axis=-1, keepdims=True) + EPS)
            h_ref[...] = (xv * rstd * g_ref[...]).astype(BF16)
            xo_ref[...] = jnp.zeros_like(xo_ref)

        h = h_ref[...]
        a = _dot_nt(h, wg_ref[0])
        b = _dot_nt(h, wu_ref[0])
        a_ref[0] = a.astype(BF16)
        b_ref[0] = b.astype(BF16)
        s = (a * _sigmoid(a)) * b
        xo_ref[...] += _dot(s.astype(BF16), wd_ref[0])

        @pl.when(j == N_CHIPS - 1)
        def _():
            out = x_ref[...] + 0.5 * xo_ref[...]
            if with_loss:
                err = out - t_ref[...]
                xo_ref[...] = err * (1.0 / D)
                part = 0.5 * jnp.sum(jnp.mean(err * err, axis=-1, keepdims=True), axis=0, keepdims=True)

                @pl.when(t == 0)
                def _():
                    loss_ref[...] = jnp.zeros_like(loss_ref)

                loss_ref[...] += jnp.broadcast_to(part, loss_ref.shape)
            else:
                xo_ref[...] = out

    tile = pl.BlockSpec((tt, D), lambda t, j: (t, 0))
    chunk = pl.BlockSpec((1, FF_CHUNK, D), lambda t, j: (j, 0, 0))
    act = pl.BlockSpec((1, tt, FF_CHUNK), lambda t, j: (j, t, 0))
    act_shape = jax.ShapeDtypeStruct((N_CHIPS, T, FF_CHUNK), BF16)
    loss_spec = [pl.BlockSpec((1, LANES), lambda t, j: (0, 0))] if with_loss else []
    loss_shape = [jax.ShapeDtypeStruct((1, LANES), F32)] if with_loss else []
    return _pcall(
        body, comm=comm, name=name, grid=(nt, N_CHIPS),
        in_specs=[tile, pl.BlockSpec((1, D), lambda t, j: (0, 0)), chunk, chunk, chunk] + ([tile] if with_loss else []),
        out_specs=[tile, tile, act, act] + loss_spec,
        out_shape=[jax.ShapeDtypeStruct((T, D), F32), jax.ShapeDtypeStruct((T, D), BF16), act_shape, act_shape]
        + loss_shape,
        compiler_params=_params(("arbitrary", "arbitrary")),
    )(x, g, wg, wu, wd, *([target] if with_loss else []))


def _ffn_bwd(dxo, h, a, b, wg, wu, wd, name, comm=None):
    T, D = dxo.shape
    tt = ROW_TILE
    nt = T // tt

    def body(dxo_ref, h_ref, a_ref, b_ref, wg_ref, wu_ref, wd_ref,
             dh_hbm, dwg_ref, dwu_ref, dwd_ref, dh_acc, acc_g, acc_u, acc_d):
        j = pl.program_id(0)
        t = pl.program_id(1)
        do = (0.5 * dxo_ref[...]).astype(BF16)
        av = a_ref[0].astype(F32)
        bv = b_ref[0].astype(F32)
        sig = _sigmoid(av)
        sa = av * sig
        ds = _dot_nt(do, wd_ref[0])
        da = (ds * bv * (sig * (1.0 + av * (1.0 - sig)))).astype(BF16)
        db = (ds * sa).astype(BF16)
        hv = h_ref[...]
        rows = pl.ds(pl.multiple_of(t * tt, tt), tt)

        @pl.when(j == 0)
        def _():
            dh_acc[rows, :] = jnp.zeros((tt, D), F32)

        @pl.when(t == 0)
        def _():
            acc_g[...] = jnp.zeros_like(acc_g)
            acc_u[...] = jnp.zeros_like(acc_u)
            acc_d[...] = jnp.zeros_like(acc_d)

        acc_d[...] += _dot_tn((sa * bv).astype(BF16), do)
        acc_g[...] += _dot_tn(da, hv)
        acc_u[...] += _dot_tn(db, hv)
        dh_acc[rows, :] += _dot(da, wg_ref[0]) + _dot(db, wu_ref[0])

        @pl.when(t == nt - 1)
        def _():
            dwg_ref[0] = acc_g[...].astype(BF16)
            dwu_ref[0] = acc_u[...].astype(BF16)
            dwd_ref[0] = acc_d[...].astype(BF16)

        @pl.when((t == nt - 1) & (j == N_CHIPS - 1))
        def _():
            pltpu.sync_copy(dh_acc, dh_hbm)

    return _pcall(
        body, comm=comm, name=name, grid=(N_CHIPS, nt),
        in_specs=[
            pl.BlockSpec((tt, D), lambda j, t: (t, 0)),
            pl.BlockSpec((tt, D), lambda j, t: (t, 0)),
            pl.BlockSpec((1, tt, FF_CHUNK), lambda j, t: (j, t, 0)),
            pl.BlockSpec((1, tt, FF_CHUNK), lambda j, t: (j, t, 0)),
            pl.BlockSpec((1, FF_CHUNK, D), lambda j, t: (j, 0, 0)),
            pl.BlockSpec((1, FF_CHUNK, D), lambda j, t: (j, 0, 0)),
            pl.BlockSpec((1, FF_CHUNK, D), lambda j, t: (j, 0, 0)),
        ],
        out_specs=[
            pl.BlockSpec(memory_space=pl.ANY),
            pl.BlockSpec((1, FF_CHUNK, D), lambda j, t: (j, 0, 0)),
            pl.BlockSpec((1, FF_CHUNK, D), lambda j, t: (j, 0, 0)),
            pl.BlockSpec((1, FF_CHUNK, D), lambda j, t: (j, 0, 0)),
        ],
        out_shape=[
            jax.ShapeDtypeStruct((T, D), F32),
            jax.ShapeDtypeStruct((N_CHIPS, FF_CHUNK, D), BF16),
            jax.ShapeDtypeStruct((N_CHIPS, FF_CHUNK, D), BF16),
            jax.ShapeDtypeStruct((N_CHIPS, FF_CHUNK, D), BF16),
        ],
        scratch_shapes=[
            pltpu.VMEM((T, D), F32),
            pltpu.VMEM((FF_CHUNK, D), F32),
            pltpu.VMEM((FF_CHUNK, D), F32),
            pltpu.VMEM((FF_CHUNK, D), F32),
        ],
        compiler_params=_params(("arbitrary", "arbitrary")),
    )(dxo, h, a, b, wg, wu, wd)


def _rms_fwd(x, g, name, comm=None):
    T, D = x.shape
    tt = min(ROW_TILE, T)

    def body(x_ref, g_ref, h_ref):
        xv = x_ref[...]
        rstd = lax.rsqrt(jnp.mean(xv * xv, axis=-1, keepdims=True) + EPS)
        h_ref[...] = (xv * rstd * g_ref[...]).astype(BF16)

    return _pcall(
        body, comm=comm, name=name, grid=(T // tt,),
        in_specs=[pl.BlockSpec((tt, D), lambda t: (t, 0)), pl.BlockSpec((1, D), lambda t: (0, 0))],
        out_specs=pl.BlockSpec((tt, D), lambda t: (t, 0)),
        out_shape=jax.ShapeDtypeStruct((T, D), BF16),
        compiler_params=_params(("arbitrary",)),
    )(x, g)


def _rms_bwd(x, g, dh, dres, name, comm=None):
    T, D = x.shape
    tt = min(ROW_TILE, T)
    has_res = dres is not None

    def body(*refs):
        if has_res:
            x_ref, g_ref, dh_ref, dres_ref, dx_ref, dg_ref = refs
        else:
            x_ref, g_ref, dh_ref, dx_ref, dg_ref = refs
        t = pl.program_id(0)
        xv = x_ref[...]
        rstd = lax.rsqrt(jnp.mean(xv * xv, axis=-1, keepdims=True) + EPS)
        xhat = xv * rstd
        dhv = dh_ref[...]
        gy = dhv * g_ref[...]
        dx = rstd * (gy - xhat * jnp.mean(gy * xhat, axis=-1, keepdims=True))
        if has_res:
            dx = dx + dres_ref[...]
        dx_ref[...] = dx
        part = jnp.sum(dhv * xhat, axis=0, keepdims=True)

        @pl.when(t == 0)
        def _():
            dg_ref[...] = part

        @pl.when(t > 0)
        def _():
            dg_ref[...] += part

    tile = pl.BlockSpec((tt, D), lambda t: (t, 0))
    vec = pl.BlockSpec((1, D), lambda t: (0, 0))
    args = [x, g, dh] + ([dres] if has_res else [])
    return _pcall(
        body, comm=comm, name=name, grid=(T // tt,),
        in_specs=[tile, vec, tile] + ([tile] if has_res else []),
        out_specs=[tile, vec],
        out_shape=[jax.ShapeDtypeStruct((T, D), F32), jax.ShapeDtypeStruct((1, D), F32)],
        compiler_params=_params(("arbitrary",)),
    )(*args)


def _mm_nn(a_list, b, res, name, comm=None):
    T = a_list[0].shape[0]
    K, N = b.shape
    tt = min(ROW_TILE, T)
    ks = [a.shape[1] for a in a_list]
    na = len(a_list)
    has_res = res is not None

    def body(*refs):
        a_refs = refs[:na]
        b_ref = refs[na]
        o_ref = refs[-1]
        acc = res_v = None
        off = 0
        for a_ref, k in zip(a_refs, ks):
            part = _dot(a_ref[...].astype(BF16), b_ref[off:off + k, :])
            acc = part if acc is None else acc + part
            off += k
        if has_res:
            acc = refs[na + 1][...] + acc
        o_ref[...] = acc

    in_specs = [pl.BlockSpec((tt, k), lambda t: (t, 0)) for k in ks] + [pl.BlockSpec((K, N), lambda t: (0, 0))]
    args = list(a_list) + [b]
    if has_res:
        in_specs.append(pl.BlockSpec((tt, N), lambda t: (t, 0)))
        args.append(res)
    return _pcall(
        body, comm=comm, name=name, grid=(T // tt,), in_specs=in_specs,
        out_specs=pl.BlockSpec((tt, N), lambda t: (t, 0)),
        out_shape=jax.ShapeDtypeStruct((T, N), F32),
        compiler_params=_params(("arbitrary",)),
    )(*args)


def _mm_nt(a_list, b, name, comm=None):
    T = a_list[0].shape[0]
    K, N = b.shape
    tt = min(ROW_TILE, T)
    ns = [a.shape[1] for a in a_list]
    na = len(a_list)

    def body(*refs):
        b_ref = refs[na]
        o_ref = refs[-1]
        acc = None
        off = 0
        for a_ref, n in zip(refs[:na], ns):
            part = _dot_nt(a_ref[...].astype(BF16), b_ref[:, off:off + n])
            acc = part if acc is None else acc + part
            off += n
        o_ref[...] = acc

    return _pcall(
        body, comm=comm, name=name, grid=(T // tt,),
        in_specs=[pl.BlockSpec((tt, n), lambda t: (t, 0)) for n in ns] + [pl.BlockSpec((K, N), lambda t: (0, 0))],
        out_specs=pl.BlockSpec((tt, K), lambda t: (t, 0)),
        out_shape=jax.ShapeDtypeStruct((T, K), F32),
        compiler_params=_params(("arbitrary",)),
    )(*a_list, b)


def _mm_tn(a_list, b_list, col_chunks, name, comm=None):
    T = a_list[0].shape[0]
    tt = min(ROW_TILE, T)
    nt = T // tt
    ms = [a.shape[1] for a in a_list]
    ns = [b.shape[1] for b in b_list]
    M, N = sum(ms), sum(ns)
    na, nb = len(a_list), len(b_list)
    cw = N // col_chunks

    def body(*refs):
        a_refs, b_refs = refs[:na], refs[na:na + nb]
        o_ref, acc = refs[na + nb], refs[na + nb + 1]
        t = pl.program_id(0)

        @pl.when(t == 0)
        def _():
            acc[...] = jnp.zeros_like(acc)

        ro = 0
        for a_ref, m in zip(a_refs, ms):
            av = a_ref[...].astype(BF16)
            co = 0
            for b_ref, n in zip(b_refs, ns):
                acc[ro:ro + m, co:co + n] += _dot_tn(av, b_ref[...].astype(BF16))
                co += n
            ro += m

        @pl.when(t == nt - 1)
        def _():
            if col_chunks == 1:
                o_ref[...] = acc[...].astype(BF16)
            else:
                for q in range(col_chunks):
                    o_ref[q] = acc[:, q * cw:(q + 1) * cw].astype(BF16)

    out_shape = (M, N) if col_chunks == 1 else (col_chunks, M, cw)
    return _pcall(
        body, comm=comm, name=name, grid=(nt,),
        in_specs=[pl.BlockSpec((tt, m), lambda t: (t, 0)) for m in ms]
        + [pl.BlockSpec((tt, n), lambda t: (t, 0)) for n in ns],
        out_specs=_full(out_shape),
        out_shape=jax.ShapeDtypeStruct(out_shape, BF16),
        scratch_shapes=[pltpu.VMEM((M, N), F32)],
        compiler_params=_params(("arbitrary",)),
    )(*a_list, *b_list)


def _head_masks():
    lane = lax.broadcasted_iota(jnp.int32, (1, LANES), 1)
    l64 = lane & (HEAD_DIM - 1)
    return lane < HEAD_DIM, l64 < ROPE_DIM // 2, l64 < ROPE_DIM


def _head_mean(v, lo):
    s_lo = jnp.sum(jnp.where(lo, v, 0.0), axis=-1, keepdims=True)
    s_hi = jnp.sum(jnp.where(lo, 0.0, v), axis=-1, keepdims=True)
    return jnp.where(lo, s_lo, s_hi) * (1.0 / HEAD_DIM)


def _rope_swap(v, first, rot):
    up = pltpu.roll(v, LANES - ROPE_DIM // 2, 1)
    down = pltpu.roll(v, ROPE_DIM // 2, 1)
    return jnp.where(first, up, jnp.where(rot, down, 0.0))


def _head_norm(x, g, lo):
    rstd = lax.rsqrt(_head_mean(x * x, lo) + EPS)
    return x * rstd * g


def _head_norm_bwd(x, g, dy, lo):
    rstd = lax.rsqrt(_head_mean(x * x, lo) + EPS)
    xhat = x * rstd
    gy = dy * g
    dx = rstd * (gy - xhat * _head_mean(gy * xhat, lo))
    return dx, dy * xhat


def _rope(xn, cos, sin, first, rot):
    return xn * cos + _rope_swap(xn, first, rot) * sin


def _rope_bwd(dy, cos, sin, first, rot):
    return dy * cos + _rope_swap(dy * sin, first, rot)


def _fold_heads(v):
    return v + pltpu.roll(v, HEAD_DIM, 1)


ATT_ROWS = 512


def _attn_prepare(q_ref, k_ref, v_ref, cos_ref, sin_ref, gq_ref, gk_ref, qs, ks, vs):
    T = q_ref.shape[0]
    lo, first, rot = _head_masks()
    ks[0:BLOCK, :] = jnp.zeros((BLOCK, KV_COLS), BF16)
    vs[0:BLOCK, :] = jnp.zeros((BLOCK, KV_COLS), BF16)

    def step(i, _):
        r0 = pl.multiple_of(i * ATT_ROWS, ATT_ROWS)
        rows = pl.ds(r0, ATT_ROWS)
        prow = pl.ds(r0 + BLOCK, ATT_ROWS)
        cos, sin = cos_ref[rows, :], sin_ref[rows, :]
        for p in range(Q_COLS // LANES):
            cols = slice(p * LANES, (p + 1) * LANES)
            xr = _rope(_head_norm(q_ref[rows, cols], gq_ref[...], lo), cos, sin, first, rot)
            qs[rows, cols] = (xr * (HEAD_DIM ** -0.5)).astype(BF16)
        kr = _rope(_head_norm(k_ref[rows, :], gk_ref[...], lo), cos, sin, first, rot)
        ks[prow, :] = kr.astype(BF16)
        vs[prow, :] = v_ref[rows, :].astype(BF16)
        return 0

    lax.fori_loop(0, T // ATT_ROWS, step, 0)


GROUP_ROWS = Q_PER_KV * BLOCK


def _group_rows(ref, r0, g, cast=None):
    parts = []
    for r in range(Q_PER_KV):
        h = g * Q_PER_KV + r
        part = ref[pl.ds(r0, BLOCK), h * HEAD_DIM:(h + 1) * HEAD_DIM]
        parts.append(part if cast is None else part.astype(cast))
    return jnp.concatenate(parts, axis=0)


def _group_sinks(sink_ref, g):
    row = lax.broadcasted_iota(jnp.int32, (GROUP_ROWS, 1), 0)
    col = jnp.full((GROUP_ROWS, 1), sink_ref[0, g * Q_PER_KV], F32)
    for r in range(1, Q_PER_KV):
        col = jnp.where(row >= r * BLOCK, sink_ref[0, g * Q_PER_KV + r], col)
    return col


def _attn_scores(qg, kw, blk, sink):
    s = _dot_nt(qg, kw)
    qi = (lax.broadcasted_iota(jnp.int32, (GROUP_ROWS, 2 * BLOCK), 0) & (BLOCK - 1)) + BLOCK
    ki = lax.broadcasted_iota(jnp.int32, (GROUP_ROWS, 2 * BLOCK), 1)
    rel = qi - ki
    valid = (rel >= 0) & (rel < BLOCK) & ((blk > 0) | (ki >= BLOCK))
    s = jnp.where(valid, s, NEG)
    m = jnp.maximum(jnp.max(s, axis=-1, keepdims=True), sink)
    p = jnp.exp(s - m)
    e_sink = jnp.exp(sink - m)
    inv = 1.0 / (jnp.sum(p, axis=-1, keepdims=True) + e_sink)
    return p * inv, e_sink * inv


def _attn_fwd(proj, cos, sin, gq2, gk2, sinks, name, comm=None):
    T = proj.shape[0]
    nb = T // BLOCK

    def body(q_ref, k_ref, v_ref, cos_ref, sin_ref, gq_ref, gk_ref, sink_ref, y_ref, qs, ks, vs):
        _attn_prepare(q_ref, k_ref, v_ref, cos_ref, sin_ref, gq_ref, gk_ref, qs, ks, vs)

        def blk_step(blk, _):
            r0 = pl.multiple_of(blk * BLOCK, BLOCK)
            for g in range(N_KV_HEADS):
                gc = slice(g * HEAD_DIM, (g + 1) * HEAD_DIM)
                kw = ks[pl.ds(r0, 2 * BLOCK), gc]
                vw = vs[pl.ds(r0, 2 * BLOCK), gc]
                w, _ws = _attn_scores(_group_rows(qs, r0, g), kw, blk, _group_sinks(sink_ref, g))
                o = _dot(w.astype(BF16), vw).astype(BF16)
                for r in range(Q_PER_KV):
                    h = g * Q_PER_KV + r
                    y_ref[pl.ds(r0, BLOCK), h * HEAD_DIM:(h + 1) * HEAD_DIM] = o[r * BLOCK:(r + 1) * BLOCK, :]
            return 0

        lax.fori_loop(0, nb, blk_step, 0)

    return _pcall(
        body, comm=comm, name=name, grid=(1,),
        in_specs=[
            pl.BlockSpec((T, Q_COLS), lambda i: (0, 0)),
            pl.BlockSpec((T, KV_COLS), lambda i: (0, Q_COLS // KV_COLS)),
            pl.BlockSpec((T, KV_COLS), lambda i: (0, Q_COLS // KV_COLS + 1)),
            _full((T, LANES)), _full((T, LANES)), _full((1, LANES)), _full((1, LANES)),
            pl.BlockSpec(memory_space=pltpu.SMEM),
        ],
        out_specs=_full((T, Q_COLS)),
        out_shape=jax.ShapeDtypeStruct((T, Q_COLS), BF16),
        scratch_shapes=[
            pltpu.VMEM((T, Q_COLS), BF16),
            pltpu.VMEM((T + BLOCK, KV_COLS), BF16),
            pltpu.VMEM((T + BLOCK, KV_COLS), BF16),
        ],
        compiler_params=_params(("arbitrary",)),
    )(proj, proj, proj, cos, sin, gq2, gk2, sinks)


def _attn_bwd(proj, cos, sin, gq2, gk2, sinks, dyc, name, comm=None):
    T = proj.shape[0]
    nb = T // BLOCK

    def body(q_ref, k_ref, v_ref, cos_ref, sin_ref, gq_ref, gk_ref, sink_ref, dy_ref,
             dq_ref, dk_ref, dv_ref, dgq_ref, dgk_ref, dsink_ref, qs, ks, vs, dqs, dks, dvs):
        _attn_prepare(q_ref, k_ref, v_ref, cos_ref, sin_ref, gq_ref, gk_ref, qs, ks, vs)
        dks[...] = jnp.zeros_like(dks)
        dvs[...] = jnp.zeros_like(dvs)
        lane = lax.broadcasted_iota(jnp.int32, (1, LANES), 1)

        def blk_step(blk, dsink):
            r0 = pl.multiple_of(blk * BLOCK, BLOCK)
            win = pl.ds(r0, 2 * BLOCK)
            for g in range(N_KV_HEADS):
                gc = slice(g * HEAD_DIM, (g + 1) * HEAD_DIM)
                kw = ks[win, gc]
                vw = vs[win, gc]
                qg = _group_rows(qs, r0, g)
                w, w_sink = _attn_scores(qg, kw, blk, _group_sinks(sink_ref, g))
                do = _group_rows(dy_ref, r0, g, cast=BF16)
                dvs[win, gc] += _dot_tn(w.astype(BF16), do)
                dw = _dot_nt(do, vw)
                delta = jnp.sum(w * dw, axis=-1, keepdims=True)
                ds = (w * (dw - delta)).astype(BF16)
                sink_part = w_sink * delta
                dq = _dot(ds, kw)
                for r in range(Q_PER_KV):
                    h = g * Q_PER_KV + r
                    slab = slice(r * BLOCK, (r + 1) * BLOCK)
                    dsink = dsink + jnp.where(lane == h, -jnp.sum(sink_part[slab, :], axis=0, keepdims=True), 0.0)
                    dqs[pl.ds(r0, BLOCK), h * HEAD_DIM:(h + 1) * HEAD_DIM] = dq[slab, :]
                dks[win, gc] += _dot_tn(ds, qg)
            return dsink

        dsink_ref[...] = lax.fori_loop(0, nb, blk_step, jnp.zeros((1, LANES), F32))

        lo, first, rot = _head_masks()

        def step(i, carry):
            dgq, dgk = carry
            r0 = pl.multiple_of(i * ATT_ROWS, ATT_ROWS)
            rows = pl.ds(r0, ATT_ROWS)
            prow = pl.ds(r0 + BLOCK, ATT_ROWS)
            cos, sin = cos_ref[rows, :], sin_ref[rows, :]
            for p in range(Q_COLS // LANES):
                cols = slice(p * LANES, (p + 1) * LANES)
                dxn = _rope_bwd(dqs[rows, cols] * (HEAD_DIM ** -0.5), cos, sin, first, rot)
                dx, dgp = _head_norm_bwd(q_ref[rows, cols], gq_ref[...], dxn, lo)
                dq_ref[rows, cols] = dx
                dgq = dgq + jnp.sum(dgp, axis=0, keepdims=True)
            dkn = _rope_bwd(dks[prow, :], cos, sin, first, rot)
            dx, dgp = _head_norm_bwd(k_ref[rows, :], gk_ref[...], dkn, lo)
            dk_ref[rows, :] = dx
            dgk = dgk + jnp.sum(dgp, axis=0, keepdims=True)
            dv_ref[rows, :] = dvs[prow, :]
            return dgq, dgk

        zero = jnp.zeros((1, LANES), F32)
        dgq, dgk = lax.fori_loop(0, T // ATT_ROWS, step, (zero, zero))
        dgq_ref[...] = _fold_heads(dgq)
        dgk_ref[...] = _fold_heads(dgk)

    vec = jax.ShapeDtypeStruct((1, LANES), F32)
    return _pcall(
        body, comm=comm, name=name, grid=(1,),
        in_specs=[
            pl.BlockSpec((T, Q_COLS), lambda i: (0, 0)),
            pl.BlockSpec((T, KV_COLS), lambda i: (0, Q_COLS // KV_COLS)),
            pl.BlockSpec((T, KV_COLS), lambda i: (0, Q_COLS // KV_COLS + 1)),
            _full((T, LANES)), _full((T, LANES)), _full((1, LANES)), _full((1, LANES)),
            pl.BlockSpec(memory_space=pltpu.SMEM),
            pl.BlockSpec((T, Q_COLS), lambda i: (0, 0)),
        ],
        out_specs=[_full((T, Q_COLS)), _full((T, KV_COLS)), _full((T, KV_COLS)),
                   _full((1, LANES)), _full((1, LANES)), _full((1, LANES))],
        out_shape=[jax.ShapeDtypeStruct((T, Q_COLS), F32), jax.ShapeDtypeStruct((T, KV_COLS), F32),
                   jax.ShapeDtypeStruct((T, KV_COLS), F32), vec, vec, vec],
        scratch_shapes=[
            pltpu.VMEM((T, Q_COLS), BF16),
            pltpu.VMEM((T + BLOCK, KV_COLS), BF16),
            pltpu.VMEM((T + BLOCK, KV_COLS), BF16),
            pltpu.VMEM((T, Q_COLS), F32),
            pltpu.VMEM((T + BLOCK, KV_COLS), F32),
            pltpu.VMEM((T + BLOCK, KV_COLS), F32),
        ],
        compiler_params=_params(("arbitrary",)),
    )(proj, proj, proj, cos, sin, gq2, gk2, sinks, dyc)


CONV_PAD = 32
CONV_ROWS = 256


def _conv_taps(src, w_ref, r0, first_off, step_sign):
    acc = None
    for i in range(CONV_WIDTH):
        term = w_ref[i:i + 1, :] * src[r0 + first_off + step_sign * i:r0 + first_off + step_sign * i + CONV_ROWS, :]
        acc = term if acc is None else acc + term
    return acc


def _conv_fwd(proj, w_dw, b_dw, g_ln, b_ln, name, comm=None):
    T = proj.shape[0]
    a_blk = (Q_COLS + 2 * KV_COLS) // CONV_CH

    def body(a_ref, gate_ref, w_ref, bdw_ref, g_ref, b_ref, y_ref, c_ref, pad):
        pad[0:CONV_PAD, :] = jnp.zeros((CONV_PAD, CONV_CH), F32)
        pad[CONV_PAD:, :] = a_ref[...] * _sigmoid(gate_ref[...])
        for n in range(T // CONV_ROWS):
            r0 = n * CONV_ROWS
            c = _conv_taps(pad, w_ref, r0, CONV_PAD - (CONV_WIDTH - 1), 1) + bdw_ref[...]
            c_ref[r0:r0 + CONV_ROWS, :] = c
            mu = jnp.mean(c, axis=-1, keepdims=True)
            cc = c - mu
            rstd = lax.rsqrt(jnp.mean(cc * cc, axis=-1, keepdims=True) + EPS)
            z = cc * rstd * g_ref[...] + b_ref[...]
            y_ref[r0:r0 + CONV_ROWS, :] = (z * _sigmoid(z)).astype(BF16)

    vec = _full((1, CONV_CH))
    return _pcall(
        body, comm=comm, name=name, grid=(1,),
        in_specs=[
            pl.BlockSpec((T, CONV_CH), lambda i: (0, a_blk)),
            pl.BlockSpec((T, CONV_CH), lambda i: (0, a_blk + 1)),
            _full((CONV_WIDTH, CONV_CH)), vec, vec, vec,
        ],
        out_specs=[_full((T, CONV_CH)), _full((T, CONV_CH))],
        out_shape=[jax.ShapeDtypeStruct((T, CONV_CH), BF16), jax.ShapeDtypeStruct((T, CONV_CH), F32)],
        scratch_shapes=[pltpu.VMEM((T + CONV_PAD, CONV_CH), F32)],
        compiler_params=_params(("arbitrary",)),
    )(proj, proj, w_dw, b_dw, g_ln, b_ln)


def _conv_bwd(proj, c, w_dw, g_ln, b_ln, dyc, name, comm=None):
    T = proj.shape[0]
    a_blk = (Q_COLS + 2 * KV_COLS) // CONV_CH
    y_blk = Q_COLS // CONV_CH

    def body(a_ref, gate_ref, c_ref, w_ref, g_ref, b_ref, dy_ref,
             da_ref, dgate_ref, dw_ref, dbdw_ref, dg_ref, db_ref, pad, dcp):
        pad[0:CONV_PAD, :] = jnp.zeros((CONV_PAD, CONV_CH), F32)
        sg = _sigmoid(gate_ref[...])
        pad[CONV_PAD:, :] = a_ref[...] * sg
        dcp[T:, :] = jnp.zeros((CONV_PAD, CONV_CH), F32)
        dg = db = dbdw = jnp.zeros((1, CONV_CH), F32)
        for n in range(T // CONV_ROWS):
            rows = slice(n * CONV_ROWS, (n + 1) * CONV_ROWS)
            cv = c_ref[rows, :]
            mu = jnp.mean(cv, axis=-1, keepdims=True)
            cc = cv - mu
            rstd = lax.rsqrt(jnp.mean(cc * cc, axis=-1, keepdims=True) + EPS)
            chat = cc * rstd
            z = chat * g_ref[...] + b_ref[...]
            sz = _sigmoid(z)
            dz = dy_ref[rows, :] * (sz * (1.0 + z * (1.0 - sz)))
            dg = dg + jnp.sum(dz * chat, axis=0, keepdims=True)
            db = db + jnp.sum(dz, axis=0, keepdims=True)
            dch = dz * g_ref[...]
            dc = rstd * (dch - jnp.mean(dch, axis=-1, keepdims=True)
                         - chat * jnp.mean(dch * chat, axis=-1, keepdims=True))
            dbdw = dbdw + jnp.sum(dc, axis=0, keepdims=True)
            dcp[rows, :] = dc
        dg_ref[...] = dg
        db_ref[...] = db
        dbdw_ref[...] = dbdw
        dw_ref[CONV_WIDTH:, :] = jnp.zeros((CONV_PAD - CONV_WIDTH, CONV_CH), F32)
        for i in range(CONV_WIDTH):
            off = CONV_PAD - (CONV_WIDTH - 1) + i
            acc = jnp.zeros((1, CONV_CH), F32)
            for n in range(T // CONV_ROWS):
                r0 = n * CONV_ROWS
                acc = acc + jnp.sum(dcp[r0:r0 + CONV_ROWS, :] * pad[r0 + off:r0 + off + CONV_ROWS, :],
                                    axis=0, keepdims=True)
            dw_ref[i:i + 1, :] = acc
        for n in range(T // CONV_ROWS):
            r0 = n * CONV_ROWS
            rows = slice(r0, r0 + CONV_ROWS)
            dhg = _conv_taps(dcp, w_ref, r0, CONV_WIDTH - 1, -1)
            sgv = sg[rows, :]
            da_ref[rows, :] = dhg * sgv
            dgate_ref[rows, :] = dhg * a_ref[rows, :] * sgv * (1.0 - sgv)

    vec = _full((1, CONV_CH))
    vshape = jax.ShapeDtypeStruct((1, CONV_CH), F32)
    return _pcall(
        body, comm=comm, name=name, grid=(1,),
        in_specs=[
            pl.BlockSpec((T, CONV_CH), lambda i: (0, a_blk)),
            pl.BlockSpec((T, CONV_CH), lambda i: (0, a_blk + 1)),
            _full((T, CONV_CH)), _full((CONV_WIDTH, CONV_CH)), vec, vec,
            pl.BlockSpec((T, CONV_CH), lambda i: (0, y_blk)),
        ],
        out_specs=[_full((T, CONV_CH)), _full((T, CONV_CH)), _full((CONV_PAD, CONV_CH)), vec, vec, vec],
        out_shape=[jax.ShapeDtypeStruct((T, CONV_CH), F32), jax.ShapeDtypeStruct((T, CONV_CH), F32),
                   jax.ShapeDtypeStruct((CONV_PAD, CONV_CH), F32), vshape, vshape, vshape],
        scratch_shapes=[pltpu.VMEM((T + CONV_PAD, CONV_CH), F32), pltpu.VMEM((T + CONV_PAD, CONV_CH), F32)],
        compiler_params=_params(("arbitrary",)),
    )(proj, proj, c, w_dw, g_ln, b_ln, dyc)


def _mem_kv(mkv_ref, gk_ref, lo, kn_s, vv_s):
    for p in range(MQ_COLS // LANES):
        cols = slice(p * LANES, (p + 1) * LANES)
        kn_s[:, cols] = _head_norm(mkv_ref[:, cols], gk_ref[...], lo).astype(BF16)
    vv_s[...] = mkv_ref[:, MQ_COLS:].astype(BF16)


def _mem_softmax(qh, kh):
    s = _dot_nt(qh, kh)
    m = jnp.max(s, axis=-1, keepdims=True)
    p = jnp.exp(s - m)
    return p / jnp.sum(p, axis=-1, keepdims=True)


def _mem_fwd(proj, mkv, gq2, gk2, name, comm=None):
    T = proj.shape[0]
    tt = ROW_TILE
    q_blk = (IN_COLS - MQ_COLS) // MQ_COLS

    def body(q_ref, mkv_ref, gq_ref, gk_ref, y_ref, kn_s, vv_s, qn_s):
        lo, _, _ = _head_masks()
        _mem_kv(mkv_ref, gk_ref, lo, kn_s, vv_s)
        for p in range(MQ_COLS // LANES):
            cols = slice(p * LANES, (p + 1) * LANES)
            qn_s[:, cols] = (_head_norm(q_ref[:, cols], gq_ref[...], lo) * (HEAD_DIM ** -0.5)).astype(BF16)
        for h in range(N_MEM_HEADS):
            hc = slice(h * HEAD_DIM, (h + 1) * HEAD_DIM)
            w = _mem_softmax(qn_s[:, hc], kn_s[:, hc])
            y_ref[:, hc] = _dot(w.astype(BF16), vv_s[:, hc]).astype(BF16)

    return _pcall(
        body, comm=comm, name=name, grid=(T // tt,),
        in_specs=[
            pl.BlockSpec((tt, MQ_COLS), lambda t: (t, q_blk)),
            pl.BlockSpec((MEM_LEN, 2 * MQ_COLS), lambda t: (0, 0)),
            pl.BlockSpec((1, LANES), lambda t: (0, 0)), pl.BlockSpec((1, LANES), lambda t: (0, 0)),
        ],
        out_specs=pl.BlockSpec((tt, MQ_COLS), lambda t: (t, 0)),
        out_shape=jax.ShapeDtypeStruct((T, MQ_COLS), BF16),
        scratch_shapes=[pltpu.VMEM((MEM_LEN, MQ_COLS), BF16), pltpu.VMEM((MEM_LEN, MQ_COLS), BF16),
                        pltpu.VMEM((tt, MQ_COLS), BF16)],
        compiler_params=_params(("arbitrary",)),
    )(proj, mkv, gq2, gk2)


def _mem_bwd(proj, mkv, gq2, gk2, dyc, name, comm=None):
    T = proj.shape[0]
    tt = 2 * ROW_TILE
    nt = T // tt
    q_blk = (IN_COLS - MQ_COLS) // MQ_COLS
    y_blk = (Q_COLS + CONV_CH) // MQ_COLS

    def body(q_ref, mkv_ref, gq_ref, gk_ref, dy_ref, dq_ref, dmkv_ref, dgq_ref, dgk_ref,
             kn_s, vv_s, qn_s, dqn_s, dkn_acc):
        t = pl.program_id(0)
        lo, _, _ = _head_masks()
        _mem_kv(mkv_ref, gk_ref, lo, kn_s, vv_s)

        @pl.when(t == 0)
        def _():
            dkn_acc[...] = jnp.zeros_like(dkn_acc)
            dmkv_ref[...] = jnp.zeros_like(dmkv_ref)
            dgq_ref[...] = jnp.zeros_like(dgq_ref)

        for p in range(MQ_COLS // LANES):
            cols = slice(p * LANES, (p + 1) * LANES)
            qn_s[:, cols] = (_head_norm(q_ref[:, cols], gq_ref[...], lo) * (HEAD_DIM ** -0.5)).astype(BF16)
        for h in range(N_MEM_HEADS):
            hc = slice(h * HEAD_DIM, (h + 1) * HEAD_DIM)
            vc = slice(MQ_COLS + h * HEAD_DIM, MQ_COLS + (h + 1) * HEAD_DIM)
            qh = qn_s[:, hc]
            w = _mem_softmax(qh, kn_s[:, hc])
            do = dy_ref[:, hc].astype(BF16)
            dmkv_ref[:, vc] += _dot_tn(w.astype(BF16), do)
            dw = _dot_nt(do, vv_s[:, hc])
            ds = (w * (dw - jnp.sum(w * dw, axis=-1, keepdims=True))).astype(BF16)
            dqn_s[:, hc] = _dot(ds, kn_s[:, hc])
            dkn_acc[:, hc] += _dot_tn(ds, qh)
        dgq = jnp.zeros((1, LANES), F32)
        for p in range(MQ_COLS // LANES):
            cols = slice(p * LANES, (p + 1) * LANES)
            dx, dgp = _head_norm_bwd(q_ref[:, cols], gq_ref[...], dqn_s[:, cols] * (HEAD_DIM ** -0.5), lo)
            dq_ref[:, cols] = dx
            dgq = dgq + jnp.sum(dgp, axis=0, keepdims=True)
        dgq_ref[...] += dgq

        @pl.when(t == nt - 1)
        def _():
            dgk = jnp.zeros((1, LANES), F32)
            for p in range(MQ_COLS // LANES):
                cols = slice(p * LANES, (p + 1) * LANES)
                dx, dgp = _head_norm_bwd(mkv_ref[:, cols], gk_ref[...], dkn_acc[:, cols], lo)
                dmkv_ref[:, cols] = dx
                dgk = dgk + jnp.sum(dgp, axis=0, keepdims=True)
            dgk_ref[...] = _fold_heads(dgk)
            dgq_ref[...] = _fold_heads(dgq_ref[...])

    vec = pl.BlockSpec((1, LANES), lambda t: (0, 0))
    vshape = jax.ShapeDtypeStruct((1, LANES), F32)
    return _pcall(
        body, comm=comm, name=name, grid=(nt,),
        in_specs=[
            pl.BlockSpec((tt, MQ_COLS), lambda t: (t, q_blk)),
            pl.BlockSpec((MEM_LEN, 2 * MQ_COLS), lambda t: (0, 0)),
            vec, vec,
            pl.BlockSpec((tt, MQ_COLS), lambda t: (t, y_blk)),
        ],
        out_specs=[pl.BlockSpec((tt, MQ_COLS), lambda t: (t, 0)),
                   pl.BlockSpec((MEM_LEN, 2 * MQ_COLS), lambda t: (0, 0)), vec, vec],
        out_shape=[jax.ShapeDtypeStruct((T, MQ_COLS), F32), jax.ShapeDtypeStruct((MEM_LEN, 2 * MQ_COLS), F32),
                   vshape, vshape],
        scratch_shapes=[pltpu.VMEM((MEM_LEN, MQ_COLS), BF16), pltpu.VMEM((MEM_LEN, MQ_COLS), BF16),
                        pltpu.VMEM((tt, MQ_COLS), BF16), pltpu.VMEM((tt, MQ_COLS), F32),
                        pltpu.VMEM((MEM_LEN, MQ_COLS), F32)],
        compiler_params=_params(("arbitrary",)),
    )(proj, mkv, gq2, gk2, dyc)


ADAMW_TILE_BYTES = 5 << 19


def _adamw(ws, gs, ms, vs, name, comm=None):
    n = len(ws)
    R, C = ws[0].shape
    budget = ADAMW_TILE_BYTES // (4 * C * n)
    tr = next((r for r in (512, 352, 256, 176, 128, 88, 64, 32, 16, 8) if R % r == 0 and r <= max(budget, 8)), R)

    def body(*refs):
        ins, outs = refs[:4 * n], refs[4 * n:]
        for i in range(n):
            w_ref, g_ref, m_ref, v_ref = ins[i], ins[n + i], ins[2 * n + i], ins[3 * n + i]
            go_ref, d_ref, nm_ref, nv_ref = outs[4 * i:4 * i + 4]
            gv = g_ref[...]
            go_ref[...] = gv
            nm = ADAM_B1 * m_ref[...] + (1.0 - ADAM_B1) * gv
            nv = ADAM_B2 * v_ref[...] + (1.0 - ADAM_B2) * (gv * gv)
            m_hat = nm / (1.0 - ADAM_B1 ** ADAM_STEP)
            v_hat = nv / (1.0 - ADAM_B2 ** ADAM_STEP)
            d_ref[...] = -ADAM_LR * (m_hat / (jnp.sqrt(v_hat) + ADAM_EPS) + ADAM_WD * w_ref[...])
            nm_ref[...] = nm
            nv_ref[...] = nv

    tile = pl.BlockSpec((tr, C), lambda i: (i, 0))
    shape = jax.ShapeDtypeStruct((R, C), F32)
    res = _pcall(
        body, comm=comm, name=name, grid=(R // tr,),
        in_specs=[tile] * (4 * n), out_specs=[tile] * (4 * n), out_shape=[shape] * (4 * n),
        compiler_params=_params(("arbitrary",)),
    )(*ws, *gs, *ms, *vs)
    return [tuple(res[4 * i:4 * i + 4]) for i in range(n)]


def _mesh_pos():
    return lax.axis_index("x"), lax.axis_index("y"), lax.axis_index("c")


def _other_chips(x, y):
    return [(1 - x, y), (x, 1 - y), (1 - x, 1 - y)]


def _quarter(ref, layout, q, rows, cols):
    if layout == "cols":
        return ref.at[rows, pl.ds(pl.multiple_of(q * cols, LANES), cols)]
    return ref.at[q, rows, :]


def _gather_weights(shards, dtypes, layouts, later, name, comm=None):
    n = len(shards)
    all_rows = slice(None)
    remote = [i for i in range(n) if i not in later]
    split = [i for i in remote if layouts[i] != "whole"]

    def body(*refs):
        ins, outs = refs[:n], refs[n:2 * n]
        st32, st16 = refs[2 * n:3 * n], refs[3 * n:4 * n]
        in_sems, own_sems, send_sems, recv_sems, fwd_send_sems, fwd_recv_sems = refs[4 * n:]
        x, y, c = _mesh_pos()
        chip = 2 * x + y
        sibling = (x, y, 1 - c)
        chips = _other_chips(x, y)

        def half(i, which):
            if i not in split:
                return all_rows
            hr = shards[i].shape[0] // 2
            return pl.ds(pl.multiple_of(which * hr, 16), hr)

        def place(i, q, rows):
            return _quarter(outs[i], layouts[i], q, rows, shards[i].shape[1])

        def ici(i, k, origin_chip, src):
            px, py = chips[k]
            return pltpu.make_async_remote_copy(
                src_ref=src, dst_ref=place(i, origin_chip, half(i, c)), send_sem=send_sems.at[i, k],
                recv_sem=recv_sems.at[i, k], device_id=(px, py, c), device_id_type=MESH)

        def forward(i, k, rows):
            px, py = chips[k]
            there = place(i, 2 * px + py, rows)
            return pltpu.make_async_remote_copy(
                src_ref=there, dst_ref=there, send_sem=fwd_send_sems.at[i, k],
                recv_sem=fwd_recv_sems.at[i, k], device_id=sibling, device_id_type=MESH)

        loads = [pltpu.make_async_copy(ins[i], st32[i], in_sems.at[i]) for i in range(n)]
        for cp in loads:
            cp.start()
        owns, sent = [], []
        for i in range(n):
            loads[i].wait()
            st16[i][...] = st32[i][...].astype(dtypes[i])
            own = pltpu.make_async_copy(st16[i], place(i, chip, all_rows), own_sems.at[i])
            own.start()
            owns.append(own)
            for k in range(3 if i in remote else 0):
                cp = ici(i, k, chip, st16[i].at[half(i, c)])
                cp.start()
                sent.append(cp)
        for i in remote:
            for k, (px, py) in enumerate(chips):
                ici(i, k, 2 * px + py, st16[i].at[half(i, c)]).wait_recv()
                if i in split:
                    cp = forward(i, k, half(i, c))
                    cp.start()
                    sent.append(cp)
        for i in split:
            for k in range(3):
                forward(i, k, half(i, 1 - c)).wait_recv()
        for cp in sent:
            cp.wait_send()
        for cp in owns:
            cp.wait()

    hbm = pl.BlockSpec(memory_space=pl.ANY)
    return _pcall(
        body, comm=comm, name=name,
        in_specs=[hbm] * n, out_specs=[hbm] * n,
        out_shape=[_gathered_shape(s.shape, d, lay) for s, d, lay in zip(shards, dtypes, layouts)],
        scratch_shapes=[pltpu.VMEM(s.shape, F32) for s in shards] + [pltpu.VMEM(s.shape, d) for s, d in zip(shards, dtypes)]
        + [pltpu.SemaphoreType.DMA((n,)), pltpu.SemaphoreType.DMA((n,)),
           pltpu.SemaphoreType.DMA((n, 3)), pltpu.SemaphoreType.DMA((n, 3)),
           pltpu.SemaphoreType.DMA((n, 3)), pltpu.SemaphoreType.DMA((n, 3))],
        compiler_params=pltpu.CompilerParams(vmem_limit_bytes=VMEM_LIMIT),
    )(*shards)


def _gathered_shape(quarter_shape, dtype, layout):
    R, C = quarter_shape
    return jax.ShapeDtypeStruct((R, N_CHIPS * C) if layout == "cols" else (N_CHIPS, R, C), dtype)


def _remote(src, dst, sems, j, device):
    return pltpu.make_async_remote_copy(src_ref=src, dst_ref=dst, send_sem=sems.at[2 * j], recv_sem=sems.at[2 * j + 1],
                                        device_id=device, device_id_type=MESH)


class _SemWindow:
    def __init__(self, sems, offset):
        self.sems, self.offset = sems, offset

    @property
    def at(self):
        return self

    def __getitem__(self, i):
        return self.sems.at[self.offset + i]


def _join_plans(a, b):
    n_ai, n_ao = len(a.ins), len(a.out_shapes)

    def plan(ins, outs, sems, finishing):
        first = a.plan(ins[:n_ai], outs[:n_ao], sems, finishing)
        second = b.plan(ins[n_ai:], outs[n_ao:], _SemWindow(sems, a.n_sems), finishing)
        return tuple(p + q for p, q in zip(first, second))

    aliases = {**a.aliases, **{n_ai + i: n_ao + o for i, o in b.aliases.items()}}
    return _Comm(a.ins + b.ins, a.out_shapes + b.out_shapes, a.n_sems + b.n_sems, plan, aliases)


def _half_rows(rows, which):
    hr = rows // 2
    return pl.ds(pl.multiple_of(which * hr, 16), hr)


def _spread_plan(fulls, quarter_shapes, layouts, peers=(0, 1, 2)):
    def plan(ins, outs, sems, finishing):
        x, y, c = _mesh_pos()
        chip = 2 * x + y
        sends, recvs = [], []
        for i, full in enumerate(outs):
            R, C = quarter_shapes[i]
            for k, (px, py) in enumerate(_other_chips(x, y)):
                if k not in peers:
                    continue
                mine = _quarter(full, layouts[i], chip, _half_rows(R, c), C)
                sends.append(_remote(mine, mine, sems, 3 * i + k, (px, py, c)))
                if finishing:
                    theirs = _quarter(full, layouts[i], 2 * px + py, _half_rows(R, c), C)
                    recvs.append(_remote(mine, theirs, sems, 3 * i + k, (px, py, c)))
        return [], sends, recvs

    shapes = [jax.ShapeDtypeStruct(f.shape, f.dtype) for f in fulls]
    return _Comm(fulls, shapes, 6 * len(fulls), plan, aliases={i: i for i in range(len(fulls))})


def _forward_plan(fulls, quarter_shapes, layouts):
    def plan(ins, outs, sems, finishing):
        x, y, c = _mesh_pos()
        sends, recvs = [], []
        for i, full in enumerate(outs):
            R, C = quarter_shapes[i]
            for k, (px, py) in enumerate(_other_chips(x, y)):
                mine = _quarter(full, layouts[i], 2 * px + py, _half_rows(R, c), C)
                sends.append(_remote(mine, mine, sems, 3 * i + k, (x, y, 1 - c)))
                if finishing:
                    theirs = _quarter(full, layouts[i], 2 * px + py, _half_rows(R, 1 - c), C)
                    recvs.append(_remote(mine, theirs, sems, 3 * i + k, (x, y, 1 - c)))
        return [], sends, recvs

    shapes = [jax.ShapeDtypeStruct(f.shape, f.dtype) for f in fulls]
    return _Comm(fulls, shapes, 6 * len(fulls), plan, aliases={i: i for i in range(len(fulls))})


def _swap_plan(grads):
    def plan(ins, outs, sems, finishing):
        x, y, c = _mesh_pos()
        sends = [_remote(g.at[:, _half_rows(g.shape[1], 1 - c), :], sib, sems, i, (x, y, 1 - c))
                 for i, (g, sib) in enumerate(zip(ins, outs))]
        return [], sends, sends

    shapes = [jax.ShapeDtypeStruct((N_CHIPS, g.shape[1] // 2, g.shape[2]), BF16) for g in grads]
    return _Comm(grads, shapes, 2 * len(grads), plan)


def _pair_sums(gs, sibs, name, comm=None):
    n = len(gs)

    def body(*refs):
        c = lax.axis_index("c")
        for g_ref, sib_ref, o_ref in zip(refs[:n], refs[n:2 * n], refs[2 * n:]):
            mine = _half_rows(g_ref.shape[1], c)
            o_ref[0] = (g_ref[0, mine, :].astype(F32) + sib_ref[0].astype(F32)).astype(BF16)

    def chunk(shape):
        return pl.BlockSpec((1,) + shape[1:], lambda q: (q, 0, 0))

    return _pcall(
        body, comm=comm, name=name, grid=(N_CHIPS,),
        in_specs=[chunk(g.shape) for g in gs] + [chunk(s.shape) for s in sibs],
        out_specs=[chunk(s.shape) for s in sibs],
        out_shape=[jax.ShapeDtypeStruct(s.shape, BF16) for s in sibs],
        compiler_params=_params(("arbitrary",)),
    )(*gs, *sibs)


def _ici_plan(sums):
    def plan(ins, outs, sems, finishing):
        x, y, c = _mesh_pos()
        sends = []
        for i, (s, rcv) in enumerate(zip(ins, outs)):
            for k, (px, py) in enumerate(_other_chips(x, y)):
                sends.append(_remote(s.at[2 * px + py], rcv.at[k], sems, 3 * i + k, (px, py, c)))
        return [], sends, sends

    shapes = [jax.ShapeDtypeStruct((3,) + s.shape[1:], BF16) for s in sums]
    return _Comm(sums, shapes, 6 * len(sums), plan)


def _finish_quarters(ss, rcvs, name, comm=None):
    n = len(ss)

    def body(*refs):
        s_refs, rcv_refs, out_refs, sems = refs[:n], refs[n:2 * n], refs[2 * n:3 * n], refs[3 * n]
        x, y, c = _mesh_pos()
        swaps = []
        for i, (s_ref, rcv_ref, out_ref) in enumerate(zip(s_refs, rcv_refs, out_refs)):
            mine = _half_rows(out_ref.shape[0], c)
            acc = s_ref[2 * x + y].astype(F32)
            for k in range(3):
                acc = acc + rcv_ref[k].astype(F32)
            out_ref[mine, :] = acc
            back = _remote(out_ref.at[mine, :], out_ref.at[mine, :], sems, i, (x, y, 1 - c))
            back.start()
            swaps.append(back)
        for back in swaps:
            back.wait()

    vmem = pl.BlockSpec(memory_space=pltpu.VMEM)
    return _pcall(
        body, comm=comm, name=name, grid=(1,),
        in_specs=[vmem] * (2 * n), out_specs=[vmem] * n,
        out_shape=[jax.ShapeDtypeStruct((2 * s.shape[1], s.shape[2]), F32) for s in ss],
        scratch_shapes=[pltpu.SemaphoreType.DMA((2 * n,))],
        compiler_params=pltpu.CompilerParams(vmem_limit_bytes=VMEM_LIMIT),
    )(*ss, *rcvs)


def _allreduce_small(v, name, comm=None):
    R, C = v.shape
    n_dev = 8

    def body(v_ref, out_ref, buf, send_sems, recv_sems):
        x, y, c = _mesh_pos()
        me = 4 * x + 2 * y + c
        buf[me] = v_ref[...]
        peers = []
        for k in range(1, n_dev):
            kx, ky, kc = (k >> 2) & 1, (k >> 1) & 1, k & 1
            px = 1 - x if kx else x
            py = 1 - y if ky else y
            pc = 1 - c if kc else c
            peers.append((px, py, pc))
        sends = []
        for k, peer in enumerate(peers):
            cp = pltpu.make_async_remote_copy(
                src_ref=v_ref, dst_ref=buf.at[me], send_sem=send_sems.at[k], recv_sem=recv_sems.at[k],
                device_id=peer, device_id_type=MESH)
            cp.start()
            sends.append(cp)
        for k, (px, py, pc) in enumerate(peers):
            pltpu.make_async_remote_copy(
                src_ref=v_ref, dst_ref=buf.at[4 * px + 2 * py + pc], send_sem=send_sems.at[k],
                recv_sem=recv_sems.at[k], device_id=(px, py, pc), device_id_type=MESH).wait_recv()
        for cp in sends:
            cp.wait_send()
        acc = buf[0]
        for i in range(1, n_dev):
            acc = acc + buf[i]
        out_ref[...] = acc

    vmem = pl.BlockSpec(memory_space=pltpu.VMEM)
    return _pcall(
        body, comm=comm, name=name, grid=(1,),
        in_specs=[vmem], out_specs=vmem,
        out_shape=jax.ShapeDtypeStruct((R, C), F32),
        scratch_shapes=[pltpu.VMEM((n_dev, R, C), F32),
                        pltpu.SemaphoreType.DMA((n_dev - 1,)), pltpu.SemaphoreType.DMA((n_dev - 1,))],
        compiler_params=pltpu.CompilerParams(vmem_limit_bytes=VMEM_LIMIT),
    )(v)


def _rope_tables(positions):
    half = ROPE_DIM // 2
    inv_freq = ROPE_THETA ** (-jnp.arange(half, dtype=F32) / half)
    ang = positions.astype(F32)[:, None] * inv_freq
    cos, sin = jnp.cos(ang), jnp.sin(ang)
    T = positions.shape[0]
    ones = jnp.ones((T, HEAD_DIM - ROPE_DIM), F32)
    c64 = jnp.concatenate([cos, cos, ones], axis=1)
    s64 = jnp.concatenate([-sin, sin, 0.0 * ones], axis=1)
    return jnp.tile(c64, (1, 2)), jnp.tile(s64, (1, 2))


def _local_step(x, mem, positions, target, small, first, own):
    cos, sin = _rope_tables(positions)
    two = lambda g: jnp.tile(g, (1, 2))
    gq2, gk2, gmq2, gmk2 = two(small["g_q"]), two(small["g_k"]), two(small["g_mq"]), two(small["g_mk"])
    mix_names = ["w_in", "w_mkv", "w_out"]
    mix_layouts = ["cols", "stack", "stack"]
    mix_quarters = [(D_MODEL, IN_CHUNK), (D_MODEL // N_CHIPS, 2 * MQ_COLS), (D_MODEL // N_CHIPS, D_MODEL)]
    ffn2_names = ["wg2", "wu2", "wd2"]
    ffn_quarter = (FF_CHUNK, D_MODEL)

    spread = _spread_plan([own["w_in"], own["w_mkv"], own["wg2"]], mix_quarters[:2] + [ffn_quarter], ["cols", "stack", "stack"])
    (x1, h1, a1, b1), (half_in, half_mkv, half_wg2) = _ffn_fwd(
        x, small["g_ffn1"], first["wg1"], first["wu1"], first["wd1"], "ffn1_fwd", comm=spread)
    hm, (w_in, w_mkv) = _rms_fwd(x1, small["g_mix"], "mix_norm",
                                 comm=_forward_plan([half_in, half_mkv], mix_quarters[:2], mix_layouts[:2]))
    w_mkv = w_mkv.reshape(D_MODEL, 2 * MQ_COLS)
    proj, (half_out,) = _mm_nn([hm], w_in, None, "in_proj",
                               comm=_spread_plan([own["w_out"]], mix_quarters[2:], mix_layouts[2:]))
    hmem = _rms_fwd(mem, small["g_mem"], "mem_norm")
    mkv = _mm_nn([hmem], w_mkv, None, "mem_proj")
    ya, (half_wu2,) = _attn_fwd(proj, cos, sin, gq2, gk2, small["sinks"], "swa_fwd",
                                comm=_spread_plan([own["wu2"]], [ffn_quarter], ["stack"]))
    (yc, cpre), (near_wd2, w_out) = _conv_fwd(
        proj, small["w_dw"], small["b_dw"], small["g_conv_ln"], small["b_conv_ln"], "conv_fwd",
        comm=_join_plans(_spread_plan([own["wd2"]], [ffn_quarter], ["stack"], peers=(0, 1)),
                         _forward_plan([half_out], mix_quarters[2:], mix_layouts[2:])))
    w_out = w_out.reshape(D_MODEL, D_MODEL)
    ym, (half_wd2,) = _mem_fwd(proj, mkv, gmq2, gmk2, "memattn_fwd",
                               comm=_spread_plan([near_wd2], [ffn_quarter], ["stack"], peers=(2,)))
    passing = _forward_plan([half_wg2, half_wu2, half_wd2], [ffn_quarter] * 3, ["stack"] * 3)
    x2, (wg2, wu2, wd2) = _mm_nn([ya, yc, ym], w_out, x1, "out_proj", comm=passing)
    dx3, h2, a2, b2, loss = _ffn_fwd(x2, small["g_ffn2"], wg2, wu2, wd2, "ffn2_fwd", target=target, rows=2 * ROW_TILE)

    dh2, dwg2, dwu2, dwd2 = _ffn_bwd(dx3, h2, a2, b2, wg2, wu2, wd2, "ffn2_bwd")
    (dx2, dg_ffn2), sibs = _rms_bwd(x2, small["g_ffn2"], dh2, dx3, "ffn2_norm_bwd", comm=_swap_plan([dwg2, dwu2, dwd2]))
    sums_ffn2 = _pair_sums([dwg2, dwu2, dwd2], sibs, "pair_sums_ffn2")
    dyc = _mm_nt([dx2], w_out, "out_proj_bwd")
    dw_out = _mm_tn([ya, yc, ym], [dx2], 1, "out_proj_wgrad").reshape(N_CHIPS, -1, D_MODEL)
    (dq, dk, dv, dgq, dgk, dsinks), rcv_wg2 = _attn_bwd(proj, cos, sin, gq2, gk2, small["sinks"], dyc, "swa_bwd",
                                                         comm=_ici_plan(sums_ffn2[:1]))
    (da, dgate, dw_dw, db_dw, dg_ln, db_ln), sibs = _conv_bwd(
        proj, cpre, small["w_dw"], small["g_conv_ln"], small["b_conv_ln"], dyc, "conv_bwd", comm=_swap_plan([dw_out]))
    dmq, dmkv, dgmq, dgmk = _mem_bwd(proj, mkv, gmq2, gmk2, dyc, "memattn_bwd")
    sums_out = _pair_sums([dw_out], sibs, "pair_sums_out")
    pieces = [dq, dk, dv, da, dgate, dmq]
    dhm, rcv_out = _mm_nt(pieces, w_in, "in_proj_bwd", comm=_ici_plan(sums_out))
    dw_in = _mm_tn([hm], pieces, N_CHIPS, "in_proj_wgrad")
    dhmem = _mm_nt([dmkv], w_mkv, "mem_proj_bwd")
    dw_mkv = _mm_tn([hmem], [dmkv], 1, "mem_proj_wgrad").reshape(N_CHIPS, -1, 2 * MQ_COLS)
    _, dg_mem = _rms_bwd(mem, small["g_mem"], dhmem, None, "mem_norm_bwd")
    (dx1, dg_mix), sibs = _rms_bwd(x1, small["g_mix"], dhm, dx2, "mix_norm_bwd", comm=_swap_plan([dw_in, dw_mkv]))
    sums_in = _pair_sums([dw_in, dw_mkv], sibs, "pair_sums_in")
    (dh1, dwg1, dwu1, dwd1), landed = _ffn_bwd(dx1, h1, a1, b1, first["wg1"], first["wu1"], first["wd1"], "ffn1_bwd",
                                                comm=_ici_plan([*sums_in, *sums_ffn2[1:]]))
    rcv_in, rcv_ffn2 = landed[:2], rcv_wg2 + landed[2:]
    dx, dg_ffn1 = _rms_bwd(x, small["g_ffn1"], dh1, dx1, "ffn1_norm_bwd")
    g_ffn2, sibs = _finish_quarters(sums_ffn2, rcv_ffn2, "finish_ffn2", comm=_swap_plan([dwg1, dwu1, dwd1]))
    sums_ffn1 = _pair_sums([dwg1, dwu1, dwd1], sibs, "pair_sums_ffn1")
    g_mix, (rcv_wg1, rcv_wu1) = _finish_quarters(sums_in + sums_out, rcv_in + rcv_out, "finish_mix",
                                                 comm=_ici_plan(sums_ffn1[:2]))
    rcv_wg1, rcv_wu1 = [rcv_wg1], [rcv_wu1]
    small_grads = dict(
        g_ffn1=dg_ffn1, g_mix=dg_mix, g_q=dgq[:, :HEAD_DIM], g_k=dgk[:, :HEAD_DIM], sinks=dsinks[:, :N_Q_HEADS],
        w_dw=dw_dw[:CONV_WIDTH], b_dw=db_dw, g_conv_ln=dg_ln, b_conv_ln=db_ln, g_mem=dg_mem,
        g_mq=dgmq[:, :HEAD_DIM], g_mk=dgmk[:, :HEAD_DIM], g_ffn2=dg_ffn2, loss=loss[:, :1])
    names = list(small_grads)
    packed, offs = _pack([small_grads[n] for n in names])
    total, rcv_wd1 = _allreduce_small(packed, "allreduce_small", comm=_ici_plan(sums_ffn1[2:]))
    summed = dict(zip(names, _unpack(total, offs, [small_grads[n].shape for n in names])))
    g_ffn1 = _finish_quarters(sums_ffn1, rcv_wg1 + rcv_wu1 + rcv_wd1, "finish_ffn1")
    big_grads = dict(zip(["wg1", "wu1", "wd1", "w_in", "w_mkv", "w_out"] + ffn2_names, [*g_ffn1, *g_mix, *g_ffn2]))
    return dx, big_grads, summed


SMALL_NAMES = ["g_ffn1", "g_mix", "g_q", "g_k", "sinks", "b_dw", "g_conv_ln", "b_conv_ln", "g_mem", "g_mq", "g_mk",
               "g_ffn2"]
PACK_COLS = 1024


def _pack(parts):
    flat = [p.reshape(-1) for p in parts]
    offs, o = [], 0
    for f in flat:
        offs.append(o)
        o += f.shape[0]
    rows = -(-o // (8 * PACK_COLS)) * 8
    pad = jnp.zeros((rows * PACK_COLS - o,), F32)
    return jnp.concatenate(flat + [pad]).reshape(rows, PACK_COLS), offs


def _unpack(packed, offs, shapes):
    flat = packed.reshape(-1)
    return [flat[o:o + math.prod(s)].reshape(s) for o, s in zip(offs, shapes)]


def kernel(x, mem, positions, g_ffn1, w_ffn1_gate, w_ffn1_up, w_ffn1_down, g_mix, w_in, g_q, g_k, sinks, w_dw, b_dw, g_conv_ln, b_conv_ln, g_mem, w_mem_kv, g_mq, g_mk, w_out, g_ffn2, w_ffn2_gate, w_ffn2_up, w_ffn2_down, loss_target, m_g_ffn1, m_w_ffn1_gate, m_w_ffn1_up, m_w_ffn1_down, m_g_mix, m_w_in, m_g_q, m_g_k, m_sinks, m_w_dw, m_b_dw, m_g_conv_ln, m_b_conv_ln, m_g_mem, m_w_mem_kv, m_g_mq, m_g_mk, m_w_out, m_g_ffn2, m_w_ffn2_gate, m_w_ffn2_up, m_w_ffn2_down, v_g_ffn1, v_w_ffn1_gate, v_w_ffn1_up, v_w_ffn1_down, v_g_mix, v_w_in, v_g_q, v_g_k, v_sinks, v_w_dw, v_b_dw, v_g_conv_ln, v_b_conv_ln, v_g_mem, v_w_mem_kv, v_g_mq, v_g_mk, v_w_out, v_g_ffn2, v_w_ffn2_gate, v_w_ffn2_up, v_w_ffn2_down):
    args = dict(locals())
    weight_names = ["g_ffn1", "w_ffn1_gate", "w_ffn1_up", "w_ffn1_down", "g_mix", "w_in", "g_q", "g_k", "sinks",
                    "w_dw", "b_dw", "g_conv_ln", "b_conv_ln", "g_mem", "w_mem_kv", "g_mq", "g_mk", "w_out", "g_ffn2",
                    "w_ffn2_gate", "w_ffn2_up", "w_ffn2_down"]
    big_names = ["w_ffn1_gate", "w_ffn1_up", "w_ffn1_down", "w_in", "w_mem_kv", "w_out",
                 "w_ffn2_gate", "w_ffn2_up", "w_ffn2_down"]
    short = dict(w_ffn1_gate="wg1", w_ffn1_up="wu1", w_ffn1_down="wd1", w_in="w_in", w_mem_kv="w_mkv",
                 w_out="w_out", w_ffn2_gate="wg2", w_ffn2_up="wu2", w_ffn2_down="wd2")

    transposed = ("w_ffn1_gate", "w_ffn1_up", "w_ffn2_gate", "w_ffn2_up")

    def quarter(a, n):
        return jnp.swapaxes(a, 1, 2)[0] if n in transposed else a[0]

    def unquarter(a, n):
        return jnp.swapaxes(a[None], 1, 2) if n in transposed else a[None]

    shards = [quarter(args[n], n) for n in big_names]
    layouts = ["cols" if n == "w_in" else "stack" for n in big_names]
    later = [i for i, n in enumerate(big_names) if short[n] not in ("wg1", "wu1", "wd1")]
    gathered = _gather_weights(shards + [w_dw[0]], [BF16] * len(shards) + [F32], layouts + ["whole"], later,
                               "gather_first")
    first = {short[n]: gathered[i] for i, n in enumerate(big_names) if i not in later}
    own = {short[n]: gathered[i] for i, n in enumerate(big_names) if i in later}
    small = {n: args[n] for n in SMALL_NAMES}
    small["w_dw"] = jnp.transpose(gathered[-1], (1, 0, 2)).reshape(CONV_WIDTH, CONV_CH)

    dx, big_grads, summed = _local_step(x[0], mem[0], positions[0], loss_target[0], small, first, own)
    chip = 2 * lax.axis_index("x") + lax.axis_index("y")
    dw_dw_full = summed.pop("w_dw")
    loss_out = summed.pop("loss").reshape(())

    grads = {n: summed[n] for n in SMALL_NAMES}
    grads["w_dw"] = lax.dynamic_slice_in_dim(dw_dw_full, chip * (CONV_CH // N_CHIPS), CONV_CH // N_CHIPS, axis=1)
    for n in big_names:
        grads[n] = big_grads[short[n]]

    delta, new_m, new_v = {}, {}, {}
    ffn_names = [n for n in big_names if "ffn" in n]
    for group, label in [(ffn_names, "adamw_ffn")] + [([n], "adamw_" + short[n]) for n in big_names if "ffn" not in n]:
        results = _adamw([quarter(args[n], n) for n in group], [grads[n] for n in group],
                         [quarter(args["m_" + n], n) for n in group], [quarter(args["v_" + n], n) for n in group], label)
        for n, (g, d, nm, nv) in zip(group, results):
            grads[n], delta[n], new_m[n], new_v[n] = (unquarter(a, n) for a in (g, d, nm, nv))
    tiny = SMALL_NAMES + ["w_dw"]
    pw, poffs = _pack([args[n] for n in tiny])
    pg, _ = _pack([grads[n] for n in tiny])
    pm, _ = _pack([args["m_" + n] for n in tiny])
    pv, _ = _pack([args["v_" + n] for n in tiny])
    ((_, pd, pnm, pnv),) = _adamw([pw], [pg], [pm], [pv], "adamw_small")
    tshapes = [args[n].shape for n in tiny]
    for store, packed_out in ((delta, pd), (new_m, pnm), (new_v, pnv)):
        for n, val in zip(tiny, _unpack(packed_out, poffs, tshapes)):
            store[n] = val

    def shaped(n, v):
        return v.reshape(args[n].shape)

    return (loss_out, dx[None],
            *[shaped(n, grads[n]) for n in weight_names],
            *[shaped(n, delta[n]) for n in weight_names],
            *[shaped(n, new_m[n]) for n in weight_names],
            *[shaped(n, new_v[n]) for n in weight_names])
```
